```python
import math
import jax, jax.numpy as jnp
from jax import lax
import numpy as np

D_MODEL = 1024
BATCH = 16
SEQ = 256
DEPTH = 4
DEC_BATCH = 4
DEC_SEQ = 2048
PAST_LEN = 256

GRID_W = 64
N_DIRS = 2
D_A = D_MODEL // 4
RW_HEAD = 64
H_A = D_A // RW_HEAD
LORA_W = 64
LORA_A = 64
LORA_G = 128
D_B = D_MODEL // 2
DIFF_DH = 64
H_B = D_B // (2 * DIFF_DH)
D_C = D_MODEL // 4
HY_ORDER = 2
HY_DIRS = 2
HY_BANDS = 16
HY_EMB = 1 + 2 * HY_BANDS
HY_FFN = 64
HY_TARGET = 1e-2
HY_SHORT_PCT = 0.3
HY_LONG_PCT = 1.5
D_FF = 4 * D_MODEL
ROPE_BASE = 10000.0
Q_BLOCK = 128
RMS_EPS = 1e-6
GN_EPS = 64e-5
COL_SIZES = (3 * D_A, 2 * LORA_W, 2 * LORA_A, LORA_G, D_B, D_B, D_B, 3 * D_C)
IN_COLS = sum(COL_SIZES)

kernel_name = 'hybrid_rwkv7_diffattn_hyena_dit_step'


def _rmsnorm(x, g):
    xf = x.astype(jnp.float32)
    y = xf * lax.rsqrt(jnp.mean(xf * xf, axis=-1, keepdims=True) + RMS_EPS)
    return (y * g.astype(jnp.float32)).astype(x.dtype)


def _split_cols(u):
    idx = []
    s = 0
    for n in COL_SIZES[:-1]:
        s += n
        idx.append(s)
    return jnp.split(u, idx, axis=-1)


def _short_conv(u, w):
    up = jnp.pad(u, ((0, 0), (1, 1), (0, 0)))
    return w[0] * up[:, :-2] + w[1] * up[:, 1:-1] + w[2] * up[:, 2:]


def _axial_rope_tables(L):
    rows = L // GRID_W
    row = jnp.repeat(jnp.arange(rows), GRID_W).astype(jnp.float32)
    col = jnp.tile(jnp.arange(GRID_W), rows).astype(jnp.float32)
    half = DIFF_DH // 2
    inv = ROPE_BASE ** (-jnp.arange(0, half, 2, dtype=jnp.float32) / half)
    ang_r = row[:, None] * inv[None]
    ang_c = col[:, None] * inv[None]
    ang = jnp.concatenate([ang_r, ang_r, ang_c, ang_c], axis=-1)
    return jnp.cos(ang), jnp.sin(ang)


def _apply_rope(x, cos, sin):
    x1, x2, x3, x4 = jnp.split(x, 4, axis=-1)
    rot = jnp.concatenate([-x2, x1, -x4, x3], axis=-1)
    cb = cos[None, :, None, None, :]
    sb = sin[None, :, None, None, :]
    return (x.astype(jnp.float32) * cb + rot.astype(jnp.float32) * sb).astype(x.dtype)


def _diff_attention(q, k, v, lam):
    bsz, lq, nh, nm, dh = q.shape
    nb = lq // Q_BLOCK
    qb = jnp.moveaxis(q.reshape(bsz, nb, Q_BLOCK, nh, nm, dh), 1, 0)
    scale = dh ** -0.5

    def block(qblk):
        s = jnp.einsum('bqhmd,bkhmd->bhmqk', qblk, k).astype(jnp.float32) * scale
        p = jax.nn.softmax(s, axis=-1)
        wts = p[:, :, 0] - lam * p[:, :, 1]
        return jnp.einsum('bhqk,bkhe->bqhe', wts.astype(v.dtype), v)

    o = lax.map(block, qb)
    return jnp.moveaxis(o, 0, 1).reshape(bsz, lq, nh, v.shape[-1])


def _rwkv_scan(s0, r, w, kk, a, k, v, reverse):
    xs = tuple(jnp.swapaxes(t, 0, 1) for t in (r, w, kk, a, k, v))

    def step(s, xt):
        r_t, w_t, kk_t, a_t, k_t, v_t = xt
        sa = jnp.einsum('bhvk,bhk->bhv', s, -kk_t)
        s = (s * w_t[:, :, None, :] + sa[..., None] * (kk_t * a_t)[:, :, None, :]
             + v_t[..., None] * k_t[:, :, None, :])
        return s, jnp.einsum('bhvk,bhk->bhv', s, r_t)

    s_fin, ys = lax.scan(step, s0, xs, reverse=reverse)
    return jnp.swapaxes(ys, 0, 1), s_fin


def _rwkv_mix(u_rkv, u_w, u_a, u_g, s0, p):
    bsz, L, _ = u_rkv.shape
    f32 = jnp.float32

    def heads(t):
        return t.astype(f32).reshape(bsz, L, H_A, RW_HEAD)

    rkv = _short_conv(u_rkv, p['rwkv_conv'])
    r, k, v = jnp.split(rkv, 3, axis=-1)
    g = jax.nn.sigmoid(u_g) @ p['rwkv_g2']
    kk = heads(k * p['rwkv_kk'])
    kk = kk * lax.rsqrt(jnp.sum(kk * kk, axis=-1, keepdims=True) + 1e-12)
    rh, vh = heads(r), heads(v)
    uw = jnp.split(u_w, N_DIRS, axis=-1)
    ua = jnp.split(u_a, N_DIRS, axis=-1)
    ys, bonuses, finals = [], [], []
    for d in range(N_DIRS):
        wl = -jax.nn.softplus(-(p['rwkv_w0'][d] + jnp.tanh(uw[d]) @ p['rwkv_w2'][d]).astype(f32)) - 0.5
        decay = jnp.exp(-jnp.exp(wl))
        a = jax.nn.sigmoid(p['rwkv_a0'][d] + ua[d] @ p['rwkv_a2'][d])
        kd = heads(k * (1.0 + (a - 1.0) * p['rwkv_ka']))
        if s0 is None:
            init = jnp.zeros((bsz, H_A, RW_HEAD, RW_HEAD), f32)
        else:
            init = s0[:, d].astype(f32)
        yd, s_fin = _rwkv_scan(init, rh, heads(decay), kk, heads(a), kd, vh, reverse=(d == 1))
        ys.append(yd)
        bonuses.append(jnp.sum(rh * kd * p['rwkv_rk'].astype(f32), axis=-1, keepdims=True) * vh)
        finals.append(s_fin)
    y = ys[0] + ys[1]
    mu = jnp.mean(y, axis=-1, keepdims=True)
    var = jnp.mean(jnp.square(y - mu), axis=-1, keepdims=True)
    yn = ((y - mu) * lax.rsqrt(var + GN_EPS)).reshape(bsz, L, D_A)
    yn = yn * p['rwkv_ln_w'].astype(f32) + p['rwkv_ln_b'].astype(f32)
    out = (yn + (bonuses[0] + bonuses[1]).reshape(bsz, L, D_A)) * g.astype(f32)
    return out.astype(u_rkv.dtype), jnp.stack(finals, axis=1)


def _hyena_filters_freq(L, p):
    f32 = jnp.float32
    t = jnp.linspace(0.0, 1.0, L, dtype=f32)[:, None]
    ang = (2.0 * math.pi / L) * jnp.arange(L, dtype=f32)[:, None]
    bands = jnp.linspace(1e-4, HY_BANDS - 1, HY_BANDS, dtype=f32)[None, :]
    emb = jnp.concatenate([t, jnp.cos(bands * ang), -jnp.sin(bands * ang)], axis=-1)
    freq = p['hy_freq'].astype(f32)
    h = jnp.sin(freq * (emb @ p['hy_w1'].astype(f32) + p['hy_b1'].astype(f32)))
    h = jnp.sin(freq * (h @ p['hy_w2'].astype(f32) + p['hy_b2'].astype(f32)))
    h = (h @ p['hy_w3'].astype(f32)).reshape(L, HY_ORDER, HY_DIRS, D_C)
    h = h * jnp.exp(-t[:, :, None, None] * jnp.abs(p['hy_decay'].astype(f32)))
    h_fwd, h_bwd = h[:, :, 0], h[:, :, 1]
    filt = jnp.concatenate([h_fwd, jnp.zeros((1, HY_ORDER, D_C), f32), h_bwd[:0:-1]], axis=0)
    filt = filt * lax.rsqrt(jnp.sum(filt * filt, axis=0, keepdims=True) + 1e-6)
    return jnp.fft.rfft(filt, axis=0)


def _long_conv(z, filt_f, bias):
    L = z.shape[1]
    zf = jnp.fft.rfft(z.astype(jnp.float32), n=2 * L, axis=1)
    y = jnp.fft.irfft(zf * filt_f[None], n=2 * L, axis=1)[:, :L]
    return (y + bias.astype(jnp.float32) * z.astype(jnp.float32)).astype(z.dtype)


def _hyena_mix(u, p):
    L = u.shape[1]
    u = _short_conv(u, p['hy_conv_w']) + p['hy_conv_b']
    x1, x2, v = jnp.split(u, 3, axis=-1)
    filt_f = _hyena_filters_freq(L, p)
    z = x1 * _long_conv(v, filt_f[:, 0], p['hy_bias'][0])
    z = x2 * _long_conv(z, filt_f[:, 1], p['hy_bias'][1])
    return z


def _trunk_layer(x, mod, p, lam_init, cache):
    bsz, L, _ = x.shape
    f32 = jnp.float32
    sh1, sc1, gt1, sh2, sc2, gt2 = jnp.split(mod, 6, axis=-1)
    h = _rmsnorm(x, p['g_mix_pre']) * (1.0 + sc1) + sh1
    u_rkv, u_w, u_a, u_g, u_q, u_k, u_v, u_hy = _split_cols(h @ p['w_in'])
    y_a, s_fin = _rwkv_mix(u_rkv, u_w, u_a, u_g, None if cache is None else cache[0], p)
    q = u_q.reshape(bsz, L, H_B, 2, DIFF_DH)
    k = u_k.reshape(bsz, L, H_B, 2, DIFF_DH)
    v = u_v.reshape(bsz, L, H_B, 2 * DIFF_DH)
    lam = (jnp.exp(jnp.dot(p['diff_lq1'].astype(f32), p['diff_lk1'].astype(f32)))
           - jnp.exp(jnp.dot(p['diff_lq2'].astype(f32), p['diff_lk2'].astype(f32))) + lam_init)
    if cache is None:
        o = _diff_attention(q, k, v, lam)
    else:
        cos, sin = _axial_rope_tables(L)
        keys = jnp.concatenate([cache[1], _apply_rope(k, cos, sin)], axis=1)
        vals = jnp.concatenate([cache[2], v], axis=1)
        o = _diff_attention(_apply_rope(q, cos, sin), keys, vals, lam)
    y_b = (_rmsnorm(o, p['diff_subln']) * (1.0 - lam_init)).reshape(bsz, L, D_B)
    y_c = _hyena_mix(u_hy, p)
    mix = jnp.concatenate([y_a, y_b.astype(x.dtype), y_c.astype(x.dtype)], axis=-1) @ p['w_out']
    x = x + gt1 * _rmsnorm(mix, p['g_mix_post'])
    h = _rmsnorm(x, p['g_ffn_pre']) * (1.0 + sc2) + sh2
    f = jnp.square(jax.nn.relu(h @ p['w_ff1'])) @ p['w_ff2']
    x = x + gt2 * _rmsnorm(f, p['g_ffn_post'])
    return x, (s_fin, k, v)


def setup_inputs(seed: int = 0) -> dict:
    key = jax.random.key(seed)
    ks = jax.random.split(key, 48)

    def nrm(i, shape, scale=1.0):
        return jax.random.normal(ks[i], shape, jnp.float32) * scale

    D = D_MODEL
    conv_base = jnp.array([0.25, 1.0, 0.25], jnp.float32)[None, :, None]
    hy_rates = jnp.linspace(-math.log(HY_TARGET) / HY_LONG_PCT, -math.log(HY_TARGET) / HY_SHORT_PCT, D_C,
                            dtype=jnp.float32)
    return {
        'x_prompt': nrm(0, (BATCH, SEQ, D)),
        'x_sample': nrm(1, (DEC_BATCH, DEC_SEQ, D)),
        'state_rwkv': nrm(2, (DEC_BATCH, DEPTH, N_DIRS, H_A, RW_HEAD, RW_HEAD), 0.1),
        'cache_k': nrm(3, (DEC_BATCH, DEPTH, PAST_LEN, H_B, 2, DIFF_DH)),
        'cache_v': nrm(4, (DEC_BATCH, DEPTH, PAST_LEN, H_B, 2 * DIFF_DH)),
        'c': nrm(5, (DEC_BATCH, D)),
        'c_ctx': nrm(6, (D,)),
        'w_mod': nrm(7, (DEPTH, D, 6 * D), 0.5 * D ** -0.5),
        'b_mod': nrm(8, (DEPTH, 6 * D), 0.1),
        'g_mix_pre': 1.0 + nrm(9, (DEPTH, D), 0.05),
        'g_mix_post': 1.0 + nrm(10, (DEPTH, D), 0.05),
        'g_ffn_pre': 1.0 + nrm(11, (DEPTH, D), 0.05),
        'g_ffn_post': 1.0 + nrm(12, (DEPTH, D), 0.05),
        'w_in': nrm(13, (DEPTH, D, IN_COLS), D ** -0.5),
        'rwkv_conv': conv_base + nrm(14, (DEPTH, 3, 3 * D_A), 0.05),
        'rwkv_w0': jnp.linspace(-6.0, -1.0, D_A, dtype=jnp.float32) + nrm(15, (DEPTH, N_DIRS, D_A), 0.1),
        'rwkv_w2': nrm(16, (DEPTH, N_DIRS, LORA_W, D_A), 0.5 * LORA_W ** -0.5),
        'rwkv_a0': nrm(17, (DEPTH, N_DIRS, D_A), 0.1),
        'rwkv_a2': nrm(18, (DEPTH, N_DIRS, LORA_A, D_A), 0.5 * LORA_A ** -0.5),
        'rwkv_g2': nrm(19, (DEPTH, LORA_G, D_A), LORA_G ** -0.5),
        'rwkv_kk': 0.85 + nrm(20, (DEPTH, D_A), 0.05),
        'rwkv_ka': 1.0 + nrm(21, (DEPTH, D_A), 0.05),
        'rwkv_rk': nrm(22, (DEPTH, H_A, RW_HEAD), 0.1),
        'rwkv_ln_w': 1.0 + nrm(23, (DEPTH, D_A), 0.05),
        'rwkv_ln_b': nrm(24, (DEPTH, D_A), 0.02),
        'diff_lq1': nrm(25, (DEPTH, DIFF_DH), 0.1),
        'diff_lk1': nrm(26, (DEPTH, DIFF_DH), 0.1),
        'diff_lq2': nrm(27, (DEPTH, DIFF_DH), 0.1),
        'diff_lk2': nrm(28, (DEPTH, DIFF_DH), 0.1),
        'diff_subln': 1.0 + nrm(29, (DEPTH, 2 * DIFF_DH), 0.05),
        'hy_conv_w': conv_base + nrm(30, (DEPTH, 3, 3 * D_C), 0.05),
        'hy_conv_b': nrm(31, (DEPTH, 3 * D_C), 0.02),
        'hy_w1': nrm(32, (DEPTH, HY_EMB, HY_FFN), HY_EMB ** -0.5),
        'hy_b1': nrm(33, (DEPTH, HY_FFN), 0.1),
        'hy_freq': 1.0 + nrm(34, (DEPTH, HY_FFN), 0.05),
        'hy_w2': nrm(35, (DEPTH, HY_FFN, HY_FFN), HY_FFN ** -0.5),
        'hy_b2': nrm(36, (DEPTH, HY_FFN), 0.1),
        'hy_w3': nrm(37, (DEPTH, HY_FFN, HY_ORDER * HY_DIRS * D_C), HY_FFN ** -0.5),
        'hy_decay': hy_rates * (1.0 + nrm(38, (DEPTH, D_C), 0.05)),
        'hy_bias': nrm(39, (DEPTH, HY_ORDER, D_C), 0.1),
        'w_out': nrm(40, (DEPTH, D, D), D ** -0.5),
        'w_ff1': nrm(41, (DEPTH, D, D_FF), D ** -0.5),
        'w_ff2': nrm(42, (DEPTH, D_FF, D), D_FF ** -0.5),
    }


def reference(x_prompt, x_sample, state_rwkv, cache_k, cache_v, c, c_ctx, w_mod, b_mod,
              g_mix_pre, g_mix_post, g_ffn_pre, g_ffn_post, w_in, rwkv_conv, rwkv_w0, rwkv_w2,
              rwkv_a0, rwkv_a2, rwkv_g2, rwkv_kk, rwkv_ka, rwkv_rk, rwkv_ln_w, rwkv_ln_b,
              diff_lq1, diff_lk1, diff_lq2, diff_lk2, diff_subln, hy_conv_w, hy_conv_b,
              hy_w1, hy_b1, hy_freq, hy_w2, hy_b2, hy_w3, hy_decay, hy_bias,
              w_out, w_ff1, w_ff2):
    xp = x_prompt
    xs = x_sample
    st_list, k_list, v_list = [], [], []
    for l in range(DEPTH):
        p = {
            'g_mix_pre': g_mix_pre[l], 'g_mix_post': g_mix_post[l],
            'g_ffn_pre': g_ffn_pre[l], 'g_ffn_post': g_ffn_post[l],
            'w_in': w_in[l], 'rwkv_conv': rwkv_conv[l], 'rwkv_w0': rwkv_w0[l], 'rwkv_w2': rwkv_w2[l],
            'rwkv_a0': rwkv_a0[l], 'rwkv_a2': rwkv_a2[l], 'rwkv_g2': rwkv_g2[l],
            'rwkv_kk': rwkv_kk[l], 'rwkv_ka': rwkv_ka[l], 'rwkv_rk': rwkv_rk[l],
            'rwkv_ln_w': rwkv_ln_w[l], 'rwkv_ln_b': rwkv_ln_b[l],
            'diff_lq1': diff_lq1[l], 'diff_lk1': diff_lk1[l], 'diff_lq2': diff_lq2[l],
            'diff_lk2': diff_lk2[l], 'diff_subln': diff_subln[l],
            'hy_conv_w': hy_conv_w[l], 'hy_conv_b': hy_conv_b[l], 'hy_w1': hy_w1[l], 'hy_b1': hy_b1[l],
            'hy_freq': hy_freq[l], 'hy_w2': hy_w2[l], 'hy_b2': hy_b2[l], 'hy_w3': hy_w3[l],
            'hy_decay': hy_decay[l], 'hy_bias': hy_bias[l],
            'w_out': w_out[l], 'w_ff1': w_ff1[l], 'w_ff2': w_ff2[l],
        }
        lam_init = 0.8 - 0.6 * math.exp(-0.3 * l)
        mod_ctx = (jax.nn.silu(c_ctx) @ w_mod[l] + b_mod[l])[None, None, :]
        xp, (s_ctx, k_ctx, v_ctx) = _trunk_layer(xp, mod_ctx, p, lam_init, None)
        st_list.append(s_ctx.astype(x_prompt.dtype))
        k_list.append(k_ctx)
        v_list.append(v_ctx)
        mod_lat = (jax.nn.silu(c) @ w_mod[l] + b_mod[l])[:, None, :]
        xs, _ = _trunk_layer(xs, mod_lat, p, lam_init, (state_rwkv[:, l], cache_k[:, l], cache_v[:, l]))
    new_state_rwkv = jnp.stack(st_list, axis=1)
    new_cache_k = jnp.stack(k_list, axis=1)
    new_cache_v = jnp.stack(v_list, axis=1)
    return (xp, xs, new_state_rwkv, new_cache_k, new_cache_v)
```

```python
import functools
import math

import jax
import jax.numpy as jnp
from jax import lax
from jax.experimental import pallas as pl
from jax.experimental.pallas import tpu as pltpu

F32 = jnp.float32
BF16 = jnp.bfloat16

D_MODEL = 1024
DEPTH = 4
GRID_W = 64
D_A = 256
RW_HEAD = 64
H_A = 4
D_B = 512
DIFF_DH = 64
H_B = 4
D_C = 256
HY_BANDS = 16
HY_EMB = 33
HY_FFN = 64
D_FF = 4096
ROPE_BASE = 10000.0
RMS_EPS = 1e-6
GN_EPS = 64e-5
N_A_COLS = 1152
IN_COLS = 3456

TL = 256
V7X_VMEM_LIMIT = 56 * 1024 * 1024


def _cparams(n_axes, vmem=None):
    return pltpu.CompilerParams(dimension_semantics=("arbitrary",) * n_axes,
                                vmem_limit_bytes=vmem)


def _split3(a):
    hi = a.astype(BF16)
    r1 = a - hi.astype(F32)
    mid = r1.astype(BF16)
    lo = (r1 - mid.astype(F32)).astype(BF16)
    return hi, mid, lo


def _split2(a):
    hi = a.astype(BF16)
    lo = (a - hi.astype(F32)).astype(BF16)
    return hi, lo


def _segsum(a, bd):
    hi, mid, lo = _split3(a)
    return (jnp.dot(hi, bd, preferred_element_type=F32) + jnp.dot(mid, bd, preferred_element_type=F32)
            + jnp.dot(lo, bd, preferred_element_type=F32))


def _block_diag(n, seg):
    r = lax.broadcasted_iota(jnp.int32, (n, n), 0) // seg
    c = lax.broadcasted_iota(jnp.int32, (n, n), 1) // seg
    return (r == c).astype(BF16)


def _rms(x, g):
    return x * lax.rsqrt(jnp.mean(x * x, axis=-1, keepdims=True) + RMS_EPS) * g


def _mod_kernel(cond_ref, w_ref, b_ref, o_ref):
    c = cond_ref[...]
    s = (c * jax.nn.sigmoid(c)).astype(BF16)
    o_ref[0] = jnp.dot(s, w_ref[0].astype(BF16), preferred_element_type=F32) + b_ref[0]


def _modulation(cond, w_mod, b_mod):
    tn = 1536
    return pl.pallas_call(
        _mod_kernel,
        grid=(DEPTH, 6 * D_MODEL // tn),
        in_specs=[pl.BlockSpec((8, D_MODEL), lambda l, n: (0, 0)),
                  pl.BlockSpec((1, D_MODEL, tn), lambda l, n: (l, 0, n)),
                  pl.BlockSpec((1, 1, tn), lambda l, n: (l, 0, n))],
        out_specs=pl.BlockSpec((1, 8, tn), lambda l, n: (l, 0, n)),
        out_shape=jax.ShapeDtypeStruct((DEPTH, 8, 6 * D_MODEL), F32),
        compiler_params=_cparams(2, 40 * 1024 * 1024),
        name="modulation",
    )(cond, w_mod, b_mod.reshape(DEPTH, 1, 6 * D_MODEL))


def _rope(x, cos, sin_a, sin_b):
    n = x.shape[-1]
    return x * cos + pltpu.roll(x, n - 16, axis=1) * sin_a + pltpu.roll(x, 16, axis=1) * sin_b


def _inproj_kernel(*refs, rope):
    if rope:
        x_ref, mod_ref, g_ref, w_ref, cos_ref, sa_ref, sb_ref, ua_ref, uq_ref, uk_ref, uv_ref, uh_ref = refs
    else:
        x_ref, mod_ref, g_ref, w_ref, ua_ref, uq_ref, uk_ref, uv_ref, uh_ref = refs
    x = x_ref[0]
    mod = mod_ref[0]
    sh1 = mod[:, 0:D_MODEL]
    sc1 = mod[:, D_MODEL:2 * D_MODEL]
    h = _rms(x, g_ref[...]) * (1.0 + sc1) + sh1
    u = jnp.dot(h.astype(BF16), w_ref[...], preferred_element_type=F32)
    ua_ref[0] = u[:, 0:N_A_COLS]
    q = u[:, N_A_COLS:N_A_COLS + D_B] * (DIFF_DH ** -0.5)
    k = u[:, N_A_COLS + D_B:N_A_COLS + 2 * D_B]
    if rope:
        q = _rope(q, cos_ref[...], sa_ref[...], sb_ref[...])
        k = _rope(k, cos_ref[...], sa_ref[...], sb_ref[...])
    uq_ref[0] = q.astype(uq_ref.dtype)
    uk_ref[0] = k.astype(uk_ref.dtype)
    uv_ref[0] = u[:, N_A_COLS + 2 * D_B:N_A_COLS + 3 * D_B].astype(uv_ref.dtype)
    uh_ref[0] = u[:, N_A_COLS + 3 * D_B:IN_COLS]


def _inproj(x, mod, g_pre, w_in, rope_tabs):
    B, L, _ = x.shape
    rope = rope_tabs is not None
    mb = mod.shape[0]
    row = lambda b, i: (b, i, 0)
    const2 = lambda b, i: (0, 0)
    in_specs = [pl.BlockSpec((1, TL, D_MODEL), row),
                pl.BlockSpec((1, 1, 6 * D_MODEL), (lambda b, i: (b, 0, 0)) if mb > 1 else (lambda b, i: (0, 0, 0))),
                pl.BlockSpec((1, D_MODEL), const2),
                pl.BlockSpec((D_MODEL, IN_COLS), const2)]
    args = [x, mod, g_pre.reshape(1, D_MODEL), w_in]
    if rope:
        in_specs += [pl.BlockSpec((TL, D_B), lambda b, i: (i, 0))] * 3
        args += list(rope_tabs)
    qkv_dt = BF16 if rope else F32
    out_shape = [jax.ShapeDtypeStruct((B, L, N_A_COLS), F32),
                 jax.ShapeDtypeStruct((B, L, D_B), qkv_dt),
                 jax.ShapeDtypeStruct((B, L, D_B), qkv_dt),
                 jax.ShapeDtypeStruct((B, L, D_B), qkv_dt),
                 jax.ShapeDtypeStruct((B, L, 3 * D_C), F32)]
    out_specs = [pl.BlockSpec((1, TL, N_A_COLS), row), pl.BlockSpec((1, TL, D_B), row),
                 pl.BlockSpec((1, TL, D_B), row), pl.BlockSpec((1, TL, D_B), row),
                 pl.BlockSpec((1, TL, 3 * D_C), row)]
    return pl.pallas_call(
        functools.partial(_inproj_kernel, rope=rope),
        grid=(B, L // TL), in_specs=in_specs, out_specs=out_specs, out_shape=out_shape,
        compiler_params=_cparams(2, 48 * 1024 * 1024), name="inproj",
    )(*args)


def _shift_conv(u, prev_row, next_row, cw):
    n = u.shape[0]
    row = lax.broadcasted_iota(jnp.int32, (n, 1), 0)
    up = jnp.where(row == 0, prev_row, pltpu.roll(u, 1, axis=0))
    un = jnp.where(row == n - 1, next_row, pltpu.roll(u, n - 1, axis=0))
    return cw[0:1] * up + cw[1:2] * u + cw[2:3] * un


def _prep_kernel(ua_ref, uap_ref, uan_ref, uh_ref, uhp_ref, uhn_ref,
                 cw_ref, w0_ref, w2_ref, a0_ref, a2_ref, g2_ref, kkw_ref, kaw_ref, rk_ref, hcw_ref, hcb_ref,
                 r_ref, v_ref, nkk_ref, wf_ref, kaf_ref, kdf_ref, wb_ref, kab_ref, kdb_ref, g_ref, bon_ref,
                 x1_ref, x2_ref, hv_ref):
    i = pl.program_id(1)
    nt = pl.num_programs(1)
    has_prev = i > 0
    has_next = i < nt - 1
    bd = _block_diag(D_A, RW_HEAD)

    ua = ua_ref[0]
    u_rkv = ua[:, 0:3 * D_A]
    prev = jnp.where(has_prev, uap_ref[0, 7:8, :], 0.0)
    nxt = jnp.where(has_next, uan_ref[0, 0:1, :], 0.0)
    rkv = _shift_conv(u_rkv, prev, nxt, cw_ref[...])
    r = rkv[:, 0:D_A]
    k = rkv[:, D_A:2 * D_A]
    v = rkv[:, 2 * D_A:3 * D_A]
    u_w = ua[:, 768:896]
    u_a = ua[:, 896:1024]
    u_g = ua[:, 1024:1152]

    g = jnp.dot(jax.nn.sigmoid(u_g).astype(BF16), g2_ref[...].astype(BF16), preferred_element_type=F32)
    kk = k * kkw_ref[...]
    kk = kk * lax.rsqrt(_segsum(kk * kk, bd) + 1e-12)
    r_ref[0] = r
    v_ref[0] = v
    nkk_ref[0] = -kk
    g_ref[0] = g

    kd_sum = None
    outs = ((wf_ref, kaf_ref, kdf_ref), (wb_ref, kab_ref, kdb_ref))
    for d in range(2):
        uw_d = u_w[:, d * 64:(d + 1) * 64]
        ua_d = u_a[:, d * 64:(d + 1) * 64]
        xw = w0_ref[d:d + 1, :] + jnp.dot(jnp.tanh(uw_d).astype(BF16), w2_ref[d].astype(BF16),
                                          preferred_element_type=F32)
        z = -xw
        softplus = jnp.maximum(z, 0.0) + jnp.log(1.0 + jnp.exp(-jnp.abs(z)))
        wl = -softplus - 0.5
        decay = jnp.exp(-jnp.exp(wl))
        a = jax.nn.sigmoid(a0_ref[d:d + 1, :] + jnp.dot(ua_d.astype(BF16), a2_ref[d].astype(BF16),
                                                        preferred_element_type=F32))
        kd = k * (1.0 + (a - 1.0) * kaw_ref[...])
        w_o, ka_o, kd_o = outs[d]
        w_o[0] = decay
        ka_o[0] = kk * a
        kd_o[0] = kd
        kd_sum = kd if kd_sum is None else kd_sum + kd
    bon_ref[0] = _segsum(r * kd_sum * rk_ref[...], bd) * v

    uh = uh_ref[0]
    prevh = jnp.where(has_prev, uhp_ref[0, 7:8, :], 0.0)
    nxth = jnp.where(has_next, uhn_ref[0, 0:1, :], 0.0)
    hc = _shift_conv(uh, prevh, nxth, hcw_ref[...]) + hcb_ref[...]
    x1_ref[0] = hc[:, 0:D_C]
    x2_ref[0] = hc[:, D_C:2 * D_C]
    hv_ref[0] = hc[:, 2 * D_C:3 * D_C]


def _prep(ua, uh, p):
    B, L, _ = ua.shape
    nb8 = L // 8
    t8 = TL // 8
    row = lambda b, i: (b, i, 0)
    prev = lambda b, i: (b, jnp.maximum(i * t8 - 1, 0), 0)
    nxt = lambda b, i: (b, jnp.minimum((i + 1) * t8, nb8 - 1), 0)

    def full(a):
        nd = a.ndim
        return pl.BlockSpec(a.shape, lambda b, i: (0,) * nd)

    params = [p['rwkv_conv'], p['rwkv_w0'], p['rwkv_w2'], p['rwkv_a0'], p['rwkv_a2'], p['rwkv_g2'],
              p['rwkv_kk'].reshape(1, D_A), p['rwkv_ka'].reshape(1, D_A), p['rwkv_rk'].reshape(1, D_A),
              p['hy_conv_w'], p['hy_conv_b'].reshape(1, 3 * D_C)]
    in_specs = [pl.BlockSpec((1, TL, N_A_COLS), row), pl.BlockSpec((1, 8, 3 * D_A), prev),
                pl.BlockSpec((1, 8, 3 * D_A), nxt),
                pl.BlockSpec((1, TL, 3 * D_C), row), pl.BlockSpec((1, 8, 3 * D_C), prev),
                pl.BlockSpec((1, 8, 3 * D_C), nxt)] + [full(a) for a in params]
    n_out = 14
    return pl.pallas_call(
        _prep_kernel, grid=(B, L // TL), in_specs=in_specs,
        out_specs=[pl.BlockSpec((1, TL, D_A), row)] * n_out,
        out_shape=[jax.ShapeDtypeStruct((B, L, D_A), F32)] * n_out,
        compiler_params=_cparams(2, 40 * 1024 * 1024), name="prep",
    )(ua, ua, ua, uh, uh, uh, *params)


def _scan_kernel(nkkf_ref, rf_ref, vf_ref, wf_ref, kaf_ref, kdf_ref,
                 nkkb_ref, rb_ref, vb_ref, wb_ref, kab_ref, kdb_ref, s0_ref,
                 yf_ref, yb_ref, sfin_ref, s_ref, *, G):
    j = pl.program_id(1)
    nj = pl.num_programs(1)

    @pl.when(j == 0)
    def _():
        s_ref[...] = s0_ref[...]

    lane = lax.broadcasted_iota(jnp.int32, (RW_HEAD, 128), 1)
    sub = lax.broadcasted_iota(jnp.int32, (RW_HEAD, 128), 0)
    lo = lane < RW_HEAD
    eye = (lane & (RW_HEAD - 1)) == sub
    eye_lo = jnp.logical_and(eye, lo)
    eye_hi = jnp.logical_and(eye, jnp.logical_not(lo))
    lo_row = lax.broadcasted_iota(jnp.int32, (1, 128), 1) < RW_HEAD
    srcs = ((nkkf_ref, rf_ref, vf_ref, wf_ref, kaf_ref, kdf_ref, yf_ref),
            (nkkb_ref, rb_ref, vb_ref, wb_ref, kab_ref, kdb_ref, yb_ref))

    def seg_cols(a_lo, a_hi):
        return jnp.where(lo, jnp.sum(a_lo, axis=1, keepdims=True), jnp.sum(a_hi, axis=1, keepdims=True))

    sub8 = lax.broadcasted_iota(jnp.int32, (8, 128), 0)
    n_groups = TL // 8

    def group(gi, carry):
        for g in range(G):
            for d in range(2):
                gg = gi if d == 0 else n_groups - 1 - gi
                rows = pl.ds(pl.multiple_of(gg * 8, 8), 8)
                nkk_r, r_r, v_r, w_r, ka_r, kd_r, y_r = srcs[d]
                for pr in range(2):
                    sl = slice(pr * 128, (pr + 1) * 128)
                    nkk_t = jnp.where(lo_row, nkk_r[g, rows, sl], 0.0)
                    nkk_u = nkk_r[g, rows, sl] - nkk_t
                    r_t = jnp.where(lo_row, r_r[g, rows, sl], 0.0)
                    r_u = r_r[g, rows, sl] - r_t
                    v_t = v_r[g, rows, sl]
                    w_t = w_r[g, rows, sl]
                    ka_t = ka_r[g, rows, sl]
                    kd_t = kd_r[g, rows, sl]
                    s = s_ref[g, d, pr]
                    yacc = jnp.zeros((8, 128), F32)
                    for u in range(8):
                        q = u if d == 0 else 7 - u
                        row = slice(q, q + 1)
                        sa = seg_cols(s * nkk_t[row], s * nkk_u[row])
                        vv = v_t[row]
                        vcol = seg_cols(jnp.where(eye_lo, vv, 0.0), jnp.where(eye_hi, vv, 0.0))
                        s = s * w_t[row] + sa * ka_t[row] + vcol * kd_t[row]
                        ycol = seg_cols(s * r_t[row], s * r_u[row])
                        yrow = jnp.sum(jnp.where(eye, ycol, 0.0), axis=0, keepdims=True)
                        yacc = jnp.where(sub8 == q, yrow, yacc)
                    y_r[g, rows, sl] = yacc
                    s_ref[g, d, pr] = s
        return carry

    lax.fori_loop(0, n_groups, group, 0)

    @pl.when(j == nj - 1)
    def _():
        sfin_ref[...] = s_ref[...]


def _scan(nkk, r, v, wf, kaf, kdf, wb, kab, kdb, s0):
    B, L, _ = r.shape
    G = 2
    nj = L // TL
    fwd = lambda b, j: (b, j, 0)
    bwd = lambda b, j: (b, nj - 1 - j, 0)
    st = lambda b, j: (b, 0, 0, 0, 0)
    blk = (G, TL, D_A)
    in_specs = ([pl.BlockSpec(blk, fwd)] * 6 + [pl.BlockSpec(blk, bwd)] * 6
                + [pl.BlockSpec((G, 2, 2, RW_HEAD, 128), st)])
    return pl.pallas_call(
        functools.partial(_scan_kernel, G=G), grid=(B // G, nj), in_specs=in_specs,
        out_specs=[pl.BlockSpec(blk, fwd), pl.BlockSpec(blk, bwd), pl.BlockSpec((G, 2, 2, RW_HEAD, 128), st)],
        out_shape=[jax.ShapeDtypeStruct((B, L, D_A), F32), jax.ShapeDtypeStruct((B, L, D_A), F32),
                   jax.ShapeDtypeStruct((B, 2, 2, RW_HEAD, 128), F32)],
        scratch_shapes=[pltpu.VMEM((G, 2, 2, RW_HEAD, 128), F32)],
        compiler_params=_cparams(2, 40 * 1024 * 1024), name="rwkv_scan",
    )(nkk, r, v, wf, kaf, kdf, nkk, r, v, wb, kab, kdb, s0)


def _state_to_tiles(s):
    B = s.shape[0]
    s = s.reshape(B, 2, 2, 2, RW_HEAD, RW_HEAD)
    return jnp.transpose(s, (0, 1, 2, 4, 3, 5)).reshape(B, 2, 2, RW_HEAD, 128)


def _tiles_to_state(t):
    B = t.shape[0]
    t = t.reshape(B, 2, 2, RW_HEAD, 2, RW_HEAD)
    return jnp.transpose(t, (0, 1, 2, 4, 3, 5)).reshape(B, 2, H_A, RW_HEAD, RW_HEAD)


def _attn_kernel(*refs, has_cache, lam_init):
    if has_cache:
        q_ref, k_ref, v_ref, ck_ref, cv_ref, lq1, lk1, lq2, lk2, sub_ref, o_ref = refs
    else:
        q_ref, k_ref, v_ref, lq1, lk1, lq2, lk2, sub_ref, o_ref = refs
    l1 = jnp.sum(lq1[...] * lk1[...], axis=-1, keepdims=True)
    l2 = jnp.sum(lq2[...] * lk2[...], axis=-1, keepdims=True)
    lam = jnp.exp(l1) - jnp.exp(l2) + lam_init
    q = q_ref[0].astype(BF16)
    kn = k_ref[0].astype(BF16)
    vn = v_ref[0].astype(BF16)
    if has_cache:
        kc = ck_ref[0, 0].astype(BF16)
        vc = cv_ref[0, 0].astype(BF16)
    dn = (((1,), (1,)), ((), ()))
    for h in range(H_B):
        w_new = None
        w_old = None
        for m in range(2):
            c0 = (h * 2 + m) * DIFF_DH
            qh = q[:, c0:c0 + DIFF_DH]
            s_n = lax.dot_general(qh, kn[:, c0:c0 + DIFF_DH], dn, preferred_element_type=F32)
            mx = jnp.max(s_n, axis=-1, keepdims=True)
            if has_cache:
                s_c = lax.dot_general(qh, kc[:, c0:c0 + DIFF_DH], dn, preferred_element_type=F32)
                mx = jnp.maximum(mx, jnp.max(s_c, axis=-1, keepdims=True))
            e_n = jnp.exp(s_n - mx)
            den = jnp.sum(e_n, axis=-1, keepdims=True)
            if has_cache:
                e_c = jnp.exp(s_c - mx)
                den = den + jnp.sum(e_c, axis=-1, keepdims=True)
            scale = (1.0 / den) if m == 0 else (-lam / den)
            w_new = e_n * scale if m == 0 else w_new + e_n * scale
            if has_cache:
                w_old = e_c * scale if m == 0 else w_old + e_c * scale
        vh = vn[:, h * 128:(h + 1) * 128]
        o = jnp.dot(w_new.astype(BF16), vh, preferred_element_type=F32)
        if has_cache:
            o = o + jnp.dot(w_old.astype(BF16), vc[:, h * 128:(h + 1) * 128], preferred_element_type=F32)
        o_ref[0, :, h * 128:(h + 1) * 128] = _rms(o, sub_ref[...]) * (1.0 - lam_init)


def _attention(q, k, v, cache, l, p, lam_init):
    B, L, _ = q.shape
    has_cache = cache is not None
    tq = TL
    in_specs = [pl.BlockSpec((1, tq, D_B), lambda b, i: (b, i, 0)),
                pl.BlockSpec((1, L, D_B), lambda b, i: (b, 0, 0)),
                pl.BlockSpec((1, L, D_B), lambda b, i: (b, 0, 0))]
    args = [q, k, v]
    if has_cache:
        ck, cv = cache
        past = ck.shape[2]
        in_specs += [pl.BlockSpec((1, 1, past, D_B), lambda b, i: (b, l, 0, 0))] * 2
        args += [ck, cv]
    small = lambda n: pl.BlockSpec((1, n), lambda b, i: (0, 0))
    in_specs += [small(DIFF_DH)] * 4 + [small(128)]
    args += [p['diff_lq1'].reshape(1, -1), p['diff_lk1'].reshape(1, -1), p['diff_lq2'].reshape(1, -1),
             p['diff_lk2'].reshape(1, -1), p['diff_subln'].reshape(1, -1)]
    return pl.pallas_call(
        functools.partial(_attn_kernel, has_cache=has_cache, lam_init=lam_init),
        grid=(B, L // tq), in_specs=in_specs,
        out_specs=pl.BlockSpec((1, tq, D_B), lambda b, i: (b, i, 0)),
        out_shape=jax.ShapeDtypeStruct((B, L, D_B), F32),
        compiler_params=_cparams(2, 48 * 1024 * 1024), name="diff_attention",
    )(*args)


def _hfilt_time_kernel(emb_ref, w1_ref, b1_ref, fr_ref, w2_ref, b2_ref, w3_ref, dec_ref, h_ref, ss_ref, acc_ref):
    i = pl.program_id(0)
    emb = emb_ref[...]
    fr = fr_ref[...]
    h = jnp.sin(fr * (jnp.dot(emb.astype(BF16), w1_ref[...].astype(BF16), preferred_element_type=F32) + b1_ref[...]))
    h = jnp.sin(fr * (jnp.dot(h.astype(BF16), w2_ref[...].astype(BF16), preferred_element_type=F32) + b2_ref[...]))
    h = jnp.dot(h.astype(BF16), w3_ref[...].astype(BF16), preferred_element_type=F32)
    h = h * jnp.exp(-emb[:, 0:1] * jnp.abs(dec_ref[...]))
    n = h.shape[0]
    row = lax.broadcasted_iota(jnp.int32, (n, 4 * D_C), 0) + i * n
    col = lax.broadcasted_iota(jnp.int32, (n, 4 * D_C), 1)
    is_bwd = ((col // D_C) % 2) == 1
    h = jnp.where(jnp.logical_and(is_bwd, row == 0), 0.0, h)
    h_ref[...] = h

    @pl.when(i == 0)
    def _():
        acc_ref[...] = jnp.zeros_like(acc_ref)

    acc_ref[...] += jnp.sum(h * h, axis=0, keepdims=True)
    ss_ref[...] = acc_ref[...]


def _hfilt_freq_kernel(h_ref, ss_ref, ch_ref, cl_ref, sh_ref, sl_ref, hr_ref, hi_ref,
                       ah_ref, al_ref, bh_ref, bl_ref, nyq_ref):
    i = pl.program_id(0)

    @pl.when(i == 0)
    def _():
        for o in range(2):
            hf = h_ref[:, o * 2 * D_C:o * 2 * D_C + D_C]
            hb = h_ref[:, o * 2 * D_C + D_C:(o + 1) * 2 * D_C]
            a = hf + hb
            b = hb - hf
            n = a.shape[0]
            alt = 1.0 - 2.0 * (lax.broadcasted_iota(jnp.int32, (n, 1), 0) % 2).astype(F32)
            nyq_ref[:, o * D_C:(o + 1) * D_C] = jnp.sum(a * alt, axis=0, keepdims=True)
            ahh, all_ = _split2(a)
            bhh, bll = _split2(b)
            ah_ref[:, o * D_C:(o + 1) * D_C] = ahh
            al_ref[:, o * D_C:(o + 1) * D_C] = all_
            bh_ref[:, o * D_C:(o + 1) * D_C] = bhh
            bl_ref[:, o * D_C:(o + 1) * D_C] = bll

    ss = ss_ref[...]
    tot = jnp.concatenate([ss[:, 0:D_C] + ss[:, D_C:2 * D_C], ss[:, 2 * D_C:3 * D_C] + ss[:, 3 * D_C:4 * D_C]], axis=1)
    scale = lax.rsqrt(tot + 1e-6)

    def dft(mh, ml, xh, xl):
        return (jnp.dot(mh, xh, preferred_element_type=F32) + jnp.dot(mh, xl, preferred_element_type=F32)
                + jnp.dot(ml, xh, preferred_element_type=F32))

    hr_ref[...] = dft(ch_ref[...], cl_ref[...], ah_ref[...], al_ref[...]) * scale
    hi = dft(sh_ref[...], sl_ref[...], bh_ref[...], bl_ref[...]) * scale
    tf = hi.shape[0]
    row = lax.broadcasted_iota(jnp.int32, (tf, 1), 0) + i * tf
    hi_ref[...] = jnp.where(row == 0, nyq_ref[...] * scale, hi)


def _hyena_filters(L, p, tabs):
    emb, ch, cl, sh, sl = tabs
    tr = 256
    w1 = jnp.pad(p['hy_w1'], ((0, 128 - HY_EMB), (0, 0)))
    full = lambda a: pl.BlockSpec(a.shape, lambda i: (0,) * a.ndim)
    params = [w1, p['hy_b1'].reshape(1, -1), p['hy_freq'].reshape(1, -1), p['hy_w2'], p['hy_b2'].reshape(1, -1),
              p['hy_w3'], jnp.tile(p['hy_decay'], 4).reshape(1, -1)]
    h_raw, ss = pl.pallas_call(
        _hfilt_time_kernel, grid=(L // tr,),
        in_specs=[pl.BlockSpec((tr, 128), lambda i: (i, 0))] + [full(a) for a in params],
        out_specs=[pl.BlockSpec((tr, 4 * D_C), lambda i: (i, 0)), pl.BlockSpec((1, 4 * D_C), lambda i: (0, 0))],
        out_shape=[jax.ShapeDtypeStruct((L, 4 * D_C), F32), jax.ShapeDtypeStruct((1, 4 * D_C), F32)],
        scratch_shapes=[pltpu.VMEM((1, 4 * D_C), F32)],
        compiler_params=_cparams(1), name="hyena_filter_time",
    )(emb, *params)
    tf = 256
    mat = pl.BlockSpec((tf, L), lambda i: (i, 0))
    return pl.pallas_call(
        _hfilt_freq_kernel, grid=(L // tf,),
        in_specs=[pl.BlockSpec((L, 4 * D_C), lambda i: (0, 0)), pl.BlockSpec((1, 4 * D_C), lambda i: (0, 0)),
                  mat, mat, mat, mat],
        out_specs=[pl.BlockSpec((tf, 2 * D_C), lambda i: (i, 0))] * 2,
        out_shape=[jax.ShapeDtypeStruct((L, 2 * D_C), F32)] * 2,
        scratch_shapes=[pltpu.VMEM((L, 2 * D_C), BF16)] * 4 + [pltpu.VMEM((1, 2 * D_C), F32)],
        compiler_params=_cparams(1, 48 * 1024 * 1024), name="hyena_filter_freq",
    )(h_raw, ss, ch, cl, sh, sl)


def _hconv_kernel(z_ref, zt_ref, gate_ref, hr_ref, hi_ref, bias_ref, ch_ref, cl_ref, sh_ref, sl_ref, o_ref,
                  zh_ref, zl_ref, yrh_ref, yrl_ref, yih_ref, yil_ref, nyq_ref, *, BG, L):
    ph = pl.program_id(1)
    i = pl.program_id(2)
    inv_n = 1.0 / (2 * L)

    @pl.when(jnp.logical_and(ph == 0, i == 0))
    def _():
        alt = 1.0 - 2.0 * (lax.broadcasted_iota(jnp.int32, (L, 1), 0) % 2).astype(F32)
        for b in range(BG):
            z = z_ref[b]
            zh, zl = _split2(z)
            zh_ref[:, b * D_C:(b + 1) * D_C] = zh
            zl_ref[:, b * D_C:(b + 1) * D_C] = zl
            nyq_ref[:, b * D_C:(b + 1) * D_C] = jnp.sum(z * alt, axis=0, keepdims=True)

    def dft(mh, ml, xh, xl):
        return (jnp.dot(mh, xh, preferred_element_type=F32) + jnp.dot(mh, xl, preferred_element_type=F32)
                + jnp.dot(ml, xh, preferred_element_type=F32))

    tf = ch_ref.shape[0]

    @pl.when(ph == 0)
    def _():
        zr = dft(ch_ref[...], cl_ref[...], zh_ref[...], zl_ref[...])
        zs = dft(sh_ref[...], sl_ref[...], zh_ref[...], zl_ref[...])
        hr = hr_ref[...]
        hi = hi_ref[...]
        row = lax.broadcasted_iota(jnp.int32, (tf, 1), 0) + i * tf
        wgt = jnp.where(row == 0, inv_n, 2.0 * inv_n)
        for b in range(BG):
            sl = slice(b * D_C, (b + 1) * D_C)
            yr = (zr[:, sl] * hr + zs[:, sl] * hi) * wgt
            yi = (zr[:, sl] * hi - zs[:, sl] * hr) * (2.0 * inv_n)
            yrh, yrl = _split2(yr)
            yih, yil = _split2(yi)
            rows = pl.ds(pl.multiple_of(i * tf, tf), tf)
            yrh_ref[rows, sl] = yrh
            yrl_ref[rows, sl] = yrl
            yih_ref[rows, sl] = yih
            yil_ref[rows, sl] = yil

        @pl.when(i == 0)
        def _():
            for b in range(BG):
                sl = slice(b * D_C, (b + 1) * D_C)
                nyq_ref[:, sl] = nyq_ref[:, sl] * hi[0:1, :] * inv_n

    @pl.when(ph == 1)
    def _():
        y = (dft(ch_ref[...], cl_ref[...], yrh_ref[...], yrl_ref[...])
             - dft(sh_ref[...], sl_ref[...], yih_ref[...], yil_ref[...]))
        row = lax.broadcasted_iota(jnp.int32, (tf, 1), 0) + i * tf
        alt = 1.0 - 2.0 * (row % 2).astype(F32)
        y = y + alt * nyq_ref[...]
        for b in range(BG):
            sl = slice(b * D_C, (b + 1) * D_C)
            o_ref[b] = gate_ref[b] * (y[:, sl] + bias_ref[...] * zt_ref[b])


def _hconv(z, gate, hr, hi, bias, order, tabs):
    B, L, _ = z.shape
    _, ch, cl, sh, sl = tabs
    BG = min(B, 8) if L <= 256 else 2
    tf = 256
    nt = L // tf
    mat = pl.BlockSpec((tf, L), lambda g, ph, i: (i, 0))
    tile = pl.BlockSpec((BG, tf, D_C), lambda g, ph, i: (g, i * ph, 0))
    hspec = pl.BlockSpec((tf, D_C), lambda g, ph, i: (i * (1 - ph), order))
    cols = BG * D_C
    return pl.pallas_call(
        functools.partial(_hconv_kernel, BG=BG, L=L), grid=(B // BG, 2, nt),
        in_specs=[pl.BlockSpec((BG, L, D_C), lambda g, ph, i: (g, 0, 0)), tile, tile, hspec, hspec,
                  pl.BlockSpec((1, D_C), lambda g, ph, i: (0, 0)), mat, mat, mat, mat],
        out_specs=tile,
        out_shape=jax.ShapeDtypeStruct((B, L, D_C), F32),
        scratch_shapes=[pltpu.VMEM((L, cols), BF16)] * 6 + [pltpu.VMEM((1, cols), F32)],
        compiler_params=_cparams(3, 48 * 1024 * 1024), name="hyena_conv",
    )(z, z, gate, hr, hi, bias[order].reshape(1, D_C), ch, cl, sh, sl)


def _outproj_kernel(x_ref, yf_ref, yb_ref, bon_ref, g_ref, ob_ref, yc_ref, mod_ref,
                    lnw_ref, lnb_ref, gpost_ref, gffn_ref, w_ref, x1_ref, h2_ref):
    bd = _block_diag(D_A, RW_HEAD)
    y = yf_ref[0] + yb_ref[0]
    mu = _segsum(y, bd) * (1.0 / RW_HEAD)
    yc = y - mu
    var = _segsum(yc * yc, bd) * (1.0 / RW_HEAD)
    yn = yc * lax.rsqrt(var + GN_EPS) * lnw_ref[...] + lnb_ref[...]
    ya = (yn + bon_ref[0]) * g_ref[0]
    mix = (jnp.dot(ya.astype(BF16), w_ref[0:D_A, :], preferred_element_type=F32)
           + jnp.dot(ob_ref[0].astype(BF16), w_ref[D_A:D_A + D_B, :], preferred_element_type=F32)
           + jnp.dot(yc_ref[0].astype(BF16), w_ref[D_A + D_B:D_MODEL, :], preferred_element_type=F32))
    mod = mod_ref[0]
    gt1 = mod[:, 2 * D_MODEL:3 * D_MODEL]
    sh2 = mod[:, 3 * D_MODEL:4 * D_MODEL]
    sc2 = mod[:, 4 * D_MODEL:5 * D_MODEL]
    x1 = x_ref[0] + gt1 * _rms(mix, gpost_ref[...])
    x1_ref[0] = x1
    h2_ref[0] = (_rms(x1, gffn_ref[...]) * (1.0 + sc2) + sh2).astype(BF16)


def _outproj(x, yf, yb, bon, g, ob, yc, mod, p, w_out):
    B, L, _ = x.shape
    mb = mod.shape[0]
    row = lambda b, i: (b, i, 0)
    c2 = lambda b, i: (0, 0)
    t = lambda n: pl.BlockSpec((1, TL, n), row)
    in_specs = [t(D_MODEL), t(D_A), t(D_A), t(D_A), t(D_A), t(D_B), t(D_C),
                pl.BlockSpec((1, 1, 6 * D_MODEL), (lambda b, i: (b, 0, 0)) if mb > 1 else (lambda b, i: (0, 0, 0))),
                pl.BlockSpec((1, D_A), c2), pl.BlockSpec((1, D_A), c2),
                pl.BlockSpec((1, D_MODEL), c2), pl.BlockSpec((1, D_MODEL), c2),
                pl.BlockSpec((D_MODEL, D_MODEL), c2)]
    return pl.pallas_call(
        _outproj_kernel, grid=(B, L // TL), in_specs=in_specs,
        out_specs=[t(D_MODEL), t(D_MODEL)],
        out_shape=[jax.ShapeDtypeStruct((B, L, D_MODEL), F32), jax.ShapeDtypeStruct((B, L, D_MODEL), BF16)],
        compiler_params=_cparams(2, 40 * 1024 * 1024), name="outproj",
    )(x, yf, yb, bon, g, ob, yc, mod, p['rwkv_ln_w'].reshape(1, -1), p['rwkv_ln_b'].reshape(1, -1),
      p['g_mix_post'].reshape(1, -1), p['g_ffn_pre'].reshape(1, -1), w_out)


def _ffn_kernel(h_ref, x_ref, mod_ref, g_ref, w1_ref, w2_ref, o_ref):
    h = h_ref[0]
    acc = None
    ck = 1024
    for c in range(D_FF // ck):
        a = jnp.dot(h, w1_ref[:, c * ck:(c + 1) * ck], preferred_element_type=F32)
        a = jnp.square(jnp.maximum(a, 0.0)).astype(BF16)
        part = jnp.dot(a, w2_ref[c * ck:(c + 1) * ck, :], preferred_element_type=F32)
        acc = part if acc is None else acc + part
    gt2 = mod_ref[0][:, 5 * D_MODEL:6 * D_MODEL]
    o_ref[0] = x_ref[0] + gt2 * _rms(acc, g_ref[...])


def _ffn(h2, x1, mod, g_post, w1, w2):
    B, L, _ = x1.shape
    mb = mod.shape[0]
    row = lambda b, i: (b, i, 0)
    c2 = lambda b, i: (0, 0)
    return pl.pallas_call(
        _ffn_kernel, grid=(B, L // TL),
        in_specs=[pl.BlockSpec((1, TL, D_MODEL), row), pl.BlockSpec((1, TL, D_MODEL), row),
                  pl.BlockSpec((1, 1, 6 * D_MODEL), (lambda b, i: (b, 0, 0)) if mb > 1 else (lambda b, i: (0, 0, 0))),
                  pl.BlockSpec((1, D_MODEL), c2),
                  pl.BlockSpec((D_MODEL, D_FF), c2), pl.BlockSpec((D_FF, D_MODEL), c2)],
        out_specs=pl.BlockSpec((1, TL, D_MODEL), row),
        out_shape=jax.ShapeDtypeStruct((B, L, D_MODEL), F32),
        compiler_params=_cparams(2, V7X_VMEM_LIMIT), name="ffn",
    )(h2, x1, mod, g_post.reshape(1, -1), w1, w2)


def _rope_tables(L):
    rows = L // GRID_W
    row = jnp.repeat(jnp.arange(rows), GRID_W).astype(F32)
    col = jnp.tile(jnp.arange(GRID_W), rows).astype(F32)
    half = DIFF_DH // 2
    inv = ROPE_BASE ** (-jnp.arange(0, half, 2, dtype=F32) / half)
    ang_r = row[:, None] * inv[None]
    ang_c = col[:, None] * inv[None]
    ang = jnp.concatenate([ang_r, ang_r, ang_c, ang_c], axis=-1)
    cos, sin = jnp.cos(ang), jnp.sin(ang)
    quarter = (jnp.arange(DIFF_DH) // 16) % 2
    sin_a = jnp.where(quarter == 0, -sin, 0.0)
    sin_b = jnp.where(quarter == 1, sin, 0.0)
    rep = D_B // DIFF_DH
    return tuple(jnp.tile(t, (1, rep)) for t in (cos, sin_a, sin_b))


def _hyena_tables(L):
    t = jnp.linspace(0.0, 1.0, L, dtype=F32)[:, None]
    ang = (2.0 * math.pi / L) * jnp.arange(L, dtype=F32)[:, None]
    bands = jnp.linspace(1e-4, HY_BANDS - 1, HY_BANDS, dtype=F32)[None, :]
    emb = jnp.concatenate([t, jnp.cos(bands * ang), -jnp.sin(bands * ang)], axis=-1)
    emb = jnp.pad(emb, ((0, 0), (0, 128 - HY_EMB)))
    n = jnp.arange(L, dtype=jnp.int32)
    k = (n[:, None] * n[None, :]) % (2 * L)
    theta = k.astype(F32) * (math.pi / L)
    c = jnp.cos(theta)
    s = jnp.sin(theta)
    ch, cl = _split2(c)
    sh, sl = _split2(s)
    return emb, ch, cl, sh, sl


def _layer(x, mod, p, wb, l, lam_init, cache, rope_tabs, hy_tabs, filt):
    B, L, _ = x.shape
    ua, uq, uk, uv, uh = _inproj(x, mod, p['g_mix_pre'], wb['w_in'], rope_tabs)
    (r, v, nkk, wf, kaf, kdf, wbk, kab, kdb, g, bon, x1h, x2h, hv) = _prep(ua, uh, p)
    if cache is None:
        s0 = jnp.zeros((B, 2, 2, RW_HEAD, 128), F32)
        kv_cache = None
    else:
        s0 = cache[0]
        kv_cache = (cache[1], cache[2])
    yf, yb, sfin = _scan(nkk, r, v, wf, kaf, kdf, wbk, kab, kdb, s0)
    ob = _attention(uq, uk, uv, kv_cache, l, p, lam_init)
    hr, hi = filt
    z1 = _hconv(hv, x1h, hr, hi, p['hy_bias'], 0, hy_tabs)
    yc = _hconv(z1, x2h, hr, hi, p['hy_bias'], 1, hy_tabs)
    x1, h2 = _outproj(x, yf, yb, bon, g, ob, yc, mod, p, wb['w_out'])
    x2 = _ffn(h2, x1, mod, p['g_ffn_post'], wb['w_ff1'], wb['w_ff2'])
    return x2, (sfin, uk, uv)


_LAYER_KEYS = ('g_mix_pre', 'g_mix_post', 'g_ffn_pre', 'g_ffn_post', 'rwkv_conv', 'rwkv_w0', 'rwkv_w2', 'rwkv_a0',
               'rwkv_a2', 'rwkv_g2', 'rwkv_kk', 'rwkv_ka', 'rwkv_rk', 'rwkv_ln_w', 'rwkv_ln_b', 'diff_lq1', 'diff_lk1',
               'diff_lq2', 'diff_lk2', 'diff_subln', 'hy_conv_w', 'hy_conv_b', 'hy_w1', 'hy_b1', 'hy_freq', 'hy_w2',
               'hy_b2', 'hy_w3', 'hy_decay', 'hy_bias')


def kernel(x_prompt, x_sample, state_rwkv, cache_k, cache_v, c, c_ctx, w_mod, b_mod, g_mix_pre, g_mix_post, g_ffn_pre, g_ffn_post, w_in, rwkv_conv, rwkv_w0, rwkv_w2, rwkv_a0, rwkv_a2, rwkv_g2, rwkv_kk, rwkv_ka, rwkv_rk, rwkv_ln_w, rwkv_ln_b, diff_lq1, diff_lk1, diff_lq2, diff_lk2, diff_subln, hy_conv_w, hy_conv_b, hy_w1, hy_b1, hy_freq, hy_w2, hy_b2, hy_w3, hy_decay, hy_bias, w_out, w_ff1, w_ff2):
    stacked = dict(g_mix_pre=g_mix_pre, g_mix_post=g_mix_post, g_ffn_pre=g_ffn_pre, g_ffn_post=g_ffn_post,
                   rwkv_conv=rwkv_conv, rwkv_w0=rwkv_w0, rwkv_w2=rwkv_w2, rwkv_a0=rwkv_a0, rwkv_a2=rwkv_a2,
                   rwkv_g2=rwkv_g2, rwkv_kk=rwkv_kk, rwkv_ka=rwkv_ka, rwkv_rk=rwkv_rk, rwkv_ln_w=rwkv_ln_w,
                   rwkv_ln_b=rwkv_ln_b, diff_lq1=diff_lq1, diff_lk1=diff_lk1, diff_lq2=diff_lq2, diff_lk2=diff_lk2,
                   diff_subln=diff_subln, hy_conv_w=hy_conv_w, hy_conv_b=hy_conv_b, hy_w1=hy_w1, hy_b1=hy_b1,
                   hy_freq=hy_freq, hy_w2=hy_w2, hy_b2=hy_b2, hy_w3=hy_w3, hy_decay=hy_decay, hy_bias=hy_bias)
    Bc, Lc, _ = x_prompt.shape
    Bs, Ls, _ = x_sample.shape
    past = cache_k.shape[2]

    cond = jnp.zeros((8, D_MODEL), F32).at[0:Bs].set(c).at[Bs].set(c_ctx)
    mod_all = _modulation(cond, w_mod, b_mod)

    rope_tabs = _rope_tables(Ls)
    tabs_c = _hyena_tables(Lc)
    tabs_s = _hyena_tables(Ls)
    ck = cache_k.reshape(Bs, DEPTH, past, D_B)
    cv = cache_v.reshape(Bs, DEPTH, past, D_B)

    xp, xs = x_prompt, x_sample
    st_list, k_list, v_list = [], [], []
    for l in range(DEPTH):
        p = {k: stacked[k][l] for k in _LAYER_KEYS}
        wb = dict(w_in=w_in[l].astype(BF16), w_out=w_out[l].astype(BF16),
                  w_ff1=w_ff1[l].astype(BF16), w_ff2=w_ff2[l].astype(BF16))
        lam_init = 0.8 - 0.6 * math.exp(-0.3 * l)
        mod_lat = mod_all[l, 0:Bs].reshape(Bs, 1, 6 * D_MODEL)
        mod_ctx = mod_all[l, Bs:Bs + 1].reshape(1, 1, 6 * D_MODEL)
        filt_c = _hyena_filters(Lc, p, tabs_c)
        filt_s = _hyena_filters(Ls, p, tabs_s)
        xp, (s_ctx, k_ctx, v_ctx) = _layer(xp, mod_ctx, p, wb, l, lam_init, None, None, tabs_c, filt_c)
        st_list.append(_tiles_to_state(s_ctx))
        k_list.append(k_ctx.reshape(Bc, Lc, H_B, 2, DIFF_DH))
        v_list.append(v_ctx.reshape(Bc, Lc, H_B, 2 * DIFF_DH))
        s0 = _state_to_tiles(state_rwkv[:, l])
        xs, _ = _layer(xs, mod_lat, p, wb, l, lam_init, (s0, ck, cv), rope_tabs, tabs_s, filt_s)
    return (xp, xs, jnp.stack(st_list, axis=1), jnp.stack(k_list, axis=1), jnp.stack(v_list, axis=1))
```

```python
import functools
import math

import jax
import jax.numpy as jnp
from jax import lax
from jax.experimental import pallas as pl
from jax.experimental.pallas import tpu as pltpu

F32 = jnp.float32
BF16 = jnp.bfloat16

D_MODEL = 1024
DEPTH = 4
GRID_W = 64
D_A = 256
RW_HEAD = 64
H_A = 4
D_B = 512
DIFF_DH = 64
H_B = 4
D_C = 256
HY_BANDS = 16
HY_EMB = 33
HY_FFN = 64
D_FF = 4096
ROPE_BASE = 10000.0
RMS_EPS = 1e-6
GN_EPS = 64e-5
N_A_COLS = 1152
IN_COLS = 3456

TL = 256
V7X_VMEM_LIMIT = 56 * 1024 * 1024


def _cparams(n_axes, vmem=None):
    return pltpu.CompilerParams(dimension_semantics=("arbitrary",) * n_axes,
                                vmem_limit_bytes=vmem)


def _split3(a):
    hi = a.astype(BF16)
    r1 = a - hi.astype(F32)
    mid = r1.astype(BF16)
    lo = (r1 - mid.astype(F32)).astype(BF16)
    return hi, mid, lo


def _split2(a):
    hi = a.astype(BF16)
    lo = (a - hi.astype(F32)).astype(BF16)
    return hi, lo


def _segsum(a, bd):
    hi, mid, lo = _split3(a)
    return (jnp.dot(hi, bd, preferred_element_type=F32) + jnp.dot(mid, bd, preferred_element_type=F32)
            + jnp.dot(lo, bd, preferred_element_type=F32))


def _block_diag(n, seg):
    r = lax.broadcasted_iota(jnp.int32, (n, n), 0) // seg
    c = lax.broadcasted_iota(jnp.int32, (n, n), 1) // seg
    return (r == c).astype(BF16)


def _rms(x, g):
    return x * lax.rsqrt(jnp.mean(x * x, axis=-1, keepdims=True) + RMS_EPS) * g


def _mod_kernel(cond_ref, w_ref, b_ref, o_ref):
    c = cond_ref[...]
    s = (c * jax.nn.sigmoid(c)).astype(BF16)
    o_ref[0] = jnp.dot(s, w_ref[0].astype(BF16), preferred_element_type=F32) + b_ref[0]


def _modulation(cond, w_mod, b_mod):
    tn = 1536
    return pl.pallas_call(
        _mod_kernel,
        grid=(DEPTH, 6 * D_MODEL // tn),
        in_specs=[pl.BlockSpec((8, D_MODEL), lambda l, n: (0, 0)),
                  pl.BlockSpec((1, D_MODEL, tn), lambda l, n: (l, 0, n)),
                  pl.BlockSpec((1, 1, tn), lambda l, n: (l, 0, n))],
        out_specs=pl.BlockSpec((1, 8, tn), lambda l, n: (l, 0, n)),
        out_shape=jax.ShapeDtypeStruct((DEPTH, 8, 6 * D_MODEL), F32),
        compiler_params=_cparams(2, 40 * 1024 * 1024),
        name="modulation",
    )(cond, w_mod, b_mod.reshape(DEPTH, 1, 6 * D_MODEL))


def _rope(x, cos, sin_a, sin_b):
    n = x.shape[-1]
    return x * cos + pltpu.roll(x, n - 16, axis=1) * sin_a + pltpu.roll(x, 16, axis=1) * sin_b


def _inproj_kernel(*refs, rope):
    if rope:
        x_ref, mod_ref, g_ref, w_ref, cos_ref, sa_ref, sb_ref, ua_ref, uq_ref, uk_ref, uv_ref, uh_ref = refs
    else:
        x_ref, mod_ref, g_ref, w_ref, ua_ref, uq_ref, uk_ref, uv_ref, uh_ref = refs
    x = x_ref[0]
    mod = mod_ref[0]
    sh1 = mod[:, 0:D_MODEL]
    sc1 = mod[:, D_MODEL:2 * D_MODEL]
    h = _rms(x, g_ref[...]) * (1.0 + sc1) + sh1
    u = jnp.dot(h.astype(BF16), w_ref[...], preferred_element_type=F32)
    ua_ref[0] = u[:, 0:N_A_COLS]
    q = u[:, N_A_COLS:N_A_COLS + D_B] * (DIFF_DH ** -0.5)
    k = u[:, N_A_COLS + D_B:N_A_COLS + 2 * D_B]
    if rope:
        q = _rope(q, cos_ref[...], sa_ref[...], sb_ref[...])
        k = _rope(k, cos_ref[...], sa_ref[...], sb_ref[...])
    uq_ref[0] = q.astype(uq_ref.dtype)
    uk_ref[0] = k.astype(uk_ref.dtype)
    uv_ref[0] = u[:, N_A_COLS + 2 * D_B:N_A_COLS + 3 * D_B].astype(uv_ref.dtype)
    uh_ref[0] = u[:, N_A_COLS + 3 * D_B:IN_COLS]


def _inproj(x, mod, g_pre, w_in, rope_tabs):
    B, L, _ = x.shape
    rope = rope_tabs is not None
    mb = mod.shape[0]
    row = lambda b, i: (b, i, 0)
    const2 = lambda b, i: (0, 0)
    in_specs = [pl.BlockSpec((1, TL, D_MODEL), row),
                pl.BlockSpec((1, 1, 6 * D_MODEL), (lambda b, i: (b, 0, 0)) if mb > 1 else (lambda b, i: (0, 0, 0))),
                pl.BlockSpec((1, D_MODEL), const2),
                pl.BlockSpec((D_MODEL, IN_COLS), const2)]
    args = [x, mod, g_pre.reshape(1, D_MODEL), w_in]
    if rope:
        in_specs += [pl.BlockSpec((TL, D_B), lambda b, i: (i, 0))] * 3
        args += list(rope_tabs)
    qkv_dt = BF16 if rope else F32
    out_shape = [jax.ShapeDtypeStruct((B, L, N_A_COLS), F32),
                 jax.ShapeDtypeStruct((B, L, D_B), qkv_dt),
                 jax.ShapeDtypeStruct((B, L, D_B), qkv_dt),
                 jax.ShapeDtypeStruct((B, L, D_B), qkv_dt),
                 jax.ShapeDtypeStruct((B, L, 3 * D_C), F32)]
    out_specs = [pl.BlockSpec((1, TL, N_A_COLS), row), pl.BlockSpec((1, TL, D_B), row),
                 pl.BlockSpec((1, TL, D_B), row), pl.BlockSpec((1, TL, D_B), row),
                 pl.BlockSpec((1, TL, 3 * D_C), row)]
    return pl.pallas_call(
        functools.partial(_inproj_kernel, rope=rope),
        grid=(B, L // TL), in_specs=in_specs, out_specs=out_specs, out_shape=out_shape,
        compiler_params=_cparams(2, 48 * 1024 * 1024), name="inproj",
    )(*args)


def _shift_conv(u, prev_row, next_row, cw):
    n = u.shape[0]
    row = lax.broadcasted_iota(jnp.int32, (n, 1), 0)
    up = jnp.where(row == 0, prev_row, pltpu.roll(u, 1, axis=0))
    un = jnp.where(row == n - 1, next_row, pltpu.roll(u, n - 1, axis=0))
    return cw[0:1] * up + cw[1:2] * u + cw[2:3] * un


def _prep_kernel(ua_ref, uap_ref, uan_ref, uh_ref, uhp_ref, uhn_ref,
                 cw_ref, w0_ref, w2_ref, a0_ref, a2_ref, g2_ref, kkw_ref, kaw_ref, rk_ref, hcw_ref, hcb_ref,
                 r_ref, v_ref, nkk_ref, wf_ref, kaf_ref, kdf_ref, wb_ref, kab_ref, kdb_ref, g_ref, bon_ref,
                 x1_ref, x2_ref, hv_ref):
    i = pl.program_id(1)
    nt = pl.num_programs(1)
    has_prev = i > 0
    has_next = i < nt - 1
    bd = _block_diag(D_A, RW_HEAD)

    ua = ua_ref[0]
    u_rkv = ua[:, 0:3 * D_A]
    prev = jnp.where(has_prev, uap_ref[0, 7:8, :], 0.0)
    nxt = jnp.where(has_next, uan_ref[0, 0:1, :], 0.0)
    rkv = _shift_conv(u_rkv, prev, nxt, cw_ref[...])
    r = rkv[:, 0:D_A]
    k = rkv[:, D_A:2 * D_A]
    v = rkv[:, 2 * D_A:3 * D_A]
    u_w = ua[:, 768:896]
    u_a = ua[:, 896:1024]
    u_g = ua[:, 1024:1152]

    g = jnp.dot(jax.nn.sigmoid(u_g).astype(BF16), g2_ref[...].astype(BF16), preferred_element_type=F32)
    kk = k * kkw_ref[...]
    kk = kk * lax.rsqrt(_segsum(kk * kk, bd) + 1e-12)
    r_ref[0] = r
    v_ref[0] = v
    nkk_ref[0] = -kk
    g_ref[0] = g

    kd_sum = None
    outs = ((wf_ref, kaf_ref, kdf_ref), (wb_ref, kab_ref, kdb_ref))
    for d in range(2):
        uw_d = u_w[:, d * 64:(d + 1) * 64]
        ua_d = u_a[:, d * 64:(d + 1) * 64]
        xw = w0_ref[d:d + 1, :] + jnp.dot(jnp.tanh(uw_d).astype(BF16), w2_ref[d].astype(BF16),
                                          preferred_element_type=F32)
        z = -xw
        softplus = jnp.maximum(z, 0.0) + jnp.log(1.0 + jnp.exp(-jnp.abs(z)))
        wl = -softplus - 0.5
        log_decay = -jnp.exp(wl)
        a = jax.nn.sigmoid(a0_ref[d:d + 1, :] + jnp.dot(ua_d.astype(BF16), a2_ref[d].astype(BF16),
                                                        preferred_element_type=F32))
        kd = k * (1.0 + (a - 1.0) * kaw_ref[...])
        w_o, ka_o, kd_o = outs[d]
        w_o[0] = log_decay
        ka_o[0] = kk * a
        kd_o[0] = kd
        kd_sum = kd if kd_sum is None else kd_sum + kd
    bon_ref[0] = _segsum(r * kd_sum * rk_ref[...], bd) * v

    uh = uh_ref[0]
    prevh = jnp.where(has_prev, uhp_ref[0, 7:8, :], 0.0)
    nxth = jnp.where(has_next, uhn_ref[0, 0:1, :], 0.0)
    hc = _shift_conv(uh, prevh, nxth, hcw_ref[...]) + hcb_ref[...]
    x1_ref[0] = hc[:, 0:D_C]
    x2_ref[0] = hc[:, D_C:2 * D_C]
    hv_ref[0] = hc[:, 2 * D_C:3 * D_C]


def _prep(ua, uh, p):
    B, L, _ = ua.shape
    nb8 = L // 8
    t8 = TL // 8
    row = lambda b, i: (b, i, 0)
    prev = lambda b, i: (b, jnp.maximum(i * t8 - 1, 0), 0)
    nxt = lambda b, i: (b, jnp.minimum((i + 1) * t8, nb8 - 1), 0)

    def full(a):
        nd = a.ndim
        return pl.BlockSpec(a.shape, lambda b, i: (0,) * nd)

    params = [p['rwkv_conv'], p['rwkv_w0'], p['rwkv_w2'], p['rwkv_a0'], p['rwkv_a2'], p['rwkv_g2'],
              p['rwkv_kk'].reshape(1, D_A), p['rwkv_ka'].reshape(1, D_A), p['rwkv_rk'].reshape(1, D_A),
              p['hy_conv_w'], p['hy_conv_b'].reshape(1, 3 * D_C)]
    in_specs = [pl.BlockSpec((1, TL, N_A_COLS), row), pl.BlockSpec((1, 8, 3 * D_A), prev),
                pl.BlockSpec((1, 8, 3 * D_A), nxt),
                pl.BlockSpec((1, TL, 3 * D_C), row), pl.BlockSpec((1, 8, 3 * D_C), prev),
                pl.BlockSpec((1, 8, 3 * D_C), nxt)] + [full(a) for a in params]
    n_out = 14
    return pl.pallas_call(
        _prep_kernel, grid=(B, L // TL), in_specs=in_specs,
        out_specs=[pl.BlockSpec((1, TL, D_A), row)] * n_out,
        out_shape=[jax.ShapeDtypeStruct((B, L, D_A), F32)] * n_out,
        compiler_params=_cparams(2, 40 * 1024 * 1024), name="prep",
    )(ua, ua, ua, uh, uh, uh, *params)


CHUNK = 64
_NN = (((1,), (0,)), ((), ()))
_NT = (((1,), (1,)), ((), ()))
_TN = (((0,), (0,)), ((), ()))


def _mm3(a, b, dims):
    ah, al = _split2(a)
    bh, bl = _split2(b)
    f = lambda x, y: lax.dot_general(x, y, dims, preferred_element_type=F32)
    return f(ah, bh) + f(ah, bl) + f(al, bh)


def _mm1(a, b, dims):
    return lax.dot_general(a.astype(BF16), b.astype(BF16), dims, preferred_element_type=F32)


def _unit_tri_inverses(ms, row, col):
    eye = (row == col).astype(F32)
    blk = lambda n: (row // n) == (col // n)
    m8 = [jnp.where(blk(8), m, 0.0) for m in ms]
    a2 = [_mm1(x, x, _NN) for x in m8]
    a4 = [_mm1(x, x, _NN) for x in a2]
    p = [eye + x + y + _mm1(x, y, _NN) for x, y in zip(m8, a2)]
    ts = [x + _mm1(x, y, _NN) for x, y in zip(p, a4)]
    n = 8
    while n < CHUNK:
        sel = jnp.logical_and(blk(2 * n), jnp.logical_not(blk(n)))
        ot = [_mm1(jnp.where(sel, m, 0.0), t, _NN) for m, t in zip(ms, ts)]
        ts = [t + _mm1(t, o, _NN) for t, o in zip(ts, ot)]
        n *= 2
    return ts


def _chunk_kernel(nkk_ref, r_ref, v_ref, lwf_ref, kaf_ref, kdf_ref, lwb_ref, kab_ref, kdb_ref,
                  rpf_ref, ypf_ref, ff_ref, gf_ref, pcf_ref, rpb_ref, ypb_ref, fb_ref, gb_ref, pcb_ref):
    C = CHUNK
    row = lax.broadcasted_iota(jnp.int32, (C, C), 0)
    col = lax.broadcasted_iota(jnp.int32, (C, C), 1)
    dirs = ((lwf_ref, kaf_ref, kdf_ref, rpf_ref, ypf_ref, ff_ref, gf_ref, pcf_ref),
            (lwb_ref, kab_ref, kdb_ref, rpb_ref, ypb_ref, fb_ref, gb_ref, pcb_ref))
    n_chunks = nkk_ref.shape[1] // C

    def body(c, carry):
        rows = pl.ds(pl.multiple_of(c * C, C), C)
        alpha = nkk_ref[0, rows, :]
        r = r_ref[0, rows, :]
        v = v_ref[0, rows, :]
        probs = []
        for d in range(2):
            lw_ref, b_ref, k_ref = dirs[d][0:3]
            before = (row > col) if d == 0 else (row < col)
            upto = (row >= col) if d == 0 else (row <= col)
            lw = lw_ref[0, rows, :]
            beta = b_ref[0, rows, :]
            kd = k_ref[0, rows, :]
            tri = upto.astype(BF16)
            l_hi, l_mid, l_lo = _split3(lw)
            lam = (jnp.dot(tri, l_hi, preferred_element_type=F32) + jnp.dot(tri, l_mid, preferred_element_type=F32)
                   + jnp.dot(tri, l_lo, preferred_element_type=F32))
            lam_prev = lam - lw
            lam_mid = lam[C // 2:C // 2 + 1, :]
            lam_end = lam[C - 1:C, :] if d == 0 else lam[0:1, :]
            e_to_mid = jnp.exp(lam_mid - lam)
            a_m = alpha * jnp.exp(lam_prev - lam_mid)
            b_m = beta * e_to_mid
            k_m = kd * e_to_mid
            r_m = r * jnp.exp(lam - lam_mid)
            a_0 = alpha * jnp.exp(lam_prev)
            r_0 = r * jnp.exp(lam)
            e_to_end = jnp.exp(lam_end - lam)
            b_e = beta * e_to_end
            k_e = kd * e_to_end
            dirs[d][7][0, pl.ds(c, 1)] = jnp.exp(lam_end).reshape(1, 1, D_A)
            for h in range(H_A):
                sl = slice(h * RW_HEAD, (h + 1) * RW_HEAD)
                probs.append(dict(
                    d=d, h=h, before=before, upto=upto,
                    ar=jnp.concatenate([a_m[:, sl], r_m[:, sl]], axis=0),
                    bk=jnp.concatenate([b_m[:, sl], k_m[:, sl]], axis=0),
                    a0=a_0[:, sl], r0=r_0[:, sl], vh=v[:, sl],
                    bke=jnp.concatenate([b_e[:, sl], k_e[:, sl]], axis=0)))
        g12 = [_mm3(q['ar'], q['bk'], _NT) for q in probs]
        ms = [jnp.where(q['before'], g[0:C, 0:C], 0.0) for q, g in zip(probs, g12)]
        ns = [jnp.where(q['before'], g[0:C, C:2 * C], 0.0) for q, g in zip(probs, g12)]
        rbk = [jnp.concatenate([jnp.where(q['upto'], g[C:2 * C, 0:C], 0.0),
                                jnp.where(q['upto'], g[C:2 * C, C:2 * C], 0.0)], axis=1) for q, g in zip(probs, g12)]
        nvs = [_mm3(n, q['vh'], _NN) for q, n in zip(probs, ns)]
        ts = _unit_tri_inverses(ms, row, col)
        xs = [_mm3(t, jnp.concatenate([q['a0'], nv], axis=1), _NN) for q, t, nv in zip(probs, ts, nvs)]
        zeros = jnp.zeros((C, RW_HEAD), F32)
        ws = [jnp.concatenate([x, jnp.concatenate([zeros, q['vh']], axis=1)], axis=0) for q, x in zip(probs, xs)]
        rys = [_mm3(rb, w, _NN) for rb, w in zip(rbk, ws)]
        fgs = [_mm3(w, q['bke'], _TN) for q, w in zip(probs, ws)]
        for q, fg in zip(probs, fgs):
            f_ref, g_ref = dirs[q['d']][5], dirs[q['d']][6]
            f_ref[0, pl.ds(c, 1), q['h']] = fg[0:RW_HEAD].reshape(1, RW_HEAD, RW_HEAD)
            g_ref[0, pl.ds(c, 1), q['h']] = fg[RW_HEAD:2 * RW_HEAD].reshape(1, RW_HEAD, RW_HEAD)
        for d in range(2):
            sub = [(q, ry) for q, ry in zip(probs, rys) if q['d'] == d]
            dirs[d][3][0, rows, :] = jnp.concatenate([q['r0'] + ry[:, 0:RW_HEAD] for q, ry in sub], axis=1)
            dirs[d][4][0, rows, :] = jnp.concatenate([ry[:, RW_HEAD:2 * RW_HEAD] for q, ry in sub], axis=1)
        return carry

    lax.fori_loop(0, n_chunks, body, 0)


def _state_kernel(rpf_ref, ypf_ref, ff_ref, gf_ref, pcf_ref, rpb_ref, ypb_ref, fb_ref, gb_ref, pcb_ref, s0_ref,
                  yf_ref, yb_ref, sfin_ref, s_ref):
    C = CHUNK
    j = pl.program_id(1)
    nj = pl.num_programs(1)
    n_chunks = rpf_ref.shape[1] // C

    @pl.when(j == 0)
    def _():
        s_ref[...] = s0_ref[0]

    dirs = ((rpf_ref, ypf_ref, ff_ref, gf_ref, pcf_ref, yf_ref), (rpb_ref, ypb_ref, fb_ref, gb_ref, pcb_ref, yb_ref))
    for cc in range(n_chunks):
        for d in range(2):
            rp_ref, yp_ref, f_ref, g_ref, pc_ref, y_ref = dirs[d]
            c = cc if d == 0 else n_chunks - 1 - cc
            rows = slice(c * C, (c + 1) * C)
            pc = pc_ref[0, c]
            y_parts = []
            for h in range(H_A):
                sl = slice(h * RW_HEAD, (h + 1) * RW_HEAD)
                s = s_ref[d, h]
                y_parts.append(yp_ref[0, rows, sl] + _mm3(rp_ref[0, rows, sl], s, _NT))
                s_ref[d, h] = s * pc[:, sl] + _mm3(s, f_ref[0, c, h], _NN) + g_ref[0, c, h]
            y_ref[0, rows, :] = jnp.concatenate(y_parts, axis=1)

    @pl.when(j == nj - 1)
    def _():
        sfin_ref[0] = s_ref[...]


def _scan(nkk, r, v, lwf, kaf, kdf, lwb, kab, kdb, s0):
    B, L, _ = r.shape
    nj = L // TL
    cpb = TL // CHUNK
    nc = L // CHUNK
    row = lambda b, j: (b, j, 0)
    tile = pl.BlockSpec((1, TL, D_A), row)
    mats = pl.BlockSpec((1, cpb, H_A, RW_HEAD, RW_HEAD), lambda b, j: (b, j, 0, 0, 0))
    pcs = pl.BlockSpec((1, cpb, 1, D_A), lambda b, j: (b, j, 0, 0))
    f32 = lambda *s: jax.ShapeDtypeStruct(s, F32)
    per_dir_shapes = [f32(B, L, D_A), f32(B, L, D_A), f32(B, nc, H_A, RW_HEAD, RW_HEAD),
                      f32(B, nc, H_A, RW_HEAD, RW_HEAD), f32(B, nc, 1, D_A)]
    per_dir_specs = [tile, tile, mats, mats, pcs]
    outs = pl.pallas_call(
        _chunk_kernel, grid=(B, nj), in_specs=[tile] * 9,
        out_specs=per_dir_specs * 2, out_shape=per_dir_shapes * 2,
        compiler_params=_cparams(2, 40 * 1024 * 1024), name="rwkv_chunk",
    )(nkk, r, v, lwf, kaf, kdf, lwb, kab, kdb)
    rpf, ypf, ff, gf, pcf, rpb, ypb, fb, gb, pcb = outs

    rrow = lambda b, j: (b, nj - 1 - j, 0)
    rtile = pl.BlockSpec((1, TL, D_A), rrow)
    rmats = pl.BlockSpec((1, cpb, H_A, RW_HEAD, RW_HEAD), lambda b, j: (b, nj - 1 - j, 0, 0, 0))
    rpcs = pl.BlockSpec((1, cpb, 1, D_A), lambda b, j: (b, nj - 1 - j, 0, 0))
    st = pl.BlockSpec((1, 2, H_A, RW_HEAD, RW_HEAD), lambda b, j: (b, 0, 0, 0, 0))
    return pl.pallas_call(
        _state_kernel, grid=(B, nj),
        in_specs=[tile, tile, mats, mats, pcs, rtile, rtile, rmats, rmats, rpcs, st],
        out_specs=[tile, rtile, st],
        out_shape=[f32(B, L, D_A), f32(B, L, D_A), f32(B, 2, H_A, RW_HEAD, RW_HEAD)],
        scratch_shapes=[pltpu.VMEM((2, H_A, RW_HEAD, RW_HEAD), F32)],
        compiler_params=_cparams(2, 40 * 1024 * 1024), name="rwkv_state",
    )(rpf, ypf, ff, gf, pcf, rpb, ypb, fb, gb, pcb, s0)


def _attn_kernel(*refs, has_cache, lam_init):
    if has_cache:
        q_ref, k_ref, v_ref, ck_ref, cv_ref, lq1, lk1, lq2, lk2, sub_ref, o_ref = refs
    else:
        q_ref, k_ref, v_ref, lq1, lk1, lq2, lk2, sub_ref, o_ref = refs
    l1 = jnp.sum(lq1[...] * lk1[...], axis=-1, keepdims=True)
    l2 = jnp.sum(lq2[...] * lk2[...], axis=-1, keepdims=True)
    lam = jnp.exp(l1) - jnp.exp(l2) + lam_init
    q = q_ref[0].astype(BF16)
    kn = k_ref[0].astype(BF16)
    vn = v_ref[0].astype(BF16)
    if has_cache:
        kc = ck_ref[0, 0].astype(BF16)
        vc = cv_ref[0, 0].astype(BF16)
    dn = (((1,), (1,)), ((), ()))
    for h in range(H_B):
        w_new = None
        w_old = None
        for m in range(2):
            c0 = (h * 2 + m) * DIFF_DH
            qh = q[:, c0:c0 + DIFF_DH]
            s_n = lax.dot_general(qh, kn[:, c0:c0 + DIFF_DH], dn, preferred_element_type=F32)
            mx = jnp.max(s_n, axis=-1, keepdims=True)
            if has_cache:
                s_c = lax.dot_general(qh, kc[:, c0:c0 + DIFF_DH], dn, preferred_element_type=F32)
                mx = jnp.maximum(mx, jnp.max(s_c, axis=-1, keepdims=True))
            e_n = jnp.exp(s_n - mx)
            den = jnp.sum(e_n, axis=-1, keepdims=True)
            if has_cache:
                e_c = jnp.exp(s_c - mx)
                den = den + jnp.sum(e_c, axis=-1, keepdims=True)
            scale = (1.0 / den) if m == 0 else (-lam / den)
            w_new = e_n * scale if m == 0 else w_new + e_n * scale
            if has_cache:
                w_old = e_c * scale if m == 0 else w_old + e_c * scale
        vh = vn[:, h * 128:(h + 1) * 128]
        o = jnp.dot(w_new.astype(BF16), vh, preferred_element_type=F32)
        if has_cache:
            o = o + jnp.dot(w_old.astype(BF16), vc[:, h * 128:(h + 1) * 128], preferred_element_type=F32)
        o_ref[0, :, h * 128:(h + 1) * 128] = _rms(o, sub_ref[...]) * (1.0 - lam_init)


def _attention(q, k, v, cache, l, p, lam_init):
    B, L, _ = q.shape
    has_cache = cache is not None
    tq = TL
    in_specs = [pl.BlockSpec((1, tq, D_B), lambda b, i: (b, i, 0)),
                pl.BlockSpec((1, L, D_B), lambda b, i: (b, 0, 0)),
                pl.BlockSpec((1, L, D_B), lambda b, i: (b, 0, 0))]
    args = [q, k, v]
    if has_cache:
        ck, cv = cache
        past = ck.shape[2]
        in_specs += [pl.BlockSpec((1, 1, past, D_B), lambda b, i: (b, l, 0, 0))] * 2
        args += [ck, cv]
    small = lambda n: pl.BlockSpec((1, n), lambda b, i: (0, 0))
    in_specs += [small(DIFF_DH)] * 4 + [small(128)]
    args += [p['diff_lq1'].reshape(1, -1), p['diff_lk1'].reshape(1, -1), p['diff_lq2'].reshape(1, -1),
             p['diff_lk2'].reshape(1, -1), p['diff_subln'].reshape(1, -1)]
    return pl.pallas_call(
        functools.partial(_attn_kernel, has_cache=has_cache, lam_init=lam_init),
        grid=(B, L // tq), in_specs=in_specs,
        out_specs=pl.BlockSpec((1, tq, D_B), lambda b, i: (b, i, 0)),
        out_shape=jax.ShapeDtypeStruct((B, L, D_B), F32),
        compiler_params=_cparams(2, 48 * 1024 * 1024), name="diff_attention",
    )(*args)


def _hfilt_time_kernel(emb_ref, w1_ref, b1_ref, fr_ref, w2_ref, b2_ref, w3_ref, dec_ref, h_ref, ss_ref, acc_ref):
    i = pl.program_id(0)
    emb = emb_ref[...]
    fr = fr_ref[...]
    h = jnp.sin(fr * (jnp.dot(emb.astype(BF16), w1_ref[...].astype(BF16), preferred_element_type=F32) + b1_ref[...]))
    h = jnp.sin(fr * (jnp.dot(h.astype(BF16), w2_ref[...].astype(BF16), preferred_element_type=F32) + b2_ref[...]))
    h = jnp.dot(h.astype(BF16), w3_ref[...].astype(BF16), preferred_element_type=F32)
    h = h * jnp.exp(-emb[:, 0:1] * jnp.abs(dec_ref[...]))
    n = h.shape[0]
    row = lax.broadcasted_iota(jnp.int32, (n, 4 * D_C), 0) + i * n
    col = lax.broadcasted_iota(jnp.int32, (n, 4 * D_C), 1)
    is_bwd = ((col // D_C) % 2) == 1
    h = jnp.where(jnp.logical_and(is_bwd, row == 0), 0.0, h)
    h_ref[...] = h

    @pl.when(i == 0)
    def _():
        acc_ref[...] = jnp.zeros_like(acc_ref)

    acc_ref[...] += jnp.sum(h * h, axis=0, keepdims=True)
    ss_ref[...] = acc_ref[...]


def _hfilt_freq_kernel(h_ref, ss_ref, ch_ref, cl_ref, sh_ref, sl_ref, hr_ref, hi_ref,
                       ah_ref, al_ref, bh_ref, bl_ref, nyq_ref):
    i = pl.program_id(0)

    @pl.when(i == 0)
    def _():
        for o in range(2):
            hf = h_ref[:, o * 2 * D_C:o * 2 * D_C + D_C]
            hb = h_ref[:, o * 2 * D_C + D_C:(o + 1) * 2 * D_C]
            a = hf + hb
            b = hb - hf
            n = a.shape[0]
            alt = 1.0 - 2.0 * (lax.broadcasted_iota(jnp.int32, (n, 1), 0) % 2).astype(F32)
            nyq_ref[:, o * D_C:(o + 1) * D_C] = jnp.sum(a * alt, axis=0, keepdims=True)
            ahh, all_ = _split2(a)
            bhh, bll = _split2(b)
            ah_ref[:, o * D_C:(o + 1) * D_C] = ahh
            al_ref[:, o * D_C:(o + 1) * D_C] = all_
            bh_ref[:, o * D_C:(o + 1) * D_C] = bhh
            bl_ref[:, o * D_C:(o + 1) * D_C] = bll

    ss = ss_ref[...]
    tot = jnp.concatenate([ss[:, 0:D_C] + ss[:, D_C:2 * D_C], ss[:, 2 * D_C:3 * D_C] + ss[:, 3 * D_C:4 * D_C]], axis=1)
    scale = lax.rsqrt(tot + 1e-6)

    def dft(mh, ml, xh, xl):
        return (jnp.dot(mh, xh, preferred_element_type=F32) + jnp.dot(mh, xl, preferred_element_type=F32)
                + jnp.dot(ml, xh, preferred_element_type=F32))

    hr_ref[...] = dft(ch_ref[...], cl_ref[...], ah_ref[...], al_ref[...]) * scale
    hi = dft(sh_ref[...], sl_ref[...], bh_ref[...], bl_ref[...]) * scale
    tf = hi.shape[0]
    row = lax.broadcasted_iota(jnp.int32, (tf, 1), 0) + i * tf
    hi_ref[...] = jnp.where(row == 0, nyq_ref[...] * scale, hi)


def _hyena_filters(L, p, tabs):
    emb, ch, cl, sh, sl = tabs
    tr = 256
    w1 = jnp.pad(p['hy_w1'], ((0, 128 - HY_EMB), (0, 0)))
    full = lambda a: pl.BlockSpec(a.shape, lambda i: (0,) * a.ndim)
    params = [w1, p['hy_b1'].reshape(1, -1), p['hy_freq'].reshape(1, -1), p['hy_w2'], p['hy_b2'].reshape(1, -1),
              p['hy_w3'], jnp.tile(p['hy_decay'], 4).reshape(1, -1)]
    h_raw, ss = pl.pallas_call(
        _hfilt_time_kernel, grid=(L // tr,),
        in_specs=[pl.BlockSpec((tr, 128), lambda i: (i, 0))] + [full(a) for a in params],
        out_specs=[pl.BlockSpec((tr, 4 * D_C), lambda i: (i, 0)), pl.BlockSpec((1, 4 * D_C), lambda i: (0, 0))],
        out_shape=[jax.ShapeDtypeStruct((L, 4 * D_C), F32), jax.ShapeDtypeStruct((1, 4 * D_C), F32)],
        scratch_shapes=[pltpu.VMEM((1, 4 * D_C), F32)],
        compiler_params=_cparams(1), name="hyena_filter_time",
    )(emb, *params)
    tf = 256
    mat = pl.BlockSpec((tf, L), lambda i: (i, 0))
    return pl.pallas_call(
        _hfilt_freq_kernel, grid=(L // tf,),
        in_specs=[pl.BlockSpec((L, 4 * D_C), lambda i: (0, 0)), pl.BlockSpec((1, 4 * D_C), lambda i: (0, 0)),
                  mat, mat, mat, mat],
        out_specs=[pl.BlockSpec((tf, 2 * D_C), lambda i: (i, 0))] * 2,
        out_shape=[jax.ShapeDtypeStruct((L, 2 * D_C), F32)] * 2,
        scratch_shapes=[pltpu.VMEM((L, 2 * D_C), BF16)] * 4 + [pltpu.VMEM((1, 2 * D_C), F32)],
        compiler_params=_cparams(1, 48 * 1024 * 1024), name="hyena_filter_freq",
    )(h_raw, ss, ch, cl, sh, sl)


def _hconv_kernel(z_ref, zt_ref, gate_ref, hr_ref, hi_ref, bias_ref, ch_ref, cl_ref, sh_ref, sl_ref, o_ref,
                  zh_ref, zl_ref, yrh_ref, yrl_ref, yih_ref, yil_ref, nyq_ref, *, BG, L):
    ph = pl.program_id(1)
    i = pl.program_id(2)
    inv_n = 1.0 / (2 * L)

    @pl.when(jnp.logical_and(ph == 0, i == 0))
    def _():
        alt = 1.0 - 2.0 * (lax.broadcasted_iota(jnp.int32, (L, 1), 0) % 2).astype(F32)
        for b in range(BG):
            z = z_ref[b]
            zh, zl = _split2(z)
            zh_ref[:, b * D_C:(b + 1) * D_C] = zh
            zl_ref[:, b * D_C:(b + 1) * D_C] = zl
            nyq_ref[:, b * D_C:(b + 1) * D_C] = jnp.sum(z * alt, axis=0, keepdims=True)

    def dft(mh, ml, xh, xl):
        return (jnp.dot(mh, xh, preferred_element_type=F32) + jnp.dot(mh, xl, preferred_element_type=F32)
                + jnp.dot(ml, xh, preferred_element_type=F32))

    tf = ch_ref.shape[0]

    @pl.when(ph == 0)
    def _():
        zr = dft(ch_ref[...], cl_ref[...], zh_ref[...], zl_ref[...])
        zs = dft(sh_ref[...], sl_ref[...], zh_ref[...], zl_ref[...])
        hr = hr_ref[...]
        hi = hi_ref[...]
        row = lax.broadcasted_iota(jnp.int32, (tf, 1), 0) + i * tf
        wgt = jnp.where(row == 0, inv_n, 2.0 * inv_n)
        for b in range(BG):
            sl = slice(b * D_C, (b + 1) * D_C)
            yr = (zr[:, sl] * hr + zs[:, sl] * hi) * wgt
            yi = (zr[:, sl] * hi - zs[:, sl] * hr) * (2.0 * inv_n)
            yrh, yrl = _split2(yr)
            yih, yil = _split2(yi)
            rows = pl.ds(pl.multiple_of(i * tf, tf), tf)
            yrh_ref[rows, sl] = yrh
            yrl_ref[rows, sl] = yrl
            yih_ref[rows, sl] = yih
            yil_ref[rows, sl] = yil

        @pl.when(i == 0)
        def _():
            for b in range(BG):
                sl = slice(b * D_C, (b + 1) * D_C)
                nyq_ref[:, sl] = nyq_ref[:, sl] * hi[0:1, :] * inv_n

    @pl.when(ph == 1)
    def _():
        y = (dft(ch_ref[...], cl_ref[...], yrh_ref[...], yrl_ref[...])
             - dft(sh_ref[...], sl_ref[...], yih_ref[...], yil_ref[...]))
        row = lax.broadcasted_iota(jnp.int32, (tf, 1), 0) + i * tf
        alt = 1.0 - 2.0 * (row % 2).astype(F32)
        y = y + alt * nyq_ref[...]
        for b in range(BG):
            sl = slice(b * D_C, (b + 1) * D_C)
            o_ref[b] = gate_ref[b] * (y[:, sl] + bias_ref[...] * zt_ref[b])


def _hconv(z, gate, hr, hi, bias, order, tabs):
    B, L, _ = z.shape
    _, ch, cl, sh, sl = tabs
    BG = min(B, 8) if L <= 256 else 2
    tf = 256
    nt = L // tf
    mat = pl.BlockSpec((tf, L), lambda g, ph, i: (i, 0))
    tile = pl.BlockSpec((BG, tf, D_C), lambda g, ph, i: (g, i * ph, 0))
    hspec = pl.BlockSpec((tf, D_C), lambda g, ph, i: (i * (1 - ph), order))
    cols = BG * D_C
    return pl.pallas_call(
        functools.partial(_hconv_kernel, BG=BG, L=L), grid=(B // BG, 2, nt),
        in_specs=[pl.BlockSpec((BG, L, D_C), lambda g, ph, i: (g, 0, 0)), tile, tile, hspec, hspec,
                  pl.BlockSpec((1, D_C), lambda g, ph, i: (0, 0)), mat, mat, mat, mat],
        out_specs=tile,
        out_shape=jax.ShapeDtypeStruct((B, L, D_C), F32),
        scratch_shapes=[pltpu.VMEM((L, cols), BF16)] * 6 + [pltpu.VMEM((1, cols), F32)],
        compiler_params=_cparams(3, 48 * 1024 * 1024), name="hyena_conv",
    )(z, z, gate, hr, hi, bias[order].reshape(1, D_C), ch, cl, sh, sl)


def _outproj_kernel(x_ref, yf_ref, yb_ref, bon_ref, g_ref, ob_ref, yc_ref, mod_ref,
                    lnw_ref, lnb_ref, gpost_ref, gffn_ref, w_ref, x1_ref, h2_ref):
    bd = _block_diag(D_A, RW_HEAD)
    y = yf_ref[0] + yb_ref[0]
    mu = _segsum(y, bd) * (1.0 / RW_HEAD)
    yc = y - mu
    var = _segsum(yc * yc, bd) * (1.0 / RW_HEAD)
    yn = yc * lax.rsqrt(var + GN_EPS) * lnw_ref[...] + lnb_ref[...]
    ya = (yn + bon_ref[0]) * g_ref[0]
    mix = (jnp.dot(ya.astype(BF16), w_ref[0:D_A, :], preferred_element_type=F32)
           + jnp.dot(ob_ref[0].astype(BF16), w_ref[D_A:D_A + D_B, :], preferred_element_type=F32)
           + jnp.dot(yc_ref[0].astype(BF16), w_ref[D_A + D_B:D_MODEL, :], preferred_element_type=F32))
    mod = mod_ref[0]
    gt1 = mod[:, 2 * D_MODEL:3 * D_MODEL]
    sh2 = mod[:, 3 * D_MODEL:4 * D_MODEL]
    sc2 = mod[:, 4 * D_MODEL:5 * D_MODEL]
    x1 = x_ref[0] + gt1 * _rms(mix, gpost_ref[...])
    x1_ref[0] = x1
    h2_ref[0] = (_rms(x1, gffn_ref[...]) * (1.0 + sc2) + sh2).astype(BF16)


def _outproj(x, yf, yb, bon, g, ob, yc, mod, p, w_out):
    B, L, _ = x.shape
    mb = mod.shape[0]
    row = lambda b, i: (b, i, 0)
    c2 = lambda b, i: (0, 0)
    t = lambda n: pl.BlockSpec((1, TL, n), row)
    in_specs = [t(D_MODEL), t(D_A), t(D_A), t(D_A), t(D_A), t(D_B), t(D_C),
                pl.BlockSpec((1, 1, 6 * D_MODEL), (lambda b, i: (b, 0, 0)) if mb > 1 else (lambda b, i: (0, 0, 0))),
                pl.BlockSpec((1, D_A), c2), pl.BlockSpec((1, D_A), c2),
                pl.BlockSpec((1, D_MODEL), c2), pl.BlockSpec((1, D_MODEL), c2),
                pl.BlockSpec((D_MODEL, D_MODEL), c2)]
    return pl.pallas_call(
        _outproj_kernel, grid=(B, L // TL), in_specs=in_specs,
        out_specs=[t(D_MODEL), t(D_MODEL)],
        out_shape=[jax.ShapeDtypeStruct((B, L, D_MODEL), F32), jax.ShapeDtypeStruct((B, L, D_MODEL), BF16)],
        compiler_params=_cparams(2, 40 * 1024 * 1024), name="outproj",
    )(x, yf, yb, bon, g, ob, yc, mod, p['rwkv_ln_w'].reshape(1, -1), p['rwkv_ln_b'].reshape(1, -1),
      p['g_mix_post'].reshape(1, -1), p['g_ffn_pre'].reshape(1, -1), w_out)


def _ffn_kernel(h_ref, x_ref, mod_ref, g_ref, w1_ref, w2_ref, o_ref):
    h = h_ref[0]
    acc = None
    ck = 1024
    for c in range(D_FF // ck):
        a = jnp.dot(h, w1_ref[:, c * ck:(c + 1) * ck], preferred_element_type=F32)
        a = jnp.square(jnp.maximum(a, 0.0)).astype(BF16)
        part = jnp.dot(a, w2_ref[c * ck:(c + 1) * ck, :], preferred_element_type=F32)
        acc = part if acc is None else acc + part
    gt2 = mod_ref[0][:, 5 * D_MODEL:6 * D_MODEL]
    o_ref[0] = x_ref[0] + gt2 * _rms(acc, g_ref[...])


def _ffn(h2, x1, mod, g_post, w1, w2):
    B, L, _ = x1.shape
    mb = mod.shape[0]
    row = lambda b, i: (b, i, 0)
    c2 = lambda b, i: (0, 0)
    return pl.pallas_call(
        _ffn_kernel, grid=(B, L // TL),
        in_specs=[pl.BlockSpec((1, TL, D_MODEL), row), pl.BlockSpec((1, TL, D_MODEL), row),
                  pl.BlockSpec((1, 1, 6 * D_MODEL), (lambda b, i: (b, 0, 0)) if mb > 1 else (lambda b, i: (0, 0, 0))),
                  pl.BlockSpec((1, D_MODEL), c2),
                  pl.BlockSpec((D_MODEL, D_FF), c2), pl.BlockSpec((D_FF, D_MODEL), c2)],
        out_specs=pl.BlockSpec((1, TL, D_MODEL), row),
        out_shape=jax.ShapeDtypeStruct((B, L, D_MODEL), F32),
        compiler_params=_cparams(2, V7X_VMEM_LIMIT), name="ffn",
    )(h2, x1, mod, g_post.reshape(1, -1), w1, w2)


def _rope_tables(L):
    rows = L // GRID_W
    row = jnp.repeat(jnp.arange(rows), GRID_W).astype(F32)
    col = jnp.tile(jnp.arange(GRID_W), rows).astype(F32)
    half = DIFF_DH // 2
    inv = ROPE_BASE ** (-jnp.arange(0, half, 2, dtype=F32) / half)
    ang_r = row[:, None] * inv[None]
    ang_c = col[:, None] * inv[None]
    ang = jnp.concatenate([ang_r, ang_r, ang_c, ang_c], axis=-1)
    cos, sin = jnp.cos(ang), jnp.sin(ang)
    quarter = (jnp.arange(DIFF_DH) // 16) % 2
    sin_a = jnp.where(quarter == 0, -sin, 0.0)
    sin_b = jnp.where(quarter == 1, sin, 0.0)
    rep = D_B // DIFF_DH
    return tuple(jnp.tile(t, (1, rep)) for t in (cos, sin_a, sin_b))


def _hyena_tables(L):
    t = jnp.linspace(0.0, 1.0, L, dtype=F32)[:, None]
    ang = (2.0 * math.pi / L) * jnp.arange(L, dtype=F32)[:, None]
    bands = jnp.linspace(1e-4, HY_BANDS - 1, HY_BANDS, dtype=F32)[None, :]
    emb = jnp.concatenate([t, jnp.cos(bands * ang), -jnp.sin(bands * ang)], axis=-1)
    emb = jnp.pad(emb, ((0, 0), (0, 128 - HY_EMB)))
    n = jnp.arange(L, dtype=jnp.int32)
    k = (n[:, None] * n[None, :]) % (2 * L)
    theta = k.astype(F32) * (math.pi / L)
    c = jnp.cos(theta)
    s = jnp.sin(theta)
    ch, cl = _split2(c)
    sh, sl = _split2(s)
    return emb, ch, cl, sh, sl


def _layer(x, mod, p, wb, l, lam_init, cache, rope_tabs, hy_tabs, filt):
    B, L, _ = x.shape
    ua, uq, uk, uv, uh = _inproj(x, mod, p['g_mix_pre'], wb['w_in'], rope_tabs)
    (r, v, nkk, wf, kaf, kdf, wbk, kab, kdb, g, bon, x1h, x2h, hv) = _prep(ua, uh, p)
    if cache is None:
        s0 = jnp.zeros((B, 2, H_A, RW_HEAD, RW_HEAD), F32)
        kv_cache = None
    else:
        s0 = cache[0]
        kv_cache = (cache[1], cache[2])
    yf, yb, sfin = _scan(nkk, r, v, wf, kaf, kdf, wbk, kab, kdb, s0)
    ob = _attention(uq, uk, uv, kv_cache, l, p, lam_init)
    hr, hi = filt
    z1 = _hconv(hv, x1h, hr, hi, p['hy_bias'], 0, hy_tabs)
    yc = _hconv(z1, x2h, hr, hi, p['hy_bias'], 1, hy_tabs)
    x1, h2 = _outproj(x, yf, yb, bon, g, ob, yc, mod, p, wb['w_out'])
    x2 = _ffn(h2, x1, mod, p['g_ffn_post'], wb['w_ff1'], wb['w_ff2'])
    return x2, (sfin, uk, uv)


_LAYER_KEYS = ('g_mix_pre', 'g_mix_post', 'g_ffn_pre', 'g_ffn_post', 'rwkv_conv', 'rwkv_w0', 'rwkv_w2', 'rwkv_a0',
               'rwkv_a2', 'rwkv_g2', 'rwkv_kk', 'rwkv_ka', 'rwkv_rk', 'rwkv_ln_w', 'rwkv_ln_b', 'diff_lq1', 'diff_lk1',
               'diff_lq2', 'diff_lk2', 'diff_subln', 'hy_conv_w', 'hy_conv_b', 'hy_w1', 'hy_b1', 'hy_freq', 'hy_w2',
               'hy_b2', 'hy_w3', 'hy_decay', 'hy_bias')


def kernel(x_prompt, x_sample, state_rwkv, cache_k, cache_v, c, c_ctx, w_mod, b_mod, g_mix_pre, g_mix_post, g_ffn_pre, g_ffn_post, w_in, rwkv_conv, rwkv_w0, rwkv_w2, rwkv_a0, rwkv_a2, rwkv_g2, rwkv_kk, rwkv_ka, rwkv_rk, rwkv_ln_w, rwkv_ln_b, diff_lq1, diff_lk1, diff_lq2, diff_lk2, diff_subln, hy_conv_w, hy_conv_b, hy_w1, hy_b1, hy_freq, hy_w2, hy_b2, hy_w3, hy_decay, hy_bias, w_out, w_ff1, w_ff2):
    stacked = dict(g_mix_pre=g_mix_pre, g_mix_post=g_mix_post, g_ffn_pre=g_ffn_pre, g_ffn_post=g_ffn_post,
                   rwkv_conv=rwkv_conv, rwkv_w0=rwkv_w0, rwkv_w2=rwkv_w2, rwkv_a0=rwkv_a0, rwkv_a2=rwkv_a2,
                   rwkv_g2=rwkv_g2, rwkv_kk=rwkv_kk, rwkv_ka=rwkv_ka, rwkv_rk=rwkv_rk, rwkv_ln_w=rwkv_ln_w,
                   rwkv_ln_b=rwkv_ln_b, diff_lq1=diff_lq1, diff_lk1=diff_lk1, diff_lq2=diff_lq2, diff_lk2=diff_lk2,
                   diff_subln=diff_subln, hy_conv_w=hy_conv_w, hy_conv_b=hy_conv_b, hy_w1=hy_w1, hy_b1=hy_b1,
                   hy_freq=hy_freq, hy_w2=hy_w2, hy_b2=hy_b2, hy_w3=hy_w3, hy_decay=hy_decay, hy_bias=hy_bias)
    Bc, Lc, _ = x_prompt.shape
    Bs, Ls, _ = x_sample.shape
    past = cache_k.shape[2]

    cond = jnp.zeros((8, D_MODEL), F32).at[0:Bs].set(c).at[Bs].set(c_ctx)
    mod_all = _modulation(cond, w_mod, b_mod)

    rope_tabs = _rope_tables(Ls)
    tabs_c = _hyena_tables(Lc)
    tabs_s = _hyena_tables(Ls)
    ck = cache_k.reshape(Bs, DEPTH, past, D_B)
    cv = cache_v.reshape(Bs, DEPTH, past, D_B)

    xp, xs = x_prompt, x_sample
    st_list, k_list, v_list = [], [], []
    for l in range(DEPTH):
        p = {k: stacked[k][l] for k in _LAYER_KEYS}
        wb = dict(w_in=w_in[l].astype(BF16), w_out=w_out[l].astype(BF16),
                  w_ff1=w_ff1[l].astype(BF16), w_ff2=w_ff2[l].astype(BF16))
        lam_init = 0.8 - 0.6 * math.exp(-0.3 * l)
        mod_lat = mod_all[l, 0:Bs].reshape(Bs, 1, 6 * D_MODEL)
        mod_ctx = mod_all[l, Bs:Bs + 1].reshape(1, 1, 6 * D_MODEL)
        filt_c = _hyena_filters(Lc, p, tabs_c)
        filt_s = _hyena_filters(Ls, p, tabs_s)
        xp, (s_ctx, k_ctx, v_ctx) = _layer(xp, mod_ctx, p, wb, l, lam_init, None, None, tabs_c, filt_c)
        st_list.append(s_ctx)
        k_list.append(k_ctx.reshape(Bc, Lc, H_B, 2, DIFF_DH))
        v_list.append(v_ctx.reshape(Bc, Lc, H_B, 2 * DIFF_DH))
        s0 = state_rwkv[:, l]
        xs, _ = _layer(xs, mod_lat, p, wb, l, lam_init, (s0, ck, cv), rope_tabs, tabs_s, filt_s)
    return (xp, xs, jnp.stack(st_list, axis=1), jnp.stack(k_list, axis=1), jnp.stack(v_list, axis=1))
```

```python
import functools
import math

import jax
import jax.numpy as jnp
from jax import lax
from jax.experimental import pallas as pl
from jax.experimental.pallas import tpu as pltpu

F32 = jnp.float32
BF16 = jnp.bfloat16

D_MODEL = 1024
DEPTH = 4
GRID_W = 64
D_A = 256
RW_HEAD = 64
H_A = 4
D_B = 512
DIFF_DH = 64
H_B = 4
D_C = 256
HY_BANDS = 16
HY_EMB = 33
HY_FFN = 64
D_FF = 4096
ROPE_BASE = 10000.0
RMS_EPS = 1e-6
GN_EPS = 64e-5
N_A_COLS = 1152
IN_COLS = 3456

TL = 256
V7X_VMEM_LIMIT = 56 * 1024 * 1024


def _cparams(n_axes, vmem=None):
    return pltpu.CompilerParams(dimension_semantics=("arbitrary",) * n_axes,
                                vmem_limit_bytes=vmem)


def _split3(a):
    hi = a.astype(BF16)
    r1 = a - hi.astype(F32)
    mid = r1.astype(BF16)
    lo = (r1 - mid.astype(F32)).astype(BF16)
    return hi, mid, lo


def _split2(a):
    hi = a.astype(BF16)
    lo = (a - hi.astype(F32)).astype(BF16)
    return hi, lo


def _segsum(a, bd):
    hi, mid, lo = _split3(a)
    return (jnp.dot(hi, bd, preferred_element_type=F32) + jnp.dot(mid, bd, preferred_element_type=F32)
            + jnp.dot(lo, bd, preferred_element_type=F32))


def _block_diag(n, seg):
    r = lax.broadcasted_iota(jnp.int32, (n, n), 0) // seg
    c = lax.broadcasted_iota(jnp.int32, (n, n), 1) // seg
    return (r == c).astype(BF16)


def _rms(x, g):
    return x * lax.rsqrt(jnp.mean(x * x, axis=-1, keepdims=True) + RMS_EPS) * g


def _mod_kernel(cond_ref, w_ref, b_ref, o_ref):
    c = cond_ref[...]
    s = (c * jax.nn.sigmoid(c)).astype(BF16)
    o_ref[0] = jnp.dot(s, w_ref[0].astype(BF16), preferred_element_type=F32) + b_ref[0]


def _modulation(cond, w_mod, b_mod):
    tn = 1536
    return pl.pallas_call(
        _mod_kernel,
        grid=(DEPTH, 6 * D_MODEL // tn),
        in_specs=[pl.BlockSpec((8, D_MODEL), lambda l, n: (0, 0)),
                  pl.BlockSpec((1, D_MODEL, tn), lambda l, n: (l, 0, n)),
                  pl.BlockSpec((1, 1, tn), lambda l, n: (l, 0, n))],
        out_specs=pl.BlockSpec((1, 8, tn), lambda l, n: (l, 0, n)),
        out_shape=jax.ShapeDtypeStruct((DEPTH, 8, 6 * D_MODEL), F32),
        compiler_params=_cparams(2, 40 * 1024 * 1024),
        name="modulation",
    )(cond, w_mod, b_mod.reshape(DEPTH, 1, 6 * D_MODEL))


def _rope(x, cos, sin_a, sin_b):
    n = x.shape[-1]
    return x * cos + pltpu.roll(x, n - 16, axis=1) * sin_a + pltpu.roll(x, 16, axis=1) * sin_b


def _inproj_kernel(*refs, rope):
    if rope:
        x_ref, mod_ref, g_ref, w_ref, cos_ref, sa_ref, sb_ref, ua_ref, uq_ref, uk_ref, uv_ref, uh_ref = refs
    else:
        x_ref, mod_ref, g_ref, w_ref, ua_ref, uq_ref, uk_ref, uv_ref, uh_ref = refs
    x = x_ref[0]
    mod = mod_ref[0]
    sh1 = mod[:, 0:D_MODEL]
    sc1 = mod[:, D_MODEL:2 * D_MODEL]
    h = _rms(x, g_ref[...]) * (1.0 + sc1) + sh1
    u = jnp.dot(h.astype(BF16), w_ref[...], preferred_element_type=F32)
    ua_ref[0] = u[:, 0:N_A_COLS]
    q = u[:, N_A_COLS:N_A_COLS + D_B] * (DIFF_DH ** -0.5)
    k = u[:, N_A_COLS + D_B:N_A_COLS + 2 * D_B]
    if rope:
        q = _rope(q, cos_ref[...], sa_ref[...], sb_ref[...])
        k = _rope(k, cos_ref[...], sa_ref[...], sb_ref[...])
    uq_ref[0] = q.astype(uq_ref.dtype)
    uk_ref[0] = k.astype(uk_ref.dtype)
    uv_ref[0] = u[:, N_A_COLS + 2 * D_B:N_A_COLS + 3 * D_B].astype(uv_ref.dtype)
    uh_ref[0] = u[:, N_A_COLS + 3 * D_B:IN_COLS]


def _inproj(x, mod, g_pre, w_in, rope_tabs):
    B, L, _ = x.shape
    rope = rope_tabs is not None
    mb = mod.shape[0]
    row = lambda b, i: (b, i, 0)
    const2 = lambda b, i: (0, 0)
    in_specs = [pl.BlockSpec((1, TL, D_MODEL), row),
                pl.BlockSpec((1, 1, 6 * D_MODEL), (lambda b, i: (b, 0, 0)) if mb > 1 else (lambda b, i: (0, 0, 0))),
                pl.BlockSpec((1, D_MODEL), const2),
                pl.BlockSpec((D_MODEL, IN_COLS), const2)]
    args = [x, mod, g_pre.reshape(1, D_MODEL), w_in]
    if rope:
        in_specs += [pl.BlockSpec((TL, D_B), lambda b, i: (i, 0))] * 3
        args += list(rope_tabs)
    qkv_dt = BF16 if rope else F32
    out_shape = [jax.ShapeDtypeStruct((B, L, N_A_COLS), F32),
                 jax.ShapeDtypeStruct((B, L, D_B), qkv_dt),
                 jax.ShapeDtypeStruct((B, L, D_B), qkv_dt),
                 jax.ShapeDtypeStruct((B, L, D_B), qkv_dt),
                 jax.ShapeDtypeStruct((B, L, 3 * D_C), F32)]
    out_specs = [pl.BlockSpec((1, TL, N_A_COLS), row), pl.BlockSpec((1, TL, D_B), row),
                 pl.BlockSpec((1, TL, D_B), row), pl.BlockSpec((1, TL, D_B), row),
                 pl.BlockSpec((1, TL, 3 * D_C), row)]
    return pl.pallas_call(
        functools.partial(_inproj_kernel, rope=rope),
        grid=(B, L // TL), in_specs=in_specs, out_specs=out_specs, out_shape=out_shape,
        compiler_params=_cparams(2, 48 * 1024 * 1024), name="inproj",
    )(*args)


def _shift_conv(u, prev_row, next_row, cw):
    n = u.shape[0]
    row = lax.broadcasted_iota(jnp.int32, (n, 1), 0)
    up = jnp.where(row == 0, prev_row, pltpu.roll(u, 1, axis=0))
    un = jnp.where(row == n - 1, next_row, pltpu.roll(u, n - 1, axis=0))
    return cw[0:1] * up + cw[1:2] * u + cw[2:3] * un


def _prep_kernel(ua_ref, uap_ref, uan_ref, uh_ref, uhp_ref, uhn_ref,
                 cw_ref, w0_ref, w2_ref, a0_ref, a2_ref, g2_ref, kkw_ref, kaw_ref, rk_ref, hcw_ref, hcb_ref,
                 r_ref, v_ref, nkk_ref, wf_ref, kaf_ref, kdf_ref, wb_ref, kab_ref, kdb_ref, g_ref, bon_ref,
                 x1_ref, x2_ref, hv_ref):
    i = pl.program_id(1)
    nt = pl.num_programs(1)
    has_prev = i > 0
    has_next = i < nt - 1
    bd = _block_diag(D_A, RW_HEAD)

    ua = ua_ref[0]
    u_rkv = ua[:, 0:3 * D_A]
    prev = jnp.where(has_prev, uap_ref[0, 7:8, :], 0.0)
    nxt = jnp.where(has_next, uan_ref[0, 0:1, :], 0.0)
    rkv = _shift_conv(u_rkv, prev, nxt, cw_ref[...])
    r = rkv[:, 0:D_A]
    k = rkv[:, D_A:2 * D_A]
    v = rkv[:, 2 * D_A:3 * D_A]
    u_w = ua[:, 768:896]
    u_a = ua[:, 896:1024]
    u_g = ua[:, 1024:1152]

    g = jnp.dot(jax.nn.sigmoid(u_g).astype(BF16), g2_ref[...].astype(BF16), preferred_element_type=F32)
    kk = k * kkw_ref[...]
    kk = kk * lax.rsqrt(_segsum(kk * kk, bd) + 1e-12)
    r_ref[0] = r
    v_ref[0] = v
    nkk_ref[0] = -kk
    g_ref[0] = g

    kd_sum = None
    outs = ((wf_ref, kaf_ref, kdf_ref), (wb_ref, kab_ref, kdb_ref))
    for d in range(2):
        uw_d = u_w[:, d * 64:(d + 1) * 64]
        ua_d = u_a[:, d * 64:(d + 1) * 64]
        xw = w0_ref[d:d + 1, :] + jnp.dot(jnp.tanh(uw_d).astype(BF16), w2_ref[d].astype(BF16),
                                          preferred_element_type=F32)
        z = -xw
        softplus = jnp.maximum(z, 0.0) + jnp.log(1.0 + jnp.exp(-jnp.abs(z)))
        wl = -softplus - 0.5
        log_decay = -jnp.exp(wl)
        a = jax.nn.sigmoid(a0_ref[d:d + 1, :] + jnp.dot(ua_d.astype(BF16), a2_ref[d].astype(BF16),
                                                        preferred_element_type=F32))
        kd = k * (1.0 + (a - 1.0) * kaw_ref[...])
        w_o, ka_o, kd_o = outs[d]
        w_o[0] = log_decay
        ka_o[0] = kk * a
        kd_o[0] = kd
        kd_sum = kd if kd_sum is None else kd_sum + kd
    bon_ref[0] = _segsum(r * kd_sum * rk_ref[...], bd) * v

    uh = uh_ref[0]
    prevh = jnp.where(has_prev, uhp_ref[0, 7:8, :], 0.0)
    nxth = jnp.where(has_next, uhn_ref[0, 0:1, :], 0.0)
    hc = _shift_conv(uh, prevh, nxth, hcw_ref[...]) + hcb_ref[...]
    x1_ref[0] = hc[:, 0:D_C]
    x2_ref[0] = hc[:, D_C:2 * D_C]
    hv_ref[0] = hc[:, 2 * D_C:3 * D_C]


def _prep(ua, uh, p):
    B, L, _ = ua.shape
    nb8 = L // 8
    t8 = TL // 8
    row = lambda b, i: (b, i, 0)
    prev = lambda b, i: (b, jnp.maximum(i * t8 - 1, 0), 0)
    nxt = lambda b, i: (b, jnp.minimum((i + 1) * t8, nb8 - 1), 0)

    def full(a):
        nd = a.ndim
        return pl.BlockSpec(a.shape, lambda b, i: (0,) * nd)

    params = [p['rwkv_conv'], p['rwkv_w0'], p['rwkv_w2'], p['rwkv_a0'], p['rwkv_a2'], p['rwkv_g2'],
              p['rwkv_kk'].reshape(1, D_A), p['rwkv_ka'].reshape(1, D_A), p['rwkv_rk'].reshape(1, D_A),
              p['hy_conv_w'], p['hy_conv_b'].reshape(1, 3 * D_C)]
    in_specs = [pl.BlockSpec((1, TL, N_A_COLS), row), pl.BlockSpec((1, 8, 3 * D_A), prev),
                pl.BlockSpec((1, 8, 3 * D_A), nxt),
                pl.BlockSpec((1, TL, 3 * D_C), row), pl.BlockSpec((1, 8, 3 * D_C), prev),
                pl.BlockSpec((1, 8, 3 * D_C), nxt)] + [full(a) for a in params]
    n_out = 14
    return pl.pallas_call(
        _prep_kernel, grid=(B, L // TL), in_specs=in_specs,
        out_specs=[pl.BlockSpec((1, TL, D_A), row)] * n_out,
        out_shape=[jax.ShapeDtypeStruct((B, L, D_A), F32)] * n_out,
        compiler_params=_cparams(2, 40 * 1024 * 1024), name="prep",
    )(ua, ua, ua, uh, uh, uh, *params)


CHUNK = 64
CHUNKS_PER_ITER = 2
_NN = (((1,), (0,)), ((), ()))
_NT = (((1,), (1,)), ((), ()))
_TN = (((0,), (0,)), ((), ()))


def _mm3(a, b, dims):
    ah, al = _split2(a)
    bh, bl = _split2(b)
    f = lambda x, y: lax.dot_general(x, y, dims, preferred_element_type=F32)
    if dims is _TN:
        return f(ah, bh) + f(ah, bl) + f(al, bh)
    m = a.shape[0]
    both = f(jnp.concatenate([ah, al], axis=0), bh)
    return both[0:m] + both[m:2 * m] + f(ah, bl)


def _mm2(a, b, dims):
    ah = a.astype(BF16)
    bh, bl = _split2(b)
    f = lambda x, y: lax.dot_general(x, y, dims, preferred_element_type=F32)
    return f(ah, bh) + f(ah, bl)


def _mm1(a, b, dims):
    return lax.dot_general(a.astype(BF16), b.astype(BF16), dims, preferred_element_type=F32)


def _unit_tri_inverses(ms, row, col):
    eye = (row == col).astype(F32)
    blk = lambda n: (row // n) == (col // n)
    m8 = [jnp.where(blk(8), m, 0.0) for m in ms]
    a2 = [_mm1(x, x, _NN) for x in m8]
    a4 = [_mm1(x, x, _NN) for x in a2]
    p = [eye + x + y + _mm1(x, y, _NN) for x, y in zip(m8, a2)]
    ts = [x + _mm1(x, y, _NN) for x, y in zip(p, a4)]
    n = 8
    while n < CHUNK:
        sel = jnp.logical_and(blk(2 * n), jnp.logical_not(blk(n)))
        ot = [_mm1(jnp.where(sel, m, 0.0), t, _NN) for m, t in zip(ms, ts)]
        ts = [t + _mm1(t, o, _NN) for t, o in zip(ts, ot)]
        n *= 2
    return ts


def _chunk_kernel(nkk_ref, r_ref, v_ref, lwf_ref, kaf_ref, kdf_ref, lwb_ref, kab_ref, kdb_ref,
                  rpf_ref, ypf_ref, ff_ref, gf_ref, pcf_ref, rpb_ref, ypb_ref, fb_ref, gb_ref, pcb_ref):
    C = CHUNK
    row = lax.broadcasted_iota(jnp.int32, (C, C), 0)
    col = lax.broadcasted_iota(jnp.int32, (C, C), 1)
    dirs = ((lwf_ref, kaf_ref, kdf_ref, rpf_ref, ypf_ref, ff_ref, gf_ref, pcf_ref),
            (lwb_ref, kab_ref, kdb_ref, rpb_ref, ypb_ref, fb_ref, gb_ref, pcb_ref))
    n_chunks = nkk_ref.shape[1] // C

    def build(c, probs):
        rows = pl.ds(pl.multiple_of(c * C, C), C)
        alpha = nkk_ref[0, rows, :]
        r = r_ref[0, rows, :]
        v = v_ref[0, rows, :]
        for d in range(2):
            lw_ref, b_ref, k_ref = dirs[d][0:3]
            before = (row > col) if d == 0 else (row < col)
            upto = (row >= col) if d == 0 else (row <= col)
            lw = lw_ref[0, rows, :]
            beta = b_ref[0, rows, :]
            kd = k_ref[0, rows, :]
            tri = upto.astype(BF16)
            l_hi, l_mid, l_lo = _split3(lw)
            lam = (jnp.dot(tri, l_hi, preferred_element_type=F32) + jnp.dot(tri, l_mid, preferred_element_type=F32)
                   + jnp.dot(tri, l_lo, preferred_element_type=F32))
            lam_prev = lam - lw
            lam_mid = lam[C // 2:C // 2 + 1, :]
            lam_end = lam[C - 1:C, :] if d == 0 else lam[0:1, :]
            e_to_mid = jnp.exp(lam_mid - lam)
            a_m = alpha * jnp.exp(lam_prev - lam_mid)
            b_m = beta * e_to_mid
            k_m = kd * e_to_mid
            r_m = r * jnp.exp(lam - lam_mid)
            a_0 = alpha * jnp.exp(lam_prev)
            r_0 = r * jnp.exp(lam)
            e_to_end = jnp.exp(lam_end - lam)
            b_e = beta * e_to_end
            k_e = kd * e_to_end
            dirs[d][7][0, pl.ds(c, 1)] = jnp.exp(lam_end).reshape(1, 1, D_A)
            for h in range(H_A):
                sl = slice(h * RW_HEAD, (h + 1) * RW_HEAD)
                probs.append(dict(
                    c=c, rows=rows, d=d, h=h, before=before, upto=upto,
                    ar=jnp.concatenate([a_m[:, sl], r_m[:, sl]], axis=0),
                    bk=jnp.concatenate([b_m[:, sl], k_m[:, sl]], axis=0),
                    a0=a_0[:, sl], r0=r_0[:, sl], vh=v[:, sl],
                    bke=jnp.concatenate([b_e[:, sl], k_e[:, sl]], axis=0)))

    def body(ci, carry):
        probs = []
        for cc in range(CHUNKS_PER_ITER):
            build(ci * CHUNKS_PER_ITER + cc, probs)
        g12 = [_mm3(q['ar'], q['bk'], _NT) for q in probs]
        ms = [jnp.where(q['before'], g[0:C, 0:C], 0.0) for q, g in zip(probs, g12)]
        ns = [jnp.where(q['before'], g[0:C, C:2 * C], 0.0) for q, g in zip(probs, g12)]
        rbk = [jnp.concatenate([jnp.where(q['upto'], g[C:2 * C, 0:C], 0.0),
                                jnp.where(q['upto'], g[C:2 * C, C:2 * C], 0.0)], axis=1) for q, g in zip(probs, g12)]
        nvs = [_mm3(n, q['vh'], _NN) for q, n in zip(probs, ns)]
        ts = _unit_tri_inverses(ms, row, col)
        xs = [_mm2(t, jnp.concatenate([q['a0'], nv], axis=1), _NN) for q, t, nv in zip(probs, ts, nvs)]
        zeros = jnp.zeros((C, RW_HEAD), F32)
        ws = [jnp.concatenate([x, jnp.concatenate([zeros, q['vh']], axis=1)], axis=0) for q, x in zip(probs, xs)]
        rys = [_mm3(rb, w, _NN) for rb, w in zip(rbk, ws)]
        fgs = [_mm3(w, q['bke'], _TN) for q, w in zip(probs, ws)]
        for q, fg in zip(probs, fgs):
            f_ref, g_ref = dirs[q['d']][5], dirs[q['d']][6]
            f_ref[0, pl.ds(q['c'], 1), q['h']] = fg[0:RW_HEAD].reshape(1, RW_HEAD, RW_HEAD)
            g_ref[0, pl.ds(q['c'], 1), q['h']] = fg[RW_HEAD:2 * RW_HEAD].reshape(1, RW_HEAD, RW_HEAD)
        for k0 in range(0, len(probs), H_A):
            sub = list(zip(probs[k0:k0 + H_A], rys[k0:k0 + H_A]))
            q0 = sub[0][0]
            dirs[q0['d']][3][0, q0['rows'], :] = jnp.concatenate([q['r0'] + ry[:, 0:RW_HEAD] for q, ry in sub], axis=1)
            dirs[q0['d']][4][0, q0['rows'], :] = jnp.concatenate([ry[:, RW_HEAD:2 * RW_HEAD] for q, ry in sub], axis=1)
        return carry

    lax.fori_loop(0, n_chunks // CHUNKS_PER_ITER, body, 0)


def _state_kernel(rpf_ref, ypf_ref, ff_ref, gf_ref, pcf_ref, rpb_ref, ypb_ref, fb_ref, gb_ref, pcb_ref, s0_ref,
                  yf_ref, yb_ref, sfin_ref, s_ref):
    C = CHUNK
    j = pl.program_id(1)
    nj = pl.num_programs(1)
    n_chunks = rpf_ref.shape[1] // C

    @pl.when(j == 0)
    def _():
        s_ref[...] = s0_ref[0]

    dirs = ((rpf_ref, ypf_ref, ff_ref, gf_ref, pcf_ref, yf_ref), (rpb_ref, ypb_ref, fb_ref, gb_ref, pcb_ref, yb_ref))
    for cc in range(n_chunks):
        for d in range(2):
            rp_ref, yp_ref, f_ref, g_ref, pc_ref, y_ref = dirs[d]
            c = cc if d == 0 else n_chunks - 1 - cc
            rows = slice(c * C, (c + 1) * C)
            pc = pc_ref[0, c]
            y_parts = []
            for h in range(H_A):
                sl = slice(h * RW_HEAD, (h + 1) * RW_HEAD)
                s = s_ref[d, h]
                y_parts.append(yp_ref[0, rows, sl] + _mm3(rp_ref[0, rows, sl], s, _NT))
                s_ref[d, h] = s * pc[:, sl] + _mm3(s, f_ref[0, c, h], _NN) + g_ref[0, c, h]
            y_ref[0, rows, :] = jnp.concatenate(y_parts, axis=1)

    @pl.when(j == nj - 1)
    def _():
        sfin_ref[0] = s_ref[...]


def _scan(nkk, r, v, lwf, kaf, kdf, lwb, kab, kdb, s0):
    B, L, _ = r.shape
    nj = L // TL
    cpb = TL // CHUNK
    nc = L // CHUNK
    row = lambda b, j: (b, j, 0)
    tile = pl.BlockSpec((1, TL, D_A), row)
    mats = pl.BlockSpec((1, cpb, H_A, RW_HEAD, RW_HEAD), lambda b, j: (b, j, 0, 0, 0))
    pcs = pl.BlockSpec((1, cpb, 1, D_A), lambda b, j: (b, j, 0, 0))
    f32 = lambda *s: jax.ShapeDtypeStruct(s, F32)
    per_dir_shapes = [f32(B, L, D_A), f32(B, L, D_A), f32(B, nc, H_A, RW_HEAD, RW_HEAD),
                      f32(B, nc, H_A, RW_HEAD, RW_HEAD), f32(B, nc, 1, D_A)]
    per_dir_specs = [tile, tile, mats, mats, pcs]
    outs = pl.pallas_call(
        _chunk_kernel, grid=(B, nj), in_specs=[tile] * 9,
        out_specs=per_dir_specs * 2, out_shape=per_dir_shapes * 2,
        compiler_params=_cparams(2, 40 * 1024 * 1024), name="rwkv_chunk",
    )(nkk, r, v, lwf, kaf, kdf, lwb, kab, kdb)
    rpf, ypf, ff, gf, pcf, rpb, ypb, fb, gb, pcb = outs

    rrow = lambda b, j: (b, nj - 1 - j, 0)
    rtile = pl.BlockSpec((1, TL, D_A), rrow)
    rmats = pl.BlockSpec((1, cpb, H_A, RW_HEAD, RW_HEAD), lambda b, j: (b, nj - 1 - j, 0, 0, 0))
    rpcs = pl.BlockSpec((1, cpb, 1, D_A), lambda b, j: (b, nj - 1 - j, 0, 0))
    st = pl.BlockSpec((1, 2, H_A, RW_HEAD, RW_HEAD), lambda b, j: (b, 0, 0, 0, 0))
    return pl.pallas_call(
        _state_kernel, grid=(B, nj),
        in_specs=[tile, tile, mats, mats, pcs, rtile, rtile, rmats, rmats, rpcs, st],
        out_specs=[tile, rtile, st],
        out_shape=[f32(B, L, D_A), f32(B, L, D_A), f32(B, 2, H_A, RW_HEAD, RW_HEAD)],
        scratch_shapes=[pltpu.VMEM((2, H_A, RW_HEAD, RW_HEAD), F32)],
        compiler_params=_cparams(2, 40 * 1024 * 1024), name="rwkv_state",
    )(rpf, ypf, ff, gf, pcf, rpb, ypb, fb, gb, pcb, s0)


def _attn_kernel(*refs, has_cache, lam_init):
    if has_cache:
        q_ref, k_ref, v_ref, ck_ref, cv_ref, lq1, lk1, lq2, lk2, sub_ref, o_ref = refs
    else:
        q_ref, k_ref, v_ref, lq1, lk1, lq2, lk2, sub_ref, o_ref = refs
    l1 = jnp.sum(lq1[...] * lk1[...], axis=-1, keepdims=True)
    l2 = jnp.sum(lq2[...] * lk2[...], axis=-1, keepdims=True)
    lam = jnp.exp(l1) - jnp.exp(l2) + lam_init
    q = q_ref[0].astype(BF16)
    kn = k_ref[0].astype(BF16)
    vn = v_ref[0].astype(BF16)
    if has_cache:
        kc = ck_ref[0, 0].astype(BF16)
        vc = cv_ref[0, 0].astype(BF16)
    dn = (((1,), (1,)), ((), ()))
    for h in range(H_B):
        w_new = None
        w_old = None
        for m in range(2):
            c0 = (h * 2 + m) * DIFF_DH
            qh = q[:, c0:c0 + DIFF_DH]
            s_n = lax.dot_general(qh, kn[:, c0:c0 + DIFF_DH], dn, preferred_element_type=F32)
            mx = jnp.max(s_n, axis=-1, keepdims=True)
            if has_cache:
                s_c = lax.dot_general(qh, kc[:, c0:c0 + DIFF_DH], dn, preferred_element_type=F32)
                mx = jnp.maximum(mx, jnp.max(s_c, axis=-1, keepdims=True))
            e_n = jnp.exp(s_n - mx)
            den = jnp.sum(e_n, axis=-1, keepdims=True)
            if has_cache:
                e_c = jnp.exp(s_c - mx)
                den = den + jnp.sum(e_c, axis=-1, keepdims=True)
            scale = (1.0 / den) if m == 0 else (-lam / den)
            w_new = e_n * scale if m == 0 else w_new + e_n * scale
            if has_cache:
                w_old = e_c * scale if m == 0 else w_old + e_c * scale
        vh = vn[:, h * 128:(h + 1) * 128]
        o = jnp.dot(w_new.astype(BF16), vh, preferred_element_type=F32)
        if has_cache:
            o = o + jnp.dot(w_old.astype(BF16), vc[:, h * 128:(h + 1) * 128], preferred_element_type=F32)
        o_ref[0, :, h * 128:(h + 1) * 128] = _rms(o, sub_ref[...]) * (1.0 - lam_init)


def _attention(q, k, v, cache, l, p, lam_init):
    B, L, _ = q.shape
    has_cache = cache is not None
    tq = TL
    in_specs = [pl.BlockSpec((1, tq, D_B), lambda b, i: (b, i, 0)),
                pl.BlockSpec((1, L, D_B), lambda b, i: (b, 0, 0)),
                pl.BlockSpec((1, L, D_B), lambda b, i: (b, 0, 0))]
    args = [q, k, v]
    if has_cache:
        ck, cv = cache
        past = ck.shape[2]
        in_specs += [pl.BlockSpec((1, 1, past, D_B), lambda b, i: (b, l, 0, 0))] * 2
        args += [ck, cv]
    small = lambda n: pl.BlockSpec((1, n), lambda b, i: (0, 0))
    in_specs += [small(DIFF_DH)] * 4 + [small(128)]
    args += [p['diff_lq1'].reshape(1, -1), p['diff_lk1'].reshape(1, -1), p['diff_lq2'].reshape(1, -1),
             p['diff_lk2'].reshape(1, -1), p['diff_subln'].reshape(1, -1)]
    return pl.pallas_call(
        functools.partial(_attn_kernel, has_cache=has_cache, lam_init=lam_init),
        grid=(B, L // tq), in_specs=in_specs,
        out_specs=pl.BlockSpec((1, tq, D_B), lambda b, i: (b, i, 0)),
        out_shape=jax.ShapeDtypeStruct((B, L, D_B), F32),
        compiler_params=_cparams(2, 48 * 1024 * 1024), name="diff_attention",
    )(*args)


def _hfilt_time_kernel(emb_ref, w1_ref, b1_ref, fr_ref, w2_ref, b2_ref, w3_ref, dec_ref, h_ref, ss_ref, acc_ref):
    i = pl.program_id(0)
    emb = emb_ref[...]
    fr = fr_ref[...]
    h = jnp.sin(fr * (jnp.dot(emb.astype(BF16), w1_ref[...].astype(BF16), preferred_element_type=F32) + b1_ref[...]))
    h = jnp.sin(fr * (jnp.dot(h.astype(BF16), w2_ref[...].astype(BF16), preferred_element_type=F32) + b2_ref[...]))
    h = jnp.dot(h.astype(BF16), w3_ref[...].astype(BF16), preferred_element_type=F32)
    h = h * jnp.exp(-emb[:, 0:1] * jnp.abs(dec_ref[...]))
    n = h.shape[0]
    row = lax.broadcasted_iota(jnp.int32, (n, 4 * D_C), 0) + i * n
    col = lax.broadcasted_iota(jnp.int32, (n, 4 * D_C), 1)
    is_bwd = ((col // D_C) % 2) == 1
    h = jnp.where(jnp.logical_and(is_bwd, row == 0), 0.0, h)
    h_ref[...] = h

    @pl.when(i == 0)
    def _():
        acc_ref[...] = jnp.zeros_like(acc_ref)

    acc_ref[...] += jnp.sum(h * h, axis=0, keepdims=True)
    ss_ref[...] = acc_ref[...]


def _hfilt_freq_kernel(h_ref, ss_ref, ch_ref, cl_ref, sh_ref, sl_ref, hr_ref, hi_ref,
                       ah_ref, al_ref, bh_ref, bl_ref, nyq_ref):
    i = pl.program_id(0)

    @pl.when(i == 0)
    def _():
        for o in range(2):
            hf = h_ref[:, o * 2 * D_C:o * 2 * D_C + D_C]
            hb = h_ref[:, o * 2 * D_C + D_C:(o + 1) * 2 * D_C]
            a = hf + hb
            b = hb - hf
            n = a.shape[0]
            alt = 1.0 - 2.0 * (lax.broadcasted_iota(jnp.int32, (n, 1), 0) % 2).astype(F32)
            nyq_ref[:, o * D_C:(o + 1) * D_C] = jnp.sum(a * alt, axis=0, keepdims=True)
            ahh, all_ = _split2(a)
            bhh, bll = _split2(b)
            ah_ref[:, o * D_C:(o + 1) * D_C] = ahh
            al_ref[:, o * D_C:(o + 1) * D_C] = all_
            bh_ref[:, o * D_C:(o + 1) * D_C] = bhh
            bl_ref[:, o * D_C:(o + 1) * D_C] = bll

    ss = ss_ref[...]
    tot = jnp.concatenate([ss[:, 0:D_C] + ss[:, D_C:2 * D_C], ss[:, 2 * D_C:3 * D_C] + ss[:, 3 * D_C:4 * D_C]], axis=1)
    scale = lax.rsqrt(tot + 1e-6)

    def dft(mh, ml, xh, xl):
        return (jnp.dot(mh, xh, preferred_element_type=F32) + jnp.dot(mh, xl, preferred_element_type=F32)
                + jnp.dot(ml, xh, preferred_element_type=F32))

    hr_ref[...] = dft(ch_ref[...], cl_ref[...], ah_ref[...], al_ref[...]) * scale
    hi = dft(sh_ref[...], sl_ref[...], bh_ref[...], bl_ref[...]) * scale
    tf = hi.shape[0]
    row = lax.broadcasted_iota(jnp.int32, (tf, 1), 0) + i * tf
    hi_ref[...] = jnp.where(row == 0, nyq_ref[...] * scale, hi)


def _hyena_filters(L, p, tabs):
    emb, ch, cl, sh, sl = tabs
    tr = 256
    w1 = jnp.pad(p['hy_w1'], ((0, 128 - HY_EMB), (0, 0)))
    full = lambda a: pl.BlockSpec(a.shape, lambda i: (0,) * a.ndim)
    params = [w1, p['hy_b1'].reshape(1, -1), p['hy_freq'].reshape(1, -1), p['hy_w2'], p['hy_b2'].reshape(1, -1),
              p['hy_w3'], jnp.tile(p['hy_decay'], 4).reshape(1, -1)]
    h_raw, ss = pl.pallas_call(
        _hfilt_time_kernel, grid=(L // tr,),
        in_specs=[pl.BlockSpec((tr, 128), lambda i: (i, 0))] + [full(a) for a in params],
        out_specs=[pl.BlockSpec((tr, 4 * D_C), lambda i: (i, 0)), pl.BlockSpec((1, 4 * D_C), lambda i: (0, 0))],
        out_shape=[jax.ShapeDtypeStruct((L, 4 * D_C), F32), jax.ShapeDtypeStruct((1, 4 * D_C), F32)],
        scratch_shapes=[pltpu.VMEM((1, 4 * D_C), F32)],
        compiler_params=_cparams(1), name="hyena_filter_time",
    )(emb, *params)
    tf = 256
    mat = pl.BlockSpec((tf, L), lambda i: (i, 0))
    return pl.pallas_call(
        _hfilt_freq_kernel, grid=(L // tf,),
        in_specs=[pl.BlockSpec((L, 4 * D_C), lambda i: (0, 0)), pl.BlockSpec((1, 4 * D_C), lambda i: (0, 0)),
                  mat, mat, mat, mat],
        out_specs=[pl.BlockSpec((tf, 2 * D_C), lambda i: (i, 0))] * 2,
        out_shape=[jax.ShapeDtypeStruct((L, 2 * D_C), F32)] * 2,
        scratch_shapes=[pltpu.VMEM((L, 2 * D_C), BF16)] * 4 + [pltpu.VMEM((1, 2 * D_C), F32)],
        compiler_params=_cparams(1, 48 * 1024 * 1024), name="hyena_filter_freq",
    )(h_raw, ss, ch, cl, sh, sl)


def _hconv_kernel(z_ref, zt_ref, gate_ref, hr_ref, hi_ref, bias_ref, c_ref, s_ref, o_ref,
                  zb_ref, yr_ref, yi_ref, nyq_ref, *, BG, L):
    ph = pl.program_id(1)
    i = pl.program_id(2)
    inv_n = 1.0 / (2 * L)

    @pl.when(jnp.logical_and(ph == 0, i == 0))
    def _():
        alt = 1.0 - 2.0 * (lax.broadcasted_iota(jnp.int32, (L, 1), 0) % 2).astype(F32)
        for b in range(BG):
            z = z_ref[b]
            zb_ref[:, b * D_C:(b + 1) * D_C] = z.astype(BF16)
            nyq_ref[:, b * D_C:(b + 1) * D_C] = jnp.sum(z * alt, axis=0, keepdims=True)

    tf = c_ref.shape[0]

    @pl.when(ph == 0)
    def _():
        zr = jnp.dot(c_ref[...], zb_ref[...], preferred_element_type=F32)
        zs = jnp.dot(s_ref[...], zb_ref[...], preferred_element_type=F32)
        hr = hr_ref[...]
        hi = hi_ref[...]
        row = lax.broadcasted_iota(jnp.int32, (tf, 1), 0) + i * tf
        wgt = jnp.where(row == 0, inv_n, 2.0 * inv_n)
        rows = pl.ds(pl.multiple_of(i * tf, tf), tf)
        for b in range(BG):
            sl = slice(b * D_C, (b + 1) * D_C)
            yr_ref[rows, sl] = ((zr[:, sl] * hr + zs[:, sl] * hi) * wgt).astype(BF16)
            yi_ref[rows, sl] = ((zr[:, sl] * hi - zs[:, sl] * hr) * (2.0 * inv_n)).astype(BF16)

        @pl.when(i == 0)
        def _():
            for b in range(BG):
                sl = slice(b * D_C, (b + 1) * D_C)
                nyq_ref[:, sl] = nyq_ref[:, sl] * hi[0:1, :] * inv_n

    @pl.when(ph == 1)
    def _():
        y = (jnp.dot(c_ref[...], yr_ref[...], preferred_element_type=F32)
             - jnp.dot(s_ref[...], yi_ref[...], preferred_element_type=F32))
        row = lax.broadcasted_iota(jnp.int32, (tf, 1), 0) + i * tf
        alt = 1.0 - 2.0 * (row % 2).astype(F32)
        y = y + alt * nyq_ref[...]
        for b in range(BG):
            sl = slice(b * D_C, (b + 1) * D_C)
            o_ref[b] = gate_ref[b] * (y[:, sl] + bias_ref[...] * zt_ref[b])


def _hconv(z, gate, hr, hi, bias, order, tabs):
    B, L, _ = z.shape
    _, ch, _, sh, _ = tabs
    BG = min(B, 8) if L <= 256 else min(B, 4)
    assert B % BG == 0
    tf = 256
    nt = L // tf
    mat = pl.BlockSpec((tf, L), lambda g, ph, i: (i, 0))
    tile = pl.BlockSpec((BG, tf, D_C), lambda g, ph, i: (g, i * ph, 0))
    hspec = pl.BlockSpec((tf, D_C), lambda g, ph, i: (i * (1 - ph), order))
    cols = BG * D_C
    return pl.pallas_call(
        functools.partial(_hconv_kernel, BG=BG, L=L), grid=(B // BG, 2, nt),
        in_specs=[pl.BlockSpec((BG, L, D_C), lambda g, ph, i: (g, 0, 0)), tile, tile, hspec, hspec,
                  pl.BlockSpec((1, D_C), lambda g, ph, i: (0, 0)), mat, mat],
        out_specs=tile,
        out_shape=jax.ShapeDtypeStruct((B, L, D_C), F32),
        scratch_shapes=[pltpu.VMEM((L, cols), BF16)] * 3 + [pltpu.VMEM((1, cols), F32)],
        compiler_params=_cparams(3, 48 * 1024 * 1024), name="hyena_conv",
    )(z, z, gate, hr, hi, bias[order].reshape(1, D_C), ch, sh)


def _outproj_kernel(x_ref, yf_ref, yb_ref, bon_ref, g_ref, ob_ref, yc_ref, mod_ref,
                    lnw_ref, lnb_ref, gpost_ref, gffn_ref, w_ref, x1_ref, h2_ref):
    bd = _block_diag(D_A, RW_HEAD)
    y = yf_ref[0] + yb_ref[0]
    mu = _segsum(y, bd) * (1.0 / RW_HEAD)
    yc = y - mu
    var = _segsum(yc * yc, bd) * (1.0 / RW_HEAD)
    yn = yc * lax.rsqrt(var + GN_EPS) * lnw_ref[...] + lnb_ref[...]
    ya = (yn + bon_ref[0]) * g_ref[0]
    mix = (jnp.dot(ya.astype(BF16), w_ref[0:D_A, :], preferred_element_type=F32)
           + jnp.dot(ob_ref[0].astype(BF16), w_ref[D_A:D_A + D_B, :], preferred_element_type=F32)
           + jnp.dot(yc_ref[0].astype(BF16), w_ref[D_A + D_B:D_MODEL, :], preferred_element_type=F32))
    mod = mod_ref[0]
    gt1 = mod[:, 2 * D_MODEL:3 * D_MODEL]
    sh2 = mod[:, 3 * D_MODEL:4 * D_MODEL]
    sc2 = mod[:, 4 * D_MODEL:5 * D_MODEL]
    x1 = x_ref[0] + gt1 * _rms(mix, gpost_ref[...])
    x1_ref[0] = x1
    h2_ref[0] = (_rms(x1, gffn_ref[...]) * (1.0 + sc2) + sh2).astype(BF16)


def _outproj(x, yf, yb, bon, g, ob, yc, mod, p, w_out):
    B, L, _ = x.shape
    mb = mod.shape[0]
    row = lambda b, i: (b, i, 0)
    c2 = lambda b, i: (0, 0)
    t = lambda n: pl.BlockSpec((1, TL, n), row)
    in_specs = [t(D_MODEL), t(D_A), t(D_A), t(D_A), t(D_A), t(D_B), t(D_C),
                pl.BlockSpec((1, 1, 6 * D_MODEL), (lambda b, i: (b, 0, 0)) if mb > 1 else (lambda b, i: (0, 0, 0))),
                pl.BlockSpec((1, D_A), c2), pl.BlockSpec((1, D_A), c2),
                pl.BlockSpec((1, D_MODEL), c2), pl.BlockSpec((1, D_MODEL), c2),
                pl.BlockSpec((D_MODEL, D_MODEL), c2)]
    return pl.pallas_call(
        _outproj_kernel, grid=(B, L // TL), in_specs=in_specs,
        out_specs=[t(D_MODEL), t(D_MODEL)],
        out_shape=[jax.ShapeDtypeStruct((B, L, D_MODEL), F32), jax.ShapeDtypeStruct((B, L, D_MODEL), BF16)],
        compiler_params=_cparams(2, 40 * 1024 * 1024), name="outproj",
    )(x, yf, yb, bon, g, ob, yc, mod, p['rwkv_ln_w'].reshape(1, -1), p['rwkv_ln_b'].reshape(1, -1),
      p['g_mix_post'].reshape(1, -1), p['g_ffn_pre'].reshape(1, -1), w_out)


def _ffn_kernel(h_ref, x_ref, mod_ref, g_ref, w1_ref, w2_ref, o_ref):
    h = h_ref[0]
    acc = None
    ck = 1024
    for c in range(D_FF // ck):
        a = jnp.dot(h, w1_ref[:, c * ck:(c + 1) * ck], preferred_element_type=F32)
        a = jnp.square(jnp.maximum(a, 0.0)).astype(BF16)
        part = jnp.dot(a, w2_ref[c * ck:(c + 1) * ck, :], preferred_element_type=F32)
        acc = part if acc is None else acc + part
    gt2 = mod_ref[0][:, 5 * D_MODEL:6 * D_MODEL]
    o_ref[0] = x_ref[0] + gt2 * _rms(acc, g_ref[...])


def _ffn(h2, x1, mod, g_post, w1, w2):
    B, L, _ = x1.shape
    mb = mod.shape[0]
    row = lambda b, i: (b, i, 0)
    c2 = lambda b, i: (0, 0)
    return pl.pallas_call(
        _ffn_kernel, grid=(B, L // TL),
        in_specs=[pl.BlockSpec((1, TL, D_MODEL), row), pl.BlockSpec((1, TL, D_MODEL), row),
                  pl.BlockSpec((1, 1, 6 * D_MODEL), (lambda b, i: (b, 0, 0)) if mb > 1 else (lambda b, i: (0, 0, 0))),
                  pl.BlockSpec((1, D_MODEL), c2),
                  pl.BlockSpec((D_MODEL, D_FF), c2), pl.BlockSpec((D_FF, D_MODEL), c2)],
        out_specs=pl.BlockSpec((1, TL, D_MODEL), row),
        out_shape=jax.ShapeDtypeStruct((B, L, D_MODEL), F32),
        compiler_params=_cparams(2, V7X_VMEM_LIMIT), name="ffn",
    )(h2, x1, mod, g_post.reshape(1, -1), w1, w2)


def _rope_tables(L):
    rows = L // GRID_W
    row = jnp.repeat(jnp.arange(rows), GRID_W).astype(F32)
    col = jnp.tile(jnp.arange(GRID_W), rows).astype(F32)
    half = DIFF_DH // 2
    inv = ROPE_BASE ** (-jnp.arange(0, half, 2, dtype=F32) / half)
    ang_r = row[:, None] * inv[None]
    ang_c = col[:, None] * inv[None]
    ang = jnp.concatenate([ang_r, ang_r, ang_c, ang_c], axis=-1)
    cos, sin = jnp.cos(ang), jnp.sin(ang)
    quarter = (jnp.arange(DIFF_DH) // 16) % 2
    sin_a = jnp.where(quarter == 0, -sin, 0.0)
    sin_b = jnp.where(quarter == 1, sin, 0.0)
    rep = D_B // DIFF_DH
    return tuple(jnp.tile(t, (1, rep)) for t in (cos, sin_a, sin_b))


def _hyena_tables(L):
    t = jnp.linspace(0.0, 1.0, L, dtype=F32)[:, None]
    ang = (2.0 * math.pi / L) * jnp.arange(L, dtype=F32)[:, None]
    bands = jnp.linspace(1e-4, HY_BANDS - 1, HY_BANDS, dtype=F32)[None, :]
    emb = jnp.concatenate([t, jnp.cos(bands * ang), -jnp.sin(bands * ang)], axis=-1)
    emb = jnp.pad(emb, ((0, 0), (0, 128 - HY_EMB)))
    n = jnp.arange(L, dtype=jnp.int32)
    k = (n[:, None] * n[None, :]) % (2 * L)
    theta = k.astype(F32) * (math.pi / L)
    c = jnp.cos(theta)
    s = jnp.sin(theta)
    ch, cl = _split2(c)
    sh, sl = _split2(s)
    return emb, ch, cl, sh, sl


def _layer(x, mod, p, wb, l, lam_init, cache, rope_tabs, hy_tabs, filt):
    B, L, _ = x.shape
    ua, uq, uk, uv, uh = _inproj(x, mod, p['g_mix_pre'], wb['w_in'], rope_tabs)
    (r, v, nkk, wf, kaf, kdf, wbk, kab, kdb, g, bon, x1h, x2h, hv) = _prep(ua, uh, p)
    if cache is None:
        s0 = jnp.zeros((B, 2, H_A, RW_HEAD, RW_HEAD), F32)
        kv_cache = None
    else:
        s0 = cache[0]
        kv_cache = (cache[1], cache[2])
    yf, yb, sfin = _scan(nkk, r, v, wf, kaf, kdf, wbk, kab, kdb, s0)
    ob = _attention(uq, uk, uv, kv_cache, l, p, lam_init)
    hr, hi = filt
    z1 = _hconv(hv, x1h, hr, hi, p['hy_bias'], 0, hy_tabs)
    yc = _hconv(z1, x2h, hr, hi, p['hy_bias'], 1, hy_tabs)
    x1, h2 = _outproj(x, yf, yb, bon, g, ob, yc, mod, p, wb['w_out'])
    x2 = _ffn(h2, x1, mod, p['g_ffn_post'], wb['w_ff1'], wb['w_ff2'])
    return x2, (sfin, uk, uv)


_LAYER_KEYS = ('g_mix_pre', 'g_mix_post', 'g_ffn_pre', 'g_ffn_post', 'rwkv_conv', 'rwkv_w0', 'rwkv_w2', 'rwkv_a0',
               'rwkv_a2', 'rwkv_g2', 'rwkv_kk', 'rwkv_ka', 'rwkv_rk', 'rwkv_ln_w', 'rwkv_ln_b', 'diff_lq1', 'diff_lk1',
               'diff_lq2', 'diff_lk2', 'diff_subln', 'hy_conv_w', 'hy_conv_b', 'hy_w1', 'hy_b1', 'hy_freq', 'hy_w2',
               'hy_b2', 'hy_w3', 'hy_decay', 'hy_bias')


def kernel(x_prompt, x_sample, state_rwkv, cache_k, cache_v, c, c_ctx, w_mod, b_mod, g_mix_pre, g_mix_post, g_ffn_pre, g_ffn_post, w_in, rwkv_conv, rwkv_w0, rwkv_w2, rwkv_a0, rwkv_a2, rwkv_g2, rwkv_kk, rwkv_ka, rwkv_rk, rwkv_ln_w, rwkv_ln_b, diff_lq1, diff_lk1, diff_lq2, diff_lk2, diff_subln, hy_conv_w, hy_conv_b, hy_w1, hy_b1, hy_freq, hy_w2, hy_b2, hy_w3, hy_decay, hy_bias, w_out, w_ff1, w_ff2):
    stacked = dict(g_mix_pre=g_mix_pre, g_mix_post=g_mix_post, g_ffn_pre=g_ffn_pre, g_ffn_post=g_ffn_post,
                   rwkv_conv=rwkv_conv, rwkv_w0=rwkv_w0, rwkv_w2=rwkv_w2, rwkv_a0=rwkv_a0, rwkv_a2=rwkv_a2,
                   rwkv_g2=rwkv_g2, rwkv_kk=rwkv_kk, rwkv_ka=rwkv_ka, rwkv_rk=rwkv_rk, rwkv_ln_w=rwkv_ln_w,
                   rwkv_ln_b=rwkv_ln_b, diff_lq1=diff_lq1, diff_lk1=diff_lk1, diff_lq2=diff_lq2, diff_lk2=diff_lk2,
                   diff_subln=diff_subln, hy_conv_w=hy_conv_w, hy_conv_b=hy_conv_b, hy_w1=hy_w1, hy_b1=hy_b1,
                   hy_freq=hy_freq, hy_w2=hy_w2, hy_b2=hy_b2, hy_w3=hy_w3, hy_decay=hy_decay, hy_bias=hy_bias)
    Bc, Lc, _ = x_prompt.shape
    Bs, Ls, _ = x_sample.shape
    past = cache_k.shape[2]

    cond = jnp.zeros((8, D_MODEL), F32).at[0:Bs].set(c).at[Bs].set(c_ctx)
    mod_all = _modulation(cond, w_mod, b_mod)

    rope_tabs = _rope_tables(Ls)
    tabs_c = _hyena_tables(Lc)
    tabs_s = _hyena_tables(Ls)
    ck = cache_k.reshape(Bs, DEPTH, past, D_B)
    cv = cache_v.reshape(Bs, DEPTH, past, D_B)

    xp, xs = x_prompt, x_sample
    st_list, k_list, v_list = [], [], []
    for l in range(DEPTH):
        p = {k: stacked[k][l] for k in _LAYER_KEYS}
        wb = dict(w_in=w_in[l].astype(BF16), w_out=w_out[l].astype(BF16),
                  w_ff1=w_ff1[l].astype(BF16), w_ff2=w_ff2[l].astype(BF16))
        lam_init = 0.8 - 0.6 * math.exp(-0.3 * l)
        mod_lat = mod_all[l, 0:Bs].reshape(Bs, 1, 6 * D_MODEL)
        mod_ctx = mod_all[l, Bs:Bs + 1].reshape(1, 1, 6 * D_MODEL)
        filt_c = _hyena_filters(Lc, p, tabs_c)
        filt_s = _hyena_filters(Ls, p, tabs_s)
        xp, (s_ctx, k_ctx, v_ctx) = _layer(xp, mod_ctx, p, wb, l, lam_init, None, None, tabs_c, filt_c)
        st_list.append(s_ctx)
        k_list.append(k_ctx.reshape(Bc, Lc, H_B, 2, DIFF_DH))
        v_list.append(v_ctx.reshape(Bc, Lc, H_B, 2 * DIFF_DH))
        s0 = state_rwkv[:, l]
        xs, _ = _layer(xs, mod_lat, p, wb, l, lam_init, (s0, ck, cv), rope_tabs, tabs_s, filt_s)
    return (xp, xs, jnp.stack(st_list, axis=1), jnp.stack(k_list, axis=1), jnp.stack(v_list, axis=1))
```

```python
import functools
import math

import jax
import jax.numpy as jnp
from jax import lax
from jax.experimental import pallas as pl
from jax.experimental.pallas import tpu as pltpu

F32 = jnp.float32
BF16 = jnp.bfloat16

D_MODEL = 1024
DEPTH = 4
GRID_W = 64
D_A = 256
RW_HEAD = 64
H_A = 4
D_B = 512
DIFF_DH = 64
H_B = 4
D_C = 256
HY_BANDS = 16
HY_EMB = 33
HY_FFN = 64
D_FF = 4096
ROPE_BASE = 10000.0
RMS_EPS = 1e-6
GN_EPS = 64e-5
N_A_COLS = 1152
IN_COLS = 3456

TL = 256
V7X_VMEM_LIMIT = 56 * 1024 * 1024


def _cparams(n_axes, vmem=None):
    return pltpu.CompilerParams(dimension_semantics=("arbitrary",) * n_axes,
                                vmem_limit_bytes=vmem)


def _split3(a):
    hi = a.astype(BF16)
    r1 = a - hi.astype(F32)
    mid = r1.astype(BF16)
    lo = (r1 - mid.astype(F32)).astype(BF16)
    return hi, mid, lo


def _split2(a):
    hi = a.astype(BF16)
    lo = (a - hi.astype(F32)).astype(BF16)
    return hi, lo


def _segsum(a, bd):
    hi, mid, lo = _split3(a)
    return (jnp.dot(hi, bd, preferred_element_type=F32) + jnp.dot(mid, bd, preferred_element_type=F32)
            + jnp.dot(lo, bd, preferred_element_type=F32))


def _block_diag(n, seg):
    r = lax.broadcasted_iota(jnp.int32, (n, n), 0) // seg
    c = lax.broadcasted_iota(jnp.int32, (n, n), 1) // seg
    return (r == c).astype(BF16)


def _rms(x, g):
    return x * lax.rsqrt(jnp.mean(x * x, axis=-1, keepdims=True) + RMS_EPS) * g


def _mod_kernel(cond_ref, w_ref, b_ref, o_ref):
    c = cond_ref[...]
    s = (c * jax.nn.sigmoid(c)).astype(BF16)
    o_ref[0] = jnp.dot(s, w_ref[0].astype(BF16), preferred_element_type=F32) + b_ref[0]


def _modulation(cond, w_mod, b_mod):
    tn = 1536
    return pl.pallas_call(
        _mod_kernel,
        grid=(DEPTH, 6 * D_MODEL // tn),
        in_specs=[pl.BlockSpec((8, D_MODEL), lambda l, n: (0, 0)),
                  pl.BlockSpec((1, D_MODEL, tn), lambda l, n: (l, 0, n)),
                  pl.BlockSpec((1, 1, tn), lambda l, n: (l, 0, n))],
        out_specs=pl.BlockSpec((1, 8, tn), lambda l, n: (l, 0, n)),
        out_shape=jax.ShapeDtypeStruct((DEPTH, 8, 6 * D_MODEL), F32),
        compiler_params=_cparams(2, 40 * 1024 * 1024),
        name="modulation",
    )(cond, w_mod, b_mod.reshape(DEPTH, 1, 6 * D_MODEL))


def _rope(x, cos, sin_a, sin_b):
    n = x.shape[-1]
    return x * cos + pltpu.roll(x, n - 16, axis=1) * sin_a + pltpu.roll(x, 16, axis=1) * sin_b


def _inproj_kernel(*refs, rope):
    if rope:
        x_ref, mod_ref, g_ref, w_ref, cos_ref, sa_ref, sb_ref, ua_ref, uq_ref, uk_ref, uv_ref, uh_ref = refs
    else:
        x_ref, mod_ref, g_ref, w_ref, ua_ref, uq_ref, uk_ref, uv_ref, uh_ref = refs
    x = x_ref[0]
    mod = mod_ref[0]
    sh1 = mod[:, 0:D_MODEL]
    sc1 = mod[:, D_MODEL:2 * D_MODEL]
    h = _rms(x, g_ref[...]) * (1.0 + sc1) + sh1
    u = jnp.dot(h.astype(BF16), w_ref[...], preferred_element_type=F32)
    ua_ref[0] = u[:, 0:N_A_COLS]
    q = u[:, N_A_COLS:N_A_COLS + D_B] * (DIFF_DH ** -0.5)
    k = u[:, N_A_COLS + D_B:N_A_COLS + 2 * D_B]
    if rope:
        q = _rope(q, cos_ref[...], sa_ref[...], sb_ref[...])
        k = _rope(k, cos_ref[...], sa_ref[...], sb_ref[...])
    uq_ref[0] = q.astype(uq_ref.dtype)
    uk_ref[0] = k.astype(uk_ref.dtype)
    uv_ref[0] = u[:, N_A_COLS + 2 * D_B:N_A_COLS + 3 * D_B].astype(uv_ref.dtype)
    uh_ref[0] = u[:, N_A_COLS + 3 * D_B:IN_COLS]


def _inproj(x, mod, g_pre, w_in, rope_tabs):
    B, L, _ = x.shape
    rope = rope_tabs is not None
    mb = mod.shape[0]
    row = lambda b, i: (b, i, 0)
    const2 = lambda b, i: (0, 0)
    in_specs = [pl.BlockSpec((1, TL, D_MODEL), row),
                pl.BlockSpec((1, 1, 6 * D_MODEL), (lambda b, i: (b, 0, 0)) if mb > 1 else (lambda b, i: (0, 0, 0))),
                pl.BlockSpec((1, D_MODEL), const2),
                pl.BlockSpec((D_MODEL, IN_COLS), const2)]
    args = [x, mod, g_pre.reshape(1, D_MODEL), w_in]
    if rope:
        in_specs += [pl.BlockSpec((TL, D_B), lambda b, i: (i, 0))] * 3
        args += list(rope_tabs)
    qkv_dt = BF16 if rope else F32
    out_shape = [jax.ShapeDtypeStruct((B, L, N_A_COLS), F32),
                 jax.ShapeDtypeStruct((B, L, D_B), qkv_dt),
                 jax.ShapeDtypeStruct((B, L, D_B), qkv_dt),
                 jax.ShapeDtypeStruct((B, L, D_B), qkv_dt),
                 jax.ShapeDtypeStruct((B, L, 3 * D_C), F32)]
    out_specs = [pl.BlockSpec((1, TL, N_A_COLS), row), pl.BlockSpec((1, TL, D_B), row),
                 pl.BlockSpec((1, TL, D_B), row), pl.BlockSpec((1, TL, D_B), row),
                 pl.BlockSpec((1, TL, 3 * D_C), row)]
    return pl.pallas_call(
        functools.partial(_inproj_kernel, rope=rope),
        grid=(B, L // TL), in_specs=in_specs, out_specs=out_specs, out_shape=out_shape,
        compiler_params=_cparams(2, 48 * 1024 * 1024), name="inproj",
    )(*args)


def _shift_conv(u, prev_row, next_row, cw):
    n = u.shape[0]
    row = lax.broadcasted_iota(jnp.int32, (n, 1), 0)
    up = jnp.where(row == 0, prev_row, pltpu.roll(u, 1, axis=0))
    un = jnp.where(row == n - 1, next_row, pltpu.roll(u, n - 1, axis=0))
    return cw[0:1] * up + cw[1:2] * u + cw[2:3] * un


def _prep_kernel(ua_ref, uap_ref, uan_ref, uh_ref, uhp_ref, uhn_ref,
                 cw_ref, w0_ref, w2_ref, a0_ref, a2_ref, g2_ref, kkw_ref, kaw_ref, rk_ref, hcw_ref, hcb_ref,
                 r_ref, v_ref, nkk_ref, wf_ref, kaf_ref, kdf_ref, wb_ref, kab_ref, kdb_ref, g_ref, bon_ref,
                 x1_ref, x2_ref, hv_ref):
    i = pl.program_id(1)
    nt = pl.num_programs(1)
    has_prev = i > 0
    has_next = i < nt - 1
    bd = _block_diag(D_A, RW_HEAD)

    ua = ua_ref[0]
    u_rkv = ua[:, 0:3 * D_A]
    prev = jnp.where(has_prev, uap_ref[0, 7:8, :], 0.0)
    nxt = jnp.where(has_next, uan_ref[0, 0:1, :], 0.0)
    rkv = _shift_conv(u_rkv, prev, nxt, cw_ref[...])
    r = rkv[:, 0:D_A]
    k = rkv[:, D_A:2 * D_A]
    v = rkv[:, 2 * D_A:3 * D_A]
    u_w = ua[:, 768:896]
    u_a = ua[:, 896:1024]
    u_g = ua[:, 1024:1152]

    g = jnp.dot(jax.nn.sigmoid(u_g).astype(BF16), g2_ref[...].astype(BF16), preferred_element_type=F32)
    kk = k * kkw_ref[...]
    kk = kk * lax.rsqrt(_segsum(kk * kk, bd) + 1e-12)
    r_ref[0] = r
    v_ref[0] = v
    nkk_ref[0] = -kk
    g_ref[0] = g

    kd_sum = None
    outs = ((wf_ref, kaf_ref, kdf_ref), (wb_ref, kab_ref, kdb_ref))
    for d in range(2):
        uw_d = u_w[:, d * 64:(d + 1) * 64]
        ua_d = u_a[:, d * 64:(d + 1) * 64]
        xw = w0_ref[d:d + 1, :] + jnp.dot(jnp.tanh(uw_d).astype(BF16), w2_ref[d].astype(BF16),
                                          preferred_element_type=F32)
        z = -xw
        softplus = jnp.maximum(z, 0.0) + jnp.log(1.0 + jnp.exp(-jnp.abs(z)))
        wl = -softplus - 0.5
        log_decay = -jnp.exp(wl)
        a = jax.nn.sigmoid(a0_ref[d:d + 1, :] + jnp.dot(ua_d.astype(BF16), a2_ref[d].astype(BF16),
                                                        preferred_element_type=F32))
        kd = k * (1.0 + (a - 1.0) * kaw_ref[...])
        w_o, ka_o, kd_o = outs[d]
        w_o[0] = log_decay
        ka_o[0] = kk * a
        kd_o[0] = kd
        kd_sum = kd if kd_sum is None else kd_sum + kd
    bon_ref[0] = _segsum(r * kd_sum * rk_ref[...], bd) * v

    uh = uh_ref[0]
    prevh = jnp.where(has_prev, uhp_ref[0, 7:8, :], 0.0)
    nxth = jnp.where(has_next, uhn_ref[0, 0:1, :], 0.0)
    hc = _shift_conv(uh, prevh, nxth, hcw_ref[...]) + hcb_ref[...]
    x1_ref[0] = hc[:, 0:D_C]
    x2_ref[0] = hc[:, D_C:2 * D_C]
    hv_ref[0] = hc[:, 2 * D_C:3 * D_C]


def _prep(ua, uh, p):
    B, L, _ = ua.shape
    nb8 = L // 8
    t8 = TL // 8
    row = lambda b, i: (b, i, 0)
    prev = lambda b, i: (b, jnp.maximum(i * t8 - 1, 0), 0)
    nxt = lambda b, i: (b, jnp.minimum((i + 1) * t8, nb8 - 1), 0)

    def full(a):
        nd = a.ndim
        return pl.BlockSpec(a.shape, lambda b, i: (0,) * nd)

    params = [p['rwkv_conv'], p['rwkv_w0'], p['rwkv_w2'], p['rwkv_a0'], p['rwkv_a2'], p['rwkv_g2'],
              p['rwkv_kk'].reshape(1, D_A), p['rwkv_ka'].reshape(1, D_A), p['rwkv_rk'].reshape(1, D_A),
              p['hy_conv_w'], p['hy_conv_b'].reshape(1, 3 * D_C)]
    in_specs = [pl.BlockSpec((1, TL, N_A_COLS), row), pl.BlockSpec((1, 8, 3 * D_A), prev),
                pl.BlockSpec((1, 8, 3 * D_A), nxt),
                pl.BlockSpec((1, TL, 3 * D_C), row), pl.BlockSpec((1, 8, 3 * D_C), prev),
                pl.BlockSpec((1, 8, 3 * D_C), nxt)] + [full(a) for a in params]
    n_out = 14
    return pl.pallas_call(
        _prep_kernel, grid=(B, L // TL), in_specs=in_specs,
        out_specs=[pl.BlockSpec((1, TL, D_A), row)] * n_out,
        out_shape=[jax.ShapeDtypeStruct((B, L, D_A), F32)] * n_out,
        compiler_params=_cparams(2, 40 * 1024 * 1024), name="prep",
    )(ua, ua, ua, uh, uh, uh, *params)


CHUNK = 64
CHUNKS_PER_ITER = 2
_NN = (((1,), (0,)), ((), ()))
_NT = (((1,), (1,)), ((), ()))
_TN = (((0,), (0,)), ((), ()))


def _mm3(a, b, dims):
    ah, al = _split2(a)
    bh, bl = _split2(b)
    f = lambda x, y: lax.dot_general(x, y, dims, preferred_element_type=F32)
    if dims is _TN:
        return f(ah, bh) + f(ah, bl) + f(al, bh)
    m = a.shape[0]
    both = f(jnp.concatenate([ah, al], axis=0), bh)
    return both[0:m] + both[m:2 * m] + f(ah, bl)


def _mm2(a, b, dims):
    ah = a.astype(BF16)
    bh, bl = _split2(b)
    f = lambda x, y: lax.dot_general(x, y, dims, preferred_element_type=F32)
    return f(ah, bh) + f(ah, bl)


def _mm1(a, b, dims):
    return lax.dot_general(a.astype(BF16), b.astype(BF16), dims, preferred_element_type=F32)


def _unit_tri_inverses(ms, row, col):
    eye = (row == col).astype(F32)
    blk = lambda n: (row // n) == (col // n)
    m8 = [jnp.where(blk(8), m, 0.0) for m in ms]
    a2 = [_mm1(x, x, _NN) for x in m8]
    a4 = [_mm1(x, x, _NN) for x in a2]
    p = [eye + x + y + _mm1(x, y, _NN) for x, y in zip(m8, a2)]
    ts = [x + _mm1(x, y, _NN) for x, y in zip(p, a4)]
    n = 8
    while n < CHUNK:
        sel = jnp.logical_and(blk(2 * n), jnp.logical_not(blk(n)))
        ot = [_mm1(jnp.where(sel, m, 0.0), t, _NN) for m, t in zip(ms, ts)]
        ts = [t + _mm1(t, o, _NN) for t, o in zip(ts, ot)]
        n *= 2
    return ts


def _chunk_kernel(nkk_ref, r_ref, v_ref, lwf_ref, kaf_ref, kdf_ref, lwb_ref, kab_ref, kdb_ref,
                  rpf_ref, ypf_ref, ff_ref, gf_ref, pcf_ref, rpb_ref, ypb_ref, fb_ref, gb_ref, pcb_ref):
    C = CHUNK
    row = lax.broadcasted_iota(jnp.int32, (C, C), 0)
    col = lax.broadcasted_iota(jnp.int32, (C, C), 1)
    dirs = ((lwf_ref, kaf_ref, kdf_ref, rpf_ref, ypf_ref, ff_ref, gf_ref, pcf_ref),
            (lwb_ref, kab_ref, kdb_ref, rpb_ref, ypb_ref, fb_ref, gb_ref, pcb_ref))
    n_chunks = nkk_ref.shape[1] // C

    def build(c, probs):
        rows = pl.ds(pl.multiple_of(c * C, C), C)
        alpha = nkk_ref[0, rows, :]
        r = r_ref[0, rows, :]
        v = v_ref[0, rows, :]
        for d in range(2):
            lw_ref, b_ref, k_ref = dirs[d][0:3]
            before = (row > col) if d == 0 else (row < col)
            upto = (row >= col) if d == 0 else (row <= col)
            lw = lw_ref[0, rows, :]
            beta = b_ref[0, rows, :]
            kd = k_ref[0, rows, :]
            tri = upto.astype(BF16)
            l_hi, l_mid, l_lo = _split3(lw)
            lam = (jnp.dot(tri, l_hi, preferred_element_type=F32) + jnp.dot(tri, l_mid, preferred_element_type=F32)
                   + jnp.dot(tri, l_lo, preferred_element_type=F32))
            lam_prev = lam - lw
            lam_mid = lam[C // 2:C // 2 + 1, :]
            lam_end = lam[C - 1:C, :] if d == 0 else lam[0:1, :]
            e_to_mid = jnp.exp(lam_mid - lam)
            a_m = alpha * jnp.exp(lam_prev - lam_mid)
            b_m = beta * e_to_mid
            k_m = kd * e_to_mid
            r_m = r * jnp.exp(lam - lam_mid)
            a_0 = alpha * jnp.exp(lam_prev)
            r_0 = r * jnp.exp(lam)
            e_to_end = jnp.exp(lam_end - lam)
            b_e = beta * e_to_end
            k_e = kd * e_to_end
            dirs[d][7][0, pl.ds(c, 1)] = jnp.exp(lam_end).reshape(1, 1, D_A)
            for h in range(H_A):
                sl = slice(h * RW_HEAD, (h + 1) * RW_HEAD)
                probs.append(dict(
                    c=c, rows=rows, d=d, h=h, before=before, upto=upto,
                    ar=jnp.concatenate([a_m[:, sl], r_m[:, sl]], axis=0),
                    bk=jnp.concatenate([b_m[:, sl], k_m[:, sl]], axis=0),
                    a0=a_0[:, sl], r0=r_0[:, sl], vh=v[:, sl],
                    bke=jnp.concatenate([b_e[:, sl], k_e[:, sl]], axis=0)))

    def body(ci, carry):
        probs = []
        for cc in range(CHUNKS_PER_ITER):
            build(ci * CHUNKS_PER_ITER + cc, probs)
        g12 = [_mm1(q['ar'], q['bk'], _NT) for q in probs]
        ms = [jnp.where(q['before'], g[0:C, 0:C], 0.0) for q, g in zip(probs, g12)]
        ns = [jnp.where(q['before'], g[0:C, C:2 * C], 0.0) for q, g in zip(probs, g12)]
        rbk = [jnp.concatenate([jnp.where(q['upto'], g[C:2 * C, 0:C], 0.0),
                                jnp.where(q['upto'], g[C:2 * C, C:2 * C], 0.0)], axis=1) for q, g in zip(probs, g12)]
        nvs = [_mm1(n, q['vh'], _NN) for q, n in zip(probs, ns)]
        ts = _unit_tri_inverses(ms, row, col)
        xs = [_mm2(t, jnp.concatenate([q['a0'], nv], axis=1), _NN) for q, t, nv in zip(probs, ts, nvs)]
        zeros = jnp.zeros((C, RW_HEAD), F32)
        ws = [jnp.concatenate([x, jnp.concatenate([zeros, q['vh']], axis=1)], axis=0) for q, x in zip(probs, xs)]
        rys = [_mm1(rb, w, _NN) for rb, w in zip(rbk, ws)]
        fgs = [_mm3(w, q['bke'], _TN) for q, w in zip(probs, ws)]
        for q, fg in zip(probs, fgs):
            f_ref, g_ref = dirs[q['d']][5], dirs[q['d']][6]
            f_ref[0, pl.ds(q['c'], 1), q['h']] = fg[0:RW_HEAD].reshape(1, RW_HEAD, RW_HEAD)
            g_ref[0, pl.ds(q['c'], 1), q['h']] = fg[RW_HEAD:2 * RW_HEAD].reshape(1, RW_HEAD, RW_HEAD)
        for k0 in range(0, len(probs), H_A):
            sub = list(zip(probs[k0:k0 + H_A], rys[k0:k0 + H_A]))
            q0 = sub[0][0]
            dirs[q0['d']][3][0, q0['rows'], :] = jnp.concatenate([q['r0'] + ry[:, 0:RW_HEAD] for q, ry in sub], axis=1)
            dirs[q0['d']][4][0, q0['rows'], :] = jnp.concatenate([ry[:, RW_HEAD:2 * RW_HEAD] for q, ry in sub], axis=1)
        return carry

    lax.fori_loop(0, n_chunks // CHUNKS_PER_ITER, body, 0)


def _state_kernel(rpf_ref, ypf_ref, ff_ref, gf_ref, pcf_ref, rpb_ref, ypb_ref, fb_ref, gb_ref, pcb_ref, s0_ref,
                  yf_ref, yb_ref, sfin_ref, s_ref):
    C = CHUNK
    j = pl.program_id(1)
    nj = pl.num_programs(1)
    n_chunks = rpf_ref.shape[1] // C

    @pl.when(j == 0)
    def _():
        s_ref[...] = s0_ref[0]

    dirs = ((rpf_ref, ypf_ref, ff_ref, gf_ref, pcf_ref, yf_ref), (rpb_ref, ypb_ref, fb_ref, gb_ref, pcb_ref, yb_ref))
    for cc in range(n_chunks):
        for d in range(2):
            rp_ref, yp_ref, f_ref, g_ref, pc_ref, y_ref = dirs[d]
            c = cc if d == 0 else n_chunks - 1 - cc
            rows = slice(c * C, (c + 1) * C)
            pc = pc_ref[0, c]
            y_parts = []
            for h in range(H_A):
                sl = slice(h * RW_HEAD, (h + 1) * RW_HEAD)
                s = s_ref[d, h]
                y_parts.append(yp_ref[0, rows, sl] + _mm1(rp_ref[0, rows, sl], s, _NT))
                s_ref[d, h] = s * pc[:, sl] + _mm1(s, f_ref[0, c, h], _NN) + g_ref[0, c, h]
            y_ref[0, rows, :] = jnp.concatenate(y_parts, axis=1)

    @pl.when(j == nj - 1)
    def _():
        sfin_ref[0] = s_ref[...]


def _scan(nkk, r, v, lwf, kaf, kdf, lwb, kab, kdb, s0):
    B, L, _ = r.shape
    nj = L // TL
    cpb = TL // CHUNK
    nc = L // CHUNK
    row = lambda b, j: (b, j, 0)
    tile = pl.BlockSpec((1, TL, D_A), row)
    mats = pl.BlockSpec((1, cpb, H_A, RW_HEAD, RW_HEAD), lambda b, j: (b, j, 0, 0, 0))
    pcs = pl.BlockSpec((1, cpb, 1, D_A), lambda b, j: (b, j, 0, 0))
    f32 = lambda *s: jax.ShapeDtypeStruct(s, F32)
    per_dir_shapes = [f32(B, L, D_A), f32(B, L, D_A), f32(B, nc, H_A, RW_HEAD, RW_HEAD),
                      f32(B, nc, H_A, RW_HEAD, RW_HEAD), f32(B, nc, 1, D_A)]
    per_dir_specs = [tile, tile, mats, mats, pcs]
    outs = pl.pallas_call(
        _chunk_kernel, grid=(B, nj), in_specs=[tile] * 9,
        out_specs=per_dir_specs * 2, out_shape=per_dir_shapes * 2,
        compiler_params=_cparams(2, 40 * 1024 * 1024), name="rwkv_chunk",
    )(nkk, r, v, lwf, kaf, kdf, lwb, kab, kdb)
    rpf, ypf, ff, gf, pcf, rpb, ypb, fb, gb, pcb = outs

    rrow = lambda b, j: (b, nj - 1 - j, 0)
    rtile = pl.BlockSpec((1, TL, D_A), rrow)
    rmats = pl.BlockSpec((1, cpb, H_A, RW_HEAD, RW_HEAD), lambda b, j: (b, nj - 1 - j, 0, 0, 0))
    rpcs = pl.BlockSpec((1, cpb, 1, D_A), lambda b, j: (b, nj - 1 - j, 0, 0))
    st = pl.BlockSpec((1, 2, H_A, RW_HEAD, RW_HEAD), lambda b, j: (b, 0, 0, 0, 0))
    return pl.pallas_call(
        _state_kernel, grid=(B, nj),
        in_specs=[tile, tile, mats, mats, pcs, rtile, rtile, rmats, rmats, rpcs, st],
        out_specs=[tile, rtile, st],
        out_shape=[f32(B, L, D_A), f32(B, L, D_A), f32(B, 2, H_A, RW_HEAD, RW_HEAD)],
        scratch_shapes=[pltpu.VMEM((2, H_A, RW_HEAD, RW_HEAD), F32)],
        compiler_params=_cparams(2, 40 * 1024 * 1024), name="rwkv_state",
    )(rpf, ypf, ff, gf, pcf, rpb, ypb, fb, gb, pcb, s0)


def _attn_kernel(*refs, has_cache, lam_init):
    if has_cache:
        q_ref, k_ref, v_ref, ck_ref, cv_ref, lq1, lk1, lq2, lk2, sub_ref, o_ref = refs
    else:
        q_ref, k_ref, v_ref, lq1, lk1, lq2, lk2, sub_ref, o_ref = refs
    l1 = jnp.sum(lq1[...] * lk1[...], axis=-1, keepdims=True)
    l2 = jnp.sum(lq2[...] * lk2[...], axis=-1, keepdims=True)
    lam = jnp.exp(l1) - jnp.exp(l2) + lam_init
    q = q_ref[0].astype(BF16)
    kn = k_ref[0].astype(BF16)
    vn = v_ref[0].astype(BF16)
    if has_cache:
        kc = ck_ref[0, 0].astype(BF16)
        vc = cv_ref[0, 0].astype(BF16)
    dn = (((1,), (1,)), ((), ()))
    for h in range(H_B):
        w_new = None
        w_old = None
        for m in range(2):
            c0 = (h * 2 + m) * DIFF_DH
            qh = q[:, c0:c0 + DIFF_DH]
            s_n = lax.dot_general(qh, kn[:, c0:c0 + DIFF_DH], dn, preferred_element_type=F32)
            mx = jnp.max(s_n, axis=-1, keepdims=True)
            if has_cache:
                s_c = lax.dot_general(qh, kc[:, c0:c0 + DIFF_DH], dn, preferred_element_type=F32)
                mx = jnp.maximum(mx, jnp.max(s_c, axis=-1, keepdims=True))
            e_n = jnp.exp(s_n - mx)
            den = jnp.sum(e_n, axis=-1, keepdims=True)
            if has_cache:
                e_c = jnp.exp(s_c - mx)
                den = den + jnp.sum(e_c, axis=-1, keepdims=True)
            scale = (1.0 / den) if m == 0 else (-lam / den)
            w_new = e_n * scale if m == 0 else w_new + e_n * scale
            if has_cache:
                w_old = e_c * scale if m == 0 else w_old + e_c * scale
        vh = vn[:, h * 128:(h + 1) * 128]
        o = jnp.dot(w_new.astype(BF16), vh, preferred_element_type=F32)
        if has_cache:
            o = o + jnp.dot(w_old.astype(BF16), vc[:, h * 128:(h + 1) * 128], preferred_element_type=F32)
        o_ref[0, :, h * 128:(h + 1) * 128] = _rms(o, sub_ref[...]) * (1.0 - lam_init)


def _attention(q, k, v, cache, l, p, lam_init):
    B, L, _ = q.shape
    has_cache = cache is not None
    tq = TL
    in_specs = [pl.BlockSpec((1, tq, D_B), lambda b, i: (b, i, 0)),
                pl.BlockSpec((1, L, D_B), lambda b, i: (b, 0, 0)),
                pl.BlockSpec((1, L, D_B), lambda b, i: (b, 0, 0))]
    args = [q, k, v]
    if has_cache:
        ck, cv = cache
        past = ck.shape[2]
        in_specs += [pl.BlockSpec((1, 1, past, D_B), lambda b, i: (b, l, 0, 0))] * 2
        args += [ck, cv]
    small = lambda n: pl.BlockSpec((1, n), lambda b, i: (0, 0))
    in_specs += [small(DIFF_DH)] * 4 + [small(128)]
    args += [p['diff_lq1'].reshape(1, -1), p['diff_lk1'].reshape(1, -1), p['diff_lq2'].reshape(1, -1),
             p['diff_lk2'].reshape(1, -1), p['diff_subln'].reshape(1, -1)]
    return pl.pallas_call(
        functools.partial(_attn_kernel, has_cache=has_cache, lam_init=lam_init),
        grid=(B, L // tq), in_specs=in_specs,
        out_specs=pl.BlockSpec((1, tq, D_B), lambda b, i: (b, i, 0)),
        out_shape=jax.ShapeDtypeStruct((B, L, D_B), F32),
        compiler_params=_cparams(2, 48 * 1024 * 1024), name="diff_attention",
    )(*args)


def _hfilt_time_kernel(emb_ref, w1_ref, b1_ref, fr_ref, w2_ref, b2_ref, w3_ref, dec_ref, h_ref, ss_ref, acc_ref):
    i = pl.program_id(0)
    emb = emb_ref[...]
    fr = fr_ref[...]
    h = jnp.sin(fr * (jnp.dot(emb.astype(BF16), w1_ref[...].astype(BF16), preferred_element_type=F32) + b1_ref[...]))
    h = jnp.sin(fr * (jnp.dot(h.astype(BF16), w2_ref[...].astype(BF16), preferred_element_type=F32) + b2_ref[...]))
    h = jnp.dot(h.astype(BF16), w3_ref[...].astype(BF16), preferred_element_type=F32)
    h = h * jnp.exp(-emb[:, 0:1] * jnp.abs(dec_ref[...]))
    n = h.shape[0]
    row = lax.broadcasted_iota(jnp.int32, (n, 4 * D_C), 0) + i * n
    col = lax.broadcasted_iota(jnp.int32, (n, 4 * D_C), 1)
    is_bwd = ((col // D_C) % 2) == 1
    h = jnp.where(jnp.logical_and(is_bwd, row == 0), 0.0, h)
    h_ref[...] = h

    @pl.when(i == 0)
    def _():
        acc_ref[...] = jnp.zeros_like(acc_ref)

    acc_ref[...] += jnp.sum(h * h, axis=0, keepdims=True)
    ss_ref[...] = acc_ref[...]


def _hfilt_freq_kernel(h_ref, ss_ref, ch_ref, cl_ref, sh_ref, sl_ref, hr_ref, hi_ref,
                       ah_ref, al_ref, bh_ref, bl_ref, nyq_ref):
    i = pl.program_id(0)

    @pl.when(i == 0)
    def _():
        for o in range(2):
            hf = h_ref[:, o * 2 * D_C:o * 2 * D_C + D_C]
            hb = h_ref[:, o * 2 * D_C + D_C:(o + 1) * 2 * D_C]
            a = hf + hb
            b = hb - hf
            n = a.shape[0]
            alt = 1.0 - 2.0 * (lax.broadcasted_iota(jnp.int32, (n, 1), 0) % 2).astype(F32)
            nyq_ref[:, o * D_C:(o + 1) * D_C] = jnp.sum(a * alt, axis=0, keepdims=True)
            ahh, all_ = _split2(a)
            bhh, bll = _split2(b)
            ah_ref[:, o * D_C:(o + 1) * D_C] = ahh
            al_ref[:, o * D_C:(o + 1) * D_C] = all_
            bh_ref[:, o * D_C:(o + 1) * D_C] = bhh
            bl_ref[:, o * D_C:(o + 1) * D_C] = bll

    ss = ss_ref[...]
    tot = jnp.concatenate([ss[:, 0:D_C] + ss[:, D_C:2 * D_C], ss[:, 2 * D_C:3 * D_C] + ss[:, 3 * D_C:4 * D_C]], axis=1)
    scale = lax.rsqrt(tot + 1e-6)

    def dft(mh, ml, xh, xl):
        return (jnp.dot(mh, xh, preferred_element_type=F32) + jnp.dot(mh, xl, preferred_element_type=F32)
                + jnp.dot(ml, xh, preferred_element_type=F32))

    hr_ref[...] = dft(ch_ref[...], cl_ref[...], ah_ref[...], al_ref[...]) * scale
    hi = dft(sh_ref[...], sl_ref[...], bh_ref[...], bl_ref[...]) * scale
    tf = hi.shape[0]
    row = lax.broadcasted_iota(jnp.int32, (tf, 1), 0) + i * tf
    hi_ref[...] = jnp.where(row == 0, nyq_ref[...] * scale, hi)


def _hyena_filters(L, p, tabs):
    emb, ch, cl, sh, sl = tabs
    tr = 256
    w1 = jnp.pad(p['hy_w1'], ((0, 128 - HY_EMB), (0, 0)))
    full = lambda a: pl.BlockSpec(a.shape, lambda i: (0,) * a.ndim)
    params = [w1, p['hy_b1'].reshape(1, -1), p['hy_freq'].reshape(1, -1), p['hy_w2'], p['hy_b2'].reshape(1, -1),
              p['hy_w3'], jnp.tile(p['hy_decay'], 4).reshape(1, -1)]
    h_raw, ss = pl.pallas_call(
        _hfilt_time_kernel, grid=(L // tr,),
        in_specs=[pl.BlockSpec((tr, 128), lambda i: (i, 0))] + [full(a) for a in params],
        out_specs=[pl.BlockSpec((tr, 4 * D_C), lambda i: (i, 0)), pl.BlockSpec((1, 4 * D_C), lambda i: (0, 0))],
        out_shape=[jax.ShapeDtypeStruct((L, 4 * D_C), F32), jax.ShapeDtypeStruct((1, 4 * D_C), F32)],
        scratch_shapes=[pltpu.VMEM((1, 4 * D_C), F32)],
        compiler_params=_cparams(1), name="hyena_filter_time",
    )(emb, *params)
    tf = 256
    mat = pl.BlockSpec((tf, L), lambda i: (i, 0))
    return pl.pallas_call(
        _hfilt_freq_kernel, grid=(L // tf,),
        in_specs=[pl.BlockSpec((L, 4 * D_C), lambda i: (0, 0)), pl.BlockSpec((1, 4 * D_C), lambda i: (0, 0)),
                  mat, mat, mat, mat],
        out_specs=[pl.BlockSpec((tf, 2 * D_C), lambda i: (i, 0))] * 2,
        out_shape=[jax.ShapeDtypeStruct((L, 2 * D_C), F32)] * 2,
        scratch_shapes=[pltpu.VMEM((L, 2 * D_C), BF16)] * 4 + [pltpu.VMEM((1, 2 * D_C), F32)],
        compiler_params=_cparams(1, 48 * 1024 * 1024), name="hyena_filter_freq",
    )(h_raw, ss, ch, cl, sh, sl)


def _hconv_kernel(z_ref, zt_ref, gate_ref, hr_ref, hi_ref, bias_ref, c_ref, s_ref, o_ref,
                  zb_ref, yr_ref, yi_ref, nyq_ref, *, BG, L):
    ph = pl.program_id(1)
    i = pl.program_id(2)
    inv_n = 1.0 / (2 * L)

    @pl.when(jnp.logical_and(ph == 0, i == 0))
    def _():
        alt = 1.0 - 2.0 * (lax.broadcasted_iota(jnp.int32, (L, 1), 0) % 2).astype(F32)
        for b in range(BG):
            z = z_ref[b]
            zb_ref[:, b * D_C:(b + 1) * D_C] = z.astype(BF16)
            nyq_ref[:, b * D_C:(b + 1) * D_C] = jnp.sum(z * alt, axis=0, keepdims=True)

    tf = c_ref.shape[0]

    @pl.when(ph == 0)
    def _():
        zr = jnp.dot(c_ref[...], zb_ref[...], preferred_element_type=F32)
        zs = jnp.dot(s_ref[...], zb_ref[...], preferred_element_type=F32)
        hr = hr_ref[...]
        hi = hi_ref[...]
        row = lax.broadcasted_iota(jnp.int32, (tf, 1), 0) + i * tf
        wgt = jnp.where(row == 0, inv_n, 2.0 * inv_n)
        rows = pl.ds(pl.multiple_of(i * tf, tf), tf)
        for b in range(BG):
            sl = slice(b * D_C, (b + 1) * D_C)
            yr_ref[rows, sl] = ((zr[:, sl] * hr + zs[:, sl] * hi) * wgt).astype(BF16)
            yi_ref[rows, sl] = ((zr[:, sl] * hi - zs[:, sl] * hr) * (2.0 * inv_n)).astype(BF16)

        @pl.when(i == 0)
        def _():
            for b in range(BG):
                sl = slice(b * D_C, (b + 1) * D_C)
                nyq_ref[:, sl] = nyq_ref[:, sl] * hi[0:1, :] * inv_n

    @pl.when(ph == 1)
    def _():
        y = (jnp.dot(c_ref[...], yr_ref[...], preferred_element_type=F32)
             - jnp.dot(s_ref[...], yi_ref[...], preferred_element_type=F32))
        row = lax.broadcasted_iota(jnp.int32, (tf, 1), 0) + i * tf
        alt = 1.0 - 2.0 * (row % 2).astype(F32)
        y = y + alt * nyq_ref[...]
        for b in range(BG):
            sl = slice(b * D_C, (b + 1) * D_C)
            o_ref[b] = gate_ref[b] * (y[:, sl] + bias_ref[...] * zt_ref[b])


def _hconv(z, gate, hr, hi, bias, order, tabs):
    B, L, _ = z.shape
    _, ch, _, sh, _ = tabs
    BG = min(B, 8) if L <= 256 else min(B, 4)
    assert B % BG == 0
    tf = 256
    nt = L // tf
    mat = pl.BlockSpec((tf, L), lambda g, ph, i: (i, 0))
    tile = pl.BlockSpec((BG, tf, D_C), lambda g, ph, i: (g, i * ph, 0))
    hspec = pl.BlockSpec((tf, D_C), lambda g, ph, i: (i * (1 - ph), order))
    cols = BG * D_C
    return pl.pallas_call(
        functools.partial(_hconv_kernel, BG=BG, L=L), grid=(B // BG, 2, nt),
        in_specs=[pl.BlockSpec((BG, L, D_C), lambda g, ph, i: (g, 0, 0)), tile, tile, hspec, hspec,
                  pl.BlockSpec((1, D_C), lambda g, ph, i: (0, 0)), mat, mat],
        out_specs=tile,
        out_shape=jax.ShapeDtypeStruct((B, L, D_C), F32),
        scratch_shapes=[pltpu.VMEM((L, cols), BF16)] * 3 + [pltpu.VMEM((1, cols), F32)],
        compiler_params=_cparams(3, 48 * 1024 * 1024), name="hyena_conv",
    )(z, z, gate, hr, hi, bias[order].reshape(1, D_C), ch, sh)


def _outproj_kernel(x_ref, yf_ref, yb_ref, bon_ref, g_ref, ob_ref, yc_ref, mod_ref,
                    lnw_ref, lnb_ref, gpost_ref, gffn_ref, w_ref, x1_ref, h2_ref):
    bd = _block_diag(D_A, RW_HEAD)
    y = yf_ref[0] + yb_ref[0]
    mu = _segsum(y, bd) * (1.0 / RW_HEAD)
    yc = y - mu
    var = _segsum(yc * yc, bd) * (1.0 / RW_HEAD)
    yn = yc * lax.rsqrt(var + GN_EPS) * lnw_ref[...] + lnb_ref[...]
    ya = (yn + bon_ref[0]) * g_ref[0]
    mix = (jnp.dot(ya.astype(BF16), w_ref[0:D_A, :], preferred_element_type=F32)
           + jnp.dot(ob_ref[0].astype(BF16), w_ref[D_A:D_A + D_B, :], preferred_element_type=F32)
           + jnp.dot(yc_ref[0].astype(BF16), w_ref[D_A + D_B:D_MODEL, :], preferred_element_type=F32))
    mod = mod_ref[0]
    gt1 = mod[:, 2 * D_MODEL:3 * D_MODEL]
    sh2 = mod[:, 3 * D_MODEL:4 * D_MODEL]
    sc2 = mod[:, 4 * D_MODEL:5 * D_MODEL]
    x1 = x_ref[0] + gt1 * _rms(mix, gpost_ref[...])
    x1_ref[0] = x1
    h2_ref[0] = (_rms(x1, gffn_ref[...]) * (1.0 + sc2) + sh2).astype(BF16)


def _outproj(x, yf, yb, bon, g, ob, yc, mod, p, w_out):
    B, L, _ = x.shape
    mb = mod.shape[0]
    row = lambda b, i: (b, i, 0)
    c2 = lambda b, i: (0, 0)
    t = lambda n: pl.BlockSpec((1, TL, n), row)
    in_specs = [t(D_MODEL), t(D_A), t(D_A), t(D_A), t(D_A), t(D_B), t(D_C),
                pl.BlockSpec((1, 1, 6 * D_MODEL), (lambda b, i: (b, 0, 0)) if mb > 1 else (lambda b, i: (0, 0, 0))),
                pl.BlockSpec((1, D_A), c2), pl.BlockSpec((1, D_A), c2),
                pl.BlockSpec((1, D_MODEL), c2), pl.BlockSpec((1, D_MODEL), c2),
                pl.BlockSpec((D_MODEL, D_MODEL), c2)]
    return pl.pallas_call(
        _outproj_kernel, grid=(B, L // TL), in_specs=in_specs,
        out_specs=[t(D_MODEL), t(D_MODEL)],
        out_shape=[jax.ShapeDtypeStruct((B, L, D_MODEL), F32), jax.ShapeDtypeStruct((B, L, D_MODEL), BF16)],
        compiler_params=_cparams(2, 40 * 1024 * 1024), name="outproj",
    )(x, yf, yb, bon, g, ob, yc, mod, p['rwkv_ln_w'].reshape(1, -1), p['rwkv_ln_b'].reshape(1, -1),
      p['g_mix_post'].reshape(1, -1), p['g_ffn_pre'].reshape(1, -1), w_out)


def _ffn_kernel(h_ref, x_ref, mod_ref, g_ref, w1_ref, w2_ref, o_ref):
    h = h_ref[0]
    acc = None
    ck = 1024
    for c in range(D_FF // ck):
        a = jnp.dot(h, w1_ref[:, c * ck:(c + 1) * ck], preferred_element_type=F32)
        a = jnp.square(jnp.maximum(a, 0.0)).astype(BF16)
        part = jnp.dot(a, w2_ref[c * ck:(c + 1) * ck, :], preferred_element_type=F32)
        acc = part if acc is None else acc + part
    gt2 = mod_ref[0][:, 5 * D_MODEL:6 * D_MODEL]
    o_ref[0] = x_ref[0] + gt2 * _rms(acc, g_ref[...])


def _ffn(h2, x1, mod, g_post, w1, w2):
    B, L, _ = x1.shape
    mb = mod.shape[0]
    row = lambda b, i: (b, i, 0)
    c2 = lambda b, i: (0, 0)
    return pl.pallas_call(
        _ffn_kernel, grid=(B, L // TL),
        in_specs=[pl.BlockSpec((1, TL, D_MODEL), row), pl.BlockSpec((1, TL, D_MODEL), row),
                  pl.BlockSpec((1, 1, 6 * D_MODEL), (lambda b, i: (b, 0, 0)) if mb > 1 else (lambda b, i: (0, 0, 0))),
                  pl.BlockSpec((1, D_MODEL), c2),
                  pl.BlockSpec((D_MODEL, D_FF), c2), pl.BlockSpec((D_FF, D_MODEL), c2)],
        out_specs=pl.BlockSpec((1, TL, D_MODEL), row),
        out_shape=jax.ShapeDtypeStruct((B, L, D_MODEL), F32),
        compiler_params=_cparams(2, V7X_VMEM_LIMIT), name="ffn",
    )(h2, x1, mod, g_post.reshape(1, -1), w1, w2)


def _rope_tables(L):
    rows = L // GRID_W
    row = jnp.repeat(jnp.arange(rows), GRID_W).astype(F32)
    col = jnp.tile(jnp.arange(GRID_W), rows).astype(F32)
    half = DIFF_DH // 2
    inv = ROPE_BASE ** (-jnp.arange(0, half, 2, dtype=F32) / half)
    ang_r = row[:, None] * inv[None]
    ang_c = col[:, None] * inv[None]
    ang = jnp.concatenate([ang_r, ang_r, ang_c, ang_c], axis=-1)
    cos, sin = jnp.cos(ang), jnp.sin(ang)
    quarter = (jnp.arange(DIFF_DH) // 16) % 2
    sin_a = jnp.where(quarter == 0, -sin, 0.0)
    sin_b = jnp.where(quarter == 1, sin, 0.0)
    rep = D_B // DIFF_DH
    return tuple(jnp.tile(t, (1, rep)) for t in (cos, sin_a, sin_b))


def _hyena_tables(L):
    t = jnp.linspace(0.0, 1.0, L, dtype=F32)[:, None]
    ang = (2.0 * math.pi / L) * jnp.arange(L, dtype=F32)[:, None]
    bands = jnp.linspace(1e-4, HY_BANDS - 1, HY_BANDS, dtype=F32)[None, :]
    emb = jnp.concatenate([t, jnp.cos(bands * ang), -jnp.sin(bands * ang)], axis=-1)
    emb = jnp.pad(emb, ((0, 0), (0, 128 - HY_EMB)))
    n = jnp.arange(L, dtype=jnp.int32)
    k = (n[:, None] * n[None, :]) % (2 * L)
    theta = k.astype(F32) * (math.pi / L)
    c = jnp.cos(theta)
    s = jnp.sin(theta)
    ch, cl = _split2(c)
    sh, sl = _split2(s)
    return emb, ch, cl, sh, sl


def _layer(x, mod, p, wb, l, lam_init, cache, rope_tabs, hy_tabs, filt):
    B, L, _ = x.shape
    ua, uq, uk, uv, uh = _inproj(x, mod, p['g_mix_pre'], wb['w_in'], rope_tabs)
    (r, v, nkk, wf, kaf, kdf, wbk, kab, kdb, g, bon, x1h, x2h, hv) = _prep(ua, uh, p)
    if cache is None:
        s0 = jnp.zeros((B, 2, H_A, RW_HEAD, RW_HEAD), F32)
        kv_cache = None
    else:
        s0 = cache[0]
        kv_cache = (cache[1], cache[2])
    yf, yb, sfin = _scan(nkk, r, v, wf, kaf, kdf, wbk, kab, kdb, s0)
    ob = _attention(uq, uk, uv, kv_cache, l, p, lam_init)
    hr, hi = filt
    z1 = _hconv(hv, x1h, hr, hi, p['hy_bias'], 0, hy_tabs)
    yc = _hconv(z1, x2h, hr, hi, p['hy_bias'], 1, hy_tabs)
    x1, h2 = _outproj(x, yf, yb, bon, g, ob, yc, mod, p, wb['w_out'])
    x2 = _ffn(h2, x1, mod, p['g_ffn_post'], wb['w_ff1'], wb['w_ff2'])
    return x2, (sfin, uk, uv)


_LAYER_KEYS = ('g_mix_pre', 'g_mix_post', 'g_ffn_pre', 'g_ffn_post', 'rwkv_conv', 'rwkv_w0', 'rwkv_w2', 'rwkv_a0',
               'rwkv_a2', 'rwkv_g2', 'rwkv_kk', 'rwkv_ka', 'rwkv_rk', 'rwkv_ln_w', 'rwkv_ln_b', 'diff_lq1', 'diff_lk1',
               'diff_lq2', 'diff_lk2', 'diff_subln', 'hy_conv_w', 'hy_conv_b', 'hy_w1', 'hy_b1', 'hy_freq', 'hy_w2',
               'hy_b2', 'hy_w3', 'hy_decay', 'hy_bias')


def kernel(x_prompt, x_sample, state_rwkv, cache_k, cache_v, c, c_ctx, w_mod, b_mod, g_mix_pre, g_mix_post, g_ffn_pre, g_ffn_post, w_in, rwkv_conv, rwkv_w0, rwkv_w2, rwkv_a0, rwkv_a2, rwkv_g2, rwkv_kk, rwkv_ka, rwkv_rk, rwkv_ln_w, rwkv_ln_b, diff_lq1, diff_lk1, diff_lq2, diff_lk2, diff_subln, hy_conv_w, hy_conv_b, hy_w1, hy_b1, hy_freq, hy_w2, hy_b2, hy_w3, hy_decay, hy_bias, w_out, w_ff1, w_ff2):
    stacked = dict(g_mix_pre=g_mix_pre, g_mix_post=g_mix_post, g_ffn_pre=g_ffn_pre, g_ffn_post=g_ffn_post,
                   rwkv_conv=rwkv_conv, rwkv_w0=rwkv_w0, rwkv_w2=rwkv_w2, rwkv_a0=rwkv_a0, rwkv_a2=rwkv_a2,
                   rwkv_g2=rwkv_g2, rwkv_kk=rwkv_kk, rwkv_ka=rwkv_ka, rwkv_rk=rwkv_rk, rwkv_ln_w=rwkv_ln_w,
                   rwkv_ln_b=rwkv_ln_b, diff_lq1=diff_lq1, diff_lk1=diff_lk1, diff_lq2=diff_lq2, diff_lk2=diff_lk2,
                   diff_subln=diff_subln, hy_conv_w=hy_conv_w, hy_conv_b=hy_conv_b, hy_w1=hy_w1, hy_b1=hy_b1,
                   hy_freq=hy_freq, hy_w2=hy_w2, hy_b2=hy_b2, hy_w3=hy_w3, hy_decay=hy_decay, hy_bias=hy_bias)
    Bc, Lc, _ = x_prompt.shape
    Bs, Ls, _ = x_sample.shape
    past = cache_k.shape[2]

    cond = jnp.zeros((8, D_MODEL), F32).at[0:Bs].set(c).at[Bs].set(c_ctx)
    mod_all = _modulation(cond, w_mod, b_mod)

    rope_tabs = _rope_tables(Ls)
    tabs_c = _hyena_tables(Lc)
    tabs_s = _hyena_tables(Ls)
    ck = cache_k.reshape(Bs, DEPTH, past, D_B)
    cv = cache_v.reshape(Bs, DEPTH, past, D_B)

    xp, xs = x_prompt, x_sample
    st_list, k_list, v_list = [], [], []
    for l in range(DEPTH):
        p = {k: stacked[k][l] for k in _LAYER_KEYS}
        wb = dict(w_in=w_in[l].astype(BF16), w_out=w_out[l].astype(BF16),
                  w_ff1=w_ff1[l].astype(BF16), w_ff2=w_ff2[l].astype(BF16))
        lam_init = 0.8 - 0.6 * math.exp(-0.3 * l)
        mod_lat = mod_all[l, 0:Bs].reshape(Bs, 1, 6 * D_MODEL)
        mod_ctx = mod_all[l, Bs:Bs + 1].reshape(1, 1, 6 * D_MODEL)
        filt_c = _hyena_filters(Lc, p, tabs_c)
        filt_s = _hyena_filters(Ls, p, tabs_s)
        xp, (s_ctx, k_ctx, v_ctx) = _layer(xp, mod_ctx, p, wb, l, lam_init, None, None, tabs_c, filt_c)
        st_list.append(s_ctx)
        k_list.append(k_ctx.reshape(Bc, Lc, H_B, 2, DIFF_DH))
        v_list.append(v_ctx.reshape(Bc, Lc, H_B, 2 * DIFF_DH))
        s0 = state_rwkv[:, l]
        xs, _ = _layer(xs, mod_lat, p, wb, l, lam_init, (s0, ck, cv), rope_tabs, tabs_s, filt_s)
    return (xp, xs, jnp.stack(st_list, axis=1), jnp.stack(k_list, axis=1), jnp.stack(v_list, axis=1))
```

```python
import functools
import math

import jax
import jax.numpy as jnp
from jax import lax
from jax.experimental import pallas as pl
from jax.experimental.pallas import tpu as pltpu

F32 = jnp.float32
BF16 = jnp.bfloat16

D_MODEL = 1024
DEPTH = 4
GRID_W = 64
D_A = 256
RW_HEAD = 64
H_A = 4
D_B = 512
DIFF_DH = 64
H_B = 4
D_C = 256
HY_BANDS = 16
HY_EMB = 33
HY_FFN = 64
D_FF = 4096
ROPE_BASE = 10000.0
RMS_EPS = 1e-6
GN_EPS = 64e-5
N_A_COLS = 1152
IN_COLS = 3456

TL = 256
V7X_VMEM_LIMIT = 56 * 1024 * 1024


def _cparams(n_axes, vmem=None):
    return pltpu.CompilerParams(dimension_semantics=("arbitrary",) * n_axes,
                                vmem_limit_bytes=vmem)


def _split3(a):
    hi = a.astype(BF16)
    r1 = a - hi.astype(F32)
    mid = r1.astype(BF16)
    lo = (r1 - mid.astype(F32)).astype(BF16)
    return hi, mid, lo


def _split2(a):
    hi = a.astype(BF16)
    lo = (a - hi.astype(F32)).astype(BF16)
    return hi, lo


def _segsum(a, bd):
    hi, mid, lo = _split3(a)
    return (jnp.dot(hi, bd, preferred_element_type=F32) + jnp.dot(mid, bd, preferred_element_type=F32)
            + jnp.dot(lo, bd, preferred_element_type=F32))


def _block_diag(n, seg):
    r = lax.broadcasted_iota(jnp.int32, (n, n), 0) // seg
    c = lax.broadcasted_iota(jnp.int32, (n, n), 1) // seg
    return (r == c).astype(BF16)


def _layer_weight_spec(w, n_grid_axes):
    stack, layer = w
    _, k, n = stack.shape
    if n_grid_axes == 2:
        return pl.BlockSpec((None, k, n), lambda b, i: (layer, 0, 0))
    raise NotImplementedError(n_grid_axes)


def _rms(x, g):
    return x * lax.rsqrt(jnp.mean(x * x, axis=-1, keepdims=True) + RMS_EPS) * g


def _mod_kernel(cond_ref, w_ref, b_ref, o_ref):
    c = cond_ref[...]
    s = (c * jax.nn.sigmoid(c)).astype(BF16)
    o_ref[0] = jnp.dot(s, w_ref[0].astype(BF16), preferred_element_type=F32) + b_ref[0]


def _modulation(cond, w_mod, b_mod):
    tn = 1536
    return pl.pallas_call(
        _mod_kernel,
        grid=(DEPTH, 6 * D_MODEL // tn),
        in_specs=[pl.BlockSpec((8, D_MODEL), lambda l, n: (0, 0)),
                  pl.BlockSpec((1, D_MODEL, tn), lambda l, n: (l, 0, n)),
                  pl.BlockSpec((1, 1, tn), lambda l, n: (l, 0, n))],
        out_specs=pl.BlockSpec((1, 8, tn), lambda l, n: (l, 0, n)),
        out_shape=jax.ShapeDtypeStruct((DEPTH, 8, 6 * D_MODEL), F32),
        compiler_params=_cparams(2, 40 * 1024 * 1024),
        name="modulation",
    )(cond, w_mod, b_mod.reshape(DEPTH, 1, 6 * D_MODEL))


def _rope(x, cos, sin_a, sin_b):
    n = x.shape[-1]
    return x * cos + pltpu.roll(x, n - 16, axis=1) * sin_a + pltpu.roll(x, 16, axis=1) * sin_b


def _inproj_kernel(*refs, rope):
    if rope:
        x_ref, mod_ref, g_ref, w_ref, cos_ref, sa_ref, sb_ref, ua_ref, uq_ref, uk_ref, uv_ref, uh_ref = refs
    else:
        x_ref, mod_ref, g_ref, w_ref, ua_ref, uq_ref, uk_ref, uv_ref, uh_ref = refs
    x = x_ref[0]
    mod = mod_ref[0]
    sh1 = mod[:, 0:D_MODEL]
    sc1 = mod[:, D_MODEL:2 * D_MODEL]
    h = _rms(x, g_ref[...]) * (1.0 + sc1) + sh1
    u = jnp.dot(h.astype(BF16), w_ref[...], preferred_element_type=F32)
    ua_ref[0] = u[:, 0:N_A_COLS]
    q = u[:, N_A_COLS:N_A_COLS + D_B] * (DIFF_DH ** -0.5 * math.log2(math.e))
    k = u[:, N_A_COLS + D_B:N_A_COLS + 2 * D_B]
    if rope:
        q = _rope(q, cos_ref[...], sa_ref[...], sb_ref[...])
        k = _rope(k, cos_ref[...], sa_ref[...], sb_ref[...])
    uq_ref[0] = q.astype(uq_ref.dtype)
    uk_ref[0] = k.astype(uk_ref.dtype)
    uv_ref[0] = u[:, N_A_COLS + 2 * D_B:N_A_COLS + 3 * D_B].astype(uv_ref.dtype)
    uh_ref[0] = u[:, N_A_COLS + 3 * D_B:IN_COLS]


def _inproj(x, mod, g_pre, w_in, rope_tabs):
    B, L, _ = x.shape
    rope = rope_tabs is not None
    mb = mod.shape[0]
    row = lambda b, i: (b, i, 0)
    const2 = lambda b, i: (0, 0)
    in_specs = [pl.BlockSpec((1, TL, D_MODEL), row),
                pl.BlockSpec((1, 1, 6 * D_MODEL), (lambda b, i: (b, 0, 0)) if mb > 1 else (lambda b, i: (0, 0, 0))),
                pl.BlockSpec((1, D_MODEL), const2),
                _layer_weight_spec(w_in, 2)]
    args = [x, mod, g_pre.reshape(1, D_MODEL), w_in[0]]
    if rope:
        in_specs += [pl.BlockSpec((TL, D_B), lambda b, i: (i, 0))] * 3
        args += list(rope_tabs)
    qkv_dt = BF16 if rope else F32
    out_shape = [jax.ShapeDtypeStruct((B, L, N_A_COLS), F32),
                 jax.ShapeDtypeStruct((B, L, D_B), qkv_dt),
                 jax.ShapeDtypeStruct((B, L, D_B), qkv_dt),
                 jax.ShapeDtypeStruct((B, L, D_B), qkv_dt),
                 jax.ShapeDtypeStruct((B, L, 3 * D_C), F32)]
    out_specs = [pl.BlockSpec((1, TL, N_A_COLS), row), pl.BlockSpec((1, TL, D_B), row),
                 pl.BlockSpec((1, TL, D_B), row), pl.BlockSpec((1, TL, D_B), row),
                 pl.BlockSpec((1, TL, 3 * D_C), row)]
    return pl.pallas_call(
        functools.partial(_inproj_kernel, rope=rope),
        grid=(B, L // TL), in_specs=in_specs, out_specs=out_specs, out_shape=out_shape,
        compiler_params=_cparams(2, 48 * 1024 * 1024), name="inproj",
    )(*args)


def _shift_conv(u, prev_row, next_row, cw):
    n = u.shape[0]
    row = lax.broadcasted_iota(jnp.int32, (n, 1), 0)
    up = jnp.where(row == 0, prev_row, pltpu.roll(u, 1, axis=0))
    un = jnp.where(row == n - 1, next_row, pltpu.roll(u, n - 1, axis=0))
    return cw[0:1] * up + cw[1:2] * u + cw[2:3] * un


def _prep_kernel(ua_ref, uap_ref, uan_ref, uh_ref, uhp_ref, uhn_ref,
                 cw_ref, w0_ref, w2_ref, a0_ref, a2_ref, g2_ref, kkw_ref, kaw_ref, rk_ref, hcw_ref, hcb_ref,
                 r_ref, v_ref, nkk_ref, wf_ref, kaf_ref, kdf_ref, wb_ref, kab_ref, kdb_ref, g_ref, bon_ref,
                 x1_ref, x2_ref, hv_ref):
    i = pl.program_id(1)
    nt = pl.num_programs(1)
    has_prev = i > 0
    has_next = i < nt - 1
    bd = _block_diag(D_A, RW_HEAD)

    ua = ua_ref[0]
    u_rkv = ua[:, 0:3 * D_A]
    prev = jnp.where(has_prev, uap_ref[0, 7:8, :], 0.0)
    nxt = jnp.where(has_next, uan_ref[0, 0:1, :], 0.0)
    rkv = _shift_conv(u_rkv, prev, nxt, cw_ref[...])
    r = rkv[:, 0:D_A]
    k = rkv[:, D_A:2 * D_A]
    v = rkv[:, 2 * D_A:3 * D_A]
    u_w = ua[:, 768:896]
    u_a = ua[:, 896:1024]
    u_g = ua[:, 1024:1152]

    g = jnp.dot(jax.nn.sigmoid(u_g).astype(BF16), g2_ref[...].astype(BF16), preferred_element_type=F32)
    kk = k * kkw_ref[...]
    kk = kk * lax.rsqrt(_segsum(kk * kk, bd) + 1e-12)
    r_ref[0] = r
    v_ref[0] = v
    nkk_ref[0] = -kk
    g_ref[0] = g

    kd_sum = None
    outs = ((wf_ref, kaf_ref, kdf_ref), (wb_ref, kab_ref, kdb_ref))
    for d in range(2):
        uw_d = u_w[:, d * 64:(d + 1) * 64]
        ua_d = u_a[:, d * 64:(d + 1) * 64]
        xw = w0_ref[d:d + 1, :] + jnp.dot(jnp.tanh(uw_d).astype(BF16), w2_ref[d].astype(BF16),
                                          preferred_element_type=F32)
        z = -xw
        softplus = jnp.maximum(z, 0.0) + jnp.log(1.0 + jnp.exp(-jnp.abs(z)))
        wl = -softplus - 0.5
        log_decay = -jnp.exp(wl)
        a = jax.nn.sigmoid(a0_ref[d:d + 1, :] + jnp.dot(ua_d.astype(BF16), a2_ref[d].astype(BF16),
                                                        preferred_element_type=F32))
        kd = k * (1.0 + (a - 1.0) * kaw_ref[...])
        w_o, ka_o, kd_o = outs[d]
        w_o[0] = log_decay
        ka_o[0] = kk * a
        kd_o[0] = kd
        kd_sum = kd if kd_sum is None else kd_sum + kd
    bon_ref[0] = _segsum(r * kd_sum * rk_ref[...], bd) * v

    uh = uh_ref[0]
    prevh = jnp.where(has_prev, uhp_ref[0, 7:8, :], 0.0)
    nxth = jnp.where(has_next, uhn_ref[0, 0:1, :], 0.0)
    hc = _shift_conv(uh, prevh, nxth, hcw_ref[...]) + hcb_ref[...]
    x1_ref[0] = hc[:, 0:D_C]
    x2_ref[0] = hc[:, D_C:2 * D_C]
    hv_ref[0] = hc[:, 2 * D_C:3 * D_C]


def _prep(ua, uh, p):
    B, L, _ = ua.shape
    nb8 = L // 8
    t8 = TL // 8
    row = lambda b, i: (b, i, 0)
    prev = lambda b, i: (b, jnp.maximum(i * t8 - 1, 0), 0)
    nxt = lambda b, i: (b, jnp.minimum((i + 1) * t8, nb8 - 1), 0)

    def full(a):
        nd = a.ndim
        return pl.BlockSpec(a.shape, lambda b, i: (0,) * nd)

    params = [p['rwkv_conv'], p['rwkv_w0'], p['rwkv_w2'], p['rwkv_a0'], p['rwkv_a2'], p['rwkv_g2'],
              p['rwkv_kk'].reshape(1, D_A), p['rwkv_ka'].reshape(1, D_A), p['rwkv_rk'].reshape(1, D_A),
              p['hy_conv_w'], p['hy_conv_b'].reshape(1, 3 * D_C)]
    in_specs = [pl.BlockSpec((1, TL, N_A_COLS), row), pl.BlockSpec((1, 8, 3 * D_A), prev),
                pl.BlockSpec((1, 8, 3 * D_A), nxt),
                pl.BlockSpec((1, TL, 3 * D_C), row), pl.BlockSpec((1, 8, 3 * D_C), prev),
                pl.BlockSpec((1, 8, 3 * D_C), nxt)] + [full(a) for a in params]
    n_out = 14
    return pl.pallas_call(
        _prep_kernel, grid=(B, L // TL), in_specs=in_specs,
        out_specs=[pl.BlockSpec((1, TL, D_A), row)] * n_out,
        out_shape=[jax.ShapeDtypeStruct((B, L, D_A), F32)] * n_out,
        compiler_params=_cparams(2, 40 * 1024 * 1024), name="prep",
    )(ua, ua, ua, uh, uh, uh, *params)


CHUNK = 64
CHUNKS_PER_ITER = 2
_NN = (((1,), (0,)), ((), ()))
_NT = (((1,), (1,)), ((), ()))
_TN = (((0,), (0,)), ((), ()))


def _mm3(a, b, dims):
    ah, al = _split2(a)
    bh, bl = _split2(b)
    f = lambda x, y: lax.dot_general(x, y, dims, preferred_element_type=F32)
    if dims is _TN:
        return f(ah, bh) + f(ah, bl) + f(al, bh)
    m = a.shape[0]
    both = f(jnp.concatenate([ah, al], axis=0), bh)
    return both[0:m] + both[m:2 * m] + f(ah, bl)


def _mm2(a, b, dims):
    ah = a.astype(BF16)
    bh, bl = _split2(b)
    f = lambda x, y: lax.dot_general(x, y, dims, preferred_element_type=F32)
    return f(ah, bh) + f(ah, bl)


def _mm1(a, b, dims):
    return lax.dot_general(a.astype(BF16), b.astype(BF16), dims, preferred_element_type=F32)


def _unit_tri_inverses(ms, row, col):
    eye = (row == col).astype(F32)
    blk = lambda n: (row // n) == (col // n)
    m8 = [jnp.where(blk(8), m, 0.0) for m in ms]
    a2 = [_mm1(x, x, _NN) for x in m8]
    a4 = [_mm1(x, x, _NN) for x in a2]
    p = [eye + x + y + _mm1(x, y, _NN) for x, y in zip(m8, a2)]
    ts = [x + _mm1(x, y, _NN) for x, y in zip(p, a4)]
    n = 8
    while n < CHUNK:
        sel = jnp.logical_and(blk(2 * n), jnp.logical_not(blk(n)))
        ot = [_mm1(jnp.where(sel, m, 0.0), t, _NN) for m, t in zip(ms, ts)]
        ts = [t + _mm1(t, o, _NN) for t, o in zip(ts, ot)]
        n *= 2
    return ts


def _chunk_kernel(nkk_ref, r_ref, v_ref, lwf_ref, kaf_ref, kdf_ref, lwb_ref, kab_ref, kdb_ref,
                  rpf_ref, ypf_ref, ff_ref, gf_ref, pcf_ref, rpb_ref, ypb_ref, fb_ref, gb_ref, pcb_ref):
    C = CHUNK
    row = lax.broadcasted_iota(jnp.int32, (C, C), 0)
    col = lax.broadcasted_iota(jnp.int32, (C, C), 1)
    dirs = ((lwf_ref, kaf_ref, kdf_ref, rpf_ref, ypf_ref, ff_ref, gf_ref, pcf_ref),
            (lwb_ref, kab_ref, kdb_ref, rpb_ref, ypb_ref, fb_ref, gb_ref, pcb_ref))
    n_chunks = nkk_ref.shape[1] // C

    def build(c, probs):
        rows = pl.ds(pl.multiple_of(c * C, C), C)
        alpha = nkk_ref[0, rows, :]
        r = r_ref[0, rows, :]
        v = v_ref[0, rows, :]
        for d in range(2):
            lw_ref, b_ref, k_ref = dirs[d][0:3]
            before = (row > col) if d == 0 else (row < col)
            upto = (row >= col) if d == 0 else (row <= col)
            lw = lw_ref[0, rows, :]
            beta = b_ref[0, rows, :]
            kd = k_ref[0, rows, :]
            tri = upto.astype(BF16)
            l_hi, l_mid, l_lo = _split3(lw)
            lam = (jnp.dot(tri, l_hi, preferred_element_type=F32) + jnp.dot(tri, l_mid, preferred_element_type=F32)
                   + jnp.dot(tri, l_lo, preferred_element_type=F32))
            lam_prev = lam - lw
            lam_mid = lam[C // 2:C // 2 + 1, :]
            lam_end = lam[C - 1:C, :] if d == 0 else lam[0:1, :]
            e_to_mid = jnp.exp(lam_mid - lam)
            a_m = alpha * jnp.exp(lam_prev - lam_mid)
            b_m = beta * e_to_mid
            k_m = kd * e_to_mid
            r_m = r * jnp.exp(lam - lam_mid)
            a_0 = alpha * jnp.exp(lam_prev)
            r_0 = r * jnp.exp(lam)
            e_to_end = jnp.exp(lam_end - lam)
            b_e = beta * e_to_end
            k_e = kd * e_to_end
            dirs[d][7][0, pl.ds(c, 1)] = jnp.exp(lam_end).reshape(1, 1, D_A)
            for h in range(H_A):
                sl = slice(h * RW_HEAD, (h + 1) * RW_HEAD)
                probs.append(dict(
                    c=c, rows=rows, d=d, h=h, before=before, upto=upto,
                    ar=jnp.concatenate([a_m[:, sl], r_m[:, sl]], axis=0),
                    bk=jnp.concatenate([b_m[:, sl], k_m[:, sl]], axis=0),
                    a0=a_0[:, sl], r0=r_0[:, sl], vh=v[:, sl],
                    bke=jnp.concatenate([b_e[:, sl], k_e[:, sl]], axis=0)))

    def body(ci, carry):
        probs = []
        for cc in range(CHUNKS_PER_ITER):
            build(ci * CHUNKS_PER_ITER + cc, probs)
        g12 = [_mm1(q['ar'], q['bk'], _NT) for q in probs]
        ms = [jnp.where(q['before'], g[0:C, 0:C], 0.0) for q, g in zip(probs, g12)]
        ns = [jnp.where(q['before'], g[0:C, C:2 * C], 0.0) for q, g in zip(probs, g12)]
        rbk = [jnp.concatenate([jnp.where(q['upto'], g[C:2 * C, 0:C], 0.0),
                                jnp.where(q['upto'], g[C:2 * C, C:2 * C], 0.0)], axis=1) for q, g in zip(probs, g12)]
        nvs = [_mm1(n, q['vh'], _NN) for q, n in zip(probs, ns)]
        ts = _unit_tri_inverses(ms, row, col)
        xs = [_mm2(t, jnp.concatenate([q['a0'], nv], axis=1), _NN) for q, t, nv in zip(probs, ts, nvs)]
        zeros = jnp.zeros((C, RW_HEAD), F32)
        ws = [jnp.concatenate([x, jnp.concatenate([zeros, q['vh']], axis=1)], axis=0) for q, x in zip(probs, xs)]
        rys = [_mm1(rb, w, _NN) for rb, w in zip(rbk, ws)]
        fgs = [_mm3(w, q['bke'], _TN) for q, w in zip(probs, ws)]
        for q, fg in zip(probs, fgs):
            f_ref, g_ref = dirs[q['d']][5], dirs[q['d']][6]
            f_ref[0, pl.ds(q['c'], 1), q['h']] = fg[0:RW_HEAD].reshape(1, RW_HEAD, RW_HEAD)
            g_ref[0, pl.ds(q['c'], 1), q['h']] = fg[RW_HEAD:2 * RW_HEAD].reshape(1, RW_HEAD, RW_HEAD)
        for k0 in range(0, len(probs), H_A):
            sub = list(zip(probs[k0:k0 + H_A], rys[k0:k0 + H_A]))
            q0 = sub[0][0]
            dirs[q0['d']][3][0, q0['rows'], :] = jnp.concatenate([q['r0'] + ry[:, 0:RW_HEAD] for q, ry in sub], axis=1)
            dirs[q0['d']][4][0, q0['rows'], :] = jnp.concatenate([ry[:, RW_HEAD:2 * RW_HEAD] for q, ry in sub], axis=1)
        return carry

    lax.fori_loop(0, n_chunks // CHUNKS_PER_ITER, body, 0)


def _state_kernel(rpf_ref, ypf_ref, ff_ref, gf_ref, pcf_ref, rpb_ref, ypb_ref, fb_ref, gb_ref, pcb_ref, s0_ref,
                  yf_ref, yb_ref, sfin_ref, s_ref):
    C = CHUNK
    j = pl.program_id(1)
    nj = pl.num_programs(1)
    n_chunks = rpf_ref.shape[1] // C

    @pl.when(j == 0)
    def _():
        s_ref[...] = s0_ref[0]

    dirs = ((rpf_ref, ypf_ref, ff_ref, gf_ref, pcf_ref, yf_ref), (rpb_ref, ypb_ref, fb_ref, gb_ref, pcb_ref, yb_ref))
    for cc in range(n_chunks):
        for d in range(2):
            rp_ref, yp_ref, f_ref, g_ref, pc_ref, y_ref = dirs[d]
            c = cc if d == 0 else n_chunks - 1 - cc
            rows = slice(c * C, (c + 1) * C)
            pc = pc_ref[0, c]
            y_parts = []
            for h in range(H_A):
                sl = slice(h * RW_HEAD, (h + 1) * RW_HEAD)
                s = s_ref[d, h]
                y_parts.append(yp_ref[0, rows, sl] + _mm1(rp_ref[0, rows, sl], s, _NT))
                s_ref[d, h] = s * pc[:, sl] + _mm1(s, f_ref[0, c, h], _NN) + g_ref[0, c, h]
            y_ref[0, rows, :] = jnp.concatenate(y_parts, axis=1)

    @pl.when(j == nj - 1)
    def _():
        sfin_ref[0] = s_ref[...]


def _scan(nkk, r, v, lwf, kaf, kdf, lwb, kab, kdb, s0):
    B, L, _ = r.shape
    nj = L // TL
    cpb = TL // CHUNK
    nc = L // CHUNK
    row = lambda b, j: (b, j, 0)
    tile = pl.BlockSpec((1, TL, D_A), row)
    mats = pl.BlockSpec((1, cpb, H_A, RW_HEAD, RW_HEAD), lambda b, j: (b, j, 0, 0, 0))
    pcs = pl.BlockSpec((1, cpb, 1, D_A), lambda b, j: (b, j, 0, 0))
    f32 = lambda *s: jax.ShapeDtypeStruct(s, F32)
    per_dir_shapes = [f32(B, L, D_A), f32(B, L, D_A), f32(B, nc, H_A, RW_HEAD, RW_HEAD),
                      f32(B, nc, H_A, RW_HEAD, RW_HEAD), f32(B, nc, 1, D_A)]
    per_dir_specs = [tile, tile, mats, mats, pcs]
    outs = pl.pallas_call(
        _chunk_kernel, grid=(B, nj), in_specs=[tile] * 9,
        out_specs=per_dir_specs * 2, out_shape=per_dir_shapes * 2,
        compiler_params=_cparams(2, 40 * 1024 * 1024), name="rwkv_chunk",
    )(nkk, r, v, lwf, kaf, kdf, lwb, kab, kdb)
    rpf, ypf, ff, gf, pcf, rpb, ypb, fb, gb, pcb = outs

    rrow = lambda b, j: (b, nj - 1 - j, 0)
    rtile = pl.BlockSpec((1, TL, D_A), rrow)
    rmats = pl.BlockSpec((1, cpb, H_A, RW_HEAD, RW_HEAD), lambda b, j: (b, nj - 1 - j, 0, 0, 0))
    rpcs = pl.BlockSpec((1, cpb, 1, D_A), lambda b, j: (b, nj - 1 - j, 0, 0))
    st = pl.BlockSpec((1, 2, H_A, RW_HEAD, RW_HEAD), lambda b, j: (b, 0, 0, 0, 0))
    return pl.pallas_call(
        _state_kernel, grid=(B, nj),
        in_specs=[tile, tile, mats, mats, pcs, rtile, rtile, rmats, rmats, rpcs, st],
        out_specs=[tile, rtile, st],
        out_shape=[f32(B, L, D_A), f32(B, L, D_A), f32(B, 2, H_A, RW_HEAD, RW_HEAD)],
        scratch_shapes=[pltpu.VMEM((2, H_A, RW_HEAD, RW_HEAD), F32)],
        compiler_params=_cparams(2, 40 * 1024 * 1024), name="rwkv_state",
    )(rpf, ypf, ff, gf, pcf, rpb, ypb, fb, gb, pcb, s0)


def _row_sum_bf16(e):
    n = e.shape[1] // 128
    acc = None
    for g0 in range(0, n, 4):
        part = e[:, g0 * 128:(g0 + 1) * 128]
        for j in range(g0 + 1, min(g0 + 4, n)):
            part = part + e[:, j * 128:(j + 1) * 128]
        pf = part.astype(F32)
        acc = pf if acc is None else acc + pf
    return jnp.sum(acc, axis=-1, keepdims=True)


def _attn_kernel(*refs, has_cache, lam_init):
    if has_cache:
        q_ref, k_ref, v_ref, ck_ref, cv_ref, lq1, lk1, lq2, lk2, sub_ref, o_ref = refs
    else:
        q_ref, k_ref, v_ref, lq1, lk1, lq2, lk2, sub_ref, o_ref = refs
    l1 = jnp.sum(lq1[...] * lk1[...], axis=-1, keepdims=True)
    l2 = jnp.sum(lq2[...] * lk2[...], axis=-1, keepdims=True)
    lam = jnp.exp(l1) - jnp.exp(l2) + lam_init
    q = q_ref[0].astype(BF16)
    kn = k_ref[0].astype(BF16)
    vn = v_ref[0].astype(BF16)
    if has_cache:
        kc = ck_ref[0, 0].astype(BF16)
        vc = cv_ref[0, 0].astype(BF16)
    dn = (((1,), (1,)), ((), ()))
    def scores(j):
        c0 = j * DIFF_DH
        qh = q[:, c0:c0 + DIFF_DH]
        s_n = lax.dot_general(qh, kn[:, c0:c0 + DIFF_DH], dn, preferred_element_type=F32)
        s_c = lax.dot_general(qh, kc[:, c0:c0 + DIFF_DH], dn, preferred_element_type=F32) if has_cache else None
        return s_n, s_c

    nxt = scores(0)
    for h in range(H_B):
        w_new = None
        w_old = None
        for m in range(2):
            j = h * 2 + m
            s_n, s_c = nxt
            if j + 1 < 2 * H_B:
                nxt = scores(j + 1)
            mx = jnp.max(s_n, axis=-1, keepdims=True)
            if has_cache:
                mx = jnp.maximum(mx, jnp.max(s_c, axis=-1, keepdims=True))
            e_n = jnp.exp2((s_n - mx).astype(BF16))
            den = _row_sum_bf16(e_n)
            if has_cache:
                e_c = jnp.exp2((s_c - mx).astype(BF16))
                den = den + _row_sum_bf16(e_c)
            scale = ((1.0 / den) if m == 0 else (-lam / den)).astype(BF16)
            w_new = e_n * scale if m == 0 else w_new + e_n * scale
            if has_cache:
                w_old = e_c * scale if m == 0 else w_old + e_c * scale
        vh = vn[:, h * 128:(h + 1) * 128]
        o = jnp.dot(w_new, vh, preferred_element_type=F32)
        if has_cache:
            o = o + jnp.dot(w_old, vc[:, h * 128:(h + 1) * 128], preferred_element_type=F32)
        o_ref[0, :, h * 128:(h + 1) * 128] = _rms(o, sub_ref[...]) * (1.0 - lam_init)


def _attention(q, k, v, cache, l, p, lam_init):
    B, L, _ = q.shape
    has_cache = cache is not None
    tq = TL
    in_specs = [pl.BlockSpec((1, tq, D_B), lambda b, i: (b, i, 0)),
                pl.BlockSpec((1, L, D_B), lambda b, i: (b, 0, 0)),
                pl.BlockSpec((1, L, D_B), lambda b, i: (b, 0, 0))]
    args = [q, k, v]
    if has_cache:
        ck, cv = cache
        past = ck.shape[2]
        in_specs += [pl.BlockSpec((1, 1, past, D_B), lambda b, i: (b, l, 0, 0))] * 2
        args += [ck, cv]
    small = lambda n: pl.BlockSpec((1, n), lambda b, i: (0, 0))
    in_specs += [small(DIFF_DH)] * 4 + [small(128)]
    args += [p['diff_lq1'].reshape(1, -1), p['diff_lk1'].reshape(1, -1), p['diff_lq2'].reshape(1, -1),
             p['diff_lk2'].reshape(1, -1), p['diff_subln'].reshape(1, -1)]
    return pl.pallas_call(
        functools.partial(_attn_kernel, has_cache=has_cache, lam_init=lam_init),
        grid=(B, L // tq), in_specs=in_specs,
        out_specs=pl.BlockSpec((1, tq, D_B), lambda b, i: (b, i, 0)),
        out_shape=jax.ShapeDtypeStruct((B, L, D_B), F32),
        compiler_params=_cparams(2, 48 * 1024 * 1024), name="diff_attention",
    )(*args)


def _hfilt_time_kernel(emb_ref, w1_ref, b1_ref, fr_ref, w2_ref, b2_ref, w3_ref, dec_ref, h_ref, ss_ref, acc_ref):
    i = pl.program_id(0)
    emb = emb_ref[...]
    fr = fr_ref[...]
    h = jnp.sin(fr * (jnp.dot(emb.astype(BF16), w1_ref[...].astype(BF16), preferred_element_type=F32) + b1_ref[...]))
    h = jnp.sin(fr * (jnp.dot(h.astype(BF16), w2_ref[...].astype(BF16), preferred_element_type=F32) + b2_ref[...]))
    h = jnp.dot(h.astype(BF16), w3_ref[...].astype(BF16), preferred_element_type=F32)
    h = h * jnp.exp(-emb[:, 0:1] * jnp.abs(dec_ref[...]))
    n = h.shape[0]
    row = lax.broadcasted_iota(jnp.int32, (n, 4 * D_C), 0) + i * n
    col = lax.broadcasted_iota(jnp.int32, (n, 4 * D_C), 1)
    is_bwd = ((col // D_C) % 2) == 1
    h = jnp.where(jnp.logical_and(is_bwd, row == 0), 0.0, h)
    h_ref[...] = h

    @pl.when(i == 0)
    def _():
        acc_ref[...] = jnp.zeros_like(acc_ref)

    acc_ref[...] += jnp.sum(h * h, axis=0, keepdims=True)
    ss_ref[...] = acc_ref[...]


def _hfilt_freq_kernel(h_ref, ss_ref, ch_ref, cl_ref, sh_ref, sl_ref, hr_ref, hi_ref,
                       ah_ref, al_ref, bh_ref, bl_ref, nyq_ref):
    i = pl.program_id(0)

    @pl.when(i == 0)
    def _():
        for o in range(2):
            hf = h_ref[:, o * 2 * D_C:o * 2 * D_C + D_C]
            hb = h_ref[:, o * 2 * D_C + D_C:(o + 1) * 2 * D_C]
            a = hf + hb
            b = hb - hf
            n = a.shape[0]
            alt = 1.0 - 2.0 * (lax.broadcasted_iota(jnp.int32, (n, 1), 0) % 2).astype(F32)
            nyq_ref[:, o * D_C:(o + 1) * D_C] = jnp.sum(a * alt, axis=0, keepdims=True)
            ahh, all_ = _split2(a)
            bhh, bll = _split2(b)
            ah_ref[:, o * D_C:(o + 1) * D_C] = ahh
            al_ref[:, o * D_C:(o + 1) * D_C] = all_
            bh_ref[:, o * D_C:(o + 1) * D_C] = bhh
            bl_ref[:, o * D_C:(o + 1) * D_C] = bll

    ss = ss_ref[...]
    tot = jnp.concatenate([ss[:, 0:D_C] + ss[:, D_C:2 * D_C], ss[:, 2 * D_C:3 * D_C] + ss[:, 3 * D_C:4 * D_C]], axis=1)
    scale = lax.rsqrt(tot + 1e-6)

    def dft(mh, ml, xh, xl):
        return (jnp.dot(mh, xh, preferred_element_type=F32) + jnp.dot(mh, xl, preferred_element_type=F32)
                + jnp.dot(ml, xh, preferred_element_type=F32))

    hr_ref[...] = dft(ch_ref[...], cl_ref[...], ah_ref[...], al_ref[...]) * scale
    hi = dft(sh_ref[...], sl_ref[...], bh_ref[...], bl_ref[...]) * scale
    tf = hi.shape[0]
    row = lax.broadcasted_iota(jnp.int32, (tf, 1), 0) + i * tf
    hi_ref[...] = jnp.where(row == 0, nyq_ref[...] * scale, hi)


def _hyena_filters(L, p, tabs):
    emb, ch, cl, sh, sl = tabs
    tr = 256
    w1 = jnp.pad(p['hy_w1'], ((0, 128 - HY_EMB), (0, 0)))
    full = lambda a: pl.BlockSpec(a.shape, lambda i: (0,) * a.ndim)
    params = [w1, p['hy_b1'].reshape(1, -1), p['hy_freq'].reshape(1, -1), p['hy_w2'], p['hy_b2'].reshape(1, -1),
              p['hy_w3'], jnp.tile(p['hy_decay'], 4).reshape(1, -1)]
    h_raw, ss = pl.pallas_call(
        _hfilt_time_kernel, grid=(L // tr,),
        in_specs=[pl.BlockSpec((tr, 128), lambda i: (i, 0))] + [full(a) for a in params],
        out_specs=[pl.BlockSpec((tr, 4 * D_C), lambda i: (i, 0)), pl.BlockSpec((1, 4 * D_C), lambda i: (0, 0))],
        out_shape=[jax.ShapeDtypeStruct((L, 4 * D_C), F32), jax.ShapeDtypeStruct((1, 4 * D_C), F32)],
        scratch_shapes=[pltpu.VMEM((1, 4 * D_C), F32)],
        compiler_params=_cparams(1), name="hyena_filter_time",
    )(emb, *params)
    tf = 256
    mat = pl.BlockSpec((tf, L), lambda i: (i, 0))
    return pl.pallas_call(
        _hfilt_freq_kernel, grid=(L // tf,),
        in_specs=[pl.BlockSpec((L, 4 * D_C), lambda i: (0, 0)), pl.BlockSpec((1, 4 * D_C), lambda i: (0, 0)),
                  mat, mat, mat, mat],
        out_specs=[pl.BlockSpec((tf, 2 * D_C), lambda i: (i, 0))] * 2,
        out_shape=[jax.ShapeDtypeStruct((L, 2 * D_C), F32)] * 2,
        scratch_shapes=[pltpu.VMEM((L, 2 * D_C), BF16)] * 4 + [pltpu.VMEM((1, 2 * D_C), F32)],
        compiler_params=_cparams(1, 48 * 1024 * 1024), name="hyena_filter_freq",
    )(h_raw, ss, ch, cl, sh, sl)


def _hconv_kernel(z_ref, zt_ref, gate_ref, hr_ref, hi_ref, bias_ref, c_ref, s_ref, o_ref,
                  zb_ref, yr_ref, yi_ref, nyq_ref, *, BG, L):
    ph = pl.program_id(1)
    i = pl.program_id(2)
    inv_n = 1.0 / (2 * L)

    @pl.when(jnp.logical_and(ph == 0, i == 0))
    def _():
        alt = 1.0 - 2.0 * (lax.broadcasted_iota(jnp.int32, (L, 1), 0) % 2).astype(F32)
        for b in range(BG):
            z = z_ref[b]
            zb_ref[:, b * D_C:(b + 1) * D_C] = z.astype(BF16)
            nyq_ref[:, b * D_C:(b + 1) * D_C] = jnp.sum(z * alt, axis=0, keepdims=True)

    tf = c_ref.shape[0]

    @pl.when(ph == 0)
    def _():
        zr = jnp.dot(c_ref[...], zb_ref[...], preferred_element_type=F32)
        zs = jnp.dot(s_ref[...], zb_ref[...], preferred_element_type=F32)
        hr = hr_ref[...]
        hi = hi_ref[...]
        row = lax.broadcasted_iota(jnp.int32, (tf, 1), 0) + i * tf
        wgt = jnp.where(row == 0, inv_n, 2.0 * inv_n)
        rows = pl.ds(pl.multiple_of(i * tf, tf), tf)
        for b in range(BG):
            sl = slice(b * D_C, (b + 1) * D_C)
            yr_ref[rows, sl] = ((zr[:, sl] * hr + zs[:, sl] * hi) * wgt).astype(BF16)
            yi_ref[rows, sl] = ((zr[:, sl] * hi - zs[:, sl] * hr) * (2.0 * inv_n)).astype(BF16)

        @pl.when(i == 0)
        def _():
            for b in range(BG):
                sl = slice(b * D_C, (b + 1) * D_C)
                nyq_ref[:, sl] = nyq_ref[:, sl] * hi[0:1, :] * inv_n

    @pl.when(ph == 1)
    def _():
        y = (jnp.dot(c_ref[...], yr_ref[...], preferred_element_type=F32)
             - jnp.dot(s_ref[...], yi_ref[...], preferred_element_type=F32))
        row = lax.broadcasted_iota(jnp.int32, (tf, 1), 0) + i * tf
        alt = 1.0 - 2.0 * (row % 2).astype(F32)
        y = y + alt * nyq_ref[...]
        for b in range(BG):
            sl = slice(b * D_C, (b + 1) * D_C)
            o_ref[b] = gate_ref[b] * (y[:, sl] + bias_ref[...] * zt_ref[b])


def _hconv(z, gate, hr, hi, bias, order, tabs):
    B, L, _ = z.shape
    _, ch, _, sh, _ = tabs
    BG = min(B, 8) if L <= 256 else min(B, 4)
    assert B % BG == 0
    tf = 256
    nt = L // tf
    mat = pl.BlockSpec((tf, L), lambda g, ph, i: (i, 0))
    tile = pl.BlockSpec((BG, tf, D_C), lambda g, ph, i: (g, i * ph, 0))
    hspec = pl.BlockSpec((tf, D_C), lambda g, ph, i: (i * (1 - ph), order))
    cols = BG * D_C
    return pl.pallas_call(
        functools.partial(_hconv_kernel, BG=BG, L=L), grid=(B // BG, 2, nt),
        in_specs=[pl.BlockSpec((BG, L, D_C), lambda g, ph, i: (g, 0, 0)), tile, tile, hspec, hspec,
                  pl.BlockSpec((1, D_C), lambda g, ph, i: (0, 0)), mat, mat],
        out_specs=tile,
        out_shape=jax.ShapeDtypeStruct((B, L, D_C), F32),
        scratch_shapes=[pltpu.VMEM((L, cols), BF16)] * 3 + [pltpu.VMEM((1, cols), F32)],
        compiler_params=_cparams(3, 48 * 1024 * 1024), name="hyena_conv",
    )(z, z, gate, hr, hi, bias[order].reshape(1, D_C), ch, sh)


def _outproj_kernel(x_ref, yf_ref, yb_ref, bon_ref, g_ref, ob_ref, yc_ref, mod_ref,
                    lnw_ref, lnb_ref, gpost_ref, gffn_ref, w_ref, x1_ref, h2_ref):
    bd = _block_diag(D_A, RW_HEAD)
    y = yf_ref[0] + yb_ref[0]
    mu = _segsum(y, bd) * (1.0 / RW_HEAD)
    yc = y - mu
    var = _segsum(yc * yc, bd) * (1.0 / RW_HEAD)
    yn = yc * lax.rsqrt(var + GN_EPS) * lnw_ref[...] + lnb_ref[...]
    ya = (yn + bon_ref[0]) * g_ref[0]
    mix = (jnp.dot(ya.astype(BF16), w_ref[0:D_A, :], preferred_element_type=F32)
           + jnp.dot(ob_ref[0].astype(BF16), w_ref[D_A:D_A + D_B, :], preferred_element_type=F32)
           + jnp.dot(yc_ref[0].astype(BF16), w_ref[D_A + D_B:D_MODEL, :], preferred_element_type=F32))
    mod = mod_ref[0]
    gt1 = mod[:, 2 * D_MODEL:3 * D_MODEL]
    sh2 = mod[:, 3 * D_MODEL:4 * D_MODEL]
    sc2 = mod[:, 4 * D_MODEL:5 * D_MODEL]
    x1 = x_ref[0] + gt1 * _rms(mix, gpost_ref[...])
    x1_ref[0] = x1
    h2_ref[0] = (_rms(x1, gffn_ref[...]) * (1.0 + sc2) + sh2).astype(BF16)


def _outproj(x, yf, yb, bon, g, ob, yc, mod, p, w_out):
    B, L, _ = x.shape
    mb = mod.shape[0]
    row = lambda b, i: (b, i, 0)
    c2 = lambda b, i: (0, 0)
    t = lambda n: pl.BlockSpec((1, TL, n), row)
    in_specs = [t(D_MODEL), t(D_A), t(D_A), t(D_A), t(D_A), t(D_B), t(D_C),
                pl.BlockSpec((1, 1, 6 * D_MODEL), (lambda b, i: (b, 0, 0)) if mb > 1 else (lambda b, i: (0, 0, 0))),
                pl.BlockSpec((1, D_A), c2), pl.BlockSpec((1, D_A), c2),
                pl.BlockSpec((1, D_MODEL), c2), pl.BlockSpec((1, D_MODEL), c2),
                _layer_weight_spec(w_out, 2)]
    return pl.pallas_call(
        _outproj_kernel, grid=(B, L // TL), in_specs=in_specs,
        out_specs=[t(D_MODEL), t(D_MODEL)],
        out_shape=[jax.ShapeDtypeStruct((B, L, D_MODEL), F32), jax.ShapeDtypeStruct((B, L, D_MODEL), BF16)],
        compiler_params=_cparams(2, 40 * 1024 * 1024), name="outproj",
    )(x, yf, yb, bon, g, ob, yc, mod, p['rwkv_ln_w'].reshape(1, -1), p['rwkv_ln_b'].reshape(1, -1),
      p['g_mix_post'].reshape(1, -1), p['g_ffn_pre'].reshape(1, -1), w_out[0])


def _ffn_kernel(h_ref, x_ref, mod_ref, g_ref, w1_ref, w2_ref, o_ref):
    h = h_ref[0]
    acc = None
    ck = 1024
    for c in range(D_FF // ck):
        a = jnp.dot(h, w1_ref[:, c * ck:(c + 1) * ck], preferred_element_type=F32)
        a = jnp.square(jnp.maximum(a, 0.0)).astype(BF16)
        part = jnp.dot(a, w2_ref[c * ck:(c + 1) * ck, :], preferred_element_type=F32)
        acc = part if acc is None else acc + part
    gt2 = mod_ref[0][:, 5 * D_MODEL:6 * D_MODEL]
    o_ref[0] = x_ref[0] + gt2 * _rms(acc, g_ref[...])


def _ffn(h2, x1, mod, g_post, w1, w2):
    B, L, _ = x1.shape
    mb = mod.shape[0]
    row = lambda b, i: (b, i, 0)
    c2 = lambda b, i: (0, 0)
    return pl.pallas_call(
        _ffn_kernel, grid=(B, L // TL),
        in_specs=[pl.BlockSpec((1, TL, D_MODEL), row), pl.BlockSpec((1, TL, D_MODEL), row),
                  pl.BlockSpec((1, 1, 6 * D_MODEL), (lambda b, i: (b, 0, 0)) if mb > 1 else (lambda b, i: (0, 0, 0))),
                  pl.BlockSpec((1, D_MODEL), c2),
                  _layer_weight_spec(w1, 2), _layer_weight_spec(w2, 2)],
        out_specs=pl.BlockSpec((1, TL, D_MODEL), row),
        out_shape=jax.ShapeDtypeStruct((B, L, D_MODEL), F32),
        compiler_params=_cparams(2, V7X_VMEM_LIMIT), name="ffn",
    )(h2, x1, mod, g_post.reshape(1, -1), w1[0], w2[0])


def _rope_tables(L):
    rows = L // GRID_W
    row = jnp.repeat(jnp.arange(rows), GRID_W).astype(F32)
    col = jnp.tile(jnp.arange(GRID_W), rows).astype(F32)
    half = DIFF_DH // 2
    inv = ROPE_BASE ** (-jnp.arange(0, half, 2, dtype=F32) / half)
    ang_r = row[:, None] * inv[None]
    ang_c = col[:, None] * inv[None]
    ang = jnp.concatenate([ang_r, ang_r, ang_c, ang_c], axis=-1)
    cos, sin = jnp.cos(ang), jnp.sin(ang)
    quarter = (jnp.arange(DIFF_DH) // 16) % 2
    sin_a = jnp.where(quarter == 0, -sin, 0.0)
    sin_b = jnp.where(quarter == 1, sin, 0.0)
    rep = D_B // DIFF_DH
    return tuple(jnp.tile(t, (1, rep)) for t in (cos, sin_a, sin_b))


def _hyena_tables(L):
    t = jnp.linspace(0.0, 1.0, L, dtype=F32)[:, None]
    ang = (2.0 * math.pi / L) * jnp.arange(L, dtype=F32)[:, None]
    bands = jnp.linspace(1e-4, HY_BANDS - 1, HY_BANDS, dtype=F32)[None, :]
    emb = jnp.concatenate([t, jnp.cos(bands * ang), -jnp.sin(bands * ang)], axis=-1)
    emb = jnp.pad(emb, ((0, 0), (0, 128 - HY_EMB)))
    n = jnp.arange(L, dtype=jnp.int32)
    k = (n[:, None] * n[None, :]) % (2 * L)
    theta = k.astype(F32) * (math.pi / L)
    c = jnp.cos(theta)
    s = jnp.sin(theta)
    ch, cl = _split2(c)
    sh, sl = _split2(s)
    return emb, ch, cl, sh, sl


def _layer(x, mod, p, wb, l, lam_init, cache, rope_tabs, hy_tabs, filt):
    B, L, _ = x.shape
    ua, uq, uk, uv, uh = _inproj(x, mod, p['g_mix_pre'], wb['w_in'], rope_tabs)
    (r, v, nkk, wf, kaf, kdf, wbk, kab, kdb, g, bon, x1h, x2h, hv) = _prep(ua, uh, p)
    if cache is None:
        s0 = jnp.zeros((B, 2, H_A, RW_HEAD, RW_HEAD), F32)
        kv_cache = None
    else:
        s0 = cache[0]
        kv_cache = (cache[1], cache[2])
    yf, yb, sfin = _scan(nkk, r, v, wf, kaf, kdf, wbk, kab, kdb, s0)
    ob = _attention(uq, uk, uv, kv_cache, l, p, lam_init)
    hr, hi = filt
    z1 = _hconv(hv, x1h, hr, hi, p['hy_bias'], 0, hy_tabs)
    yc = _hconv(z1, x2h, hr, hi, p['hy_bias'], 1, hy_tabs)
    x1, h2 = _outproj(x, yf, yb, bon, g, ob, yc, mod, p, wb['w_out'])
    x2 = _ffn(h2, x1, mod, p['g_ffn_post'], wb['w_ff1'], wb['w_ff2'])
    return x2, (sfin, uk, uv)


_LAYER_KEYS = ('g_mix_pre', 'g_mix_post', 'g_ffn_pre', 'g_ffn_post', 'rwkv_conv', 'rwkv_w0', 'rwkv_w2', 'rwkv_a0',
               'rwkv_a2', 'rwkv_g2', 'rwkv_kk', 'rwkv_ka', 'rwkv_rk', 'rwkv_ln_w', 'rwkv_ln_b', 'diff_lq1', 'diff_lk1',
               'diff_lq2', 'diff_lk2', 'diff_subln', 'hy_conv_w', 'hy_conv_b', 'hy_w1', 'hy_b1', 'hy_freq', 'hy_w2',
               'hy_b2', 'hy_w3', 'hy_decay', 'hy_bias')


def kernel(x_prompt, x_sample, state_rwkv, cache_k, cache_v, c, c_ctx, w_mod, b_mod, g_mix_pre, g_mix_post, g_ffn_pre, g_ffn_post, w_in, rwkv_conv, rwkv_w0, rwkv_w2, rwkv_a0, rwkv_a2, rwkv_g2, rwkv_kk, rwkv_ka, rwkv_rk, rwkv_ln_w, rwkv_ln_b, diff_lq1, diff_lk1, diff_lq2, diff_lk2, diff_subln, hy_conv_w, hy_conv_b, hy_w1, hy_b1, hy_freq, hy_w2, hy_b2, hy_w3, hy_decay, hy_bias, w_out, w_ff1, w_ff2):
    stacked = dict(g_mix_pre=g_mix_pre, g_mix_post=g_mix_post, g_ffn_pre=g_ffn_pre, g_ffn_post=g_ffn_post,
                   rwkv_conv=rwkv_conv, rwkv_w0=rwkv_w0, rwkv_w2=rwkv_w2, rwkv_a0=rwkv_a0, rwkv_a2=rwkv_a2,
                   rwkv_g2=rwkv_g2, rwkv_kk=rwkv_kk, rwkv_ka=rwkv_ka, rwkv_rk=rwkv_rk, rwkv_ln_w=rwkv_ln_w,
                   rwkv_ln_b=rwkv_ln_b, diff_lq1=diff_lq1, diff_lk1=diff_lk1, diff_lq2=diff_lq2, diff_lk2=diff_lk2,
                   diff_subln=diff_subln, hy_conv_w=hy_conv_w, hy_conv_b=hy_conv_b, hy_w1=hy_w1, hy_b1=hy_b1,
                   hy_freq=hy_freq, hy_w2=hy_w2, hy_b2=hy_b2, hy_w3=hy_w3, hy_decay=hy_decay, hy_bias=hy_bias)
    Bc, Lc, _ = x_prompt.shape
    Bs, Ls, _ = x_sample.shape
    past = cache_k.shape[2]

    cond = jnp.zeros((8, D_MODEL), F32).at[0:Bs].set(c).at[Bs].set(c_ctx)
    mod_all = _modulation(cond, w_mod, b_mod)

    rope_tabs = _rope_tables(Ls)
    tabs_c = _hyena_tables(Lc)
    tabs_s = _hyena_tables(Ls)
    ck = cache_k.reshape(Bs, DEPTH, past, D_B)
    cv = cache_v.reshape(Bs, DEPTH, past, D_B)

    w_in_b, w_out_b, w_ff1_b, w_ff2_b = (w.astype(BF16) for w in (w_in, w_out, w_ff1, w_ff2))
    xp, xs = x_prompt, x_sample
    st_list, k_list, v_list = [], [], []
    for l in range(DEPTH):
        p = {k: stacked[k][l] for k in _LAYER_KEYS}
        wb = dict(w_in=(w_in_b, l), w_out=(w_out_b, l), w_ff1=(w_ff1_b, l), w_ff2=(w_ff2_b, l))
        lam_init = 0.8 - 0.6 * math.exp(-0.3 * l)
        mod_lat = mod_all[l, 0:Bs].reshape(Bs, 1, 6 * D_MODEL)
        mod_ctx = mod_all[l, Bs:Bs + 1].reshape(1, 1, 6 * D_MODEL)
        filt_c = _hyena_filters(Lc, p, tabs_c)
        filt_s = _hyena_filters(Ls, p, tabs_s)
        xp, (s_ctx, k_ctx, v_ctx) = _layer(xp, mod_ctx, p, wb, l, lam_init, None, None, tabs_c, filt_c)
        st_list.append(s_ctx)
        k_list.append(k_ctx.reshape(Bc, Lc, H_B, 2, DIFF_DH))
        v_list.append(v_ctx.reshape(Bc, Lc, H_B, 2 * DIFF_DH))
        s0 = state_rwkv[:, l]
        xs, _ = _layer(xs, mod_lat, p, wb, l, lam_init, (s0, ck, cv), rope_tabs, tabs_s, filt_s)
    return (xp, xs, jnp.stack(st_list, axis=1), jnp.stack(k_list, axis=1), jnp.stack(v_list, axis=1))
```

```python
import functools
import math

import jax
import jax.numpy as jnp
import numpy as np
from jax import lax
from jax.experimental import pallas as pl
from jax.experimental.pallas import tpu as pltpu

F32 = jnp.float32
BF16 = jnp.bfloat16

D_MODEL = 1024
DEPTH = 4
GRID_W = 64
D_A = 256
RW_HEAD = 64
H_A = 4
D_B = 512
DIFF_DH = 64
H_B = 4
D_C = 256
HY_BANDS = 16
HY_EMB = 33
HY_FFN = 64
D_FF = 4096
ROPE_BASE = 10000.0
RMS_EPS = 1e-6
GN_EPS = 64e-5
N_A_COLS = 1152
IN_COLS = 3456

TL = 256
V7X_VMEM_LIMIT = 56 * 1024 * 1024


def _cparams(n_axes, vmem=None):
    return pltpu.CompilerParams(dimension_semantics=("arbitrary",) * n_axes,
                                vmem_limit_bytes=vmem)


def _split3(a):
    hi = a.astype(BF16)
    r1 = a - hi.astype(F32)
    mid = r1.astype(BF16)
    lo = (r1 - mid.astype(F32)).astype(BF16)
    return hi, mid, lo


def _split2(a):
    hi = a.astype(BF16)
    lo = (a - hi.astype(F32)).astype(BF16)
    return hi, lo


def _segsum(a, bd):
    hi, lo = _split2(a)
    return jnp.dot(hi, bd, preferred_element_type=F32) + jnp.dot(lo, bd, preferred_element_type=F32)


def _block_diag(n, seg):
    r = lax.broadcasted_iota(jnp.int32, (n, n), 0) // seg
    c = lax.broadcasted_iota(jnp.int32, (n, n), 1) // seg
    return (r == c).astype(BF16)


def _layer_weight_spec(w, n_grid_axes):
    stack, layer = w
    _, k, n = stack.shape
    if n_grid_axes == 2:
        return pl.BlockSpec((None, k, n), lambda b, i: (layer, 0, 0))
    raise NotImplementedError(n_grid_axes)


def _rms(x, g):
    return x * lax.rsqrt(jnp.mean(x * x, axis=-1, keepdims=True) + RMS_EPS) * g


def _mod_kernel(cond_ref, w_ref, b_ref, o_ref):
    c = cond_ref[...]
    s = (c * jax.nn.sigmoid(c)).astype(BF16)
    o_ref[0] = jnp.dot(s, w_ref[0].astype(BF16), preferred_element_type=F32) + b_ref[0]


def _modulation(cond, w_mod, b_mod):
    tn = 1536
    return pl.pallas_call(
        _mod_kernel,
        grid=(DEPTH, 6 * D_MODEL // tn),
        in_specs=[pl.BlockSpec((8, D_MODEL), lambda l, n: (0, 0)),
                  pl.BlockSpec((1, D_MODEL, tn), lambda l, n: (l, 0, n)),
                  pl.BlockSpec((1, 1, tn), lambda l, n: (l, 0, n))],
        out_specs=pl.BlockSpec((1, 8, tn), lambda l, n: (l, 0, n)),
        out_shape=jax.ShapeDtypeStruct((DEPTH, 8, 6 * D_MODEL), F32),
        compiler_params=_cparams(2, 40 * 1024 * 1024),
        name="modulation",
    )(cond, w_mod, b_mod.reshape(DEPTH, 1, 6 * D_MODEL))


def _rope(x, cos, sin_a, sin_b):
    n = x.shape[-1]
    return x * cos + pltpu.roll(x, n - 16, axis=1) * sin_a + pltpu.roll(x, 16, axis=1) * sin_b


def _inproj_kernel(*refs, rope):
    if rope:
        x_ref, mod_ref, g_ref, w_ref, cos_ref, sa_ref, sb_ref, ua_ref, uq_ref, uk_ref, uv_ref, uh_ref = refs
    else:
        x_ref, mod_ref, g_ref, w_ref, ua_ref, uq_ref, uk_ref, uv_ref, uh_ref = refs
    x = x_ref[0]
    mod = mod_ref[0]
    sh1 = mod[:, 0:D_MODEL]
    sc1 = mod[:, D_MODEL:2 * D_MODEL]
    h = _rms(x, g_ref[...]) * (1.0 + sc1) + sh1
    u = jnp.dot(h.astype(BF16), w_ref[...], preferred_element_type=F32)
    ua_ref[0] = u[:, 0:N_A_COLS]
    q = u[:, N_A_COLS:N_A_COLS + D_B] * (DIFF_DH ** -0.5 * math.log2(math.e))
    k = u[:, N_A_COLS + D_B:N_A_COLS + 2 * D_B]
    if rope:
        q = _rope(q, cos_ref[...], sa_ref[...], sb_ref[...])
        k = _rope(k, cos_ref[...], sa_ref[...], sb_ref[...])
    uq_ref[0] = q.astype(uq_ref.dtype)
    uk_ref[0] = k.astype(uk_ref.dtype)
    uv_ref[0] = u[:, N_A_COLS + 2 * D_B:N_A_COLS + 3 * D_B].astype(uv_ref.dtype)
    uh_ref[0] = u[:, N_A_COLS + 3 * D_B:IN_COLS]


def _inproj(x, mod, g_pre, w_in, rope_tabs):
    B, L, _ = x.shape
    rope = rope_tabs is not None
    mb = mod.shape[0]
    row = lambda b, i: (b, i, 0)
    const2 = lambda b, i: (0, 0)
    in_specs = [pl.BlockSpec((1, TL, D_MODEL), row),
                pl.BlockSpec((1, 1, 6 * D_MODEL), (lambda b, i: (b, 0, 0)) if mb > 1 else (lambda b, i: (0, 0, 0))),
                pl.BlockSpec((1, D_MODEL), const2),
                _layer_weight_spec(w_in, 2)]
    args = [x, mod, g_pre.reshape(1, D_MODEL), w_in[0]]
    if rope:
        in_specs += [pl.BlockSpec((TL, D_B), lambda b, i: (i, 0))] * 3
        args += list(rope_tabs)
    qkv_dt = BF16 if rope else F32
    out_shape = [jax.ShapeDtypeStruct((B, L, N_A_COLS), F32),
                 jax.ShapeDtypeStruct((B, L, D_B), qkv_dt),
                 jax.ShapeDtypeStruct((B, L, D_B), qkv_dt),
                 jax.ShapeDtypeStruct((B, L, D_B), qkv_dt),
                 jax.ShapeDtypeStruct((B, L, 3 * D_C), F32)]
    out_specs = [pl.BlockSpec((1, TL, N_A_COLS), row), pl.BlockSpec((1, TL, D_B), row),
                 pl.BlockSpec((1, TL, D_B), row), pl.BlockSpec((1, TL, D_B), row),
                 pl.BlockSpec((1, TL, 3 * D_C), row)]
    return pl.pallas_call(
        functools.partial(_inproj_kernel, rope=rope),
        grid=(B, L // TL), in_specs=in_specs, out_specs=out_specs, out_shape=out_shape,
        compiler_params=_cparams(2, 48 * 1024 * 1024), name="inproj",
    )(*args)


def _shift_conv(u, prev_row, next_row, cw):
    n = u.shape[0]
    row = lax.broadcasted_iota(jnp.int32, (n, 1), 0)
    up = jnp.where(row == 0, prev_row, pltpu.roll(u, 1, axis=0))
    un = jnp.where(row == n - 1, next_row, pltpu.roll(u, n - 1, axis=0))
    return cw[0:1] * up + cw[1:2] * u + cw[2:3] * un


def _prep_kernel(ua_ref, uap_ref, uan_ref, uh_ref, uhp_ref, uhn_ref,
                 cw_ref, w0_ref, w2_ref, a0_ref, a2_ref, g2_ref, kkw_ref, kaw_ref, rk_ref, hcw_ref, hcb_ref,
                 r_ref, v_ref, nkk_ref, wf_ref, kaf_ref, kdf_ref, wb_ref, kab_ref, kdb_ref, g_ref, bon_ref,
                 x1_ref, x2_ref, hv_ref):
    i = pl.program_id(1)
    nt = pl.num_programs(1)
    has_prev = i > 0
    has_next = i < nt - 1
    bd = _block_diag(D_A, RW_HEAD)

    ua = ua_ref[0]
    u_rkv = ua[:, 0:3 * D_A]
    prev = jnp.where(has_prev, uap_ref[0, 7:8, :], 0.0)
    nxt = jnp.where(has_next, uan_ref[0, 0:1, :], 0.0)
    rkv = _shift_conv(u_rkv, prev, nxt, cw_ref[...])
    r = rkv[:, 0:D_A]
    k = rkv[:, D_A:2 * D_A]
    v = rkv[:, 2 * D_A:3 * D_A]
    u_w = ua[:, 768:896]
    u_a = ua[:, 896:1024]
    u_g = ua[:, 1024:1152]

    g = jnp.dot(jax.nn.sigmoid(u_g).astype(BF16), g2_ref[...].astype(BF16), preferred_element_type=F32)
    kk = k * kkw_ref[...]
    kk = kk * lax.rsqrt(_segsum(kk * kk, bd) + 1e-12)
    r_ref[0] = r
    v_ref[0] = v
    nkk_ref[0] = -kk
    g_ref[0] = g

    kd_sum = None
    outs = ((wf_ref, kaf_ref, kdf_ref), (wb_ref, kab_ref, kdb_ref))
    for d in range(2):
        uw_d = u_w[:, d * 64:(d + 1) * 64]
        ua_d = u_a[:, d * 64:(d + 1) * 64]
        xw = w0_ref[d:d + 1, :] + jnp.dot(jnp.tanh(uw_d).astype(BF16), w2_ref[d].astype(BF16),
                                          preferred_element_type=F32)
        z = -xw
        softplus = jnp.maximum(z, 0.0) + jnp.log(1.0 + jnp.exp(-jnp.abs(z)))
        wl = -softplus - 0.5
        log_decay = -jnp.exp(wl)
        a = jax.nn.sigmoid(a0_ref[d:d + 1, :] + jnp.dot(ua_d.astype(BF16), a2_ref[d].astype(BF16),
                                                        preferred_element_type=F32))
        kd = k * (1.0 + (a - 1.0) * kaw_ref[...])
        w_o, ka_o, kd_o = outs[d]
        w_o[0] = log_decay
        ka_o[0] = kk * a
        kd_o[0] = kd
        kd_sum = kd if kd_sum is None else kd_sum + kd
    bon_ref[0] = _segsum(r * kd_sum * rk_ref[...], bd) * v

    uh = uh_ref[0]
    prevh = jnp.where(has_prev, uhp_ref[0, 7:8, :], 0.0)
    nxth = jnp.where(has_next, uhn_ref[0, 0:1, :], 0.0)
    hc = _shift_conv(uh, prevh, nxth, hcw_ref[...]) + hcb_ref[...]
    x1_ref[0] = hc[:, 0:D_C]
    x2_ref[0] = hc[:, D_C:2 * D_C]
    hv_ref[0] = hc[:, 2 * D_C:3 * D_C]


def _prep(ua, uh, p):
    B, L, _ = ua.shape
    nb8 = L // 8
    t8 = TL // 8
    row = lambda b, i: (b, i, 0)
    prev = lambda b, i: (b, jnp.maximum(i * t8 - 1, 0), 0)
    nxt = lambda b, i: (b, jnp.minimum((i + 1) * t8, nb8 - 1), 0)

    def full(a):
        nd = a.ndim
        return pl.BlockSpec(a.shape, lambda b, i: (0,) * nd)

    params = [p['rwkv_conv'], p['rwkv_w0'], p['rwkv_w2'], p['rwkv_a0'], p['rwkv_a2'], p['rwkv_g2'],
              p['rwkv_kk'].reshape(1, D_A), p['rwkv_ka'].reshape(1, D_A), p['rwkv_rk'].reshape(1, D_A),
              p['hy_conv_w'], p['hy_conv_b'].reshape(1, 3 * D_C)]
    in_specs = [pl.BlockSpec((1, TL, N_A_COLS), row), pl.BlockSpec((1, 8, 3 * D_A), prev),
                pl.BlockSpec((1, 8, 3 * D_A), nxt),
                pl.BlockSpec((1, TL, 3 * D_C), row), pl.BlockSpec((1, 8, 3 * D_C), prev),
                pl.BlockSpec((1, 8, 3 * D_C), nxt)] + [full(a) for a in params]
    n_out = 14
    return pl.pallas_call(
        _prep_kernel, grid=(B, L // TL), in_specs=in_specs,
        out_specs=[pl.BlockSpec((1, TL, D_A), row)] * n_out,
        out_shape=[jax.ShapeDtypeStruct((B, L, D_A), F32)] * n_out,
        compiler_params=_cparams(2, 40 * 1024 * 1024), name="prep",
    )(ua, ua, ua, uh, uh, uh, *params)


CHUNK = 64
CHUNKS_PER_ITER = 2
_NN = (((1,), (0,)), ((), ()))
_NT = (((1,), (1,)), ((), ()))
_TN = (((0,), (0,)), ((), ()))


def _mm3(a, b, dims):
    ah, al = _split2(a)
    bh, bl = _split2(b)
    f = lambda x, y: lax.dot_general(x, y, dims, preferred_element_type=F32)
    if dims is _TN:
        return f(ah, bh) + f(ah, bl) + f(al, bh)
    m = a.shape[0]
    both = f(jnp.concatenate([ah, al], axis=0), bh)
    return both[0:m] + both[m:2 * m] + f(ah, bl)


def _mm2(a, b, dims):
    ah = a.astype(BF16)
    bh, bl = _split2(b)
    f = lambda x, y: lax.dot_general(x, y, dims, preferred_element_type=F32)
    return f(ah, bh) + f(ah, bl)


def _mm1(a, b, dims):
    return lax.dot_general(a.astype(BF16), b.astype(BF16), dims, preferred_element_type=F32)


def _unit_tri_inverses(ms, row, col):
    eye = (row == col).astype(F32)
    blk = lambda n: (row // n) == (col // n)
    m8 = [jnp.where(blk(8), m, 0.0) for m in ms]
    a2 = [_mm1(x, x, _NN) for x in m8]
    a4 = [_mm1(x, x, _NN) for x in a2]
    p = [eye + x + y + _mm1(x, y, _NN) for x, y in zip(m8, a2)]
    ts = [x + _mm1(x, y, _NN) for x, y in zip(p, a4)]
    n = 8
    while n < CHUNK:
        sel = jnp.logical_and(blk(2 * n), jnp.logical_not(blk(n)))
        ot = [_mm1(jnp.where(sel, m, 0.0), t, _NN) for m, t in zip(ms, ts)]
        ts = [t + _mm1(t, o, _NN) for t, o in zip(ts, ot)]
        n *= 2
    return ts


def _chunk_kernel(nkk_ref, r_ref, v_ref, lwf_ref, kaf_ref, kdf_ref, lwb_ref, kab_ref, kdb_ref,
                  rpf_ref, ypf_ref, ff_ref, gf_ref, pcf_ref, rpb_ref, ypb_ref, fb_ref, gb_ref, pcb_ref):
    C = CHUNK
    row = lax.broadcasted_iota(jnp.int32, (C, C), 0)
    col = lax.broadcasted_iota(jnp.int32, (C, C), 1)
    dirs = ((lwf_ref, kaf_ref, kdf_ref, rpf_ref, ypf_ref, ff_ref, gf_ref, pcf_ref),
            (lwb_ref, kab_ref, kdb_ref, rpb_ref, ypb_ref, fb_ref, gb_ref, pcb_ref))
    n_chunks = nkk_ref.shape[1] // C

    def build(c, probs):
        rows = pl.ds(pl.multiple_of(c * C, C), C)
        alpha = nkk_ref[0, rows, :]
        r = r_ref[0, rows, :]
        v = v_ref[0, rows, :]
        for d in range(2):
            lw_ref, b_ref, k_ref = dirs[d][0:3]
            before = (row > col) if d == 0 else (row < col)
            upto = (row >= col) if d == 0 else (row <= col)
            lw = lw_ref[0, rows, :]
            beta = b_ref[0, rows, :]
            kd = k_ref[0, rows, :]
            tri = upto.astype(BF16)
            l_hi, l_mid, l_lo = _split3(lw)
            lam = (jnp.dot(tri, l_hi, preferred_element_type=F32) + jnp.dot(tri, l_mid, preferred_element_type=F32)
                   + jnp.dot(tri, l_lo, preferred_element_type=F32))
            lam_prev = lam - lw
            lam_mid = lam[C // 2:C // 2 + 1, :]
            lam_end = lam[C - 1:C, :] if d == 0 else lam[0:1, :]
            e_to_mid = jnp.exp(lam_mid - lam)
            a_m = alpha * jnp.exp(lam_prev - lam_mid)
            b_m = beta * e_to_mid
            k_m = kd * e_to_mid
            r_m = r * jnp.exp(lam - lam_mid)
            a_0 = alpha * jnp.exp(lam_prev)
            r_0 = r * jnp.exp(lam)
            e_to_end = jnp.exp(lam_end - lam)
            b_e = beta * e_to_end
            k_e = kd * e_to_end
            dirs[d][7][0, pl.ds(c, 1)] = jnp.exp(lam_end).reshape(1, 1, D_A)
            for h in range(H_A):
                sl = slice(h * RW_HEAD, (h + 1) * RW_HEAD)
                probs.append(dict(
                    c=c, rows=rows, d=d, h=h, before=before, upto=upto,
                    ar=jnp.concatenate([a_m[:, sl], r_m[:, sl]], axis=0),
                    bk=jnp.concatenate([b_m[:, sl], k_m[:, sl]], axis=0),
                    a0=a_0[:, sl], r0=r_0[:, sl], vh=v[:, sl],
                    bke=jnp.concatenate([b_e[:, sl], k_e[:, sl]], axis=0)))

    def body(ci, carry):
        probs = []
        for cc in range(CHUNKS_PER_ITER):
            build(ci * CHUNKS_PER_ITER + cc, probs)
        g12 = [_mm1(q['ar'], q['bk'], _NT) for q in probs]
        ms = [jnp.where(q['before'], g[0:C, 0:C], 0.0) for q, g in zip(probs, g12)]
        ns = [jnp.where(q['before'], g[0:C, C:2 * C], 0.0) for q, g in zip(probs, g12)]
        rbk = [jnp.concatenate([jnp.where(q['upto'], g[C:2 * C, 0:C], 0.0),
                                jnp.where(q['upto'], g[C:2 * C, C:2 * C], 0.0)], axis=1) for q, g in zip(probs, g12)]
        nvs = [_mm1(n, q['vh'], _NN) for q, n in zip(probs, ns)]
        ts = _unit_tri_inverses(ms, row, col)
        xs = [_mm2(t, jnp.concatenate([q['a0'], nv], axis=1), _NN) for q, t, nv in zip(probs, ts, nvs)]
        zeros = jnp.zeros((C, RW_HEAD), F32)
        ws = [jnp.concatenate([x, jnp.concatenate([zeros, q['vh']], axis=1)], axis=0) for q, x in zip(probs, xs)]
        rys = [_mm1(rb, w, _NN) for rb, w in zip(rbk, ws)]
        fgs = [_mm3(w, q['bke'], _TN) for q, w in zip(probs, ws)]
        for q, fg in zip(probs, fgs):
            f_ref, g_ref = dirs[q['d']][5], dirs[q['d']][6]
            f_ref[0, pl.ds(q['c'], 1), q['h']] = fg[0:RW_HEAD].reshape(1, RW_HEAD, RW_HEAD)
            g_ref[0, pl.ds(q['c'], 1), q['h']] = fg[RW_HEAD:2 * RW_HEAD].reshape(1, RW_HEAD, RW_HEAD)
        for k0 in range(0, len(probs), H_A):
            sub = list(zip(probs[k0:k0 + H_A], rys[k0:k0 + H_A]))
            q0 = sub[0][0]
            dirs[q0['d']][3][0, q0['rows'], :] = jnp.concatenate([q['r0'] + ry[:, 0:RW_HEAD] for q, ry in sub], axis=1)
            dirs[q0['d']][4][0, q0['rows'], :] = jnp.concatenate([ry[:, RW_HEAD:2 * RW_HEAD] for q, ry in sub], axis=1)
        return carry

    lax.fori_loop(0, n_chunks // CHUNKS_PER_ITER, body, 0)


def _state_kernel(rpf_ref, ypf_ref, ff_ref, gf_ref, pcf_ref, rpb_ref, ypb_ref, fb_ref, gb_ref, pcb_ref, s0_ref,
                  yf_ref, yb_ref, sfin_ref, s_ref):
    C = CHUNK
    j = pl.program_id(1)
    nj = pl.num_programs(1)
    n_chunks = rpf_ref.shape[1] // C

    @pl.when(j == 0)
    def _():
        s_ref[...] = s0_ref[0]

    dirs = ((rpf_ref, ypf_ref, ff_ref, gf_ref, pcf_ref, yf_ref), (rpb_ref, ypb_ref, fb_ref, gb_ref, pcb_ref, yb_ref))
    for cc in range(n_chunks):
        for d in range(2):
            rp_ref, yp_ref, f_ref, g_ref, pc_ref, y_ref = dirs[d]
            c = cc if d == 0 else n_chunks - 1 - cc
            rows = slice(c * C, (c + 1) * C)
            pc = pc_ref[0, c]
            y_parts = []
            for h in range(H_A):
                sl = slice(h * RW_HEAD, (h + 1) * RW_HEAD)
                s = s_ref[d, h]
                y_parts.append(yp_ref[0, rows, sl] + _mm1(rp_ref[0, rows, sl], s, _NT))
                s_ref[d, h] = s * pc[:, sl] + _mm1(s, f_ref[0, c, h], _NN) + g_ref[0, c, h]
            y_ref[0, rows, :] = jnp.concatenate(y_parts, axis=1)

    @pl.when(j == nj - 1)
    def _():
        sfin_ref[0] = s_ref[...]


def _scan(nkk, r, v, lwf, kaf, kdf, lwb, kab, kdb, s0):
    B, L, _ = r.shape
    nj = L // TL
    cpb = TL // CHUNK
    nc = L // CHUNK
    row = lambda b, j: (b, j, 0)
    tile = pl.BlockSpec((1, TL, D_A), row)
    mats = pl.BlockSpec((1, cpb, H_A, RW_HEAD, RW_HEAD), lambda b, j: (b, j, 0, 0, 0))
    pcs = pl.BlockSpec((1, cpb, 1, D_A), lambda b, j: (b, j, 0, 0))
    f32 = lambda *s: jax.ShapeDtypeStruct(s, F32)
    per_dir_shapes = [f32(B, L, D_A), f32(B, L, D_A), f32(B, nc, H_A, RW_HEAD, RW_HEAD),
                      f32(B, nc, H_A, RW_HEAD, RW_HEAD), f32(B, nc, 1, D_A)]
    per_dir_specs = [tile, tile, mats, mats, pcs]
    outs = pl.pallas_call(
        _chunk_kernel, grid=(B, nj), in_specs=[tile] * 9,
        out_specs=per_dir_specs * 2, out_shape=per_dir_shapes * 2,
        compiler_params=_cparams(2, 40 * 1024 * 1024), name="rwkv_chunk",
    )(nkk, r, v, lwf, kaf, kdf, lwb, kab, kdb)
    rpf, ypf, ff, gf, pcf, rpb, ypb, fb, gb, pcb = outs

    rrow = lambda b, j: (b, nj - 1 - j, 0)
    rtile = pl.BlockSpec((1, TL, D_A), rrow)
    rmats = pl.BlockSpec((1, cpb, H_A, RW_HEAD, RW_HEAD), lambda b, j: (b, nj - 1 - j, 0, 0, 0))
    rpcs = pl.BlockSpec((1, cpb, 1, D_A), lambda b, j: (b, nj - 1 - j, 0, 0))
    st = pl.BlockSpec((1, 2, H_A, RW_HEAD, RW_HEAD), lambda b, j: (b, 0, 0, 0, 0))
    return pl.pallas_call(
        _state_kernel, grid=(B, nj),
        in_specs=[tile, tile, mats, mats, pcs, rtile, rtile, rmats, rmats, rpcs, st],
        out_specs=[tile, rtile, st],
        out_shape=[f32(B, L, D_A), f32(B, L, D_A), f32(B, 2, H_A, RW_HEAD, RW_HEAD)],
        scratch_shapes=[pltpu.VMEM((2, H_A, RW_HEAD, RW_HEAD), F32)],
        compiler_params=_cparams(2, 40 * 1024 * 1024), name="rwkv_state",
    )(rpf, ypf, ff, gf, pcf, rpb, ypb, fb, gb, pcb, s0)


def _row_sum_bf16(e):
    n = e.shape[1] // 128
    acc = None
    for g0 in range(0, n, 4):
        part = e[:, g0 * 128:(g0 + 1) * 128]
        for j in range(g0 + 1, min(g0 + 4, n)):
            part = part + e[:, j * 128:(j + 1) * 128]
        pf = part.astype(F32)
        acc = pf if acc is None else acc + pf
    return jnp.sum(acc, axis=-1, keepdims=True)


def _attn_kernel(*refs, has_cache, lam_init):
    if has_cache:
        q_ref, k_ref, v_ref, ck_ref, cv_ref, lq1, lk1, lq2, lk2, sub_ref, o_ref = refs
    else:
        q_ref, k_ref, v_ref, lq1, lk1, lq2, lk2, sub_ref, o_ref = refs
    l1 = jnp.sum(lq1[...] * lk1[...], axis=-1, keepdims=True)
    l2 = jnp.sum(lq2[...] * lk2[...], axis=-1, keepdims=True)
    lam = jnp.exp(l1) - jnp.exp(l2) + lam_init
    q = q_ref[0].astype(BF16)
    kn = k_ref[0].astype(BF16)
    vn = v_ref[0].astype(BF16)
    if has_cache:
        kc = ck_ref[0, 0].astype(BF16)
        vc = cv_ref[0, 0].astype(BF16)
    dn = (((1,), (1,)), ((), ()))
    def scores(j):
        c0 = j * DIFF_DH
        qh = q[:, c0:c0 + DIFF_DH]
        s_n = lax.dot_general(qh, kn[:, c0:c0 + DIFF_DH], dn, preferred_element_type=F32)
        s_c = lax.dot_general(qh, kc[:, c0:c0 + DIFF_DH], dn, preferred_element_type=F32) if has_cache else None
        return s_n, s_c

    nxt = scores(0)
    for h in range(H_B):
        w_new = None
        w_old = None
        for m in range(2):
            j = h * 2 + m
            s_n, s_c = nxt
            if j + 1 < 2 * H_B:
                nxt = scores(j + 1)
            mx = jnp.max(s_n, axis=-1, keepdims=True)
            if has_cache:
                mx = jnp.maximum(mx, jnp.max(s_c, axis=-1, keepdims=True))
            e_n = jnp.exp2((s_n - mx).astype(BF16))
            den = _row_sum_bf16(e_n)
            if has_cache:
                e_c = jnp.exp2((s_c - mx).astype(BF16))
                den = den + _row_sum_bf16(e_c)
            scale = ((1.0 / den) if m == 0 else (-lam / den)).astype(BF16)
            w_new = e_n * scale if m == 0 else w_new + e_n * scale
            if has_cache:
                w_old = e_c * scale if m == 0 else w_old + e_c * scale
        vh = vn[:, h * 128:(h + 1) * 128]
        o = jnp.dot(w_new, vh, preferred_element_type=F32)
        if has_cache:
            o = o + jnp.dot(w_old, vc[:, h * 128:(h + 1) * 128], preferred_element_type=F32)
        o_ref[0, :, h * 128:(h + 1) * 128] = _rms(o, sub_ref[...]) * (1.0 - lam_init)


def _attention(q, k, v, cache, l, p, lam_init):
    B, L, _ = q.shape
    has_cache = cache is not None
    tq = TL
    in_specs = [pl.BlockSpec((1, tq, D_B), lambda b, i: (b, i, 0)),
                pl.BlockSpec((1, L, D_B), lambda b, i: (b, 0, 0)),
                pl.BlockSpec((1, L, D_B), lambda b, i: (b, 0, 0))]
    args = [q, k, v]
    if has_cache:
        ck, cv = cache
        past = ck.shape[2]
        in_specs += [pl.BlockSpec((1, 1, past, D_B), lambda b, i: (b, l, 0, 0))] * 2
        args += [ck, cv]
    small = lambda n: pl.BlockSpec((1, n), lambda b, i: (0, 0))
    in_specs += [small(DIFF_DH)] * 4 + [small(128)]
    args += [p['diff_lq1'].reshape(1, -1), p['diff_lk1'].reshape(1, -1), p['diff_lq2'].reshape(1, -1),
             p['diff_lk2'].reshape(1, -1), p['diff_subln'].reshape(1, -1)]
    return pl.pallas_call(
        functools.partial(_attn_kernel, has_cache=has_cache, lam_init=lam_init),
        grid=(B, L // tq), in_specs=in_specs,
        out_specs=pl.BlockSpec((1, tq, D_B), lambda b, i: (b, i, 0)),
        out_shape=jax.ShapeDtypeStruct((B, L, D_B), F32),
        compiler_params=_cparams(2, 48 * 1024 * 1024), name="diff_attention",
    )(*args)


def _hfilt_time_kernel(emb_ref, w1_ref, b1_ref, fr_ref, w2_ref, b2_ref, w3_ref, dec_ref, h_ref, ss_ref, acc_ref):
    i = pl.program_id(0)
    emb = emb_ref[...]
    fr = fr_ref[...]
    h = jnp.sin(fr * (jnp.dot(emb.astype(BF16), w1_ref[...].astype(BF16), preferred_element_type=F32) + b1_ref[...]))
    h = jnp.sin(fr * (jnp.dot(h.astype(BF16), w2_ref[...].astype(BF16), preferred_element_type=F32) + b2_ref[...]))
    h = jnp.dot(h.astype(BF16), w3_ref[...].astype(BF16), preferred_element_type=F32)
    h = h * jnp.exp(-emb[:, 0:1] * jnp.abs(dec_ref[...]))
    n = h.shape[0]
    row = lax.broadcasted_iota(jnp.int32, (n, 4 * D_C), 0) + i * n
    col = lax.broadcasted_iota(jnp.int32, (n, 4 * D_C), 1)
    is_bwd = ((col // D_C) % 2) == 1
    h = jnp.where(jnp.logical_and(is_bwd, row == 0), 0.0, h)
    h_ref[...] = h

    @pl.when(i == 0)
    def _():
        acc_ref[...] = jnp.zeros_like(acc_ref)

    acc_ref[...] += jnp.sum(h * h, axis=0, keepdims=True)
    ss_ref[...] = acc_ref[...]


def _hfilt_freq_kernel(h_ref, ss_ref, c_ref, s_ref, hr_ref, hi_ref, a_ref, b_ref, nyq_ref):
    i = pl.program_id(0)

    @pl.when(i == 0)
    def _():
        for o in range(2):
            hf = h_ref[:, o * 2 * D_C:o * 2 * D_C + D_C]
            hb = h_ref[:, o * 2 * D_C + D_C:(o + 1) * 2 * D_C]
            a = hf + hb
            n = a.shape[0]
            alt = 1.0 - 2.0 * (lax.broadcasted_iota(jnp.int32, (n, 1), 0) % 2).astype(F32)
            nyq_ref[:, o * D_C:(o + 1) * D_C] = jnp.sum(a * alt, axis=0, keepdims=True)
            a_ref[:, o * D_C:(o + 1) * D_C] = a.astype(BF16)
            b_ref[:, o * D_C:(o + 1) * D_C] = (hb - hf).astype(BF16)

    ss = ss_ref[...]
    tot = jnp.concatenate([ss[:, 0:D_C] + ss[:, D_C:2 * D_C], ss[:, 2 * D_C:3 * D_C] + ss[:, 3 * D_C:4 * D_C]], axis=1)
    scale = lax.rsqrt(tot + 1e-6)

    hr_ref[...] = jnp.dot(c_ref[...], a_ref[...], preferred_element_type=F32) * scale
    hi = jnp.dot(s_ref[...], b_ref[...], preferred_element_type=F32) * scale
    tf = hi.shape[0]
    row = lax.broadcasted_iota(jnp.int32, (tf, 1), 0) + i * tf
    hi_ref[...] = jnp.where(row == 0, nyq_ref[...] * scale, hi)


def _hyena_filters(L, p, tabs):
    emb, cmat, smat = tabs
    tr = 256
    w1 = jnp.pad(p['hy_w1'], ((0, 128 - HY_EMB), (0, 0)))
    full = lambda a: pl.BlockSpec(a.shape, lambda i: (0,) * a.ndim)
    params = [w1, p['hy_b1'].reshape(1, -1), p['hy_freq'].reshape(1, -1), p['hy_w2'], p['hy_b2'].reshape(1, -1),
              p['hy_w3'], jnp.tile(p['hy_decay'], 4).reshape(1, -1)]
    h_raw, ss = pl.pallas_call(
        _hfilt_time_kernel, grid=(L // tr,),
        in_specs=[pl.BlockSpec((tr, 128), lambda i: (i, 0))] + [full(a) for a in params],
        out_specs=[pl.BlockSpec((tr, 4 * D_C), lambda i: (i, 0)), pl.BlockSpec((1, 4 * D_C), lambda i: (0, 0))],
        out_shape=[jax.ShapeDtypeStruct((L, 4 * D_C), F32), jax.ShapeDtypeStruct((1, 4 * D_C), F32)],
        scratch_shapes=[pltpu.VMEM((1, 4 * D_C), F32)],
        compiler_params=_cparams(1), name="hyena_filter_time",
    )(emb, *params)
    tf = 256
    mat = pl.BlockSpec((tf, L), lambda i: (i, 0))
    return pl.pallas_call(
        _hfilt_freq_kernel, grid=(L // tf,),
        in_specs=[pl.BlockSpec((L, 4 * D_C), lambda i: (0, 0)), pl.BlockSpec((1, 4 * D_C), lambda i: (0, 0)),
                  mat, mat],
        out_specs=[pl.BlockSpec((tf, 2 * D_C), lambda i: (i, 0))] * 2,
        out_shape=[jax.ShapeDtypeStruct((L, 2 * D_C), F32)] * 2,
        scratch_shapes=[pltpu.VMEM((L, 2 * D_C), BF16)] * 2 + [pltpu.VMEM((1, 2 * D_C), F32)],
        compiler_params=_cparams(1, 48 * 1024 * 1024), name="hyena_filter_freq",
    )(h_raw, ss, cmat, smat)


def _hconv_kernel(z_ref, zt_ref, gate_ref, hr_ref, hi_ref, bias_ref, c_ref, s_ref, o_ref,
                  zb_ref, yr_ref, yi_ref, nyq_ref, *, BG, L):
    ph = pl.program_id(1)
    i = pl.program_id(2)
    inv_n = 1.0 / (2 * L)

    @pl.when(jnp.logical_and(ph == 0, i == 0))
    def _():
        alt = 1.0 - 2.0 * (lax.broadcasted_iota(jnp.int32, (L, 1), 0) % 2).astype(F32)
        for b in range(BG):
            z = z_ref[b]
            zb_ref[:, b * D_C:(b + 1) * D_C] = z.astype(BF16)
            nyq_ref[:, b * D_C:(b + 1) * D_C] = jnp.sum(z * alt, axis=0, keepdims=True)

    tf = c_ref.shape[0]

    @pl.when(ph == 0)
    def _():
        zr = jnp.dot(c_ref[...], zb_ref[...], preferred_element_type=F32)
        zs = jnp.dot(s_ref[...], zb_ref[...], preferred_element_type=F32)
        hr = hr_ref[...]
        hi = hi_ref[...]
        row = lax.broadcasted_iota(jnp.int32, (tf, 1), 0) + i * tf
        wgt = jnp.where(row == 0, inv_n, 2.0 * inv_n)
        rows = pl.ds(pl.multiple_of(i * tf, tf), tf)
        for b in range(BG):
            sl = slice(b * D_C, (b + 1) * D_C)
            yr_ref[rows, sl] = ((zr[:, sl] * hr + zs[:, sl] * hi) * wgt).astype(BF16)
            yi_ref[rows, sl] = ((zr[:, sl] * hi - zs[:, sl] * hr) * (2.0 * inv_n)).astype(BF16)

        @pl.when(i == 0)
        def _():
            for b in range(BG):
                sl = slice(b * D_C, (b + 1) * D_C)
                nyq_ref[:, sl] = nyq_ref[:, sl] * hi[0:1, :] * inv_n

    @pl.when(ph == 1)
    def _():
        y = (jnp.dot(c_ref[...], yr_ref[...], preferred_element_type=F32)
             - jnp.dot(s_ref[...], yi_ref[...], preferred_element_type=F32))
        row = lax.broadcasted_iota(jnp.int32, (tf, 1), 0) + i * tf
        alt = 1.0 - 2.0 * (row % 2).astype(F32)
        y = y + alt * nyq_ref[...]
        for b in range(BG):
            sl = slice(b * D_C, (b + 1) * D_C)
            o_ref[b] = gate_ref[b] * (y[:, sl] + bias_ref[...] * zt_ref[b])


def _hconv(z, gate, hr, hi, bias, order, tabs):
    B, L, _ = z.shape
    _, ch, sh = tabs
    BG =min(B, 8) if L <= 256 else min(B, 4)
    assert B % BG == 0
    tf = 256
    nt = L // tf
    mat = pl.BlockSpec((tf, L), lambda g, ph, i: (i, 0))
    tile = pl.BlockSpec((BG, tf, D_C), lambda g, ph, i: (g, i * ph, 0))
    hspec = pl.BlockSpec((tf, D_C), lambda g, ph, i: (i * (1 - ph), order))
    cols = BG * D_C
    return pl.pallas_call(
        functools.partial(_hconv_kernel, BG=BG, L=L), grid=(B // BG, 2, nt),
        in_specs=[pl.BlockSpec((BG, L, D_C), lambda g, ph, i: (g, 0, 0)), tile, tile, hspec, hspec,
                  pl.BlockSpec((1, D_C), lambda g, ph, i: (0, 0)), mat, mat],
        out_specs=tile,
        out_shape=jax.ShapeDtypeStruct((B, L, D_C), F32),
        scratch_shapes=[pltpu.VMEM((L, cols), BF16)] * 3 + [pltpu.VMEM((1, cols), F32)],
        compiler_params=_cparams(3, 48 * 1024 * 1024), name="hyena_conv",
    )(z, z, gate, hr, hi, bias[order].reshape(1, D_C), ch, sh)


def _outproj_kernel(x_ref, yf_ref, yb_ref, bon_ref, g_ref, ob_ref, yc_ref, mod_ref,
                    lnw_ref, lnb_ref, gpost_ref, gffn_ref, w_ref, x1_ref, h2_ref):
    bd = _block_diag(D_A, RW_HEAD)
    y = yf_ref[0] + yb_ref[0]
    mu = _segsum(y, bd) * (1.0 / RW_HEAD)
    yc = y - mu
    var = _segsum(yc * yc, bd) * (1.0 / RW_HEAD)
    yn = yc * lax.rsqrt(var + GN_EPS) * lnw_ref[...] + lnb_ref[...]
    ya = (yn + bon_ref[0]) * g_ref[0]
    mix = (jnp.dot(ya.astype(BF16), w_ref[0:D_A, :], preferred_element_type=F32)
           + jnp.dot(ob_ref[0].astype(BF16), w_ref[D_A:D_A + D_B, :], preferred_element_type=F32)
           + jnp.dot(yc_ref[0].astype(BF16), w_ref[D_A + D_B:D_MODEL, :], preferred_element_type=F32))
    mod = mod_ref[0]
    gt1 = mod[:, 2 * D_MODEL:3 * D_MODEL]
    sh2 = mod[:, 3 * D_MODEL:4 * D_MODEL]
    sc2 = mod[:, 4 * D_MODEL:5 * D_MODEL]
    x1 = x_ref[0] + gt1 * _rms(mix, gpost_ref[...])
    x1_ref[0] = x1
    h2_ref[0] = (_rms(x1, gffn_ref[...]) * (1.0 + sc2) + sh2).astype(BF16)


def _outproj(x, yf, yb, bon, g, ob, yc, mod, p, w_out):
    B, L, _ = x.shape
    mb = mod.shape[0]
    row = lambda b, i: (b, i, 0)
    c2 = lambda b, i: (0, 0)
    t = lambda n: pl.BlockSpec((1, TL, n), row)
    in_specs = [t(D_MODEL), t(D_A), t(D_A), t(D_A), t(D_A), t(D_B), t(D_C),
                pl.BlockSpec((1, 1, 6 * D_MODEL), (lambda b, i: (b, 0, 0)) if mb > 1 else (lambda b, i: (0, 0, 0))),
                pl.BlockSpec((1, D_A), c2), pl.BlockSpec((1, D_A), c2),
                pl.BlockSpec((1, D_MODEL), c2), pl.BlockSpec((1, D_MODEL), c2),
                _layer_weight_spec(w_out, 2)]
    return pl.pallas_call(
        _outproj_kernel, grid=(B, L // TL), in_specs=in_specs,
        out_specs=[t(D_MODEL), t(D_MODEL)],
        out_shape=[jax.ShapeDtypeStruct((B, L, D_MODEL), F32), jax.ShapeDtypeStruct((B, L, D_MODEL), BF16)],
        compiler_params=_cparams(2, 40 * 1024 * 1024), name="outproj",
    )(x, yf, yb, bon, g, ob, yc, mod, p['rwkv_ln_w'].reshape(1, -1), p['rwkv_ln_b'].reshape(1, -1),
      p['g_mix_post'].reshape(1, -1), p['g_ffn_pre'].reshape(1, -1), w_out[0])


def _ffn_kernel(h_ref, x_ref, mod_ref, g_ref, w1_ref, w2_ref, o_ref):
    h = h_ref[0]
    acc = None
    ck = 1024
    for c in range(D_FF // ck):
        a = jnp.dot(h, w1_ref[:, c * ck:(c + 1) * ck], preferred_element_type=F32)
        a = jnp.square(jnp.maximum(a, 0.0)).astype(BF16)
        part = jnp.dot(a, w2_ref[c * ck:(c + 1) * ck, :], preferred_element_type=F32)
        acc = part if acc is None else acc + part
    gt2 = mod_ref[0][:, 5 * D_MODEL:6 * D_MODEL]
    o_ref[0] = x_ref[0] + gt2 * _rms(acc, g_ref[...])


def _ffn(h2, x1, mod, g_post, w1, w2):
    B, L, _ = x1.shape
    mb = mod.shape[0]
    row = lambda b, i: (b, i, 0)
    c2 = lambda b, i: (0, 0)
    return pl.pallas_call(
        _ffn_kernel, grid=(B, L // TL),
        in_specs=[pl.BlockSpec((1, TL, D_MODEL), row), pl.BlockSpec((1, TL, D_MODEL), row),
                  pl.BlockSpec((1, 1, 6 * D_MODEL), (lambda b, i: (b, 0, 0)) if mb > 1 else (lambda b, i: (0, 0, 0))),
                  pl.BlockSpec((1, D_MODEL), c2),
                  _layer_weight_spec(w1, 2), _layer_weight_spec(w2, 2)],
        out_specs=pl.BlockSpec((1, TL, D_MODEL), row),
        out_shape=jax.ShapeDtypeStruct((B, L, D_MODEL), F32),
        compiler_params=_cparams(2, V7X_VMEM_LIMIT), name="ffn",
    )(h2, x1, mod, g_post.reshape(1, -1), w1[0], w2[0])


@functools.lru_cache(maxsize=None)
def _rope_tables(L):
    rows = L // GRID_W
    row = np.repeat(np.arange(rows), GRID_W).astype(np.float64)
    col = np.tile(np.arange(GRID_W), rows).astype(np.float64)
    half = DIFF_DH // 2
    inv = ROPE_BASE ** (-np.arange(0, half, 2, dtype=np.float64) / half)
    ang_r = row[:, None] * inv[None]
    ang_c = col[:, None] * inv[None]
    ang = np.concatenate([ang_r, ang_r, ang_c, ang_c], axis=-1)
    cos, sin = np.cos(ang), np.sin(ang)
    quarter = (np.arange(DIFF_DH) // 16) % 2
    sin_a = np.where(quarter == 0, -sin, 0.0)
    sin_b = np.where(quarter == 1, sin, 0.0)
    rep = D_B // DIFF_DH
    return tuple(np.tile(t, (1, rep)).astype(np.float32) for t in (cos, sin_a, sin_b))


@functools.lru_cache(maxsize=None)
def _hyena_tables(L):
    t = np.linspace(0.0, 1.0, L, dtype=np.float32).astype(np.float64)[:, None]
    ang = (2.0 * math.pi / L) * np.arange(L, dtype=np.float64)[:, None]
    bands = np.linspace(1e-4, HY_BANDS - 1, HY_BANDS, dtype=np.float32).astype(np.float64)[None, :]
    emb = np.concatenate([t, np.cos(bands * ang), -np.sin(bands * ang)], axis=-1)
    emb = np.pad(emb, ((0, 0), (0, 128 - HY_EMB))).astype(np.float32)
    n = np.arange(L, dtype=np.int64)
    theta = ((n[:, None] * n[None, :]) % (2 * L)).astype(np.float64) * (math.pi / L)
    return emb, np.cos(theta).astype(BF16), np.sin(theta).astype(BF16)


def _layer(x, mod, p, wb, l, lam_init, cache, rope_tabs, hy_tabs, filt):
    B, L, _ = x.shape
    ua, uq, uk, uv, uh = _inproj(x, mod, p['g_mix_pre'], wb['w_in'], rope_tabs)
    (r, v, nkk, wf, kaf, kdf, wbk, kab, kdb, g, bon, x1h, x2h, hv) = _prep(ua, uh, p)
    if cache is None:
        s0 = jnp.zeros((B, 2, H_A, RW_HEAD, RW_HEAD), F32)
        kv_cache = None
    else:
        s0 = cache[0]
        kv_cache = (cache[1], cache[2])
    yf, yb, sfin = _scan(nkk, r, v, wf, kaf, kdf, wbk, kab, kdb, s0)
    ob = _attention(uq, uk, uv, kv_cache, l, p, lam_init)
    hr, hi = filt
    z1 = _hconv(hv, x1h, hr, hi, p['hy_bias'], 0, hy_tabs)
    yc = _hconv(z1, x2h, hr, hi, p['hy_bias'], 1, hy_tabs)
    x1, h2 = _outproj(x, yf, yb, bon, g, ob, yc, mod, p, wb['w_out'])
    x2 = _ffn(h2, x1, mod, p['g_ffn_post'], wb['w_ff1'], wb['w_ff2'])
    return x2, (sfin, uk, uv)


_LAYER_KEYS = ('g_mix_pre', 'g_mix_post', 'g_ffn_pre', 'g_ffn_post', 'rwkv_conv', 'rwkv_w0', 'rwkv_w2', 'rwkv_a0',
               'rwkv_a2', 'rwkv_g2', 'rwkv_kk', 'rwkv_ka', 'rwkv_rk', 'rwkv_ln_w', 'rwkv_ln_b', 'diff_lq1', 'diff_lk1',
               'diff_lq2', 'diff_lk2', 'diff_subln', 'hy_conv_w', 'hy_conv_b', 'hy_w1', 'hy_b1', 'hy_freq', 'hy_w2',
               'hy_b2', 'hy_w3', 'hy_decay', 'hy_bias')


def kernel(x_prompt, x_sample, state_rwkv, cache_k, cache_v, c, c_ctx, w_mod, b_mod, g_mix_pre, g_mix_post, g_ffn_pre, g_ffn_post, w_in, rwkv_conv, rwkv_w0, rwkv_w2, rwkv_a0, rwkv_a2, rwkv_g2, rwkv_kk, rwkv_ka, rwkv_rk, rwkv_ln_w, rwkv_ln_b, diff_lq1, diff_lk1, diff_lq2, diff_lk2, diff_subln, hy_conv_w, hy_conv_b, hy_w1, hy_b1, hy_freq, hy_w2, hy_b2, hy_w3, hy_decay, hy_bias, w_out, w_ff1, w_ff2):
    stacked = dict(g_mix_pre=g_mix_pre, g_mix_post=g_mix_post, g_ffn_pre=g_ffn_pre, g_ffn_post=g_ffn_post,
                   rwkv_conv=rwkv_conv, rwkv_w0=rwkv_w0, rwkv_w2=rwkv_w2, rwkv_a0=rwkv_a0, rwkv_a2=rwkv_a2,
                   rwkv_g2=rwkv_g2, rwkv_kk=rwkv_kk, rwkv_ka=rwkv_ka, rwkv_rk=rwkv_rk, rwkv_ln_w=rwkv_ln_w,
                   rwkv_ln_b=rwkv_ln_b, diff_lq1=diff_lq1, diff_lk1=diff_lk1, diff_lq2=diff_lq2, diff_lk2=diff_lk2,
                   diff_subln=diff_subln, hy_conv_w=hy_conv_w, hy_conv_b=hy_conv_b, hy_w1=hy_w1, hy_b1=hy_b1,
                   hy_freq=hy_freq, hy_w2=hy_w2, hy_b2=hy_b2, hy_w3=hy_w3, hy_decay=hy_decay, hy_bias=hy_bias)
    Bc, Lc, _ = x_prompt.shape
    Bs, Ls, _ = x_sample.shape
    past = cache_k.shape[2]

    cond = jnp.zeros((8, D_MODEL), F32).at[0:Bs].set(c).at[Bs].set(c_ctx)
    mod_all = _modulation(cond, w_mod, b_mod)

    rope_tabs = _rope_tables(Ls)
    tabs_c = _hyena_tables(Lc)
    tabs_s = _hyena_tables(Ls)
    ck = cache_k.reshape(Bs, DEPTH, past, D_B)
    cv = cache_v.reshape(Bs, DEPTH, past, D_B)

    w_in_b, w_out_b, w_ff1_b, w_ff2_b = (w.astype(BF16) for w in (w_in, w_out, w_ff1, w_ff2))
    xp, xs = x_prompt, x_sample
    st_list, k_list, v_list = [], [], []
    for l in range(DEPTH):
        p = {k: stacked[k][l] for k in _LAYER_KEYS}
        wb = dict(w_in=(w_in_b, l), w_out=(w_out_b, l), w_ff1=(w_ff1_b, l), w_ff2=(w_ff2_b, l))
        lam_init = 0.8 - 0.6 * math.exp(-0.3 * l)
        mod_lat = mod_all[l, 0:Bs].reshape(Bs, 1, 6 * D_MODEL)
        mod_ctx = mod_all[l, Bs:Bs + 1].reshape(1, 1, 6 * D_MODEL)
        filt_c = _hyena_filters(Lc, p, tabs_c)
        filt_s = _hyena_filters(Ls, p, tabs_s)
        xp, (s_ctx, k_ctx, v_ctx) = _layer(xp, mod_ctx, p, wb, l, lam_init, None, None, tabs_c, filt_c)
        st_list.append(s_ctx)
        k_list.append(k_ctx.reshape(Bc, Lc, H_B, 2, DIFF_DH))
        v_list.append(v_ctx.reshape(Bc, Lc, H_B, 2 * DIFF_DH))
        s0 = state_rwkv[:, l]
        xs, _ = _layer(xs, mod_lat, p, wb, l, lam_init, (s0, ck, cv), rope_tabs, tabs_s, filt_s)
    return (xp, xs, jnp.stack(st_list, axis=1), jnp.stack(k_list, axis=1), jnp.stack(v_list, axis=1))
```

```python
import functools
import math

import jax
import jax.numpy as jnp
import numpy as np
from jax import lax
from jax.experimental import pallas as pl
from jax.experimental.pallas import tpu as pltpu

F32 = jnp.float32
BF16 = jnp.bfloat16

D_MODEL = 1024
DEPTH = 4
GRID_W = 64
D_A = 256
RW_HEAD = 64
H_A = 4
D_B = 512
DIFF_DH = 64
H_B = 4
D_C = 256
HY_BANDS = 16
HY_EMB = 33
HY_FFN = 64
D_FF = 4096
ROPE_BASE = 10000.0
RMS_EPS = 1e-6
GN_EPS = 64e-5
N_A_COLS = 1152
IN_COLS = 3456

TL = 256
V7X_VMEM_LIMIT = 56 * 1024 * 1024


def _cparams(n_axes, vmem=None):
    return pltpu.CompilerParams(dimension_semantics=("arbitrary",) * n_axes,
                                vmem_limit_bytes=vmem)


def _split3(a):
    hi = a.astype(BF16)
    r1 = a - hi.astype(F32)
    mid = r1.astype(BF16)
    lo = (r1 - mid.astype(F32)).astype(BF16)
    return hi, mid, lo


def _split2(a):
    hi = a.astype(BF16)
    lo = (a - hi.astype(F32)).astype(BF16)
    return hi, lo


def _segsum(a, bd):
    hi, lo = _split2(a)
    return jnp.dot(hi, bd, preferred_element_type=F32) + jnp.dot(lo, bd, preferred_element_type=F32)


def _block_diag(n, seg):
    r = lax.broadcasted_iota(jnp.int32, (n, n), 0) // seg
    c = lax.broadcasted_iota(jnp.int32, (n, n), 1) // seg
    return (r == c).astype(BF16)


def _layer_weight_spec(w, n_grid_axes):
    stack, layer = w
    _, k, n = stack.shape
    if n_grid_axes == 2:
        return pl.BlockSpec((None, k, n), lambda b, i: (layer, 0, 0))
    raise NotImplementedError(n_grid_axes)


def _rms(x, g):
    return x * lax.rsqrt(jnp.mean(x * x, axis=-1, keepdims=True) + RMS_EPS) * g


def _mod_kernel(cond_ref, w_ref, b_ref, o_ref):
    c = cond_ref[...]
    s = (c * jax.nn.sigmoid(c)).astype(BF16)
    o_ref[0] = jnp.dot(s, w_ref[0].astype(BF16), preferred_element_type=F32) + b_ref[0]


def _modulation(cond, w_mod, b_mod):
    tn = 1536
    return pl.pallas_call(
        _mod_kernel,
        grid=(DEPTH, 6 * D_MODEL // tn),
        in_specs=[pl.BlockSpec((8, D_MODEL), lambda l, n: (0, 0)),
                  pl.BlockSpec((1, D_MODEL, tn), lambda l, n: (l, 0, n)),
                  pl.BlockSpec((1, 1, tn), lambda l, n: (l, 0, n))],
        out_specs=pl.BlockSpec((1, 8, tn), lambda l, n: (l, 0, n)),
        out_shape=jax.ShapeDtypeStruct((DEPTH, 8, 6 * D_MODEL), F32),
        compiler_params=_cparams(2, 40 * 1024 * 1024),
        name="modulation",
    )(cond, w_mod, b_mod.reshape(DEPTH, 1, 6 * D_MODEL))


def _rope(x, cos, sin_a, sin_b):
    n = x.shape[-1]
    return x * cos + pltpu.roll(x, n - 16, axis=1) * sin_a + pltpu.roll(x, 16, axis=1) * sin_b


def _inproj_kernel(*refs, rope):
    if rope:
        x_ref, mod_ref, g_ref, w_ref, cos_ref, sa_ref, sb_ref, ua_ref, uq_ref, uk_ref, uv_ref, uh_ref = refs
    else:
        x_ref, mod_ref, g_ref, w_ref, ua_ref, uq_ref, uk_ref, uv_ref, uh_ref = refs
    x = x_ref[0]
    mod = mod_ref[0]
    sh1 = mod[:, 0:D_MODEL]
    sc1 = mod[:, D_MODEL:2 * D_MODEL]
    h = _rms(x, g_ref[...]) * (1.0 + sc1) + sh1
    u = jnp.dot(h.astype(BF16), w_ref[...], preferred_element_type=F32)
    ua_ref[0] = u[:, 0:N_A_COLS]
    q = u[:, N_A_COLS:N_A_COLS + D_B] * (DIFF_DH ** -0.5 * math.log2(math.e))
    k = u[:, N_A_COLS + D_B:N_A_COLS + 2 * D_B]
    if rope:
        q = _rope(q, cos_ref[...], sa_ref[...], sb_ref[...])
        k = _rope(k, cos_ref[...], sa_ref[...], sb_ref[...])
    uq_ref[0] = q.astype(uq_ref.dtype)
    uk_ref[0] = k.astype(uk_ref.dtype)
    uv_ref[0] = u[:, N_A_COLS + 2 * D_B:N_A_COLS + 3 * D_B].astype(uv_ref.dtype)
    uh_ref[0] = u[:, N_A_COLS + 3 * D_B:IN_COLS]


def _inproj(x, mod, g_pre, w_in, rope_tabs):
    B, L, _ = x.shape
    rope = rope_tabs is not None
    mb = mod.shape[0]
    row = lambda b, i: (b, i, 0)
    const2 = lambda b, i: (0, 0)
    in_specs = [pl.BlockSpec((1, TL, D_MODEL), row),
                pl.BlockSpec((1, 1, 6 * D_MODEL), (lambda b, i: (b, 0, 0)) if mb > 1 else (lambda b, i: (0, 0, 0))),
                pl.BlockSpec((1, D_MODEL), const2),
                _layer_weight_spec(w_in, 2)]
    args = [x, mod, g_pre.reshape(1, D_MODEL), w_in[0]]
    if rope:
        in_specs += [pl.BlockSpec((TL, D_B), lambda b, i: (i, 0))] * 3
        args += list(rope_tabs)
    qkv_dt = BF16 if rope else F32
    out_shape = [jax.ShapeDtypeStruct((B, L, N_A_COLS), F32),
                 jax.ShapeDtypeStruct((B, L, D_B), qkv_dt),
                 jax.ShapeDtypeStruct((B, L, D_B), qkv_dt),
                 jax.ShapeDtypeStruct((B, L, D_B), qkv_dt),
                 jax.ShapeDtypeStruct((B, L, 3 * D_C), F32)]
    out_specs = [pl.BlockSpec((1, TL, N_A_COLS), row), pl.BlockSpec((1, TL, D_B), row),
                 pl.BlockSpec((1, TL, D_B), row), pl.BlockSpec((1, TL, D_B), row),
                 pl.BlockSpec((1, TL, 3 * D_C), row)]
    return pl.pallas_call(
        functools.partial(_inproj_kernel, rope=rope),
        grid=(B, L // TL), in_specs=in_specs, out_specs=out_specs, out_shape=out_shape,
        compiler_params=_cparams(2, 48 * 1024 * 1024), name="inproj",
    )(*args)


def _shift_conv(u, prev_row, next_row, cw):
    n = u.shape[0]
    row = lax.broadcasted_iota(jnp.int32, (n, 1), 0)
    up = jnp.where(row == 0, prev_row, pltpu.roll(u, 1, axis=0))
    un = jnp.where(row == n - 1, next_row, pltpu.roll(u, n - 1, axis=0))
    return cw[0:1] * up + cw[1:2] * u + cw[2:3] * un


def _prep_kernel(ua_ref, uap_ref, uan_ref, uh_ref, uhp_ref, uhn_ref,
                 cw_ref, w0_ref, w2_ref, a0_ref, a2_ref, g2_ref, kkw_ref, kaw_ref, rk_ref, hcw_ref, hcb_ref,
                 r_ref, v_ref, nkk_ref, wf_ref, kaf_ref, kdf_ref, wb_ref, kab_ref, kdb_ref, g_ref, bon_ref,
                 x1_ref, x2_ref, hv_ref):
    i = pl.program_id(1)
    nt = pl.num_programs(1)
    has_prev = i > 0
    has_next = i < nt - 1
    bd = _block_diag(D_A, RW_HEAD)

    ua = ua_ref[0]
    u_rkv = ua[:, 0:3 * D_A]
    prev = jnp.where(has_prev, uap_ref[0, 7:8, :], 0.0)
    nxt = jnp.where(has_next, uan_ref[0, 0:1, :], 0.0)
    rkv = _shift_conv(u_rkv, prev, nxt, cw_ref[...])
    r = rkv[:, 0:D_A]
    k = rkv[:, D_A:2 * D_A]
    v = rkv[:, 2 * D_A:3 * D_A]
    u_w = ua[:, 768:896]
    u_a = ua[:, 896:1024]
    u_g = ua[:, 1024:1152]

    g = jnp.dot(jax.nn.sigmoid(u_g).astype(BF16), g2_ref[...].astype(BF16), preferred_element_type=F32)
    kk = k * kkw_ref[...]
    kk = kk * lax.rsqrt(_segsum(kk * kk, bd) + 1e-12)
    r_ref[0] = r
    v_ref[0] = v
    nkk_ref[0] = -kk
    g_ref[0] = g

    kd_sum = None
    outs = ((wf_ref, kaf_ref, kdf_ref), (wb_ref, kab_ref, kdb_ref))
    for d in range(2):
        uw_d = u_w[:, d * 64:(d + 1) * 64]
        ua_d = u_a[:, d * 64:(d + 1) * 64]
        xw = w0_ref[d:d + 1, :] + jnp.dot(jnp.tanh(uw_d).astype(BF16), w2_ref[d].astype(BF16),
                                          preferred_element_type=F32)
        z = -xw
        softplus = jnp.maximum(z, 0.0) + jnp.log(1.0 + jnp.exp(-jnp.abs(z)))
        wl = -softplus - 0.5
        log_decay = -jnp.exp(wl)
        a = jax.nn.sigmoid(a0_ref[d:d + 1, :] + jnp.dot(ua_d.astype(BF16), a2_ref[d].astype(BF16),
                                                        preferred_element_type=F32))
        kd = k * (1.0 + (a - 1.0) * kaw_ref[...])
        w_o, ka_o, kd_o = outs[d]
        w_o[0] = log_decay
        ka_o[0] = kk * a
        kd_o[0] = kd
        kd_sum = kd if kd_sum is None else kd_sum + kd
    bon_ref[0] = _segsum(r * kd_sum * rk_ref[...], bd) * v

    uh = uh_ref[0]
    prevh = jnp.where(has_prev, uhp_ref[0, 7:8, :], 0.0)
    nxth = jnp.where(has_next, uhn_ref[0, 0:1, :], 0.0)
    hc = _shift_conv(uh, prevh, nxth, hcw_ref[...]) + hcb_ref[...]
    x1_ref[0] = hc[:, 0:D_C]
    x2_ref[0] = hc[:, D_C:2 * D_C]
    hv_ref[0] = hc[:, 2 * D_C:3 * D_C]


def _prep(ua, uh, p):
    B, L, _ = ua.shape
    nb8 = L // 8
    t8 = TL // 8
    row = lambda b, i: (b, i, 0)
    prev = lambda b, i: (b, jnp.maximum(i * t8 - 1, 0), 0)
    nxt = lambda b, i: (b, jnp.minimum((i + 1) * t8, nb8 - 1), 0)

    def full(a):
        nd = a.ndim
        return pl.BlockSpec(a.shape, lambda b, i: (0,) * nd)

    params = [p['rwkv_conv'], p['rwkv_w0'], p['rwkv_w2'], p['rwkv_a0'], p['rwkv_a2'], p['rwkv_g2'],
              p['rwkv_kk'].reshape(1, D_A), p['rwkv_ka'].reshape(1, D_A), p['rwkv_rk'].reshape(1, D_A),
              p['hy_conv_w'], p['hy_conv_b'].reshape(1, 3 * D_C)]
    in_specs = [pl.BlockSpec((1, TL, N_A_COLS), row), pl.BlockSpec((1, 8, 3 * D_A), prev),
                pl.BlockSpec((1, 8, 3 * D_A), nxt),
                pl.BlockSpec((1, TL, 3 * D_C), row), pl.BlockSpec((1, 8, 3 * D_C), prev),
                pl.BlockSpec((1, 8, 3 * D_C), nxt)] + [full(a) for a in params]
    n_out = 14
    return pl.pallas_call(
        _prep_kernel, grid=(B, L // TL), in_specs=in_specs,
        out_specs=[pl.BlockSpec((1, TL, D_A), row)] * n_out,
        out_shape=[jax.ShapeDtypeStruct((B, L, D_A), F32)] * n_out,
        compiler_params=_cparams(2, 40 * 1024 * 1024), name="prep",
    )(ua, ua, ua, uh, uh, uh, *params)


CHUNK = 64
CHUNKS_PER_ITER = 4
_NN = (((1,), (0,)), ((), ()))
_NT = (((1,), (1,)), ((), ()))
_TN = (((0,), (0,)), ((), ()))


def _mm3(a, b, dims):
    ah, al = _split2(a)
    bh, bl = _split2(b)
    f = lambda x, y: lax.dot_general(x, y, dims, preferred_element_type=F32)
    if dims is _TN:
        return f(ah, bh) + f(ah, bl) + f(al, bh)
    m = a.shape[0]
    both = f(jnp.concatenate([ah, al], axis=0), bh)
    return both[0:m] + both[m:2 * m] + f(ah, bl)


def _mm2(a, b, dims):
    ah = a.astype(BF16)
    bh, bl = _split2(b)
    n = b.shape[1]
    both = lax.dot_general(ah, jnp.concatenate([bh, bl], axis=1), dims, preferred_element_type=F32)
    return both[:, 0:n] + both[:, n:2 * n]


def _mm1(a, b, dims):
    return lax.dot_general(a.astype(BF16), b.astype(BF16), dims, preferred_element_type=F32)


def _unit_tri_inverses(ms, row, col):
    eye = (row == col).astype(F32)
    blk = lambda n: (row // n) == (col // n)
    m8 = [jnp.where(blk(8), m, 0.0) for m in ms]
    a2 = [_mm1(x, x, _NN) for x in m8]
    a4 = [_mm1(x, x, _NN) for x in a2]
    p = [eye + x + y + _mm1(x, y, _NN) for x, y in zip(m8, a2)]
    ts = [x + _mm1(x, y, _NN) for x, y in zip(p, a4)]
    n = 8
    while n < CHUNK:
        sel = jnp.logical_and(blk(2 * n), jnp.logical_not(blk(n)))
        ot = [_mm1(jnp.where(sel, m, 0.0), t, _NN) for m, t in zip(ms, ts)]
        ts = [t + _mm1(t, o, _NN) for t, o in zip(ts, ot)]
        n *= 2
    return ts


def _scan_kernel(nkkf_ref, rf_ref, vf_ref, lwf_ref, kaf_ref, kdf_ref,
                 nkkb_ref, rb_ref, vb_ref, lwb_ref, kab_ref, kdb_ref, s0_ref,
                 yf_ref, yb_ref, sfin_ref, s_ref):
    C = CHUNK
    j = pl.program_id(1)
    nj = pl.num_programs(1)
    row = lax.broadcasted_iota(jnp.int32, (C, C), 0)
    col = lax.broadcasted_iota(jnp.int32, (C, C), 1)
    dirs = ((nkkf_ref, rf_ref, vf_ref, lwf_ref, kaf_ref, kdf_ref, yf_ref),
            (nkkb_ref, rb_ref, vb_ref, lwb_ref, kab_ref, kdb_ref, yb_ref))
    n_chunks = nkkf_ref.shape[1] // C

    @pl.when(j == 0)
    def _():
        s_ref[...] = s0_ref[0]

    def build(c, d, probs):
        rows = pl.ds(pl.multiple_of(c * C, C), C)
        nkk_ref, r_ref, v_ref, lw_ref, b_ref, k_ref = dirs[d][0:6]
        alpha = nkk_ref[0, rows, :]
        r = r_ref[0, rows, :]
        v = v_ref[0, rows, :]
        before = (row > col) if d == 0 else (row < col)
        upto = (row >= col) if d == 0 else (row <= col)
        lw = lw_ref[0, rows, :]
        beta = b_ref[0, rows, :]
        kd = k_ref[0, rows, :]
        tri = upto.astype(BF16)
        l_hi, l_mid, l_lo = _split3(lw)
        lam = (jnp.dot(tri, l_hi, preferred_element_type=F32) + jnp.dot(tri, l_mid, preferred_element_type=F32)
               + jnp.dot(tri, l_lo, preferred_element_type=F32))
        lam_prev = lam - lw
        lam_mid = lam[C // 2:C // 2 + 1, :]
        lam_end = lam[C - 1:C, :] if d == 0 else lam[0:1, :]
        e_to_mid = jnp.exp(lam_mid - lam)
        a_m = alpha * jnp.exp(lam_prev - lam_mid)
        b_m = beta * e_to_mid
        k_m = kd * e_to_mid
        r_m = r * jnp.exp(lam - lam_mid)
        a_0 = alpha * jnp.exp(lam_prev)
        r_0 = r * jnp.exp(lam)
        e_to_end = jnp.exp(lam_end - lam)
        b_e = beta * e_to_end
        k_e = kd * e_to_end
        p_end = jnp.exp(lam_end)
        for h in range(H_A):
            sl = slice(h * RW_HEAD, (h + 1) * RW_HEAD)
            probs.append(dict(
                rows=rows, d=d, h=h, before=before, upto=upto, pc=p_end[:, sl],
                ar=jnp.concatenate([a_m[:, sl], r_m[:, sl]], axis=0),
                bk=jnp.concatenate([b_m[:, sl], k_m[:, sl]], axis=0),
                a0=a_0[:, sl], r0=r_0[:, sl], vh=v[:, sl],
                bke=jnp.concatenate([b_e[:, sl], k_e[:, sl]], axis=0)))

    def body(ci, carry):
        probs = []
        for cc in range(CHUNKS_PER_ITER):
            cf = ci * CHUNKS_PER_ITER + cc
            build(cf, 0, probs)
            build(n_chunks - 1 - cf, 1, probs)
        g12 = [_mm1(q['ar'], q['bk'], _NT) for q in probs]
        ms = [jnp.where(q['before'], g[0:C, 0:C], 0.0) for q, g in zip(probs, g12)]
        ns = [jnp.where(q['before'], g[0:C, C:2 * C], 0.0) for q, g in zip(probs, g12)]
        rbk = [jnp.concatenate([jnp.where(q['upto'], g[C:2 * C, 0:C], 0.0),
                                jnp.where(q['upto'], g[C:2 * C, C:2 * C], 0.0)], axis=1) for q, g in zip(probs, g12)]
        nvs = [_mm1(n, q['vh'], _NN) for q, n in zip(probs, ns)]
        ts = _unit_tri_inverses(ms, row, col)
        xs = [_mm2(t, jnp.concatenate([q['a0'], nv], axis=1), _NN) for q, t, nv in zip(probs, ts, nvs)]
        zeros = jnp.zeros((C, RW_HEAD), F32)
        ws = [jnp.concatenate([x, jnp.concatenate([zeros, q['vh']], axis=1)], axis=0) for q, x in zip(probs, xs)]
        rys = [_mm1(rb, w, _NN) for rb, w in zip(rbk, ws)]
        fgs = [_mm3(w, q['bke'], _TN) for q, w in zip(probs, ws)]
        ys = []
        for q, ry, fg in zip(probs, rys, fgs):
            s = s_ref[q['d'], q['h']]
            rp = q['r0'] + ry[:, 0:RW_HEAD]
            ys.append(ry[:, RW_HEAD:2 * RW_HEAD] + _mm1(rp, s, _NT))
            s_ref[q['d'], q['h']] = s * q['pc'] + _mm1(s, fg[0:RW_HEAD], _NN) + fg[RW_HEAD:2 * RW_HEAD]
        for k0 in range(0, len(probs), H_A):
            q0 = probs[k0]
            dirs[q0['d']][6][0, q0['rows'], :] = jnp.concatenate(ys[k0:k0 + H_A], axis=1)
        return carry

    lax.fori_loop(0, n_chunks // CHUNKS_PER_ITER, body, 0)

    @pl.when(j == nj - 1)
    def _():
        sfin_ref[0] = s_ref[...]


def _scan(nkk, r, v, lwf, kaf, kdf, lwb, kab, kdb, s0):
    B, L, _ = r.shape
    nj = L // TL
    tile = pl.BlockSpec((1, TL, D_A), lambda b, j: (b, j, 0))
    rtile = pl.BlockSpec((1, TL, D_A), lambda b, j: (b, nj - 1 - j, 0))
    st = pl.BlockSpec((1, 2, H_A, RW_HEAD, RW_HEAD), lambda b, j: (b, 0, 0, 0, 0))
    f32 = lambda *s: jax.ShapeDtypeStruct(s, F32)
    return pl.pallas_call(
        _scan_kernel, grid=(B, nj),
        in_specs=[tile] * 6 + [rtile] * 6 + [st],
        out_specs=[tile, rtile, st],
        out_shape=[f32(B, L, D_A), f32(B, L, D_A), f32(B, 2, H_A, RW_HEAD, RW_HEAD)],
        scratch_shapes=[pltpu.VMEM((2, H_A, RW_HEAD, RW_HEAD), F32)],
        compiler_params=_cparams(2, 40 * 1024 * 1024), name="rwkv_scan",
    )(nkk, r, v, lwf, kaf, kdf, nkk, r, v, lwb, kab, kdb, s0)


def _row_sum_bf16(e):
    n = e.shape[1] // 128
    acc = None
    for g0 in range(0, n, 4):
        part = e[:, g0 * 128:(g0 + 1) * 128]
        for j in range(g0 + 1, min(g0 + 4, n)):
            part = part + e[:, j * 128:(j + 1) * 128]
        pf = part.astype(F32)
        acc = pf if acc is None else acc + pf
    return jnp.sum(acc, axis=-1, keepdims=True)


def _attn_kernel(*refs, has_cache, lam_init):
    if has_cache:
        q_ref, k_ref, v_ref, ck_ref, cv_ref, lq1, lk1, lq2, lk2, sub_ref, o_ref = refs
    else:
        q_ref, k_ref, v_ref, lq1, lk1, lq2, lk2, sub_ref, o_ref = refs
    l1 = jnp.sum(lq1[...] * lk1[...], axis=-1, keepdims=True)
    l2 = jnp.sum(lq2[...] * lk2[...], axis=-1, keepdims=True)
    lam = jnp.exp(l1) - jnp.exp(l2) + lam_init
    q = q_ref[0].astype(BF16)
    kn = k_ref[0].astype(BF16)
    vn = v_ref[0].astype(BF16)
    if has_cache:
        kc = ck_ref[0, 0].astype(BF16)
        vc = cv_ref[0, 0].astype(BF16)
    dn = (((1,), (1,)), ((), ()))
    def scores(j):
        c0 = j * DIFF_DH
        qh = q[:, c0:c0 + DIFF_DH]
        s_n = lax.dot_general(qh, kn[:, c0:c0 + DIFF_DH], dn, preferred_element_type=F32)
        s_c = lax.dot_general(qh, kc[:, c0:c0 + DIFF_DH], dn, preferred_element_type=F32) if has_cache else None
        return s_n, s_c

    nxt = scores(0)
    for h in range(H_B):
        w_new = None
        w_old = None
        for m in range(2):
            j = h * 2 + m
            s_n, s_c = nxt
            if j + 1 < 2 * H_B:
                nxt = scores(j + 1)
            mx = jnp.max(s_n, axis=-1, keepdims=True)
            if has_cache:
                mx = jnp.maximum(mx, jnp.max(s_c, axis=-1, keepdims=True))
            e_n = jnp.exp2((s_n - mx).astype(BF16))
            den = _row_sum_bf16(e_n)
            if has_cache:
                e_c = jnp.exp2((s_c - mx).astype(BF16))
                den = den + _row_sum_bf16(e_c)
            scale = ((1.0 / den) if m == 0 else (-lam / den)).astype(BF16)
            w_new = e_n * scale if m == 0 else w_new + e_n * scale
            if has_cache:
                w_old = e_c * scale if m == 0 else w_old + e_c * scale
        vh = vn[:, h * 128:(h + 1) * 128]
        o = jnp.dot(w_new, vh, preferred_element_type=F32)
        if has_cache:
            o = o + jnp.dot(w_old, vc[:, h * 128:(h + 1) * 128], preferred_element_type=F32)
        o_ref[0, :, h * 128:(h + 1) * 128] = _rms(o, sub_ref[...]) * (1.0 - lam_init)


def _attention(q, k, v, cache, l, p, lam_init):
    B, L, _ = q.shape
    has_cache = cache is not None
    tq = TL
    in_specs = [pl.BlockSpec((1, tq, D_B), lambda b, i: (b, i, 0)),
                pl.BlockSpec((1, L, D_B), lambda b, i: (b, 0, 0)),
                pl.BlockSpec((1, L, D_B), lambda b, i: (b, 0, 0))]
    args = [q, k, v]
    if has_cache:
        ck, cv = cache
        past = ck.shape[2]
        in_specs += [pl.BlockSpec((1, 1, past, D_B), lambda b, i: (b, l, 0, 0))] * 2
        args += [ck, cv]
    small = lambda n: pl.BlockSpec((1, n), lambda b, i: (0, 0))
    in_specs += [small(DIFF_DH)] * 4 + [small(128)]
    args += [p['diff_lq1'].reshape(1, -1), p['diff_lk1'].reshape(1, -1), p['diff_lq2'].reshape(1, -1),
             p['diff_lk2'].reshape(1, -1), p['diff_subln'].reshape(1, -1)]
    return pl.pallas_call(
        functools.partial(_attn_kernel, has_cache=has_cache, lam_init=lam_init),
        grid=(B, L // tq), in_specs=in_specs,
        out_specs=pl.BlockSpec((1, tq, D_B), lambda b, i: (b, i, 0)),
        out_shape=jax.ShapeDtypeStruct((B, L, D_B), F32),
        compiler_params=_cparams(2, 48 * 1024 * 1024), name="diff_attention",
    )(*args)


def _hfilt_time_kernel(emb_ref, w1_ref, b1_ref, fr_ref, w2_ref, b2_ref, w3_ref, dec_ref, h_ref, ss_ref, acc_ref):
    i = pl.program_id(0)
    emb = emb_ref[...]
    fr = fr_ref[...]
    h = jnp.sin(fr * (jnp.dot(emb.astype(BF16), w1_ref[...].astype(BF16), preferred_element_type=F32) + b1_ref[...]))
    h = jnp.sin(fr * (jnp.dot(h.astype(BF16), w2_ref[...].astype(BF16), preferred_element_type=F32) + b2_ref[...]))
    h = jnp.dot(h.astype(BF16), w3_ref[...].astype(BF16), preferred_element_type=F32)
    h = h * jnp.exp(-emb[:, 0:1] * jnp.abs(dec_ref[...]))
    n = h.shape[0]
    row = lax.broadcasted_iota(jnp.int32, (n, 4 * D_C), 0) + i * n
    col = lax.broadcasted_iota(jnp.int32, (n, 4 * D_C), 1)
    is_bwd = ((col // D_C) % 2) == 1
    h = jnp.where(jnp.logical_and(is_bwd, row == 0), 0.0, h)
    h_ref[...] = h

    @pl.when(i == 0)
    def _():
        acc_ref[...] = jnp.zeros_like(acc_ref)

    acc_ref[...] += jnp.sum(h * h, axis=0, keepdims=True)
    ss_ref[...] = acc_ref[...]


def _hfilt_freq_kernel(h_ref, ss_ref, c_ref, s_ref, hr_ref, hi_ref, a_ref, b_ref, nyq_ref):
    i = pl.program_id(0)

    @pl.when(i == 0)
    def _():
        for o in range(2):
            hf = h_ref[:, o * 2 * D_C:o * 2 * D_C + D_C]
            hb = h_ref[:, o * 2 * D_C + D_C:(o + 1) * 2 * D_C]
            a = hf + hb
            n = a.shape[0]
            alt = 1.0 - 2.0 * (lax.broadcasted_iota(jnp.int32, (n, 1), 0) % 2).astype(F32)
            nyq_ref[:, o * D_C:(o + 1) * D_C] = jnp.sum(a * alt, axis=0, keepdims=True)
            a_ref[:, o * D_C:(o + 1) * D_C] = a.astype(BF16)
            b_ref[:, o * D_C:(o + 1) * D_C] = (hb - hf).astype(BF16)

    ss = ss_ref[...]
    tot = jnp.concatenate([ss[:, 0:D_C] + ss[:, D_C:2 * D_C], ss[:, 2 * D_C:3 * D_C] + ss[:, 3 * D_C:4 * D_C]], axis=1)
    scale = lax.rsqrt(tot + 1e-6)

    hr_ref[...] = jnp.dot(c_ref[...], a_ref[...], preferred_element_type=F32) * scale
    hi = jnp.dot(s_ref[...], b_ref[...], preferred_element_type=F32) * scale
    tf = hi.shape[0]
    row = lax.broadcasted_iota(jnp.int32, (tf, 1), 0) + i * tf
    hi_ref[...] = jnp.where(row == 0, nyq_ref[...] * scale, hi)


def _hyena_filters(L, p, tabs):
    emb, cmat, smat = tabs
    tr = 256
    w1 = jnp.pad(p['hy_w1'], ((0, 128 - HY_EMB), (0, 0)))
    full = lambda a: pl.BlockSpec(a.shape, lambda i: (0,) * a.ndim)
    params = [w1, p['hy_b1'].reshape(1, -1), p['hy_freq'].reshape(1, -1), p['hy_w2'], p['hy_b2'].reshape(1, -1),
              p['hy_w3'], jnp.tile(p['hy_decay'], 4).reshape(1, -1)]
    h_raw, ss = pl.pallas_call(
        _hfilt_time_kernel, grid=(L // tr,),
        in_specs=[pl.BlockSpec((tr, 128), lambda i: (i, 0))] + [full(a) for a in params],
        out_specs=[pl.BlockSpec((tr, 4 * D_C), lambda i: (i, 0)), pl.BlockSpec((1, 4 * D_C), lambda i: (0, 0))],
        out_shape=[jax.ShapeDtypeStruct((L, 4 * D_C), F32), jax.ShapeDtypeStruct((1, 4 * D_C), F32)],
        scratch_shapes=[pltpu.VMEM((1, 4 * D_C), F32)],
        compiler_params=_cparams(1), name="hyena_filter_time",
    )(emb, *params)
    tf = 256
    mat = pl.BlockSpec((tf, L), lambda i: (i, 0))
    return pl.pallas_call(
        _hfilt_freq_kernel, grid=(L // tf,),
        in_specs=[pl.BlockSpec((L, 4 * D_C), lambda i: (0, 0)), pl.BlockSpec((1, 4 * D_C), lambda i: (0, 0)),
                  mat, mat],
        out_specs=[pl.BlockSpec((tf, 2 * D_C), lambda i: (i, 0))] * 2,
        out_shape=[jax.ShapeDtypeStruct((L, 2 * D_C), F32)] * 2,
        scratch_shapes=[pltpu.VMEM((L, 2 * D_C), BF16)] * 2 + [pltpu.VMEM((1, 2 * D_C), F32)],
        compiler_params=_cparams(1, 48 * 1024 * 1024), name="hyena_filter_freq",
    )(h_raw, ss, cmat, smat)


def _hconv_kernel(z_ref, zt_ref, gate_ref, hr_ref, hi_ref, bias_ref, c_ref, s_ref, o_ref,
                  zb_ref, yr_ref, yi_ref, nyq_ref, *, BG, L):
    ph = pl.program_id(1)
    i = pl.program_id(2)
    inv_n = 1.0 / (2 * L)

    @pl.when(jnp.logical_and(ph == 0, i == 0))
    def _():
        alt = 1.0 - 2.0 * (lax.broadcasted_iota(jnp.int32, (L, 1), 0) % 2).astype(F32)
        for b in range(BG):
            z = z_ref[b]
            zb_ref[:, b * D_C:(b + 1) * D_C] = z.astype(BF16)
            nyq_ref[:, b * D_C:(b + 1) * D_C] = jnp.sum(z * alt, axis=0, keepdims=True)

    tf = c_ref.shape[0]

    @pl.when(ph == 0)
    def _():
        zr = jnp.dot(c_ref[...], zb_ref[...], preferred_element_type=F32)
        zs = jnp.dot(s_ref[...], zb_ref[...], preferred_element_type=F32)
        hr = hr_ref[...]
        hi = hi_ref[...]
        row = lax.broadcasted_iota(jnp.int32, (tf, 1), 0) + i * tf
        wgt = jnp.where(row == 0, inv_n, 2.0 * inv_n)
        rows = pl.ds(pl.multiple_of(i * tf, tf), tf)
        for b in range(BG):
            sl = slice(b * D_C, (b + 1) * D_C)
            yr_ref[rows, sl] = ((zr[:, sl] * hr + zs[:, sl] * hi) * wgt).astype(BF16)
            yi_ref[rows, sl] = ((zr[:, sl] * hi - zs[:, sl] * hr) * (2.0 * inv_n)).astype(BF16)

        @pl.when(i == 0)
        def _():
            for b in range(BG):
                sl = slice(b * D_C, (b + 1) * D_C)
                nyq_ref[:, sl] = nyq_ref[:, sl] * hi[0:1, :] * inv_n

    @pl.when(ph == 1)
    def _():
        y = (jnp.dot(c_ref[...], yr_ref[...], preferred_element_type=F32)
             - jnp.dot(s_ref[...], yi_ref[...], preferred_element_type=F32))
        row = lax.broadcasted_iota(jnp.int32, (tf, 1), 0) + i * tf
        alt = 1.0 - 2.0 * (row % 2).astype(F32)
        y = y + alt * nyq_ref[...]
        for b in range(BG):
            sl = slice(b * D_C, (b + 1) * D_C)
            o_ref[b] = gate_ref[b] * (y[:, sl] + bias_ref[...] * zt_ref[b])


def _hconv(z, gate, hr, hi, bias, order, tabs):
    B, L, _ = z.shape
    _, ch, sh = tabs
    BG =min(B, 8) if L <= 256 else min(B, 4)
    assert B % BG == 0
    tf = 256
    nt = L // tf
    mat = pl.BlockSpec((tf, L), lambda g, ph, i: (i, 0))
    tile = pl.BlockSpec((BG, tf, D_C), lambda g, ph, i: (g, i * ph, 0))
    hspec = pl.BlockSpec((tf, D_C), lambda g, ph, i: (i * (1 - ph), order))
    cols = BG * D_C
    return pl.pallas_call(
        functools.partial(_hconv_kernel, BG=BG, L=L), grid=(B // BG, 2, nt),
        in_specs=[pl.BlockSpec((BG, L, D_C), lambda g, ph, i: (g, 0, 0)), tile, tile, hspec, hspec,
                  pl.BlockSpec((1, D_C), lambda g, ph, i: (0, 0)), mat, mat],
        out_specs=tile,
        out_shape=jax.ShapeDtypeStruct((B, L, D_C), F32),
        scratch_shapes=[pltpu.VMEM((L, cols), BF16)] * 3 + [pltpu.VMEM((1, cols), F32)],
        compiler_params=_cparams(3, 48 * 1024 * 1024), name="hyena_conv",
    )(z, z, gate, hr, hi, bias[order].reshape(1, D_C), ch, sh)


def _outproj_kernel(x_ref, yf_ref, yb_ref, bon_ref, g_ref, ob_ref, yc_ref, mod_ref,
                    lnw_ref, lnb_ref, gpost_ref, gffn_ref, w_ref, x1_ref, h2_ref):
    bd = _block_diag(D_A, RW_HEAD)
    y = yf_ref[0] + yb_ref[0]
    mu = _segsum(y, bd) * (1.0 / RW_HEAD)
    yc = y - mu
    var = _segsum(yc * yc, bd) * (1.0 / RW_HEAD)
    yn = yc * lax.rsqrt(var + GN_EPS) * lnw_ref[...] + lnb_ref[...]
    ya = (yn + bon_ref[0]) * g_ref[0]
    mix = (jnp.dot(ya.astype(BF16), w_ref[0:D_A, :], preferred_element_type=F32)
           + jnp.dot(ob_ref[0].astype(BF16), w_ref[D_A:D_A + D_B, :], preferred_element_type=F32)
           + jnp.dot(yc_ref[0].astype(BF16), w_ref[D_A + D_B:D_MODEL, :], preferred_element_type=F32))
    mod = mod_ref[0]
    gt1 = mod[:, 2 * D_MODEL:3 * D_MODEL]
    sh2 = mod[:, 3 * D_MODEL:4 * D_MODEL]
    sc2 = mod[:, 4 * D_MODEL:5 * D_MODEL]
    x1 = x_ref[0] + gt1 * _rms(mix, gpost_ref[...])
    x1_ref[0] = x1
    h2_ref[0] = (_rms(x1, gffn_ref[...]) * (1.0 + sc2) + sh2).astype(BF16)


def _outproj(x, yf, yb, bon, g, ob, yc, mod, p, w_out):
    B, L, _ = x.shape
    mb = mod.shape[0]
    row = lambda b, i: (b, i, 0)
    c2 = lambda b, i: (0, 0)
    t = lambda n: pl.BlockSpec((1, TL, n), row)
    in_specs = [t(D_MODEL), t(D_A), t(D_A), t(D_A), t(D_A), t(D_B), t(D_C),
                pl.BlockSpec((1, 1, 6 * D_MODEL), (lambda b, i: (b, 0, 0)) if mb > 1 else (lambda b, i: (0, 0, 0))),
                pl.BlockSpec((1, D_A), c2), pl.BlockSpec((1, D_A), c2),
                pl.BlockSpec((1, D_MODEL), c2), pl.BlockSpec((1, D_MODEL), c2),
                _layer_weight_spec(w_out, 2)]
    return pl.pallas_call(
        _outproj_kernel, grid=(B, L // TL), in_specs=in_specs,
        out_specs=[t(D_MODEL), t(D_MODEL)],
        out_shape=[jax.ShapeDtypeStruct((B, L, D_MODEL), F32), jax.ShapeDtypeStruct((B, L, D_MODEL), BF16)],
        compiler_params=_cparams(2, 40 * 1024 * 1024), name="outproj",
    )(x, yf, yb, bon, g, ob, yc, mod, p['rwkv_ln_w'].reshape(1, -1), p['rwkv_ln_b'].reshape(1, -1),
      p['g_mix_post'].reshape(1, -1), p['g_ffn_pre'].reshape(1, -1), w_out[0])


def _ffn_kernel(h_ref, x_ref, mod_ref, g_ref, w1_ref, w2_ref, o_ref):
    h = h_ref[0]
    acc = None
    ck = 1024
    for c in range(D_FF // ck):
        a = jnp.dot(h, w1_ref[:, c * ck:(c + 1) * ck], preferred_element_type=F32)
        a = jnp.square(jnp.maximum(a, 0.0)).astype(BF16)
        part = jnp.dot(a, w2_ref[c * ck:(c + 1) * ck, :], preferred_element_type=F32)
        acc = part if acc is None else acc + part
    gt2 = mod_ref[0][:, 5 * D_MODEL:6 * D_MODEL]
    o_ref[0] = x_ref[0] + gt2 * _rms(acc, g_ref[...])


def _ffn(h2, x1, mod, g_post, w1, w2):
    B, L, _ = x1.shape
    mb = mod.shape[0]
    row = lambda b, i: (b, i, 0)
    c2 = lambda b, i: (0, 0)
    return pl.pallas_call(
        _ffn_kernel, grid=(B, L // TL),
        in_specs=[pl.BlockSpec((1, TL, D_MODEL), row), pl.BlockSpec((1, TL, D_MODEL), row),
                  pl.BlockSpec((1, 1, 6 * D_MODEL), (lambda b, i: (b, 0, 0)) if mb > 1 else (lambda b, i: (0, 0, 0))),
                  pl.BlockSpec((1, D_MODEL), c2),
                  _layer_weight_spec(w1, 2), _layer_weight_spec(w2, 2)],
        out_specs=pl.BlockSpec((1, TL, D_MODEL), row),
        out_shape=jax.ShapeDtypeStruct((B, L, D_MODEL), F32),
        compiler_params=_cparams(2, V7X_VMEM_LIMIT), name="ffn",
    )(h2, x1, mod, g_post.reshape(1, -1), w1[0], w2[0])


@functools.lru_cache(maxsize=None)
def _rope_tables(L):
    rows = L // GRID_W
    row = np.repeat(np.arange(rows), GRID_W).astype(np.float64)
    col = np.tile(np.arange(GRID_W), rows).astype(np.float64)
    half = DIFF_DH // 2
    inv = ROPE_BASE ** (-np.arange(0, half, 2, dtype=np.float64) / half)
    ang_r = row[:, None] * inv[None]
    ang_c = col[:, None] * inv[None]
    ang = np.concatenate([ang_r, ang_r, ang_c, ang_c], axis=-1)
    cos, sin = np.cos(ang), np.sin(ang)
    quarter = (np.arange(DIFF_DH) // 16) % 2
    sin_a = np.where(quarter == 0, -sin, 0.0)
    sin_b = np.where(quarter == 1, sin, 0.0)
    rep = D_B // DIFF_DH
    return tuple(np.tile(t, (1, rep)).astype(np.float32) for t in (cos, sin_a, sin_b))


@functools.lru_cache(maxsize=None)
def _hyena_tables(L):
    t = np.linspace(0.0, 1.0, L, dtype=np.float32).astype(np.float64)[:, None]
    ang = (2.0 * math.pi / L) * np.arange(L, dtype=np.float64)[:, None]
    bands = np.linspace(1e-4, HY_BANDS - 1, HY_BANDS, dtype=np.float32).astype(np.float64)[None, :]
    emb = np.concatenate([t, np.cos(bands * ang), -np.sin(bands * ang)], axis=-1)
    emb = np.pad(emb, ((0, 0), (0, 128 - HY_EMB))).astype(np.float32)
    n = np.arange(L, dtype=np.int64)
    theta = ((n[:, None] * n[None, :]) % (2 * L)).astype(np.float64) * (math.pi / L)
    return emb, np.cos(theta).astype(BF16), np.sin(theta).astype(BF16)


def _layer(x, mod, p, wb, l, lam_init, cache, rope_tabs, hy_tabs, filt):
    B, L, _ = x.shape
    ua, uq, uk, uv, uh = _inproj(x, mod, p['g_mix_pre'], wb['w_in'], rope_tabs)
    (r, v, nkk, wf, kaf, kdf, wbk, kab, kdb, g, bon, x1h, x2h, hv) = _prep(ua, uh, p)
    if cache is None:
        s0 = jnp.zeros((B, 2, H_A, RW_HEAD, RW_HEAD), F32)
        kv_cache = None
    else:
        s0 = cache[0]
        kv_cache = (cache[1], cache[2])
    yf, yb, sfin = _scan(nkk, r, v, wf, kaf, kdf, wbk, kab, kdb, s0)
    ob = _attention(uq, uk, uv, kv_cache, l, p, lam_init)
    hr, hi = filt
    z1 = _hconv(hv, x1h, hr, hi, p['hy_bias'], 0, hy_tabs)
    yc = _hconv(z1, x2h, hr, hi, p['hy_bias'], 1, hy_tabs)
    x1, h2 = _outproj(x, yf, yb, bon, g, ob, yc, mod, p, wb['w_out'])
    x2 = _ffn(h2, x1, mod, p['g_ffn_post'], wb['w_ff1'], wb['w_ff2'])
    return x2, (sfin, uk, uv)


_LAYER_KEYS = ('g_mix_pre', 'g_mix_post', 'g_ffn_pre', 'g_ffn_post', 'rwkv_conv', 'rwkv_w0', 'rwkv_w2', 'rwkv_a0',
               'rwkv_a2', 'rwkv_g2', 'rwkv_kk', 'rwkv_ka', 'rwkv_rk', 'rwkv_ln_w', 'rwkv_ln_b', 'diff_lq1', 'diff_lk1',
               'diff_lq2', 'diff_lk2', 'diff_subln', 'hy_conv_w', 'hy_conv_b', 'hy_w1', 'hy_b1', 'hy_freq', 'hy_w2',
               'hy_b2', 'hy_w3', 'hy_decay', 'hy_bias')


def kernel(x_prompt, x_sample, state_rwkv, cache_k, cache_v, c, c_ctx, w_mod, b_mod, g_mix_pre, g_mix_post, g_ffn_pre, g_ffn_post, w_in, rwkv_conv, rwkv_w0, rwkv_w2, rwkv_a0, rwkv_a2, rwkv_g2, rwkv_kk, rwkv_ka, rwkv_rk, rwkv_ln_w, rwkv_ln_b, diff_lq1, diff_lk1, diff_lq2, diff_lk2, diff_subln, hy_conv_w, hy_conv_b, hy_w1, hy_b1, hy_freq, hy_w2, hy_b2, hy_w3, hy_decay, hy_bias, w_out, w_ff1, w_ff2):
    stacked = dict(g_mix_pre=g_mix_pre, g_mix_post=g_mix_post, g_ffn_pre=g_ffn_pre, g_ffn_post=g_ffn_post,
                   rwkv_conv=rwkv_conv, rwkv_w0=rwkv_w0, rwkv_w2=rwkv_w2, rwkv_a0=rwkv_a0, rwkv_a2=rwkv_a2,
                   rwkv_g2=rwkv_g2, rwkv_kk=rwkv_kk, rwkv_ka=rwkv_ka, rwkv_rk=rwkv_rk, rwkv_ln_w=rwkv_ln_w,
                   rwkv_ln_b=rwkv_ln_b, diff_lq1=diff_lq1, diff_lk1=diff_lk1, diff_lq2=diff_lq2, diff_lk2=diff_lk2,
                   diff_subln=diff_subln, hy_conv_w=hy_conv_w, hy_conv_b=hy_conv_b, hy_w1=hy_w1, hy_b1=hy_b1,
                   hy_freq=hy_freq, hy_w2=hy_w2, hy_b2=hy_b2, hy_w3=hy_w3, hy_decay=hy_decay, hy_bias=hy_bias)
    Bc, Lc, _ = x_prompt.shape
    Bs, Ls, _ = x_sample.shape
    past = cache_k.shape[2]

    cond = jnp.zeros((8, D_MODEL), F32).at[0:Bs].set(c).at[Bs].set(c_ctx)
    mod_all = _modulation(cond, w_mod, b_mod)

    rope_tabs = _rope_tables(Ls)
    tabs_c = _hyena_tables(Lc)
    tabs_s = _hyena_tables(Ls)
    ck = cache_k.reshape(Bs, DEPTH, past, D_B)
    cv = cache_v.reshape(Bs, DEPTH, past, D_B)

    w_in_b, w_out_b, w_ff1_b, w_ff2_b = (w.astype(BF16) for w in (w_in, w_out, w_ff1, w_ff2))
    xp, xs = x_prompt, x_sample
    st_list, k_list, v_list = [], [], []
    for l in range(DEPTH):
        p = {k: stacked[k][l] for k in _LAYER_KEYS}
        wb = dict(w_in=(w_in_b, l), w_out=(w_out_b, l), w_ff1=(w_ff1_b, l), w_ff2=(w_ff2_b, l))
        lam_init = 0.8 - 0.6 * math.exp(-0.3 * l)
        mod_lat = mod_all[l, 0:Bs].reshape(Bs, 1, 6 * D_MODEL)
        mod_ctx = mod_all[l, Bs:Bs + 1].reshape(1, 1, 6 * D_MODEL)
        filt_c = _hyena_filters(Lc, p, tabs_c)
        filt_s = _hyena_filters(Ls, p, tabs_s)
        xp, (s_ctx, k_ctx, v_ctx) = _layer(xp, mod_ctx, p, wb, l, lam_init, None, None, tabs_c, filt_c)
        st_list.append(s_ctx)
        k_list.append(k_ctx.reshape(Bc, Lc, H_B, 2, DIFF_DH))
        v_list.append(v_ctx.reshape(Bc, Lc, H_B, 2 * DIFF_DH))
        s0 = state_rwkv[:, l]
        xs, _ = _layer(xs, mod_lat, p, wb, l, lam_init, (s0, ck, cv), rope_tabs, tabs_s, filt_s)
    return (xp, xs, jnp.stack(st_list, axis=1), jnp.stack(k_list, axis=1), jnp.stack(v_list, axis=1))
```

```python
import functools
import math

import jax
import jax.numpy as jnp
import numpy as np
from jax import lax
from jax.experimental import pallas as pl
from jax.experimental.pallas import tpu as pltpu

F32 = jnp.float32
BF16 = jnp.bfloat16

D_MODEL = 1024
DEPTH = 4
GRID_W = 64
D_A = 256
RW_HEAD = 64
H_A = 4
D_B = 512
DIFF_DH = 64
H_B = 4
D_C = 256
HY_BANDS = 16
HY_EMB = 33
HY_FFN = 64
D_FF = 4096
FFN_CHUNK = 1024
ROPE_BASE = 10000.0
RMS_EPS = 1e-6
GN_EPS = 64e-5
N_A_COLS = 1152
IN_COLS = 3456

TL = 256
V7X_VMEM_LIMIT = 56 * 1024 * 1024


def _cparams(n_axes, vmem=None):
    return pltpu.CompilerParams(dimension_semantics=("arbitrary",) * n_axes,
                                vmem_limit_bytes=vmem)


def _split3(a):
    hi = a.astype(BF16)
    r1 = a - hi.astype(F32)
    mid = r1.astype(BF16)
    lo = (r1 - mid.astype(F32)).astype(BF16)
    return hi, mid, lo


def _split2(a):
    hi = a.astype(BF16)
    lo = (a - hi.astype(F32)).astype(BF16)
    return hi, lo


def _segsum(a, bd):
    hi, lo = _split2(a)
    return jnp.dot(hi, bd, preferred_element_type=F32) + jnp.dot(lo, bd, preferred_element_type=F32)


def _block_diag(n, seg):
    r = lax.broadcasted_iota(jnp.int32, (n, n), 0) // seg
    c = lax.broadcasted_iota(jnp.int32, (n, n), 1) // seg
    return (r == c).astype(BF16)


def _layer_weight_spec(w, n_grid_axes):
    stack, layer = w
    _, k, n = stack.shape
    if n_grid_axes == 2:
        return pl.BlockSpec((None, k, n), lambda b, i: (layer, 0, 0))
    raise NotImplementedError(n_grid_axes)


def _rms(x, g):
    return x * lax.rsqrt(jnp.mean(x * x, axis=-1, keepdims=True) + RMS_EPS) * g


def _mod_kernel(cond_ref, w_ref, b_ref, o_ref):
    c = cond_ref[...]
    s = (c * jax.nn.sigmoid(c)).astype(BF16)
    o_ref[0] = jnp.dot(s, w_ref[0].astype(BF16), preferred_element_type=F32) + b_ref[0]


def _modulation(cond, w_mod, b_mod):
    tn = 1536
    return pl.pallas_call(
        _mod_kernel,
        grid=(DEPTH, 6 * D_MODEL // tn),
        in_specs=[pl.BlockSpec((8, D_MODEL), lambda l, n: (0, 0)),
                  pl.BlockSpec((1, D_MODEL, tn), lambda l, n: (l, 0, n)),
                  pl.BlockSpec((1, 1, tn), lambda l, n: (l, 0, n))],
        out_specs=pl.BlockSpec((1, 8, tn), lambda l, n: (l, 0, n)),
        out_shape=jax.ShapeDtypeStruct((DEPTH, 8, 6 * D_MODEL), F32),
        compiler_params=_cparams(2, 40 * 1024 * 1024),
        name="modulation",
    )(cond, w_mod, b_mod.reshape(DEPTH, 1, 6 * D_MODEL))


def _rope(x, cos, sin_a, sin_b):
    n = x.shape[-1]
    return x * cos + pltpu.roll(x, n - 16, axis=1) * sin_a + pltpu.roll(x, 16, axis=1) * sin_b


def _shift_conv(u, prev_row, next_row, cw):
    n = u.shape[0]
    row = lax.broadcasted_iota(jnp.int32, (n, 1), 0)
    up = jnp.where(row == 0, prev_row, pltpu.roll(u, 1, axis=0))
    un = jnp.where(row == n - 1, next_row, pltpu.roll(u, n - 1, axis=0))
    return cw[0:1] * up + cw[1:2] * u + cw[2:3] * un


def _prep_body(ua, prev, nxt, uh, prevh, nxth,
               cw_ref, w0_ref, w2_ref, a0_ref, a2_ref, g2_ref, kkw_ref, kaw_ref, rk_ref, hcw_ref, hcb_ref,
               r_ref, v_ref, nkk_ref, wf_ref, kaf_ref, kdf_ref, wb_ref, kab_ref, kdb_ref, g_ref, bon_ref,
               x1_ref, x2_ref, hv_ref):
    bd = _block_diag(D_A, RW_HEAD)
    u_rkv = ua[:, 0:3 * D_A]
    rkv = _shift_conv(u_rkv, prev, nxt, cw_ref[...])
    r = rkv[:, 0:D_A]
    k = rkv[:, D_A:2 * D_A]
    v = rkv[:, 2 * D_A:3 * D_A]
    u_w = ua[:, 768:896]
    u_a = ua[:, 896:1024]
    u_g = ua[:, 1024:1152]

    g = jnp.dot(jax.nn.sigmoid(u_g).astype(BF16), g2_ref[...].astype(BF16), preferred_element_type=F32)
    kk = k * kkw_ref[...]
    kk = kk * lax.rsqrt(_segsum(kk * kk, bd) + 1e-12)
    r_ref[0] = r
    v_ref[0] = v
    nkk_ref[0] = -kk
    g_ref[0] = g

    kd_sum = None
    outs = ((wf_ref, kaf_ref, kdf_ref), (wb_ref, kab_ref, kdb_ref))
    for d in range(2):
        uw_d = u_w[:, d * 64:(d + 1) * 64]
        ua_d = u_a[:, d * 64:(d + 1) * 64]
        xw = w0_ref[d:d + 1, :] + jnp.dot(jnp.tanh(uw_d).astype(BF16), w2_ref[d].astype(BF16),
                                          preferred_element_type=F32)
        z = -xw
        softplus = jnp.maximum(z, 0.0) + jnp.log(1.0 + jnp.exp(-jnp.abs(z)))
        wl = -softplus - 0.5
        log_decay = -jnp.exp(wl)
        a = jax.nn.sigmoid(a0_ref[d:d + 1, :] + jnp.dot(ua_d.astype(BF16), a2_ref[d].astype(BF16),
                                                        preferred_element_type=F32))
        kd = k * (1.0 + (a - 1.0) * kaw_ref[...])
        w_o, ka_o, kd_o = outs[d]
        w_o[0] = log_decay
        ka_o[0] = kk * a
        kd_o[0] = kd
        kd_sum = kd if kd_sum is None else kd_sum + kd
    bon_ref[0] = _segsum(r * kd_sum * rk_ref[...], bd) * v

    hc = _shift_conv(uh, prevh, nxth, hcw_ref[...]) + hcb_ref[...]
    x1_ref[0] = hc[:, 0:D_C]
    x2_ref[0] = hc[:, D_C:2 * D_C]
    hv_ref[0] = hc[:, 2 * D_C:3 * D_C]


N_PREP_PARAMS = 11
N_PREP_OUTS = 14


def _inproj_kernel(*refs, rope):
    n_in = 6 + (3 if rope else 0)
    x_ref, xp_ref, xn_ref, mod_ref, g_ref, w_ref = refs[0:6]
    rope_refs = refs[6:n_in]
    prep_params = refs[n_in:n_in + N_PREP_PARAMS]
    uq_ref, uk_ref, uv_ref = refs[n_in + N_PREP_PARAMS:n_in + N_PREP_PARAMS + 3]
    prep_outs = refs[n_in + N_PREP_PARAMS + 3:]
    i = pl.program_id(1)
    nt = pl.num_programs(1)
    mod = mod_ref[0]
    sh1 = mod[:, 0:D_MODEL]
    sc1 = mod[:, D_MODEL:2 * D_MODEL]

    def normed(x):
        return (_rms(x, g_ref[...]) * (1.0 + sc1) + sh1).astype(BF16)

    u = jnp.dot(normed(x_ref[0]), w_ref[...], preferred_element_type=F32)
    q = u[:, N_A_COLS:N_A_COLS + D_B] * (DIFF_DH ** -0.5 * math.log2(math.e))
    k = u[:, N_A_COLS + D_B:N_A_COLS + 2 * D_B]
    if rope:
        cos_ref, sa_ref, sb_ref = rope_refs
        q = _rope(q, cos_ref[...], sa_ref[...], sb_ref[...])
        k = _rope(k, cos_ref[...], sa_ref[...], sb_ref[...])
    uq_ref[0] = q.astype(uq_ref.dtype)
    uk_ref[0] = k.astype(uk_ref.dtype)
    uv_ref[0] = u[:, N_A_COLS + 2 * D_B:N_A_COLS + 3 * D_B].astype(uv_ref.dtype)

    he = normed(jnp.concatenate([xp_ref[0], xn_ref[0]], axis=0))
    hy0 = N_A_COLS + 3 * D_B
    e_rkv = jnp.dot(he, w_ref[:, 0:3 * D_A], preferred_element_type=F32)
    e_hy = jnp.dot(he, w_ref[:, hy0:IN_COLS], preferred_element_type=F32)
    has_prev = i > 0
    has_next = i < nt - 1
    _prep_body(u[:, 0:N_A_COLS], jnp.where(has_prev, e_rkv[7:8], 0.0), jnp.where(has_next, e_rkv[8:9], 0.0),
               u[:, hy0:IN_COLS], jnp.where(has_prev, e_hy[7:8], 0.0), jnp.where(has_next, e_hy[8:9], 0.0),
               *prep_params, *prep_outs)


def _inproj(x, mod, g_pre, w_in, rope_tabs, p):
    B, L, _ = x.shape
    rope = rope_tabs is not None
    mb = mod.shape[0]
    nb8 = L // 8
    t8 = TL // 8
    row = lambda b, i: (b, i, 0)
    prev = lambda b, i: (b, jnp.maximum(i * t8 - 1, 0), 0)
    nxt = lambda b, i: (b, jnp.minimum((i + 1) * t8, nb8 - 1), 0)
    const2 = lambda b, i: (0, 0)

    def full(a):
        nd = a.ndim
        return pl.BlockSpec(a.shape, lambda b, i: (0,) * nd)

    params = [p['rwkv_conv'], p['rwkv_w0'], p['rwkv_w2'], p['rwkv_a0'], p['rwkv_a2'], p['rwkv_g2'],
              p['rwkv_kk'].reshape(1, D_A), p['rwkv_ka'].reshape(1, D_A), p['rwkv_rk'].reshape(1, D_A),
              p['hy_conv_w'], p['hy_conv_b'].reshape(1, 3 * D_C)]
    assert len(params) == N_PREP_PARAMS
    in_specs = [pl.BlockSpec((1, TL, D_MODEL), row), pl.BlockSpec((1, 8, D_MODEL), prev),
                pl.BlockSpec((1, 8, D_MODEL), nxt),
                pl.BlockSpec((1, 1, 6 * D_MODEL), (lambda b, i: (b, 0, 0)) if mb > 1 else (lambda b, i: (0, 0, 0))),
                pl.BlockSpec((1, D_MODEL), const2),
                _layer_weight_spec(w_in, 2)]
    args = [x, x, x, mod, g_pre.reshape(1, D_MODEL), w_in[0]]
    if rope:
        in_specs += [pl.BlockSpec((TL, D_B), lambda b, i: (i, 0))] * 3
        args += list(rope_tabs)
    in_specs += [full(a) for a in params]
    args += params
    qkv_dt = BF16 if rope else F32
    out_shape = ([jax.ShapeDtypeStruct((B, L, D_B), qkv_dt)] * 3
                 + [jax.ShapeDtypeStruct((B, L, D_A), F32)] * N_PREP_OUTS)
    out_specs = [pl.BlockSpec((1, TL, D_B), row)] * 3 + [pl.BlockSpec((1, TL, D_A), row)] * N_PREP_OUTS
    return pl.pallas_call(
        functools.partial(_inproj_kernel, rope=rope),
        grid=(B, L // TL), in_specs=in_specs, out_specs=out_specs, out_shape=out_shape,
        compiler_params=_cparams(2, 48 * 1024 * 1024), name="inproj",
    )(*args)


CHUNK = 64
CHUNKS_PER_ITER = 4
_NN = (((1,), (0,)), ((), ()))
_NT = (((1,), (1,)), ((), ()))
_TN = (((0,), (0,)), ((), ()))


def _mm3(a, b, dims):
    ah, al = _split2(a)
    bh, bl = _split2(b)
    f = lambda x, y: lax.dot_general(x, y, dims, preferred_element_type=F32)
    if dims is _TN:
        return f(ah, bh) + f(ah, bl) + f(al, bh)
    m = a.shape[0]
    both = f(jnp.concatenate([ah, al], axis=0), bh)
    return both[0:m] + both[m:2 * m] + f(ah, bl)


def _mm2(a, b, dims):
    ah = a.astype(BF16)
    bh, bl = _split2(b)
    n = b.shape[1]
    both = lax.dot_general(ah, jnp.concatenate([bh, bl], axis=1), dims, preferred_element_type=F32)
    return both[:, 0:n] + both[:, n:2 * n]


def _mm1(a, b, dims):
    return lax.dot_general(a.astype(BF16), b.astype(BF16), dims, preferred_element_type=F32)


def _unit_tri_inverses(ms, row, col):
    eye = (row == col).astype(F32)
    blk = lambda n: (row // n) == (col // n)
    m8 = [jnp.where(blk(8), m, 0.0) for m in ms]
    a2 = [_mm1(x, x, _NN) for x in m8]
    a4 = [_mm1(x, x, _NN) for x in a2]
    p = [eye + x + y + _mm1(x, y, _NN) for x, y in zip(m8, a2)]
    ts = [x + _mm1(x, y, _NN) for x, y in zip(p, a4)]
    n = 8
    while n < CHUNK:
        sel = jnp.logical_and(blk(2 * n), jnp.logical_not(blk(n)))
        ot = [_mm1(jnp.where(sel, m, 0.0), t, _NN) for m, t in zip(ms, ts)]
        ts = [t + _mm1(t, o, _NN) for t, o in zip(ts, ot)]
        n *= 2
    return ts


def _scan_kernel(nkkf_ref, rf_ref, vf_ref, lwf_ref, kaf_ref, kdf_ref,
                 nkkb_ref, rb_ref, vb_ref, lwb_ref, kab_ref, kdb_ref, s0_ref,
                 yf_ref, yb_ref, sfin_ref, s_ref):
    C = CHUNK
    j = pl.program_id(1)
    nj = pl.num_programs(1)
    row = lax.broadcasted_iota(jnp.int32, (C, C), 0)
    col = lax.broadcasted_iota(jnp.int32, (C, C), 1)
    dirs = ((nkkf_ref, rf_ref, vf_ref, lwf_ref, kaf_ref, kdf_ref, yf_ref),
            (nkkb_ref, rb_ref, vb_ref, lwb_ref, kab_ref, kdb_ref, yb_ref))
    n_chunks = nkkf_ref.shape[1] // C

    @pl.when(j == 0)
    def _():
        s_ref[...] = s0_ref[0]

    def build(c, d, probs):
        rows = pl.ds(pl.multiple_of(c * C, C), C)
        nkk_ref, r_ref, v_ref, lw_ref, b_ref, k_ref = dirs[d][0:6]
        alpha = nkk_ref[0, rows, :]
        r = r_ref[0, rows, :]
        v = v_ref[0, rows, :]
        before = (row > col) if d == 0 else (row < col)
        upto = (row >= col) if d == 0 else (row <= col)
        lw = lw_ref[0, rows, :]
        beta = b_ref[0, rows, :]
        kd = k_ref[0, rows, :]
        tri = upto.astype(BF16)
        l_hi, l_mid, l_lo = _split3(lw)
        lam = (jnp.dot(tri, l_hi, preferred_element_type=F32) + jnp.dot(tri, l_mid, preferred_element_type=F32)
               + jnp.dot(tri, l_lo, preferred_element_type=F32))
        lam_prev = lam - lw
        lam_mid = lam[C // 2:C // 2 + 1, :]
        lam_end = lam[C - 1:C, :] if d == 0 else lam[0:1, :]
        e_to_mid = jnp.exp(lam_mid - lam)
        a_m = alpha * jnp.exp(lam_prev - lam_mid)
        b_m = beta * e_to_mid
        k_m = kd * e_to_mid
        r_m = r * jnp.exp(lam - lam_mid)
        a_0 = alpha * jnp.exp(lam_prev)
        r_0 = r * jnp.exp(lam)
        e_to_end = jnp.exp(lam_end - lam)
        b_e = beta * e_to_end
        k_e = kd * e_to_end
        p_end = jnp.exp(lam_end)
        for h in range(H_A):
            sl = slice(h * RW_HEAD, (h + 1) * RW_HEAD)
            probs.append(dict(
                rows=rows, d=d, h=h, before=before, upto=upto, pc=p_end[:, sl],
                ar=jnp.concatenate([a_m[:, sl], r_m[:, sl]], axis=0),
                bk=jnp.concatenate([b_m[:, sl], k_m[:, sl]], axis=0),
                a0=a_0[:, sl], r0=r_0[:, sl], vh=v[:, sl],
                bke=jnp.concatenate([b_e[:, sl], k_e[:, sl]], axis=0)))

    def body(ci, carry):
        probs = []
        for cc in range(CHUNKS_PER_ITER):
            cf = ci * CHUNKS_PER_ITER + cc
            build(cf, 0, probs)
            build(n_chunks - 1 - cf, 1, probs)
        g12 = [_mm1(q['ar'], q['bk'], _NT) for q in probs]
        ms = [jnp.where(q['before'], g[0:C, 0:C], 0.0) for q, g in zip(probs, g12)]
        ns = [jnp.where(q['before'], g[0:C, C:2 * C], 0.0) for q, g in zip(probs, g12)]
        rbk = [jnp.concatenate([jnp.where(q['upto'], g[C:2 * C, 0:C], 0.0),
                                jnp.where(q['upto'], g[C:2 * C, C:2 * C], 0.0)], axis=1) for q, g in zip(probs, g12)]
        nvs = [_mm1(n, q['vh'], _NN) for q, n in zip(probs, ns)]
        ts = _unit_tri_inverses(ms, row, col)
        xs = [_mm2(t, jnp.concatenate([q['a0'], nv], axis=1), _NN) for q, t, nv in zip(probs, ts, nvs)]
        zeros = jnp.zeros((C, RW_HEAD), F32)
        ws = [jnp.concatenate([x, jnp.concatenate([zeros, q['vh']], axis=1)], axis=0) for q, x in zip(probs, xs)]
        rys = [_mm1(rb, w, _NN) for rb, w in zip(rbk, ws)]
        fgs = [_mm3(w, q['bke'], _TN) for q, w in zip(probs, ws)]
        ys = []
        for q, ry, fg in zip(probs, rys, fgs):
            s = s_ref[q['d'], q['h']]
            rp = q['r0'] + ry[:, 0:RW_HEAD]
            ys.append(ry[:, RW_HEAD:2 * RW_HEAD] + _mm1(rp, s, _NT))
            s_ref[q['d'], q['h']] = s * q['pc'] + _mm1(s, fg[0:RW_HEAD], _NN) + fg[RW_HEAD:2 * RW_HEAD]
        for k0 in range(0, len(probs), H_A):
            q0 = probs[k0]
            dirs[q0['d']][6][0, q0['rows'], :] = jnp.concatenate(ys[k0:k0 + H_A], axis=1)
        return carry

    lax.fori_loop(0, n_chunks // CHUNKS_PER_ITER, body, 0)

    @pl.when(j == nj - 1)
    def _():
        sfin_ref[0] = s_ref[...]


def _scan(nkk, r, v, lwf, kaf, kdf, lwb, kab, kdb, s0):
    B, L, _ = r.shape
    nj = L // TL
    tile = pl.BlockSpec((1, TL, D_A), lambda b, j: (b, j, 0))
    rtile = pl.BlockSpec((1, TL, D_A), lambda b, j: (b, nj - 1 - j, 0))
    st = pl.BlockSpec((1, 2, H_A, RW_HEAD, RW_HEAD), lambda b, j: (b, 0, 0, 0, 0))
    f32 = lambda *s: jax.ShapeDtypeStruct(s, F32)
    return pl.pallas_call(
        _scan_kernel, grid=(B, nj),
        in_specs=[tile] * 6 + [rtile] * 6 + [st],
        out_specs=[tile, rtile, st],
        out_shape=[f32(B, L, D_A), f32(B, L, D_A), f32(B, 2, H_A, RW_HEAD, RW_HEAD)],
        scratch_shapes=[pltpu.VMEM((2, H_A, RW_HEAD, RW_HEAD), F32)],
        compiler_params=_cparams(2, 40 * 1024 * 1024), name="rwkv_scan",
    )(nkk, r, v, lwf, kaf, kdf, nkk, r, v, lwb, kab, kdb, s0)


def _row_sum_bf16(e):
    n = e.shape[1] // 128
    acc = None
    for g0 in range(0, n, 4):
        part = e[:, g0 * 128:(g0 + 1) * 128]
        for j in range(g0 + 1, min(g0 + 4, n)):
            part = part + e[:, j * 128:(j + 1) * 128]
        pf = part.astype(F32)
        acc = pf if acc is None else acc + pf
    return jnp.sum(acc, axis=-1, keepdims=True)


def _attn_kernel(*refs, has_cache, lam_init):
    if has_cache:
        q_ref, k_ref, v_ref, ck_ref, cv_ref, lq1, lk1, lq2, lk2, sub_ref, o_ref = refs
    else:
        q_ref, k_ref, v_ref, lq1, lk1, lq2, lk2, sub_ref, o_ref = refs
    l1 = jnp.sum(lq1[...] * lk1[...], axis=-1, keepdims=True)
    l2 = jnp.sum(lq2[...] * lk2[...], axis=-1, keepdims=True)
    lam = jnp.exp(l1) - jnp.exp(l2) + lam_init
    q = q_ref[0].astype(BF16)
    kn = k_ref[0].astype(BF16)
    vn = v_ref[0].astype(BF16)
    if has_cache:
        kc = ck_ref[0, 0].astype(BF16)
        vc = cv_ref[0, 0].astype(BF16)
    dn = (((1,), (1,)), ((), ()))
    def scores(j):
        c0 = j * DIFF_DH
        qh = q[:, c0:c0 + DIFF_DH]
        s_n = lax.dot_general(qh, kn[:, c0:c0 + DIFF_DH], dn, preferred_element_type=F32)
        s_c = lax.dot_general(qh, kc[:, c0:c0 + DIFF_DH], dn, preferred_element_type=F32) if has_cache else None
        return s_n, s_c

    nxt = scores(0)
    for h in range(H_B):
        w_new = None
        w_old = None
        for m in range(2):
            j = h * 2 + m
            s_n, s_c = nxt
            if j + 1 < 2 * H_B:
                nxt = scores(j + 1)
            mx = jnp.max(s_n, axis=-1, keepdims=True)
            if has_cache:
                mx = jnp.maximum(mx, jnp.max(s_c, axis=-1, keepdims=True))
            e_n = jnp.exp2((s_n - mx).astype(BF16))
            den = _row_sum_bf16(e_n)
            if has_cache:
                e_c = jnp.exp2((s_c - mx).astype(BF16))
                den = den + _row_sum_bf16(e_c)
            scale = ((1.0 / den) if m == 0 else (-lam / den)).astype(BF16)
            w_new = e_n * scale if m == 0 else w_new + e_n * scale
            if has_cache:
                w_old = e_c * scale if m == 0 else w_old + e_c * scale
        vh = vn[:, h * 128:(h + 1) * 128]
        o = jnp.dot(w_new, vh, preferred_element_type=F32)
        if has_cache:
            o = o + jnp.dot(w_old, vc[:, h * 128:(h + 1) * 128], preferred_element_type=F32)
        o_ref[0, :, h * 128:(h + 1) * 128] = _rms(o, sub_ref[...]) * (1.0 - lam_init)


def _attention(q, k, v, cache, l, p, lam_init):
    B, L, _ = q.shape
    has_cache = cache is not None
    tq = TL
    in_specs = [pl.BlockSpec((1, tq, D_B), lambda b, i: (b, i, 0)),
                pl.BlockSpec((1, L, D_B), lambda b, i: (b, 0, 0)),
                pl.BlockSpec((1, L, D_B), lambda b, i: (b, 0, 0))]
    args = [q, k, v]
    if has_cache:
        ck, cv = cache
        past = ck.shape[2]
        in_specs += [pl.BlockSpec((1, 1, past, D_B), lambda b, i: (b, l, 0, 0))] * 2
        args += [ck, cv]
    small = lambda n: pl.BlockSpec((1, n), lambda b, i: (0, 0))
    in_specs += [small(DIFF_DH)] * 4 + [small(128)]
    args += [p['diff_lq1'].reshape(1, -1), p['diff_lk1'].reshape(1, -1), p['diff_lq2'].reshape(1, -1),
             p['diff_lk2'].reshape(1, -1), p['diff_subln'].reshape(1, -1)]
    return pl.pallas_call(
        functools.partial(_attn_kernel, has_cache=has_cache, lam_init=lam_init),
        grid=(B, L // tq), in_specs=in_specs,
        out_specs=pl.BlockSpec((1, tq, D_B), lambda b, i: (b, i, 0)),
        out_shape=jax.ShapeDtypeStruct((B, L, D_B), F32),
        compiler_params=_cparams(2, 48 * 1024 * 1024), name="diff_attention",
    )(*args)


def _hfilt_time_kernel(emb_ref, w1_ref, b1_ref, fr_ref, w2_ref, b2_ref, w3_ref, dec_ref, h_ref, ss_ref, acc_ref):
    i = pl.program_id(0)
    emb = emb_ref[...]
    fr = fr_ref[...]
    h = jnp.sin(fr * (jnp.dot(emb.astype(BF16), w1_ref[...].astype(BF16), preferred_element_type=F32) + b1_ref[...]))
    h = jnp.sin(fr * (jnp.dot(h.astype(BF16), w2_ref[...].astype(BF16), preferred_element_type=F32) + b2_ref[...]))
    h = jnp.dot(h.astype(BF16), w3_ref[...].astype(BF16), preferred_element_type=F32)
    h = h * jnp.exp(-emb[:, 0:1] * jnp.abs(dec_ref[...]))
    n = h.shape[0]
    row = lax.broadcasted_iota(jnp.int32, (n, 4 * D_C), 0) + i * n
    col = lax.broadcasted_iota(jnp.int32, (n, 4 * D_C), 1)
    is_bwd = ((col // D_C) % 2) == 1
    h = jnp.where(jnp.logical_and(is_bwd, row == 0), 0.0, h)
    h_ref[...] = h

    @pl.when(i == 0)
    def _():
        acc_ref[...] = jnp.zeros_like(acc_ref)

    acc_ref[...] += jnp.sum(h * h, axis=0, keepdims=True)
    ss_ref[...] = acc_ref[...]


def _hfilt_freq_kernel(h_ref, ss_ref, c_ref, s_ref, hr_ref, hi_ref, a_ref, b_ref, nyq_ref):
    i = pl.program_id(0)

    @pl.when(i == 0)
    def _():
        for o in range(2):
            hf = h_ref[:, o * 2 * D_C:o * 2 * D_C + D_C]
            hb = h_ref[:, o * 2 * D_C + D_C:(o + 1) * 2 * D_C]
            a = hf + hb
            n = a.shape[0]
            alt = 1.0 - 2.0 * (lax.broadcasted_iota(jnp.int32, (n, 1), 0) % 2).astype(F32)
            nyq_ref[:, o * D_C:(o + 1) * D_C] = jnp.sum(a * alt, axis=0, keepdims=True)
            a_ref[:, o * D_C:(o + 1) * D_C] = a.astype(BF16)
            b_ref[:, o * D_C:(o + 1) * D_C] = (hb - hf).astype(BF16)

    ss = ss_ref[...]
    tot = jnp.concatenate([ss[:, 0:D_C] + ss[:, D_C:2 * D_C], ss[:, 2 * D_C:3 * D_C] + ss[:, 3 * D_C:4 * D_C]], axis=1)
    scale = lax.rsqrt(tot + 1e-6)

    hr_ref[...] = jnp.dot(c_ref[...], a_ref[...], preferred_element_type=F32) * scale
    hi = jnp.dot(s_ref[...], b_ref[...], preferred_element_type=F32) * scale
    tf = hi.shape[0]
    row = lax.broadcasted_iota(jnp.int32, (tf, 1), 0) + i * tf
    hi_ref[...] = jnp.where(row == 0, nyq_ref[...] * scale, hi)


def _hyena_filters(L, p, tabs):
    emb, cmat, smat = tabs
    tr = 256
    w1 = jnp.pad(p['hy_w1'], ((0, 128 - HY_EMB), (0, 0)))
    full = lambda a: pl.BlockSpec(a.shape, lambda i: (0,) * a.ndim)
    params = [w1, p['hy_b1'].reshape(1, -1), p['hy_freq'].reshape(1, -1), p['hy_w2'], p['hy_b2'].reshape(1, -1),
              p['hy_w3'], jnp.tile(p['hy_decay'], 4).reshape(1, -1)]
    h_raw, ss = pl.pallas_call(
        _hfilt_time_kernel, grid=(L // tr,),
        in_specs=[pl.BlockSpec((tr, 128), lambda i: (i, 0))] + [full(a) for a in params],
        out_specs=[pl.BlockSpec((tr, 4 * D_C), lambda i: (i, 0)), pl.BlockSpec((1, 4 * D_C), lambda i: (0, 0))],
        out_shape=[jax.ShapeDtypeStruct((L, 4 * D_C), F32), jax.ShapeDtypeStruct((1, 4 * D_C), F32)],
        scratch_shapes=[pltpu.VMEM((1, 4 * D_C), F32)],
        compiler_params=_cparams(1), name="hyena_filter_time",
    )(emb, *params)
    tf = 256
    mat = pl.BlockSpec((tf, L), lambda i: (i, 0))
    return pl.pallas_call(
        _hfilt_freq_kernel, grid=(L // tf,),
        in_specs=[pl.BlockSpec((L, 4 * D_C), lambda i: (0, 0)), pl.BlockSpec((1, 4 * D_C), lambda i: (0, 0)),
                  mat, mat],
        out_specs=[pl.BlockSpec((tf, 2 * D_C), lambda i: (i, 0))] * 2,
        out_shape=[jax.ShapeDtypeStruct((L, 2 * D_C), F32)] * 2,
        scratch_shapes=[pltpu.VMEM((L, 2 * D_C), BF16)] * 2 + [pltpu.VMEM((1, 2 * D_C), F32)],
        compiler_params=_cparams(1, 48 * 1024 * 1024), name="hyena_filter_freq",
    )(h_raw, ss, cmat, smat)


def _hconv_kernel(z_ref, zt_ref, gate_ref, hr_ref, hi_ref, bias_ref, c_ref, s_ref, o_ref,
                  zb_ref, yr_ref, yi_ref, nyq_ref, *, BG, L):
    ph = pl.program_id(1)
    i = pl.program_id(2)
    inv_n = 1.0 / (2 * L)

    @pl.when(jnp.logical_and(ph == 0, i == 0))
    def _():
        alt = 1.0 - 2.0 * (lax.broadcasted_iota(jnp.int32, (L, 1), 0) % 2).astype(F32)
        for b in range(BG):
            z = z_ref[b]
            zb_ref[:, b * D_C:(b + 1) * D_C] = z.astype(BF16)
            nyq_ref[:, b * D_C:(b + 1) * D_C] = jnp.sum(z * alt, axis=0, keepdims=True)

    tf = c_ref.shape[0]

    @pl.when(ph == 0)
    def _():
        zr = jnp.dot(c_ref[...], zb_ref[...], preferred_element_type=F32)
        zs = jnp.dot(s_ref[...], zb_ref[...], preferred_element_type=F32)
        hr = hr_ref[...]
        hi = hi_ref[...]
        row = lax.broadcasted_iota(jnp.int32, (tf, 1), 0) + i * tf
        wgt = jnp.where(row == 0, inv_n, 2.0 * inv_n)
        rows = pl.ds(pl.multiple_of(i * tf, tf), tf)
        for b in range(BG):
            sl = slice(b * D_C, (b + 1) * D_C)
            yr_ref[rows, sl] = ((zr[:, sl] * hr + zs[:, sl] * hi) * wgt).astype(BF16)
            yi_ref[rows, sl] = ((zr[:, sl] * hi - zs[:, sl] * hr) * (2.0 * inv_n)).astype(BF16)

        @pl.when(i == 0)
        def _():
            for b in range(BG):
                sl = slice(b * D_C, (b + 1) * D_C)
                nyq_ref[:, sl] = nyq_ref[:, sl] * hi[0:1, :] * inv_n

    @pl.when(ph == 1)
    def _():
        y = (jnp.dot(c_ref[...], yr_ref[...], preferred_element_type=F32)
             - jnp.dot(s_ref[...], yi_ref[...], preferred_element_type=F32))
        row = lax.broadcasted_iota(jnp.int32, (tf, 1), 0) + i * tf
        alt = 1.0 - 2.0 * (row % 2).astype(F32)
        y = y + alt * nyq_ref[...]
        for b in range(BG):
            sl = slice(b * D_C, (b + 1) * D_C)
            o_ref[b] = gate_ref[b] * (y[:, sl] + bias_ref[...] * zt_ref[b])


def _hconv(z, gate, hr, hi, bias, order, tabs):
    B, L, _ = z.shape
    _, ch, sh = tabs
    BG =min(B, 8) if L <= 256 else min(B, 4)
    assert B % BG == 0
    tf = 256
    nt = L // tf
    mat = pl.BlockSpec((tf, L), lambda g, ph, i: (i, 0))
    tile = pl.BlockSpec((BG, tf, D_C), lambda g, ph, i: (g, i * ph, 0))
    hspec = pl.BlockSpec((tf, D_C), lambda g, ph, i: (i * (1 - ph), order))
    cols = BG * D_C
    return pl.pallas_call(
        functools.partial(_hconv_kernel, BG=BG, L=L), grid=(B // BG, 2, nt),
        in_specs=[pl.BlockSpec((BG, L, D_C), lambda g, ph, i: (g, 0, 0)), tile, tile, hspec, hspec,
                  pl.BlockSpec((1, D_C), lambda g, ph, i: (0, 0)), mat, mat],
        out_specs=tile,
        out_shape=jax.ShapeDtypeStruct((B, L, D_C), F32),
        scratch_shapes=[pltpu.VMEM((L, cols), BF16)] * 3 + [pltpu.VMEM((1, cols), F32)],
        compiler_params=_cparams(3, 48 * 1024 * 1024), name="hyena_conv",
    )(z, z, gate, hr, hi, bias[order].reshape(1, D_C), ch, sh)


def _outffn_kernel(x_ref, yf_ref, yb_ref, bon_ref, g_ref, ob_ref, yc_ref, mod_ref,
                   lnw_ref, lnb_ref, gpost_ref, gffn_ref, gffn_post_ref, w_ref, w1_ref, w2_ref, o_ref):
    bd = _block_diag(D_A, RW_HEAD)
    y = yf_ref[0] + yb_ref[0]
    mu = _segsum(y, bd) * (1.0 / RW_HEAD)
    yc = y - mu
    var = _segsum(yc * yc, bd) * (1.0 / RW_HEAD)
    yn = yc * lax.rsqrt(var + GN_EPS) * lnw_ref[...] + lnb_ref[...]
    ya = (yn + bon_ref[0]) * g_ref[0]
    mix = (jnp.dot(ya.astype(BF16), w_ref[0:D_A, :], preferred_element_type=F32)
           + jnp.dot(ob_ref[0].astype(BF16), w_ref[D_A:D_A + D_B, :], preferred_element_type=F32)
           + jnp.dot(yc_ref[0].astype(BF16), w_ref[D_A + D_B:D_MODEL, :], preferred_element_type=F32))
    mod = mod_ref[0]
    gt1 = mod[:, 2 * D_MODEL:3 * D_MODEL]
    sh2 = mod[:, 3 * D_MODEL:4 * D_MODEL]
    sc2 = mod[:, 4 * D_MODEL:5 * D_MODEL]
    gt2 = mod[:, 5 * D_MODEL:6 * D_MODEL]
    x1 = x_ref[0] + gt1 * _rms(mix, gpost_ref[...])
    h = (_rms(x1, gffn_ref[...]) * (1.0 + sc2) + sh2).astype(BF16)
    acc = None
    for c in range(D_FF // FFN_CHUNK):
        a = jnp.dot(h, w1_ref[:, c * FFN_CHUNK:(c + 1) * FFN_CHUNK], preferred_element_type=F32)
        a = jnp.square(jnp.maximum(a, 0.0)).astype(BF16)
        part = jnp.dot(a, w2_ref[c * FFN_CHUNK:(c + 1) * FFN_CHUNK, :], preferred_element_type=F32)
        acc = part if acc is None else acc + part
    o_ref[0] = x1 + gt2 * _rms(acc, gffn_post_ref[...])


def _outffn(x, yf, yb, bon, g, ob, yc, mod, p, w_out, w1, w2):
    B, L, _ = x.shape
    mb = mod.shape[0]
    row = lambda b, i: (b, i, 0)
    c2 = lambda b, i: (0, 0)
    t = lambda n: pl.BlockSpec((1, TL, n), row)
    vec = lambda n: pl.BlockSpec((1, n), c2)
    in_specs = [t(D_MODEL), t(D_A), t(D_A), t(D_A), t(D_A), t(D_B), t(D_C),
                pl.BlockSpec((1, 1, 6 * D_MODEL), (lambda b, i: (b, 0, 0)) if mb > 1 else (lambda b, i: (0, 0, 0))),
                vec(D_A), vec(D_A), vec(D_MODEL), vec(D_MODEL), vec(D_MODEL),
                _layer_weight_spec(w_out, 2), _layer_weight_spec(w1, 2), _layer_weight_spec(w2, 2)]
    return pl.pallas_call(
        _outffn_kernel, grid=(B, L // TL), in_specs=in_specs,
        out_specs=t(D_MODEL), out_shape=jax.ShapeDtypeStruct((B, L, D_MODEL), F32),
        compiler_params=_cparams(2, V7X_VMEM_LIMIT), name="outproj_ffn",
    )(x, yf, yb, bon, g, ob, yc, mod, p['rwkv_ln_w'].reshape(1, -1), p['rwkv_ln_b'].reshape(1, -1),
      p['g_mix_post'].reshape(1, -1), p['g_ffn_pre'].reshape(1, -1), p['g_ffn_post'].reshape(1, -1),
      w_out[0], w1[0], w2[0])


@functools.lru_cache(maxsize=None)
def _rope_tables(L):
    rows = L // GRID_W
    row = np.repeat(np.arange(rows), GRID_W).astype(np.float64)
    col = np.tile(np.arange(GRID_W), rows).astype(np.float64)
    half = DIFF_DH // 2
    inv = ROPE_BASE ** (-np.arange(0, half, 2, dtype=np.float64) / half)
    ang_r = row[:, None] * inv[None]
    ang_c = col[:, None] * inv[None]
    ang = np.concatenate([ang_r, ang_r, ang_c, ang_c], axis=-1)
    cos, sin = np.cos(ang), np.sin(ang)
    quarter = (np.arange(DIFF_DH) // 16) % 2
    sin_a = np.where(quarter == 0, -sin, 0.0)
    sin_b = np.where(quarter == 1, sin, 0.0)
    rep = D_B // DIFF_DH
    return tuple(np.tile(t, (1, rep)).astype(np.float32) for t in (cos, sin_a, sin_b))


@functools.lru_cache(maxsize=None)
def _hyena_tables(L):
    t = np.linspace(0.0, 1.0, L, dtype=np.float32).astype(np.float64)[:, None]
    ang = (2.0 * math.pi / L) * np.arange(L, dtype=np.float64)[:, None]
    bands = np.linspace(1e-4, HY_BANDS - 1, HY_BANDS, dtype=np.float32).astype(np.float64)[None, :]
    emb = np.concatenate([t, np.cos(bands * ang), -np.sin(bands * ang)], axis=-1)
    emb = np.pad(emb, ((0, 0), (0, 128 - HY_EMB))).astype(np.float32)
    n = np.arange(L, dtype=np.int64)
    theta = ((n[:, None] * n[None, :]) % (2 * L)).astype(np.float64) * (math.pi / L)
    return emb, np.cos(theta).astype(BF16), np.sin(theta).astype(BF16)


def _layer(x, mod, p, wb, l, lam_init, cache, rope_tabs, hy_tabs, filt):
    B, L, _ = x.shape
    (uq, uk, uv, r, v, nkk, wf, kaf, kdf, wbk, kab, kdb, g, bon, x1h, x2h, hv) = _inproj(
        x, mod, p['g_mix_pre'], wb['w_in'], rope_tabs, p)
    if cache is None:
        s0 = jnp.zeros((B, 2, H_A, RW_HEAD, RW_HEAD), F32)
        kv_cache = None
    else:
        s0 = cache[0]
        kv_cache = (cache[1], cache[2])
    yf, yb, sfin = _scan(nkk, r, v, wf, kaf, kdf, wbk, kab, kdb, s0)
    ob = _attention(uq, uk, uv, kv_cache, l, p, lam_init)
    hr, hi = filt
    z1 = _hconv(hv, x1h, hr, hi, p['hy_bias'], 0, hy_tabs)
    yc = _hconv(z1, x2h, hr, hi, p['hy_bias'], 1, hy_tabs)
    x2 = _outffn(x, yf, yb, bon, g, ob, yc, mod, p, wb['w_out'], wb['w_ff1'], wb['w_ff2'])
    return x2, (sfin, uk, uv)


_LAYER_KEYS = ('g_mix_pre', 'g_mix_post', 'g_ffn_pre', 'g_ffn_post', 'rwkv_conv', 'rwkv_w0', 'rwkv_w2', 'rwkv_a0',
               'rwkv_a2', 'rwkv_g2', 'rwkv_kk', 'rwkv_ka', 'rwkv_rk', 'rwkv_ln_w', 'rwkv_ln_b', 'diff_lq1', 'diff_lk1',
               'diff_lq2', 'diff_lk2', 'diff_subln', 'hy_conv_w', 'hy_conv_b', 'hy_w1', 'hy_b1', 'hy_freq', 'hy_w2',
               'hy_b2', 'hy_w3', 'hy_decay', 'hy_bias')


def kernel(x_prompt, x_sample, state_rwkv, cache_k, cache_v, c, c_ctx, w_mod, b_mod, g_mix_pre, g_mix_post, g_ffn_pre, g_ffn_post, w_in, rwkv_conv, rwkv_w0, rwkv_w2, rwkv_a0, rwkv_a2, rwkv_g2, rwkv_kk, rwkv_ka, rwkv_rk, rwkv_ln_w, rwkv_ln_b, diff_lq1, diff_lk1, diff_lq2, diff_lk2, diff_subln, hy_conv_w, hy_conv_b, hy_w1, hy_b1, hy_freq, hy_w2, hy_b2, hy_w3, hy_decay, hy_bias, w_out, w_ff1, w_ff2):
    stacked = dict(g_mix_pre=g_mix_pre, g_mix_post=g_mix_post, g_ffn_pre=g_ffn_pre, g_ffn_post=g_ffn_post,
                   rwkv_conv=rwkv_conv, rwkv_w0=rwkv_w0, rwkv_w2=rwkv_w2, rwkv_a0=rwkv_a0, rwkv_a2=rwkv_a2,
                   rwkv_g2=rwkv_g2, rwkv_kk=rwkv_kk, rwkv_ka=rwkv_ka, rwkv_rk=rwkv_rk, rwkv_ln_w=rwkv_ln_w,
                   rwkv_ln_b=rwkv_ln_b, diff_lq1=diff_lq1, diff_lk1=diff_lk1, diff_lq2=diff_lq2, diff_lk2=diff_lk2,
                   diff_subln=diff_subln, hy_conv_w=hy_conv_w, hy_conv_b=hy_conv_b, hy_w1=hy_w1, hy_b1=hy_b1,
                   hy_freq=hy_freq, hy_w2=hy_w2, hy_b2=hy_b2, hy_w3=hy_w3, hy_decay=hy_decay, hy_bias=hy_bias)
    Bc, Lc, _ = x_prompt.shape
    Bs, Ls, _ = x_sample.shape
    past = cache_k.shape[2]

    cond = jnp.zeros((8, D_MODEL), F32).at[0:Bs].set(c).at[Bs].set(c_ctx)
    mod_all = _modulation(cond, w_mod, b_mod)

    rope_tabs = _rope_tables(Ls)
    tabs_c = _hyena_tables(Lc)
    tabs_s = _hyena_tables(Ls)
    ck = cache_k.reshape(Bs, DEPTH, past, D_B)
    cv = cache_v.reshape(Bs, DEPTH, past, D_B)

    w_in_b, w_out_b, w_ff1_b, w_ff2_b = (w.astype(BF16) for w in (w_in, w_out, w_ff1, w_ff2))
    xp, xs = x_prompt, x_sample
    st_list, k_list, v_list = [], [], []
    for l in range(DEPTH):
        p = {k: stacked[k][l] for k in _LAYER_KEYS}
        wb = dict(w_in=(w_in_b, l), w_out=(w_out_b, l), w_ff1=(w_ff1_b, l), w_ff2=(w_ff2_b, l))
        lam_init = 0.8 - 0.6 * math.exp(-0.3 * l)
        mod_lat = mod_all[l, 0:Bs].reshape(Bs, 1, 6 * D_MODEL)
        mod_ctx = mod_all[l, Bs:Bs + 1].reshape(1, 1, 6 * D_MODEL)
        filt_c = _hyena_filters(Lc, p, tabs_c)
        filt_s = _hyena_filters(Ls, p, tabs_s)
        xp, (s_ctx, k_ctx, v_ctx) = _layer(xp, mod_ctx, p, wb, l, lam_init, None, None, tabs_c, filt_c)
        st_list.append(s_ctx)
        k_list.append(k_ctx.reshape(Bc, Lc, H_B, 2, DIFF_DH))
        v_list.append(v_ctx.reshape(Bc, Lc, H_B, 2 * DIFF_DH))
        s0 = state_rwkv[:, l]
        xs, _ = _layer(xs, mod_lat, p, wb, l, lam_init, (s0, ck, cv), rope_tabs, tabs_s, filt_s)
    return (xp, xs, jnp.stack(st_list, axis=1), jnp.stack(k_list, axis=1), jnp.stack(v_list, axis=1))
```

```python
import functools
import math

import jax
import jax.numpy as jnp
import numpy as np
from jax import lax
from jax.experimental import pallas as pl
from jax.experimental.pallas import tpu as pltpu

F32 = jnp.float32
BF16 = jnp.bfloat16

D_MODEL = 1024
DEPTH = 4
GRID_W = 64
D_A = 256
RW_HEAD = 64
H_A = 4
D_B = 512
DIFF_DH = 64
H_B = 4
D_C = 256
HY_BANDS = 16
HY_EMB = 33
HY_FFN = 64
D_FF = 4096
FFN_CHUNK = 1024
ROPE_BASE = 10000.0
RMS_EPS = 1e-6
GN_EPS = 64e-5
N_A_COLS = 1152
IN_COLS = 3456

TL = 256
V7X_VMEM_LIMIT = 56 * 1024 * 1024


def _cparams(n_axes, vmem=None):
    return pltpu.CompilerParams(dimension_semantics=("arbitrary",) * n_axes,
                                vmem_limit_bytes=vmem)


def _split3(a):
    hi = a.astype(BF16)
    r1 = a - hi.astype(F32)
    mid = r1.astype(BF16)
    lo = (r1 - mid.astype(F32)).astype(BF16)
    return hi, mid, lo


def _split2(a):
    hi = a.astype(BF16)
    lo = (a - hi.astype(F32)).astype(BF16)
    return hi, lo


def _segsum(a, bd):
    hi, lo = _split2(a)
    return jnp.dot(hi, bd, preferred_element_type=F32) + jnp.dot(lo, bd, preferred_element_type=F32)


def _block_diag(n, seg):
    r = lax.broadcasted_iota(jnp.int32, (n, n), 0) // seg
    c = lax.broadcasted_iota(jnp.int32, (n, n), 1) // seg
    return (r == c).astype(BF16)


def _layer_weight_spec(w, n_grid_axes):
    stack, layer = w
    _, k, n = stack.shape
    if n_grid_axes == 2:
        return pl.BlockSpec((None, k, n), lambda b, i: (layer, 0, 0))
    raise NotImplementedError(n_grid_axes)


def _rms(x, g):
    return x * lax.rsqrt(jnp.mean(x * x, axis=-1, keepdims=True) + RMS_EPS) * g


def _mod_kernel(cond_ref, w_ref, b_ref, o_ref):
    c = cond_ref[...]
    s = (c * jax.nn.sigmoid(c)).astype(BF16)
    o_ref[0] = jnp.dot(s, w_ref[0].astype(BF16), preferred_element_type=F32) + b_ref[0]


def _modulation(cond, w_mod, b_mod):
    tn = 1536
    return pl.pallas_call(
        _mod_kernel,
        grid=(DEPTH, 6 * D_MODEL // tn),
        in_specs=[pl.BlockSpec((8, D_MODEL), lambda l, n: (0, 0)),
                  pl.BlockSpec((1, D_MODEL, tn), lambda l, n: (l, 0, n)),
                  pl.BlockSpec((1, 1, tn), lambda l, n: (l, 0, n))],
        out_specs=pl.BlockSpec((1, 8, tn), lambda l, n: (l, 0, n)),
        out_shape=jax.ShapeDtypeStruct((DEPTH, 8, 6 * D_MODEL), F32),
        compiler_params=_cparams(2, 40 * 1024 * 1024),
        name="modulation",
    )(cond, w_mod, b_mod.reshape(DEPTH, 1, 6 * D_MODEL))


def _rope(x, cos, sin_a, sin_b):
    n = x.shape[-1]
    return x * cos + pltpu.roll(x, n - 16, axis=1) * sin_a + pltpu.roll(x, 16, axis=1) * sin_b


def _shift_conv(u, prev_row, next_row, cw):
    n = u.shape[0]
    row = lax.broadcasted_iota(jnp.int32, (n, 1), 0)
    up = jnp.where(row == 0, prev_row, pltpu.roll(u, 1, axis=0))
    un = jnp.where(row == n - 1, next_row, pltpu.roll(u, n - 1, axis=0))
    return cw[0:1] * up + cw[1:2] * u + cw[2:3] * un


def _prep_body(ua, prev, nxt, uh, prevh, nxth,
               cw_ref, w0_ref, w2_ref, a0_ref, a2_ref, g2_ref, kkw_ref, kaw_ref, rk_ref, hcw_ref, hcb_ref,
               r_ref, v_ref, nkk_ref, wf_ref, kaf_ref, kdf_ref, wb_ref, kab_ref, kdb_ref, g_ref, bon_ref,
               x1_ref, x2_ref, hv_ref):
    bd = _block_diag(D_A, RW_HEAD)
    u_rkv = ua[:, 0:3 * D_A]
    rkv = _shift_conv(u_rkv, prev, nxt, cw_ref[...])
    r = rkv[:, 0:D_A]
    k = rkv[:, D_A:2 * D_A]
    v = rkv[:, 2 * D_A:3 * D_A]
    u_w = ua[:, 768:896]
    u_a = ua[:, 896:1024]
    u_g = ua[:, 1024:1152]

    g = jnp.dot(jax.nn.sigmoid(u_g).astype(BF16), g2_ref[...].astype(BF16), preferred_element_type=F32)
    kk = k * kkw_ref[...]
    kk = kk * lax.rsqrt(_segsum(kk * kk, bd) + 1e-12)
    r_ref[0] = r
    v_ref[0] = v
    nkk_ref[0] = -kk
    g_ref[0] = g

    kd_sum = None
    outs = ((wf_ref, kaf_ref, kdf_ref), (wb_ref, kab_ref, kdb_ref))
    for d in range(2):
        uw_d = u_w[:, d * 64:(d + 1) * 64]
        ua_d = u_a[:, d * 64:(d + 1) * 64]
        xw = w0_ref[d:d + 1, :] + jnp.dot(jnp.tanh(uw_d).astype(BF16), w2_ref[d].astype(BF16),
                                          preferred_element_type=F32)
        z = -xw
        softplus = jnp.maximum(z, 0.0) + jnp.log(1.0 + jnp.exp(-jnp.abs(z)))
        wl = -softplus - 0.5
        log_decay = -jnp.exp(wl)
        a = jax.nn.sigmoid(a0_ref[d:d + 1, :] + jnp.dot(ua_d.astype(BF16), a2_ref[d].astype(BF16),
                                                        preferred_element_type=F32))
        kd = k * (1.0 + (a - 1.0) * kaw_ref[...])
        w_o, ka_o, kd_o = outs[d]
        w_o[0] = log_decay
        ka_o[0] = kk * a
        kd_o[0] = kd
        kd_sum = kd if kd_sum is None else kd_sum + kd
    bon_ref[0] = _segsum(r * kd_sum * rk_ref[...], bd) * v

    hc = _shift_conv(uh, prevh, nxth, hcw_ref[...]) + hcb_ref[...]
    x1_ref[0] = hc[:, 0:D_C]
    x2_ref[0] = hc[:, D_C:2 * D_C]
    hv_ref[0] = hc[:, 2 * D_C:3 * D_C]


N_PREP_PARAMS = 11
N_PREP_OUTS = 14


def _inproj_kernel(*refs, rope):
    n_in = 6 + (3 if rope else 0)
    x_ref, xp_ref, xn_ref, mod_ref, g_ref, w_ref = refs[0:6]
    rope_refs = refs[6:n_in]
    prep_params = refs[n_in:n_in + N_PREP_PARAMS]
    uq_ref, uk_ref, uv_ref = refs[n_in + N_PREP_PARAMS:n_in + N_PREP_PARAMS + 3]
    prep_outs = refs[n_in + N_PREP_PARAMS + 3:]
    i = pl.program_id(1)
    nt = pl.num_programs(1)
    mod = mod_ref[0]
    sh1 = mod[:, 0:D_MODEL]
    sc1 = mod[:, D_MODEL:2 * D_MODEL]

    def normed(x):
        return (_rms(x, g_ref[...]) * (1.0 + sc1) + sh1).astype(BF16)

    u = jnp.dot(normed(x_ref[0]), w_ref[...], preferred_element_type=F32)
    q = u[:, N_A_COLS:N_A_COLS + D_B] * (DIFF_DH ** -0.5 * math.log2(math.e))
    k = u[:, N_A_COLS + D_B:N_A_COLS + 2 * D_B]
    if rope:
        cos_ref, sa_ref, sb_ref = rope_refs
        q = _rope(q, cos_ref[...], sa_ref[...], sb_ref[...])
        k = _rope(k, cos_ref[...], sa_ref[...], sb_ref[...])
    uq_ref[0] = q.astype(uq_ref.dtype)
    uk_ref[0] = k.astype(uk_ref.dtype)
    uv_ref[0] = u[:, N_A_COLS + 2 * D_B:N_A_COLS + 3 * D_B].astype(uv_ref.dtype)

    he = normed(jnp.concatenate([xp_ref[0], xn_ref[0]], axis=0))
    hy0 = N_A_COLS + 3 * D_B
    e_rkv = jnp.dot(he, w_ref[:, 0:3 * D_A], preferred_element_type=F32)
    e_hy = jnp.dot(he, w_ref[:, hy0:IN_COLS], preferred_element_type=F32)
    has_prev = i > 0
    has_next = i < nt - 1
    _prep_body(u[:, 0:N_A_COLS], jnp.where(has_prev, e_rkv[7:8], 0.0), jnp.where(has_next, e_rkv[8:9], 0.0),
               u[:, hy0:IN_COLS], jnp.where(has_prev, e_hy[7:8], 0.0), jnp.where(has_next, e_hy[8:9], 0.0),
               *prep_params, *prep_outs)


def _inproj(x, mod, g_pre, w_in, rope_tabs, p):
    B, L, _ = x.shape
    rope = rope_tabs is not None
    mb = mod.shape[0]
    nb8 = L // 8
    t8 = TL // 8
    row = lambda b, i: (b, i, 0)
    prev = lambda b, i: (b, jnp.maximum(i * t8 - 1, 0), 0)
    nxt = lambda b, i: (b, jnp.minimum((i + 1) * t8, nb8 - 1), 0)
    const2 = lambda b, i: (0, 0)

    def full(a):
        nd = a.ndim
        return pl.BlockSpec(a.shape, lambda b, i: (0,) * nd)

    params = [p['rwkv_conv'], p['rwkv_w0'], p['rwkv_w2'], p['rwkv_a0'], p['rwkv_a2'], p['rwkv_g2'],
              p['rwkv_kk'].reshape(1, D_A), p['rwkv_ka'].reshape(1, D_A), p['rwkv_rk'].reshape(1, D_A),
              p['hy_conv_w'], p['hy_conv_b'].reshape(1, 3 * D_C)]
    assert len(params) == N_PREP_PARAMS
    in_specs = [pl.BlockSpec((1, TL, D_MODEL), row), pl.BlockSpec((1, 8, D_MODEL), prev),
                pl.BlockSpec((1, 8, D_MODEL), nxt),
                pl.BlockSpec((1, 1, 6 * D_MODEL), (lambda b, i: (b, 0, 0)) if mb > 1 else (lambda b, i: (0, 0, 0))),
                pl.BlockSpec((1, D_MODEL), const2),
                _layer_weight_spec(w_in, 2)]
    args = [x, x, x, mod, g_pre.reshape(1, D_MODEL), w_in[0]]
    if rope:
        in_specs += [pl.BlockSpec((TL, D_B), lambda b, i: (i, 0))] * 3
        args += list(rope_tabs)
    in_specs += [full(a) for a in params]
    args += params
    qkv_dt = BF16 if rope else F32
    out_shape = ([jax.ShapeDtypeStruct((B, L, D_B), qkv_dt)] * 3
                 + [jax.ShapeDtypeStruct((B, L, D_A), F32)] * N_PREP_OUTS)
    out_specs = [pl.BlockSpec((1, TL, D_B), row)] * 3 + [pl.BlockSpec((1, TL, D_A), row)] * N_PREP_OUTS
    return pl.pallas_call(
        functools.partial(_inproj_kernel, rope=rope),
        grid=(B, L // TL), in_specs=in_specs, out_specs=out_specs, out_shape=out_shape,
        compiler_params=_cparams(2, 48 * 1024 * 1024), name="inproj",
    )(*args)


CHUNK = 64
CHUNKS_PER_ITER = 4
_NN = (((1,), (0,)), ((), ()))
_NT = (((1,), (1,)), ((), ()))
_TN = (((0,), (0,)), ((), ()))


def _mm3(a, b, dims):
    ah, al = _split2(a)
    bh, bl = _split2(b)
    f = lambda x, y: lax.dot_general(x, y, dims, preferred_element_type=F32)
    if dims is _TN:
        return f(ah, bh) + f(ah, bl) + f(al, bh)
    m = a.shape[0]
    both = f(jnp.concatenate([ah, al], axis=0), bh)
    return both[0:m] + both[m:2 * m] + f(ah, bl)


def _mm2(a, b, dims):
    ah = a.astype(BF16)
    bh, bl = _split2(b)
    n = b.shape[1]
    both = lax.dot_general(ah, jnp.concatenate([bh, bl], axis=1), dims, preferred_element_type=F32)
    return both[:, 0:n] + both[:, n:2 * n]


def _mm1(a, b, dims):
    return lax.dot_general(a.astype(BF16), b.astype(BF16), dims, preferred_element_type=F32)


def _unit_tri_inverses(ms, row, col):
    eye = (row == col).astype(F32)
    blk = lambda n: (row // n) == (col // n)
    m8 = [jnp.where(blk(8), m, 0.0) for m in ms]
    a2 = [_mm1(x, x, _NN) for x in m8]
    a4 = [_mm1(x, x, _NN) for x in a2]
    p = [eye + x + y + _mm1(x, y, _NN) for x, y in zip(m8, a2)]
    ts = [x + _mm1(x, y, _NN) for x, y in zip(p, a4)]
    n = 8
    while n < CHUNK:
        sel = jnp.logical_and(blk(2 * n), jnp.logical_not(blk(n)))
        ot = [_mm1(jnp.where(sel, m, 0.0), t, _NN) for m, t in zip(ms, ts)]
        ts = [t + _mm1(t, o, _NN) for t, o in zip(ts, ot)]
        n *= 2
    return ts


def _scan_kernel(nkkf_ref, rf_ref, vf_ref, lwf_ref, kaf_ref, kdf_ref,
                 nkkb_ref, rb_ref, vb_ref, lwb_ref, kab_ref, kdb_ref, s0_ref,
                 yf_ref, yb_ref, sfin_ref, s_ref):
    C = CHUNK
    j = pl.program_id(1)
    nj = pl.num_programs(1)
    row = lax.broadcasted_iota(jnp.int32, (C, C), 0)
    col = lax.broadcasted_iota(jnp.int32, (C, C), 1)
    dirs = ((nkkf_ref, rf_ref, vf_ref, lwf_ref, kaf_ref, kdf_ref, yf_ref),
            (nkkb_ref, rb_ref, vb_ref, lwb_ref, kab_ref, kdb_ref, yb_ref))
    n_chunks = nkkf_ref.shape[1] // C

    @pl.when(j == 0)
    def _():
        s_ref[...] = s0_ref[0]

    def build(c, d, probs):
        rows = pl.ds(pl.multiple_of(c * C, C), C)
        nkk_ref, r_ref, v_ref, lw_ref, b_ref, k_ref = dirs[d][0:6]
        alpha = nkk_ref[0, rows, :]
        r = r_ref[0, rows, :]
        v = v_ref[0, rows, :]
        before = (row > col) if d == 0 else (row < col)
        upto = (row >= col) if d == 0 else (row <= col)
        lw = lw_ref[0, rows, :]
        beta = b_ref[0, rows, :]
        kd = k_ref[0, rows, :]
        tri = upto.astype(BF16)
        l_hi, l_mid, l_lo = _split3(lw)
        lam = (jnp.dot(tri, l_hi, preferred_element_type=F32) + jnp.dot(tri, l_mid, preferred_element_type=F32)
               + jnp.dot(tri, l_lo, preferred_element_type=F32))
        lam_prev = lam - lw
        lam_mid = lam[C // 2:C // 2 + 1, :]
        lam_end = lam[C - 1:C, :] if d == 0 else lam[0:1, :]
        e_to_mid = jnp.exp(lam_mid - lam)
        a_m = alpha * jnp.exp(lam_prev - lam_mid)
        b_m = beta * e_to_mid
        k_m = kd * e_to_mid
        r_m = r * jnp.exp(lam - lam_mid)
        a_0 = alpha * jnp.exp(lam_prev)
        r_0 = r * jnp.exp(lam)
        e_to_end = jnp.exp(lam_end - lam)
        b_e = beta * e_to_end
        k_e = kd * e_to_end
        p_end = jnp.exp(lam_end)
        for h in range(H_A):
            sl = slice(h * RW_HEAD, (h + 1) * RW_HEAD)
            probs.append(dict(
                rows=rows, d=d, h=h, before=before, upto=upto, pc=p_end[:, sl],
                ar=jnp.concatenate([a_m[:, sl], r_m[:, sl]], axis=0),
                bk=jnp.concatenate([b_m[:, sl], k_m[:, sl]], axis=0),
                a0=a_0[:, sl], r0=r_0[:, sl], vh=v[:, sl],
                bke=jnp.concatenate([b_e[:, sl], k_e[:, sl]], axis=0)))

    def body(ci, carry):
        probs = []
        for cc in range(CHUNKS_PER_ITER):
            cf = ci * CHUNKS_PER_ITER + cc
            build(cf, 0, probs)
            build(n_chunks - 1 - cf, 1, probs)
        g12 = [_mm1(q['ar'], q['bk'], _NT) for q in probs]
        ms = [jnp.where(q['before'], g[0:C, 0:C], 0.0) for q, g in zip(probs, g12)]
        ns = [jnp.where(q['before'], g[0:C, C:2 * C], 0.0) for q, g in zip(probs, g12)]
        rbk = [jnp.concatenate([jnp.where(q['upto'], g[C:2 * C, 0:C], 0.0),
                                jnp.where(q['upto'], g[C:2 * C, C:2 * C], 0.0)], axis=1) for q, g in zip(probs, g12)]
        nvs = [_mm1(n, q['vh'], _NN) for q, n in zip(probs, ns)]
        ts = _unit_tri_inverses(ms, row, col)
        xs = [_mm2(t, jnp.concatenate([q['a0'], nv], axis=1), _NN) for q, t, nv in zip(probs, ts, nvs)]
        zeros = jnp.zeros((C, RW_HEAD), F32)
        ws = [jnp.concatenate([x, jnp.concatenate([zeros, q['vh']], axis=1)], axis=0) for q, x in zip(probs, xs)]
        rys = [_mm1(rb, w, _NN) for rb, w in zip(rbk, ws)]
        fgs = [_mm3(w, q['bke'], _TN) for q, w in zip(probs, ws)]
        ys = []
        for q, ry, fg in zip(probs, rys, fgs):
            s = s_ref[q['d'], q['h']]
            rp = q['r0'] + ry[:, 0:RW_HEAD]
            ys.append(ry[:, RW_HEAD:2 * RW_HEAD] + _mm1(rp, s, _NT))
            s_ref[q['d'], q['h']] = s * q['pc'] + _mm1(s, fg[0:RW_HEAD], _NN) + fg[RW_HEAD:2 * RW_HEAD]
        for k0 in range(0, len(probs), H_A):
            q0 = probs[k0]
            dirs[q0['d']][6][0, q0['rows'], :] = jnp.concatenate(ys[k0:k0 + H_A], axis=1)
        return carry

    lax.fori_loop(0, n_chunks // CHUNKS_PER_ITER, body, 0)

    @pl.when(j == nj - 1)
    def _():
        sfin_ref[0] = s_ref[...]


def _scan(nkk, r, v, lwf, kaf, kdf, lwb, kab, kdb, s0):
    B, L, _ = r.shape
    nj = L // TL
    tile = pl.BlockSpec((1, TL, D_A), lambda b, j: (b, j, 0))
    rtile = pl.BlockSpec((1, TL, D_A), lambda b, j: (b, nj - 1 - j, 0))
    st = pl.BlockSpec((1, 2, H_A, RW_HEAD, RW_HEAD), lambda b, j: (b, 0, 0, 0, 0))
    f32 = lambda *s: jax.ShapeDtypeStruct(s, F32)
    return pl.pallas_call(
        _scan_kernel, grid=(B, nj),
        in_specs=[tile] * 6 + [rtile] * 6 + [st],
        out_specs=[tile, rtile, st],
        out_shape=[f32(B, L, D_A), f32(B, L, D_A), f32(B, 2, H_A, RW_HEAD, RW_HEAD)],
        scratch_shapes=[pltpu.VMEM((2, H_A, RW_HEAD, RW_HEAD), F32)],
        compiler_params=_cparams(2, 40 * 1024 * 1024), name="rwkv_scan",
    )(nkk, r, v, lwf, kaf, kdf, nkk, r, v, lwb, kab, kdb, s0)


def _row_sum_bf16(e):
    n = e.shape[1] // 128
    acc = None
    for g0 in range(0, n, 4):
        part = e[:, g0 * 128:(g0 + 1) * 128]
        for j in range(g0 + 1, min(g0 + 4, n)):
            part = part + e[:, j * 128:(j + 1) * 128]
        pf = part.astype(F32)
        acc = pf if acc is None else acc + pf
    return jnp.sum(acc, axis=-1, keepdims=True)


def _attn_kernel(*refs, has_cache, lam_init):
    if has_cache:
        q_ref, k_ref, v_ref, ck_ref, cv_ref, lq1, lk1, lq2, lk2, sub_ref, o_ref = refs
    else:
        q_ref, k_ref, v_ref, lq1, lk1, lq2, lk2, sub_ref, o_ref = refs
    l1 = jnp.sum(lq1[...] * lk1[...], axis=-1, keepdims=True)
    l2 = jnp.sum(lq2[...] * lk2[...], axis=-1, keepdims=True)
    lam = jnp.exp(l1) - jnp.exp(l2) + lam_init
    q = q_ref[0].astype(BF16)
    kn = k_ref[0].astype(BF16)
    vn = v_ref[0].astype(BF16)
    if has_cache:
        kc = ck_ref[0, 0].astype(BF16)
        vc = cv_ref[0, 0].astype(BF16)
    dn = (((1,), (1,)), ((), ()))
    def scores(j):
        c0 = j * DIFF_DH
        qh = q[:, c0:c0 + DIFF_DH]
        s_n = lax.dot_general(qh, kn[:, c0:c0 + DIFF_DH], dn, preferred_element_type=F32)
        s_c = lax.dot_general(qh, kc[:, c0:c0 + DIFF_DH], dn, preferred_element_type=F32) if has_cache else None
        return s_n, s_c

    nxt = scores(0)
    for h in range(H_B):
        o = None
        for m in range(2):
            j = h * 2 + m
            s_n, s_c = nxt
            if j + 1 < 2 * H_B:
                nxt = scores(j + 1)
            mx = jnp.max(s_n, axis=-1, keepdims=True)
            if has_cache:
                mx = jnp.maximum(mx, jnp.max(s_c, axis=-1, keepdims=True))
            e_n = jnp.exp2((s_n - mx).astype(BF16))
            den = _row_sum_bf16(e_n)
            if has_cache:
                e_c = jnp.exp2((s_c - mx).astype(BF16))
                den = den + _row_sum_bf16(e_c)
            o_m = jnp.dot(e_n, vn[:, h * 128:(h + 1) * 128], preferred_element_type=F32)
            if has_cache:
                o_m = o_m + jnp.dot(e_c, vc[:, h * 128:(h + 1) * 128], preferred_element_type=F32)
            o = o_m / den if m == 0 else o - o_m * (lam / den)
        o_ref[0, :, h * 128:(h + 1) * 128] = _rms(o, sub_ref[...]) * (1.0 - lam_init)


def _attention(q, k, v, cache, l, p, lam_init):
    B, L, _ = q.shape
    has_cache = cache is not None
    tq = TL
    in_specs = [pl.BlockSpec((1, tq, D_B), lambda b, i: (b, i, 0)),
                pl.BlockSpec((1, L, D_B), lambda b, i: (b, 0, 0)),
                pl.BlockSpec((1, L, D_B), lambda b, i: (b, 0, 0))]
    args = [q, k, v]
    if has_cache:
        ck, cv = cache
        past = ck.shape[2]
        in_specs += [pl.BlockSpec((1, 1, past, D_B), lambda b, i: (b, l, 0, 0))] * 2
        args += [ck, cv]
    small = lambda n: pl.BlockSpec((1, n), lambda b, i: (0, 0))
    in_specs += [small(DIFF_DH)] * 4 + [small(128)]
    args += [p['diff_lq1'].reshape(1, -1), p['diff_lk1'].reshape(1, -1), p['diff_lq2'].reshape(1, -1),
             p['diff_lk2'].reshape(1, -1), p['diff_subln'].reshape(1, -1)]
    return pl.pallas_call(
        functools.partial(_attn_kernel, has_cache=has_cache, lam_init=lam_init),
        grid=(B, L // tq), in_specs=in_specs,
        out_specs=pl.BlockSpec((1, tq, D_B), lambda b, i: (b, i, 0)),
        out_shape=jax.ShapeDtypeStruct((B, L, D_B), F32),
        compiler_params=_cparams(2, 48 * 1024 * 1024), name="diff_attention",
    )(*args)


def _hfilt_time_kernel(emb_ref, w1_ref, b1_ref, fr_ref, w2_ref, b2_ref, w3_ref, dec_ref, h_ref, ss_ref, acc_ref):
    i = pl.program_id(0)
    emb = emb_ref[...]
    fr = fr_ref[...]
    h = jnp.sin(fr * (jnp.dot(emb.astype(BF16), w1_ref[...].astype(BF16), preferred_element_type=F32) + b1_ref[...]))
    h = jnp.sin(fr * (jnp.dot(h.astype(BF16), w2_ref[...].astype(BF16), preferred_element_type=F32) + b2_ref[...]))
    h = jnp.dot(h.astype(BF16), w3_ref[...].astype(BF16), preferred_element_type=F32)
    h = h * jnp.exp(-emb[:, 0:1] * jnp.abs(dec_ref[...]))
    n = h.shape[0]
    row = lax.broadcasted_iota(jnp.int32, (n, 4 * D_C), 0) + i * n
    col = lax.broadcasted_iota(jnp.int32, (n, 4 * D_C), 1)
    is_bwd = ((col // D_C) % 2) == 1
    h = jnp.where(jnp.logical_and(is_bwd, row == 0), 0.0, h)
    h_ref[...] = h

    @pl.when(i == 0)
    def _():
        acc_ref[...] = jnp.zeros_like(acc_ref)

    acc_ref[...] += jnp.sum(h * h, axis=0, keepdims=True)
    ss_ref[...] = acc_ref[...]


def _hfilt_freq_kernel(h_ref, ss_ref, c_ref, s_ref, hr_ref, hi_ref, a_ref, b_ref, nyq_ref):
    i = pl.program_id(0)

    @pl.when(i == 0)
    def _():
        for o in range(2):
            hf = h_ref[:, o * 2 * D_C:o * 2 * D_C + D_C]
            hb = h_ref[:, o * 2 * D_C + D_C:(o + 1) * 2 * D_C]
            a = hf + hb
            n = a.shape[0]
            alt = 1.0 - 2.0 * (lax.broadcasted_iota(jnp.int32, (n, 1), 0) % 2).astype(F32)
            nyq_ref[:, o * D_C:(o + 1) * D_C] = jnp.sum(a * alt, axis=0, keepdims=True)
            a_ref[:, o * D_C:(o + 1) * D_C] = a.astype(BF16)
            b_ref[:, o * D_C:(o + 1) * D_C] = (hb - hf).astype(BF16)

    ss = ss_ref[...]
    tot = jnp.concatenate([ss[:, 0:D_C] + ss[:, D_C:2 * D_C], ss[:, 2 * D_C:3 * D_C] + ss[:, 3 * D_C:4 * D_C]], axis=1)
    scale = lax.rsqrt(tot + 1e-6)

    hr_ref[...] = jnp.dot(c_ref[...], a_ref[...], preferred_element_type=F32) * scale
    hi = jnp.dot(s_ref[...], b_ref[...], preferred_element_type=F32) * scale
    tf = hi.shape[0]
    row = lax.broadcasted_iota(jnp.int32, (tf, 1), 0) + i * tf
    hi_ref[...] = jnp.where(row == 0, nyq_ref[...] * scale, hi)


def _hyena_filters(L, p, tabs):
    emb, cmat, smat = tabs
    tr = 256
    w1 = jnp.pad(p['hy_w1'], ((0, 128 - HY_EMB), (0, 0)))
    full = lambda a: pl.BlockSpec(a.shape, lambda i: (0,) * a.ndim)
    params = [w1, p['hy_b1'].reshape(1, -1), p['hy_freq'].reshape(1, -1), p['hy_w2'], p['hy_b2'].reshape(1, -1),
              p['hy_w3'], jnp.tile(p['hy_decay'], 4).reshape(1, -1)]
    h_raw, ss = pl.pallas_call(
        _hfilt_time_kernel, grid=(L // tr,),
        in_specs=[pl.BlockSpec((tr, 128), lambda i: (i, 0))] + [full(a) for a in params],
        out_specs=[pl.BlockSpec((tr, 4 * D_C), lambda i: (i, 0)), pl.BlockSpec((1, 4 * D_C), lambda i: (0, 0))],
        out_shape=[jax.ShapeDtypeStruct((L, 4 * D_C), F32), jax.ShapeDtypeStruct((1, 4 * D_C), F32)],
        scratch_shapes=[pltpu.VMEM((1, 4 * D_C), F32)],
        compiler_params=_cparams(1), name="hyena_filter_time",
    )(emb, *params)
    tf = 256
    mat = pl.BlockSpec((tf, L), lambda i: (i, 0))
    return pl.pallas_call(
        _hfilt_freq_kernel, grid=(L // tf,),
        in_specs=[pl.BlockSpec((L, 4 * D_C), lambda i: (0, 0)), pl.BlockSpec((1, 4 * D_C), lambda i: (0, 0)),
                  mat, mat],
        out_specs=[pl.BlockSpec((tf, 2 * D_C), lambda i: (i, 0))] * 2,
        out_shape=[jax.ShapeDtypeStruct((L, 2 * D_C), F32)] * 2,
        scratch_shapes=[pltpu.VMEM((L, 2 * D_C), BF16)] * 2 + [pltpu.VMEM((1, 2 * D_C), F32)],
        compiler_params=_cparams(1, 48 * 1024 * 1024), name="hyena_filter_freq",
    )(h_raw, ss, cmat, smat)


def _hconv_kernel(z_ref, zt_ref, gate_ref, hr_ref, hi_ref, bias_ref, c_ref, s_ref, o_ref,
                  zb_ref, yr_ref, yi_ref, nyq_ref, *, BG, L):
    ph = pl.program_id(1)
    i = pl.program_id(2)
    inv_n = 1.0 / (2 * L)

    @pl.when(jnp.logical_and(ph == 0, i == 0))
    def _():
        alt = 1.0 - 2.0 * (lax.broadcasted_iota(jnp.int32, (L, 1), 0) % 2).astype(F32)
        for b in range(BG):
            z = z_ref[b]
            zb_ref[:, b * D_C:(b + 1) * D_C] = z.astype(BF16)
            nyq_ref[:, b * D_C:(b + 1) * D_C] = jnp.sum(z * alt, axis=0, keepdims=True)

    tf = c_ref.shape[0]

    @pl.when(ph == 0)
    def _():
        zr = jnp.dot(c_ref[...], zb_ref[...], preferred_element_type=F32)
        zs = jnp.dot(s_ref[...], zb_ref[...], preferred_element_type=F32)
        hr = hr_ref[...]
        hi = hi_ref[...]
        row = lax.broadcasted_iota(jnp.int32, (tf, 1), 0) + i * tf
        wgt = jnp.where(row == 0, inv_n, 2.0 * inv_n)
        rows = pl.ds(pl.multiple_of(i * tf, tf), tf)
        for b in range(BG):
            sl = slice(b * D_C, (b + 1) * D_C)
            yr_ref[rows, sl] = ((zr[:, sl] * hr + zs[:, sl] * hi) * wgt).astype(BF16)
            yi_ref[rows, sl] = ((zr[:, sl] * hi - zs[:, sl] * hr) * (2.0 * inv_n)).astype(BF16)

        @pl.when(i == 0)
        def _():
            for b in range(BG):
                sl = slice(b * D_C, (b + 1) * D_C)
                nyq_ref[:, sl] = nyq_ref[:, sl] * hi[0:1, :] * inv_n

    @pl.when(ph == 1)
    def _():
        y = (jnp.dot(c_ref[...], yr_ref[...], preferred_element_type=F32)
             - jnp.dot(s_ref[...], yi_ref[...], preferred_element_type=F32))
        row = lax.broadcasted_iota(jnp.int32, (tf, 1), 0) + i * tf
        alt = 1.0 - 2.0 * (row % 2).astype(F32)
        y = y + alt * nyq_ref[...]
        for b in range(BG):
            sl = slice(b * D_C, (b + 1) * D_C)
            o_ref[b] = gate_ref[b] * (y[:, sl] + bias_ref[...] * zt_ref[b])


def _hconv(z, gate, hr, hi, bias, order, tabs):
    B, L, _ = z.shape
    _, ch, sh = tabs
    BG =min(B, 8) if L <= 256 else min(B, 4)
    assert B % BG == 0
    tf = 256
    nt = L // tf
    mat = pl.BlockSpec((tf, L), lambda g, ph, i: (i, 0))
    tile = pl.BlockSpec((BG, tf, D_C), lambda g, ph, i: (g, i * ph, 0))
    hspec = pl.BlockSpec((tf, D_C), lambda g, ph, i: (i * (1 - ph), order))
    cols = BG * D_C
    return pl.pallas_call(
        functools.partial(_hconv_kernel, BG=BG, L=L), grid=(B // BG, 2, nt),
        in_specs=[pl.BlockSpec((BG, L, D_C), lambda g, ph, i: (g, 0, 0)), tile, tile, hspec, hspec,
                  pl.BlockSpec((1, D_C), lambda g, ph, i: (0, 0)), mat, mat],
        out_specs=tile,
        out_shape=jax.ShapeDtypeStruct((B, L, D_C), F32),
        scratch_shapes=[pltpu.VMEM((L, cols), BF16)] * 3 + [pltpu.VMEM((1, cols), F32)],
        compiler_params=_cparams(3, 48 * 1024 * 1024), name="hyena_conv",
    )(z, z, gate, hr, hi, bias[order].reshape(1, D_C), ch, sh)


def _outffn_kernel(x_ref, yf_ref, yb_ref, bon_ref, g_ref, ob_ref, yc_ref, mod_ref,
                   lnw_ref, lnb_ref, gpost_ref, gffn_ref, gffn_post_ref, w_ref, w1_ref, w2_ref, o_ref):
    bd = _block_diag(D_A, RW_HEAD)
    y = yf_ref[0] + yb_ref[0]
    mu = _segsum(y, bd) * (1.0 / RW_HEAD)
    yc = y - mu
    var = _segsum(yc * yc, bd) * (1.0 / RW_HEAD)
    yn = yc * lax.rsqrt(var + GN_EPS) * lnw_ref[...] + lnb_ref[...]
    ya = (yn + bon_ref[0]) * g_ref[0]
    mix = (jnp.dot(ya.astype(BF16), w_ref[0:D_A, :], preferred_element_type=F32)
           + jnp.dot(ob_ref[0].astype(BF16), w_ref[D_A:D_A + D_B, :], preferred_element_type=F32)
           + jnp.dot(yc_ref[0].astype(BF16), w_ref[D_A + D_B:D_MODEL, :], preferred_element_type=F32))
    mod = mod_ref[0]
    gt1 = mod[:, 2 * D_MODEL:3 * D_MODEL]
    sh2 = mod[:, 3 * D_MODEL:4 * D_MODEL]
    sc2 = mod[:, 4 * D_MODEL:5 * D_MODEL]
    gt2 = mod[:, 5 * D_MODEL:6 * D_MODEL]
    x1 = x_ref[0] + gt1 * _rms(mix, gpost_ref[...])
    h = (_rms(x1, gffn_ref[...]) * (1.0 + sc2) + sh2).astype(BF16)
    acc = None
    for c in range(D_FF // FFN_CHUNK):
        a = jnp.dot(h, w1_ref[:, c * FFN_CHUNK:(c + 1) * FFN_CHUNK], preferred_element_type=F32)
        a = jnp.square(jnp.maximum(a, 0.0)).astype(BF16)
        part = jnp.dot(a, w2_ref[c * FFN_CHUNK:(c + 1) * FFN_CHUNK, :], preferred_element_type=F32)
        acc = part if acc is None else acc + part
    o_ref[0] = x1 + gt2 * _rms(acc, gffn_post_ref[...])


def _outffn(x, yf, yb, bon, g, ob, yc, mod, p, w_out, w1, w2):
    B, L, _ = x.shape
    mb = mod.shape[0]
    row = lambda b, i: (b, i, 0)
    c2 = lambda b, i: (0, 0)
    t = lambda n: pl.BlockSpec((1, TL, n), row)
    vec = lambda n: pl.BlockSpec((1, n), c2)
    in_specs = [t(D_MODEL), t(D_A), t(D_A), t(D_A), t(D_A), t(D_B), t(D_C),
                pl.BlockSpec((1, 1, 6 * D_MODEL), (lambda b, i: (b, 0, 0)) if mb > 1 else (lambda b, i: (0, 0, 0))),
                vec(D_A), vec(D_A), vec(D_MODEL), vec(D_MODEL), vec(D_MODEL),
                _layer_weight_spec(w_out, 2), _layer_weight_spec(w1, 2), _layer_weight_spec(w2, 2)]
    return pl.pallas_call(
        _outffn_kernel, grid=(B, L // TL), in_specs=in_specs,
        out_specs=t(D_MODEL), out_shape=jax.ShapeDtypeStruct((B, L, D_MODEL), F32),
        compiler_params=_cparams(2, V7X_VMEM_LIMIT), name="outproj_ffn",
    )(x, yf, yb, bon, g, ob, yc, mod, p['rwkv_ln_w'].reshape(1, -1), p['rwkv_ln_b'].reshape(1, -1),
      p['g_mix_post'].reshape(1, -1), p['g_ffn_pre'].reshape(1, -1), p['g_ffn_post'].reshape(1, -1),
      w_out[0], w1[0], w2[0])


@functools.lru_cache(maxsize=None)
def _rope_tables(L):
    rows = L // GRID_W
    row = np.repeat(np.arange(rows), GRID_W).astype(np.float64)
    col = np.tile(np.arange(GRID_W), rows).astype(np.float64)
    half = DIFF_DH // 2
    inv = ROPE_BASE ** (-np.arange(0, half, 2, dtype=np.float64) / half)
    ang_r = row[:, None] * inv[None]
    ang_c = col[:, None] * inv[None]
    ang = np.concatenate([ang_r, ang_r, ang_c, ang_c], axis=-1)
    cos, sin = np.cos(ang), np.sin(ang)
    quarter = (np.arange(DIFF_DH) // 16) % 2
    sin_a = np.where(quarter == 0, -sin, 0.0)
    sin_b = np.where(quarter == 1, sin, 0.0)
    rep = D_B // DIFF_DH
    return tuple(np.tile(t, (1, rep)).astype(np.float32) for t in (cos, sin_a, sin_b))


@functools.lru_cache(maxsize=None)
def _hyena_tables(L):
    t = np.linspace(0.0, 1.0, L, dtype=np.float32).astype(np.float64)[:, None]
    ang = (2.0 * math.pi / L) * np.arange(L, dtype=np.float64)[:, None]
    bands = np.linspace(1e-4, HY_BANDS - 1, HY_BANDS, dtype=np.float32).astype(np.float64)[None, :]
    emb = np.concatenate([t, np.cos(bands * ang), -np.sin(bands * ang)], axis=-1)
    emb = np.pad(emb, ((0, 0), (0, 128 - HY_EMB))).astype(np.float32)
    n = np.arange(L, dtype=np.int64)
    theta = ((n[:, None] * n[None, :]) % (2 * L)).astype(np.float64) * (math.pi / L)
    return emb, np.cos(theta).astype(BF16), np.sin(theta).astype(BF16)


def _layer(x, mod, p, wb, l, lam_init, cache, rope_tabs, hy_tabs, filt):
    B, L, _ = x.shape
    (uq, uk, uv, r, v, nkk, wf, kaf, kdf, wbk, kab, kdb, g, bon, x1h, x2h, hv) = _inproj(
        x, mod, p['g_mix_pre'], wb['w_in'], rope_tabs, p)
    if cache is None:
        s0 = jnp.zeros((B, 2, H_A, RW_HEAD, RW_HEAD), F32)
        kv_cache = None
    else:
        s0 = cache[0]
        kv_cache = (cache[1], cache[2])
    yf, yb, sfin = _scan(nkk, r, v, wf, kaf, kdf, wbk, kab, kdb, s0)
    ob = _attention(uq, uk, uv, kv_cache, l, p, lam_init)
    hr, hi = filt
    z1 = _hconv(hv, x1h, hr, hi, p['hy_bias'], 0, hy_tabs)
    yc = _hconv(z1, x2h, hr, hi, p['hy_bias'], 1, hy_tabs)
    x2 = _outffn(x, yf, yb, bon, g, ob, yc, mod, p, wb['w_out'], wb['w_ff1'], wb['w_ff2'])
    return x2, (sfin, uk, uv)


_LAYER_KEYS = ('g_mix_pre', 'g_mix_post', 'g_ffn_pre', 'g_ffn_post', 'rwkv_conv', 'rwkv_w0', 'rwkv_w2', 'rwkv_a0',
               'rwkv_a2', 'rwkv_g2', 'rwkv_kk', 'rwkv_ka', 'rwkv_rk', 'rwkv_ln_w', 'rwkv_ln_b', 'diff_lq1', 'diff_lk1',
               'diff_lq2', 'diff_lk2', 'diff_subln', 'hy_conv_w', 'hy_conv_b', 'hy_w1', 'hy_b1', 'hy_freq', 'hy_w2',
               'hy_b2', 'hy_w3', 'hy_decay', 'hy_bias')


def kernel(x_prompt, x_sample, state_rwkv, cache_k, cache_v, c, c_ctx, w_mod, b_mod, g_mix_pre, g_mix_post, g_ffn_pre, g_ffn_post, w_in, rwkv_conv, rwkv_w0, rwkv_w2, rwkv_a0, rwkv_a2, rwkv_g2, rwkv_kk, rwkv_ka, rwkv_rk, rwkv_ln_w, rwkv_ln_b, diff_lq1, diff_lk1, diff_lq2, diff_lk2, diff_subln, hy_conv_w, hy_conv_b, hy_w1, hy_b1, hy_freq, hy_w2, hy_b2, hy_w3, hy_decay, hy_bias, w_out, w_ff1, w_ff2):
    stacked = dict(g_mix_pre=g_mix_pre, g_mix_post=g_mix_post, g_ffn_pre=g_ffn_pre, g_ffn_post=g_ffn_post,
                   rwkv_conv=rwkv_conv, rwkv_w0=rwkv_w0, rwkv_w2=rwkv_w2, rwkv_a0=rwkv_a0, rwkv_a2=rwkv_a2,
                   rwkv_g2=rwkv_g2, rwkv_kk=rwkv_kk, rwkv_ka=rwkv_ka, rwkv_rk=rwkv_rk, rwkv_ln_w=rwkv_ln_w,
                   rwkv_ln_b=rwkv_ln_b, diff_lq1=diff_lq1, diff_lk1=diff_lk1, diff_lq2=diff_lq2, diff_lk2=diff_lk2,
                   diff_subln=diff_subln, hy_conv_w=hy_conv_w, hy_conv_b=hy_conv_b, hy_w1=hy_w1, hy_b1=hy_b1,
                   hy_freq=hy_freq, hy_w2=hy_w2, hy_b2=hy_b2, hy_w3=hy_w3, hy_decay=hy_decay, hy_bias=hy_bias)
    Bc, Lc, _ = x_prompt.shape
    Bs, Ls, _ = x_sample.shape
    past = cache_k.shape[2]

    cond = jnp.zeros((8, D_MODEL), F32).at[0:Bs].set(c).at[Bs].set(c_ctx)
    mod_all = _modulation(cond, w_mod, b_mod)

    rope_tabs = _rope_tables(Ls)
    tabs_c = _hyena_tables(Lc)
    tabs_s = _hyena_tables(Ls)
    ck = cache_k.reshape(Bs, DEPTH, past, D_B)
    cv = cache_v.reshape(Bs, DEPTH, past, D_B)

    w_in_b, w_out_b, w_ff1_b, w_ff2_b = (w.astype(BF16) for w in (w_in, w_out, w_ff1, w_ff2))
    xp, xs = x_prompt, x_sample
    st_list, k_list, v_list = [], [], []
    for l in range(DEPTH):
        p = {k: stacked[k][l] for k in _LAYER_KEYS}
        wb = dict(w_in=(w_in_b, l), w_out=(w_out_b, l), w_ff1=(w_ff1_b, l), w_ff2=(w_ff2_b, l))
        lam_init = 0.8 - 0.6 * math.exp(-0.3 * l)
        mod_lat = mod_all[l, 0:Bs].reshape(Bs, 1, 6 * D_MODEL)
        mod_ctx = mod_all[l, Bs:Bs + 1].reshape(1, 1, 6 * D_MODEL)
        filt_c = _hyena_filters(Lc, p, tabs_c)
        filt_s = _hyena_filters(Ls, p, tabs_s)
        xp, (s_ctx, k_ctx, v_ctx) = _layer(xp, mod_ctx, p, wb, l, lam_init, None, None, tabs_c, filt_c)
        st_list.append(s_ctx)
        k_list.append(k_ctx)
        v_list.append(v_ctx)
        s0 = state_rwkv[:, l]
        xs, _ = _layer(xs, mod_lat, p, wb, l, lam_init, (s0, ck, cv), rope_tabs, tabs_s, filt_s)
    new_k = jnp.stack(k_list, axis=1).reshape(Bc, DEPTH, Lc, H_B, 2, DIFF_DH)
    new_v = jnp.stack(v_list, axis=1).reshape(Bc, DEPTH, Lc, H_B, 2 * DIFF_DH)
    return (xp, xs, jnp.stack(st_list, axis=1), new_k, new_v)
```

```python
import functools
import math

import jax
import jax.numpy as jnp
import numpy as np
from jax import lax
from jax.experimental import pallas as pl
from jax.experimental.pallas import tpu as pltpu

F32 = jnp.float32
BF16 = jnp.bfloat16

D_MODEL = 1024
DEPTH = 4
GRID_W = 64
D_A = 256
RW_HEAD = 64
H_A = 4
D_B = 512
DIFF_DH = 64
H_B = 4
D_C = 256
HY_BANDS = 16
HY_EMB = 33
HY_FFN = 64
D_FF = 4096
FFN_CHUNK = 1024
ROPE_BASE = 10000.0
RMS_EPS = 1e-6
GN_EPS = 64e-5
N_A_COLS = 1152
IN_COLS = 3456

TL = 256
V7X_VMEM_LIMIT = 56 * 1024 * 1024


def _cparams(n_axes, vmem=None):
    return pltpu.CompilerParams(dimension_semantics=("arbitrary",) * n_axes,
                                vmem_limit_bytes=vmem)


def _split3(a):
    hi = a.astype(BF16)
    r1 = a - hi.astype(F32)
    mid = r1.astype(BF16)
    lo = (r1 - mid.astype(F32)).astype(BF16)
    return hi, mid, lo


def _split2(a):
    hi = a.astype(BF16)
    lo = (a - hi.astype(F32)).astype(BF16)
    return hi, lo


def _segsum(a, bd):
    hi, lo = _split2(a)
    return jnp.dot(hi, bd, preferred_element_type=F32) + jnp.dot(lo, bd, preferred_element_type=F32)


def _block_diag(n, seg):
    r = lax.broadcasted_iota(jnp.int32, (n, n), 0) // seg
    c = lax.broadcasted_iota(jnp.int32, (n, n), 1) // seg
    return (r == c).astype(BF16)


class _LayerParams:
    def __init__(self, stacked, layer):
        self.stacked = stacked
        self.layer = layer

    def __getitem__(self, name):
        return self.stacked[name][self.layer]


def _pspec(p, name, shape):
    if isinstance(p, _LayerParams):
        stack = p.stacked[name]
        stack = stack.reshape((stack.shape[0],) + shape)
        layer = p.layer
        return pl.BlockSpec((None,) + shape, lambda *g: (layer,) + (0,) * len(shape)), stack
    return pl.BlockSpec(shape, lambda *g: (0,) * len(shape)), p[name].reshape(shape)


def _layer_weight_spec(w, n_grid_axes):
    stack, layer = w
    _, k, n = stack.shape
    if n_grid_axes == 2:
        return pl.BlockSpec((None, k, n), lambda b, i: (layer, 0, 0))
    raise NotImplementedError(n_grid_axes)


def _rms(x, g):
    return x * lax.rsqrt(jnp.mean(x * x, axis=-1, keepdims=True) + RMS_EPS) * g


def _mod_kernel(cond_ref, w_ref, b_ref, o_ref):
    c = cond_ref[...]
    s = (c * jax.nn.sigmoid(c)).astype(BF16)
    o_ref[0] = jnp.dot(s, w_ref[0].astype(BF16), preferred_element_type=F32) + b_ref[0]


def _modulation(cond, w_mod, b_mod):
    tn = 1536
    return pl.pallas_call(
        _mod_kernel,
        grid=(DEPTH, 6 * D_MODEL // tn),
        in_specs=[pl.BlockSpec((8, D_MODEL), lambda l, n: (0, 0)),
                  pl.BlockSpec((1, D_MODEL, tn), lambda l, n: (l, 0, n)),
                  pl.BlockSpec((1, 1, tn), lambda l, n: (l, 0, n))],
        out_specs=pl.BlockSpec((1, 8, tn), lambda l, n: (l, 0, n)),
        out_shape=jax.ShapeDtypeStruct((DEPTH, 8, 6 * D_MODEL), F32),
        compiler_params=_cparams(2, 40 * 1024 * 1024),
        name="modulation",
    )(cond, w_mod, b_mod.reshape(DEPTH, 1, 6 * D_MODEL))


def _rope(x, cos, sin_a, sin_b):
    n = x.shape[-1]
    return x * cos + pltpu.roll(x, n - 16, axis=1) * sin_a + pltpu.roll(x, 16, axis=1) * sin_b


def _shift_conv(u, prev_row, next_row, cw):
    n = u.shape[0]
    row = lax.broadcasted_iota(jnp.int32, (n, 1), 0)
    up = jnp.where(row == 0, prev_row, pltpu.roll(u, 1, axis=0))
    un = jnp.where(row == n - 1, next_row, pltpu.roll(u, n - 1, axis=0))
    return cw[0:1] * up + cw[1:2] * u + cw[2:3] * un


def _prep_body(ua, prev, nxt, uh, prevh, nxth,
               cw_ref, w0_ref, w2_ref, a0_ref, a2_ref, g2_ref, kkw_ref, kaw_ref, rk_ref, hcw_ref, hcb_ref,
               r_ref, v_ref, nkk_ref, wf_ref, kaf_ref, kdf_ref, wb_ref, kab_ref, kdb_ref, g_ref, bon_ref,
               x1_ref, x2_ref, hv_ref):
    bd = _block_diag(D_A, RW_HEAD)
    u_rkv = ua[:, 0:3 * D_A]
    rkv = _shift_conv(u_rkv, prev, nxt, cw_ref[...])
    r = rkv[:, 0:D_A]
    k = rkv[:, D_A:2 * D_A]
    v = rkv[:, 2 * D_A:3 * D_A]
    u_w = ua[:, 768:896]
    u_a = ua[:, 896:1024]
    u_g = ua[:, 1024:1152]

    g = jnp.dot(jax.nn.sigmoid(u_g).astype(BF16), g2_ref[...].astype(BF16), preferred_element_type=F32)
    kk = k * kkw_ref[...]
    kk = kk * lax.rsqrt(_segsum(kk * kk, bd) + 1e-12)
    r_ref[0] = r
    v_ref[0] = v
    nkk_ref[0] = -kk
    g_ref[0] = g

    kd_sum = None
    outs = ((wf_ref, kaf_ref, kdf_ref), (wb_ref, kab_ref, kdb_ref))
    for d in range(2):
        uw_d = u_w[:, d * 64:(d + 1) * 64]
        ua_d = u_a[:, d * 64:(d + 1) * 64]
        xw = w0_ref[d:d + 1, :] + jnp.dot(jnp.tanh(uw_d).astype(BF16), w2_ref[d].astype(BF16),
                                          preferred_element_type=F32)
        z = -xw
        softplus = jnp.maximum(z, 0.0) + jnp.log(1.0 + jnp.exp(-jnp.abs(z)))
        wl = -softplus - 0.5
        log_decay = -jnp.exp(wl)
        a = jax.nn.sigmoid(a0_ref[d:d + 1, :] + jnp.dot(ua_d.astype(BF16), a2_ref[d].astype(BF16),
                                                        preferred_element_type=F32))
        kd = k * (1.0 + (a - 1.0) * kaw_ref[...])
        w_o, ka_o, kd_o = outs[d]
        w_o[0] = log_decay
        ka_o[0] = kk * a
        kd_o[0] = kd
        kd_sum = kd if kd_sum is None else kd_sum + kd
    bon_ref[0] = _segsum(r * kd_sum * rk_ref[...], bd) * v

    hc = _shift_conv(uh, prevh, nxth, hcw_ref[...]) + hcb_ref[...]
    x1_ref[0] = hc[:, 0:D_C]
    x2_ref[0] = hc[:, D_C:2 * D_C]
    hv_ref[0] = hc[:, 2 * D_C:3 * D_C]


N_PREP_PARAMS = 11
N_PREP_OUTS = 14


def _inproj_kernel(*refs, rope):
    n_in = 6 + (3 if rope else 0)
    x_ref, xp_ref, xn_ref, mod_ref, g_ref, w_ref = refs[0:6]
    rope_refs = refs[6:n_in]
    prep_params = refs[n_in:n_in + N_PREP_PARAMS]
    uq_ref, uk_ref, uv_ref = refs[n_in + N_PREP_PARAMS:n_in + N_PREP_PARAMS + 3]
    prep_outs = refs[n_in + N_PREP_PARAMS + 3:]
    i = pl.program_id(1)
    nt = pl.num_programs(1)
    mod = mod_ref[0]
    sh1 = mod[:, 0:D_MODEL]
    sc1 = mod[:, D_MODEL:2 * D_MODEL]

    def normed(x):
        return (_rms(x, g_ref[...]) * (1.0 + sc1) + sh1).astype(BF16)

    u = jnp.dot(normed(x_ref[0]), w_ref[...], preferred_element_type=F32)
    q = u[:, N_A_COLS:N_A_COLS + D_B] * (DIFF_DH ** -0.5 * math.log2(math.e))
    k = u[:, N_A_COLS + D_B:N_A_COLS + 2 * D_B]
    if rope:
        cos_ref, sa_ref, sb_ref = rope_refs
        q = _rope(q, cos_ref[...], sa_ref[...], sb_ref[...])
        k = _rope(k, cos_ref[...], sa_ref[...], sb_ref[...])
    uq_ref[0] = q.astype(uq_ref.dtype)
    uk_ref[0] = k.astype(uk_ref.dtype)
    uv_ref[0] = u[:, N_A_COLS + 2 * D_B:N_A_COLS + 3 * D_B].astype(uv_ref.dtype)

    he = normed(jnp.concatenate([xp_ref[0], xn_ref[0]], axis=0))
    hy0 = N_A_COLS + 3 * D_B
    e_rkv = jnp.dot(he, w_ref[:, 0:3 * D_A], preferred_element_type=F32)
    e_hy = jnp.dot(he, w_ref[:, hy0:IN_COLS], preferred_element_type=F32)
    has_prev = i > 0
    has_next = i < nt - 1
    _prep_body(u[:, 0:N_A_COLS], jnp.where(has_prev, e_rkv[7:8], 0.0), jnp.where(has_next, e_rkv[8:9], 0.0),
               u[:, hy0:IN_COLS], jnp.where(has_prev, e_hy[7:8], 0.0), jnp.where(has_next, e_hy[8:9], 0.0),
               *prep_params, *prep_outs)


def _inproj(x, mod, w_in, rope_tabs, p):
    B, L, _ = x.shape
    rope = rope_tabs is not None
    mb = mod.shape[0]
    nb8 = L // 8
    t8 = TL // 8
    row = lambda b, i: (b, i, 0)
    prev = lambda b, i: (b, jnp.maximum(i * t8 - 1, 0), 0)
    nxt = lambda b, i: (b, jnp.minimum((i + 1) * t8, nb8 - 1), 0)
    const2 = lambda b, i: (0, 0)

    params = [_pspec(p, 'rwkv_conv', (3, 3 * D_A)), _pspec(p, 'rwkv_w0', (2, D_A)),
              _pspec(p, 'rwkv_w2', (2, 64, D_A)), _pspec(p, 'rwkv_a0', (2, D_A)), _pspec(p, 'rwkv_a2', (2, 64, D_A)),
              _pspec(p, 'rwkv_g2', (128, D_A)), _pspec(p, 'rwkv_kk', (1, D_A)), _pspec(p, 'rwkv_ka', (1, D_A)),
              _pspec(p, 'rwkv_rk', (1, D_A)), _pspec(p, 'hy_conv_w', (3, 3 * D_C)),
              _pspec(p, 'hy_conv_b', (1, 3 * D_C))]
    assert len(params) == N_PREP_PARAMS
    g_spec, g_arg = _pspec(p, 'g_mix_pre', (1, D_MODEL))
    in_specs = [pl.BlockSpec((1, TL, D_MODEL), row), pl.BlockSpec((1, 8, D_MODEL), prev),
                pl.BlockSpec((1, 8, D_MODEL), nxt),
                pl.BlockSpec((1, 1, 6 * D_MODEL), (lambda b, i: (b, 0, 0)) if mb > 1 else (lambda b, i: (0, 0, 0))),
                g_spec,
                _layer_weight_spec(w_in, 2)]
    args = [x, x, x, mod, g_arg, w_in[0]]
    if rope:
        in_specs += [pl.BlockSpec((TL, D_B), lambda b, i: (i, 0))] * 3
        args += list(rope_tabs)
    in_specs += [s for s, _ in params]
    args += [a for _, a in params]
    qkv_dt = BF16 if rope else F32
    out_shape = ([jax.ShapeDtypeStruct((B, L, D_B), qkv_dt)] * 3
                 + [jax.ShapeDtypeStruct((B, L, D_A), F32)] * N_PREP_OUTS)
    out_specs = [pl.BlockSpec((1, TL, D_B), row)] * 3 + [pl.BlockSpec((1, TL, D_A), row)] * N_PREP_OUTS
    return pl.pallas_call(
        functools.partial(_inproj_kernel, rope=rope),
        grid=(B, L // TL), in_specs=in_specs, out_specs=out_specs, out_shape=out_shape,
        compiler_params=_cparams(2, 48 * 1024 * 1024), name="inproj",
    )(*args)


CHUNK = 64
CHUNKS_PER_ITER = 4
_NN = (((1,), (0,)), ((), ()))
_NT = (((1,), (1,)), ((), ()))
_TN = (((0,), (0,)), ((), ()))


def _mm3(a, b, dims):
    ah, al = _split2(a)
    bh, bl = _split2(b)
    f = lambda x, y: lax.dot_general(x, y, dims, preferred_element_type=F32)
    if dims is _TN:
        return f(ah, bh) + f(ah, bl) + f(al, bh)
    m = a.shape[0]
    both = f(jnp.concatenate([ah, al], axis=0), bh)
    return both[0:m] + both[m:2 * m] + f(ah, bl)


def _mm2(a, b, dims):
    ah = a.astype(BF16)
    bh, bl = _split2(b)
    n = b.shape[1]
    both = lax.dot_general(ah, jnp.concatenate([bh, bl], axis=1), dims, preferred_element_type=F32)
    return both[:, 0:n] + both[:, n:2 * n]


def _mm1(a, b, dims):
    return lax.dot_general(a.astype(BF16), b.astype(BF16), dims, preferred_element_type=F32)


def _unit_tri_inverses(ms, row, col):
    eye = (row == col).astype(F32)
    blk = lambda n: (row // n) == (col // n)
    m8 = [jnp.where(blk(8), m, 0.0) for m in ms]
    a2 = [_mm1(x, x, _NN) for x in m8]
    a4 = [_mm1(x, x, _NN) for x in a2]
    p = [eye + x + y + _mm1(x, y, _NN) for x, y in zip(m8, a2)]
    ts = [x + _mm1(x, y, _NN) for x, y in zip(p, a4)]
    n = 8
    while n < CHUNK:
        sel = jnp.logical_and(blk(2 * n), jnp.logical_not(blk(n)))
        ot = [_mm1(jnp.where(sel, m, 0.0), t, _NN) for m, t in zip(ms, ts)]
        ts = [t + _mm1(t, o, _NN) for t, o in zip(ts, ot)]
        n *= 2
    return ts


def _scan_kernel(nkkf_ref, rf_ref, vf_ref, lwf_ref, kaf_ref, kdf_ref,
                 nkkb_ref, rb_ref, vb_ref, lwb_ref, kab_ref, kdb_ref, s0_ref,
                 yf_ref, yb_ref, sfin_ref, s_ref):
    C = CHUNK
    j = pl.program_id(1)
    nj = pl.num_programs(1)
    row = lax.broadcasted_iota(jnp.int32, (C, C), 0)
    col = lax.broadcasted_iota(jnp.int32, (C, C), 1)
    dirs = ((nkkf_ref, rf_ref, vf_ref, lwf_ref, kaf_ref, kdf_ref, yf_ref),
            (nkkb_ref, rb_ref, vb_ref, lwb_ref, kab_ref, kdb_ref, yb_ref))
    n_chunks = nkkf_ref.shape[1] // C

    @pl.when(j == 0)
    def _():
        s_ref[...] = s0_ref[0]

    def build(c, d, probs):
        rows = pl.ds(pl.multiple_of(c * C, C), C)
        nkk_ref, r_ref, v_ref, lw_ref, b_ref, k_ref = dirs[d][0:6]
        alpha = nkk_ref[0, rows, :]
        r = r_ref[0, rows, :]
        v = v_ref[0, rows, :]
        before = (row > col) if d == 0 else (row < col)
        upto = (row >= col) if d == 0 else (row <= col)
        lw = lw_ref[0, rows, :]
        beta = b_ref[0, rows, :]
        kd = k_ref[0, rows, :]
        tri = upto.astype(BF16)
        l_hi, l_mid, l_lo = _split3(lw)
        lam = (jnp.dot(tri, l_hi, preferred_element_type=F32) + jnp.dot(tri, l_mid, preferred_element_type=F32)
               + jnp.dot(tri, l_lo, preferred_element_type=F32))
        lam_prev = lam - lw
        lam_mid = lam[C // 2:C // 2 + 1, :]
        lam_end = lam[C - 1:C, :] if d == 0 else lam[0:1, :]
        e_to_mid = jnp.exp(lam_mid - lam)
        a_m = alpha * jnp.exp(lam_prev - lam_mid)
        b_m = beta * e_to_mid
        k_m = kd * e_to_mid
        r_m = r * jnp.exp(lam - lam_mid)
        a_0 = alpha * jnp.exp(lam_prev)
        r_0 = r * jnp.exp(lam)
        e_to_end = jnp.exp(lam_end - lam)
        b_e = beta * e_to_end
        k_e = kd * e_to_end
        p_end = jnp.exp(lam_end)
        for h in range(H_A):
            sl = slice(h * RW_HEAD, (h + 1) * RW_HEAD)
            probs.append(dict(
                rows=rows, d=d, h=h, before=before, upto=upto, pc=p_end[:, sl],
                ar=jnp.concatenate([a_m[:, sl], r_m[:, sl]], axis=0),
                bk=jnp.concatenate([b_m[:, sl], k_m[:, sl]], axis=0),
                a0=a_0[:, sl], r0=r_0[:, sl], vh=v[:, sl],
                bke=jnp.concatenate([b_e[:, sl], k_e[:, sl]], axis=0)))

    def body(ci, carry):
        probs = []
        for cc in range(CHUNKS_PER_ITER):
            cf = ci * CHUNKS_PER_ITER + cc
            build(cf, 0, probs)
            build(n_chunks - 1 - cf, 1, probs)
        g12 = [_mm1(q['ar'], q['bk'], _NT) for q in probs]
        ms = [jnp.where(q['before'], g[0:C, 0:C], 0.0) for q, g in zip(probs, g12)]
        ns = [jnp.where(q['before'], g[0:C, C:2 * C], 0.0) for q, g in zip(probs, g12)]
        rbk = [jnp.concatenate([jnp.where(q['upto'], g[C:2 * C, 0:C], 0.0),
                                jnp.where(q['upto'], g[C:2 * C, C:2 * C], 0.0)], axis=1) for q, g in zip(probs, g12)]
        nvs = [_mm1(n, q['vh'], _NN) for q, n in zip(probs, ns)]
        ts = _unit_tri_inverses(ms, row, col)
        xs = [_mm2(t, jnp.concatenate([q['a0'], nv], axis=1), _NN) for q, t, nv in zip(probs, ts, nvs)]
        zeros = jnp.zeros((C, RW_HEAD), F32)
        ws = [jnp.concatenate([x, jnp.concatenate([zeros, q['vh']], axis=1)], axis=0) for q, x in zip(probs, xs)]
        rys = [_mm1(rb, w, _NN) for rb, w in zip(rbk, ws)]
        fgs = [_mm3(w, q['bke'], _TN) for q, w in zip(probs, ws)]
        ys = []
        for q, ry, fg in zip(probs, rys, fgs):
            s = s_ref[q['d'], q['h']]
            rp = q['r0'] + ry[:, 0:RW_HEAD]
            ys.append(ry[:, RW_HEAD:2 * RW_HEAD] + _mm1(rp, s, _NT))
            s_ref[q['d'], q['h']] = s * q['pc'] + _mm1(s, fg[0:RW_HEAD], _NN) + fg[RW_HEAD:2 * RW_HEAD]
        for k0 in range(0, len(probs), H_A):
            q0 = probs[k0]
            dirs[q0['d']][6][0, q0['rows'], :] = jnp.concatenate(ys[k0:k0 + H_A], axis=1)
        return carry

    lax.fori_loop(0, n_chunks // CHUNKS_PER_ITER, body, 0)

    @pl.when(j == nj - 1)
    def _():
        sfin_ref[0] = s_ref[...]


def _scan(nkk, r, v, lwf, kaf, kdf, lwb, kab, kdb, s0):
    B, L, _ = r.shape
    nj = L // TL
    tile = pl.BlockSpec((1, TL, D_A), lambda b, j: (b, j, 0))
    rtile = pl.BlockSpec((1, TL, D_A), lambda b, j: (b, nj - 1 - j, 0))
    st = pl.BlockSpec((1, 2, H_A, RW_HEAD, RW_HEAD), lambda b, j: (b, 0, 0, 0, 0))
    f32 = lambda *s: jax.ShapeDtypeStruct(s, F32)
    return pl.pallas_call(
        _scan_kernel, grid=(B, nj),
        in_specs=[tile] * 6 + [rtile] * 6 + [st],
        out_specs=[tile, rtile, st],
        out_shape=[f32(B, L, D_A), f32(B, L, D_A), f32(B, 2, H_A, RW_HEAD, RW_HEAD)],
        scratch_shapes=[pltpu.VMEM((2, H_A, RW_HEAD, RW_HEAD), F32)],
        compiler_params=_cparams(2, 40 * 1024 * 1024), name="rwkv_scan",
    )(nkk, r, v, lwf, kaf, kdf, nkk, r, v, lwb, kab, kdb, s0)


def _row_sum_bf16(e):
    n = e.shape[1] // 128
    acc = None
    for g0 in range(0, n, 4):
        part = e[:, g0 * 128:(g0 + 1) * 128]
        for j in range(g0 + 1, min(g0 + 4, n)):
            part = part + e[:, j * 128:(j + 1) * 128]
        pf = part.astype(F32)
        acc = pf if acc is None else acc + pf
    return jnp.sum(acc, axis=-1, keepdims=True)


def _attn_kernel(*refs, has_cache, lam_init):
    if has_cache:
        q_ref, k_ref, v_ref, ck_ref, cv_ref, lq1, lk1, lq2, lk2, sub_ref, o_ref = refs
    else:
        q_ref, k_ref, v_ref, lq1, lk1, lq2, lk2, sub_ref, o_ref = refs
    l1 = jnp.sum(lq1[...] * lk1[...], axis=-1, keepdims=True)
    l2 = jnp.sum(lq2[...] * lk2[...], axis=-1, keepdims=True)
    lam = jnp.exp(l1) - jnp.exp(l2) + lam_init
    q = q_ref[0].astype(BF16)
    kn = k_ref[0].astype(BF16)
    vn = v_ref[0].astype(BF16)
    if has_cache:
        kc = ck_ref[0, 0].astype(BF16)
        vc = cv_ref[0, 0].astype(BF16)
    dn = (((1,), (1,)), ((), ()))
    def scores(j):
        c0 = j * DIFF_DH
        qh = q[:, c0:c0 + DIFF_DH]
        s_n = lax.dot_general(qh, kn[:, c0:c0 + DIFF_DH], dn, preferred_element_type=F32)
        s_c = lax.dot_general(qh, kc[:, c0:c0 + DIFF_DH], dn, preferred_element_type=F32) if has_cache else None
        return s_n, s_c

    nxt = scores(0)
    for h in range(H_B):
        o = None
        for m in range(2):
            j = h * 2 + m
            s_n, s_c = nxt
            if j + 1 < 2 * H_B:
                nxt = scores(j + 1)
            mx = jnp.max(s_n, axis=-1, keepdims=True)
            if has_cache:
                mx = jnp.maximum(mx, jnp.max(s_c, axis=-1, keepdims=True))
            e_n = jnp.exp2((s_n - mx).astype(BF16))
            den = _row_sum_bf16(e_n)
            if has_cache:
                e_c = jnp.exp2((s_c - mx).astype(BF16))
                den = den + _row_sum_bf16(e_c)
            o_m = jnp.dot(e_n, vn[:, h * 128:(h + 1) * 128], preferred_element_type=F32)
            if has_cache:
                o_m = o_m + jnp.dot(e_c, vc[:, h * 128:(h + 1) * 128], preferred_element_type=F32)
            o = o_m / den if m == 0 else o - o_m * (lam / den)
        o_ref[0, :, h * 128:(h + 1) * 128] = _rms(o, sub_ref[...]) * (1.0 - lam_init)


def _attention(q, k, v, cache, l, p, lam_init):
    B, L, _ = q.shape
    has_cache = cache is not None
    tq = TL
    in_specs = [pl.BlockSpec((1, tq, D_B), lambda b, i: (b, i, 0)),
                pl.BlockSpec((1, L, D_B), lambda b, i: (b, 0, 0)),
                pl.BlockSpec((1, L, D_B), lambda b, i: (b, 0, 0))]
    args = [q, k, v]
    if has_cache:
        ck, cv = cache
        past = ck.shape[2]
        in_specs += [pl.BlockSpec((1, 1, past, D_B), lambda b, i: (b, l, 0, 0))] * 2
        args += [ck, cv]
    small = [_pspec(p, 'diff_lq1', (1, DIFF_DH)), _pspec(p, 'diff_lk1', (1, DIFF_DH)),
             _pspec(p, 'diff_lq2', (1, DIFF_DH)), _pspec(p, 'diff_lk2', (1, DIFF_DH)),
             _pspec(p, 'diff_subln', (1, 2 * DIFF_DH))]
    in_specs += [sp for sp, _ in small]
    args += [a for _, a in small]
    return pl.pallas_call(
        functools.partial(_attn_kernel, has_cache=has_cache, lam_init=lam_init),
        grid=(B, L // tq), in_specs=in_specs,
        out_specs=pl.BlockSpec((1, tq, D_B), lambda b, i: (b, i, 0)),
        out_shape=jax.ShapeDtypeStruct((B, L, D_B), F32),
        compiler_params=_cparams(2, 48 * 1024 * 1024), name="diff_attention",
    )(*args)


def _hfilt_time_kernel(emb_ref, w1_ref, b1_ref, fr_ref, w2_ref, b2_ref, w3_ref, dec_ref, h_ref, ss_ref, acc_ref):
    i = pl.program_id(0)
    emb = emb_ref[...]
    fr = fr_ref[...]
    h = jnp.sin(fr * (jnp.dot(emb.astype(BF16), w1_ref[...].astype(BF16), preferred_element_type=F32) + b1_ref[...]))
    h = jnp.sin(fr * (jnp.dot(h.astype(BF16), w2_ref[...].astype(BF16), preferred_element_type=F32) + b2_ref[...]))
    h = jnp.dot(h.astype(BF16), w3_ref[...].astype(BF16), preferred_element_type=F32)
    h = h * jnp.exp(-emb[:, 0:1] * jnp.abs(dec_ref[...]))
    n = h.shape[0]
    row = lax.broadcasted_iota(jnp.int32, (n, 4 * D_C), 0) + i * n
    col = lax.broadcasted_iota(jnp.int32, (n, 4 * D_C), 1)
    is_bwd = ((col // D_C) % 2) == 1
    h = jnp.where(jnp.logical_and(is_bwd, row == 0), 0.0, h)
    h_ref[...] = h

    @pl.when(i == 0)
    def _():
        acc_ref[...] = jnp.zeros_like(acc_ref)

    acc_ref[...] += jnp.sum(h * h, axis=0, keepdims=True)
    ss_ref[...] = acc_ref[...]


def _hfilt_freq_kernel(h_ref, ss_ref, c_ref, s_ref, hr_ref, hi_ref, a_ref, b_ref, nyq_ref):
    i = pl.program_id(0)

    @pl.when(i == 0)
    def _():
        for o in range(2):
            hf = h_ref[:, o * 2 * D_C:o * 2 * D_C + D_C]
            hb = h_ref[:, o * 2 * D_C + D_C:(o + 1) * 2 * D_C]
            a = hf + hb
            n = a.shape[0]
            alt = 1.0 - 2.0 * (lax.broadcasted_iota(jnp.int32, (n, 1), 0) % 2).astype(F32)
            nyq_ref[:, o * D_C:(o + 1) * D_C] = jnp.sum(a * alt, axis=0, keepdims=True)
            a_ref[:, o * D_C:(o + 1) * D_C] = a.astype(BF16)
            b_ref[:, o * D_C:(o + 1) * D_C] = (hb - hf).astype(BF16)

    ss = ss_ref[...]
    tot = jnp.concatenate([ss[:, 0:D_C] + ss[:, D_C:2 * D_C], ss[:, 2 * D_C:3 * D_C] + ss[:, 3 * D_C:4 * D_C]], axis=1)
    scale = lax.rsqrt(tot + 1e-6)

    hr_ref[...] = jnp.dot(c_ref[...], a_ref[...], preferred_element_type=F32) * scale
    hi = jnp.dot(s_ref[...], b_ref[...], preferred_element_type=F32) * scale
    tf = hi.shape[0]
    row = lax.broadcasted_iota(jnp.int32, (tf, 1), 0) + i * tf
    hi_ref[...] = jnp.where(row == 0, nyq_ref[...] * scale, hi)


def _hyena_filters(L, p, tabs):
    emb, cmat, smat = tabs
    tr = 256
    params = [_pspec(p, 'hy_w1_pad', (128, HY_FFN)), _pspec(p, 'hy_b1', (1, HY_FFN)), _pspec(p, 'hy_freq', (1, HY_FFN)),
              _pspec(p, 'hy_w2', (HY_FFN, HY_FFN)), _pspec(p, 'hy_b2', (1, HY_FFN)),
              _pspec(p, 'hy_w3', (HY_FFN, 4 * D_C)), _pspec(p, 'hy_decay4', (1, 4 * D_C))]
    h_raw, ss = pl.pallas_call(
        _hfilt_time_kernel, grid=(L // tr,),
        in_specs=[pl.BlockSpec((tr, 128), lambda i: (i, 0))] + [sp for sp, _ in params],
        out_specs=[pl.BlockSpec((tr, 4 * D_C), lambda i: (i, 0)), pl.BlockSpec((1, 4 * D_C), lambda i: (0, 0))],
        out_shape=[jax.ShapeDtypeStruct((L, 4 * D_C), F32), jax.ShapeDtypeStruct((1, 4 * D_C), F32)],
        scratch_shapes=[pltpu.VMEM((1, 4 * D_C), F32)],
        compiler_params=_cparams(1), name="hyena_filter_time",
    )(emb, *[a for _, a in params])
    tf = 256
    mat = pl.BlockSpec((tf, L), lambda i: (i, 0))
    return pl.pallas_call(
        _hfilt_freq_kernel, grid=(L // tf,),
        in_specs=[pl.BlockSpec((L, 4 * D_C), lambda i: (0, 0)), pl.BlockSpec((1, 4 * D_C), lambda i: (0, 0)),
                  mat, mat],
        out_specs=[pl.BlockSpec((tf, 2 * D_C), lambda i: (i, 0))] * 2,
        out_shape=[jax.ShapeDtypeStruct((L, 2 * D_C), F32)] * 2,
        scratch_shapes=[pltpu.VMEM((L, 2 * D_C), BF16)] * 2 + [pltpu.VMEM((1, 2 * D_C), F32)],
        compiler_params=_cparams(1, 48 * 1024 * 1024), name="hyena_filter_freq",
    )(h_raw, ss, cmat, smat)


def _hconv_kernel(z_ref, zt_ref, gate_ref, hr_ref, hi_ref, bias_ref, c_ref, s_ref, o_ref,
                  zb_ref, yr_ref, yi_ref, nyq_ref, *, BG, L):
    ph = pl.program_id(1)
    i = pl.program_id(2)
    inv_n = 1.0 / (2 * L)

    @pl.when(jnp.logical_and(ph == 0, i == 0))
    def _():
        alt = 1.0 - 2.0 * (lax.broadcasted_iota(jnp.int32, (L, 1), 0) % 2).astype(F32)
        for b in range(BG):
            z = z_ref[b]
            zb_ref[:, b * D_C:(b + 1) * D_C] = z.astype(BF16)
            nyq_ref[:, b * D_C:(b + 1) * D_C] = jnp.sum(z * alt, axis=0, keepdims=True)

    tf = c_ref.shape[0]

    @pl.when(ph == 0)
    def _():
        zr = jnp.dot(c_ref[...], zb_ref[...], preferred_element_type=F32)
        zs = jnp.dot(s_ref[...], zb_ref[...], preferred_element_type=F32)
        hr = hr_ref[...]
        hi = hi_ref[...]
        row = lax.broadcasted_iota(jnp.int32, (tf, 1), 0) + i * tf
        wgt = jnp.where(row == 0, inv_n, 2.0 * inv_n)
        rows = pl.ds(pl.multiple_of(i * tf, tf), tf)
        for b in range(BG):
            sl = slice(b * D_C, (b + 1) * D_C)
            yr_ref[rows, sl] = ((zr[:, sl] * hr + zs[:, sl] * hi) * wgt).astype(BF16)
            yi_ref[rows, sl] = ((zr[:, sl] * hi - zs[:, sl] * hr) * (2.0 * inv_n)).astype(BF16)

        @pl.when(i == 0)
        def _():
            for b in range(BG):
                sl = slice(b * D_C, (b + 1) * D_C)
                nyq_ref[:, sl] = nyq_ref[:, sl] * hi[0:1, :] * inv_n

    @pl.when(ph == 1)
    def _():
        y = (jnp.dot(c_ref[...], yr_ref[...], preferred_element_type=F32)
             - jnp.dot(s_ref[...], yi_ref[...], preferred_element_type=F32))
        row = lax.broadcasted_iota(jnp.int32, (tf, 1), 0) + i * tf
        alt = 1.0 - 2.0 * (row % 2).astype(F32)
        y = y + alt * nyq_ref[...]
        for b in range(BG):
            sl = slice(b * D_C, (b + 1) * D_C)
            o_ref[b] = gate_ref[b] * (y[:, sl] + bias_ref[...] * zt_ref[b])


def _hconv(z, gate, hr, hi, p, order, tabs):
    B, L, _ = z.shape
    if isinstance(p, _LayerParams):
        layer = p.layer
        bias_arg = p.stacked['hy_bias'].reshape(DEPTH, 2, 1, D_C)
        bias_spec = pl.BlockSpec((None, None, 1, D_C), lambda g, ph, i: (layer, order, 0, 0))
    else:
        bias_arg = p['hy_bias'][order].reshape(1, D_C)
        bias_spec = pl.BlockSpec((1, D_C), lambda g, ph, i: (0, 0))
    _, ch, sh = tabs
    BG =min(B, 8) if L <= 256 else min(B, 4)
    assert B % BG == 0
    tf = 256
    nt = L // tf
    mat = pl.BlockSpec((tf, L), lambda g, ph, i: (i, 0))
    tile = pl.BlockSpec((BG, tf, D_C), lambda g, ph, i: (g, i * ph, 0))
    hspec = pl.BlockSpec((tf, D_C), lambda g, ph, i: (i * (1 - ph), order))
    cols = BG * D_C
    return pl.pallas_call(
        functools.partial(_hconv_kernel, BG=BG, L=L), grid=(B // BG, 2, nt),
        in_specs=[pl.BlockSpec((BG, L, D_C), lambda g, ph, i: (g, 0, 0)), tile, tile, hspec, hspec,
                  bias_spec, mat, mat],
        out_specs=tile,
        out_shape=jax.ShapeDtypeStruct((B, L, D_C), F32),
        scratch_shapes=[pltpu.VMEM((L, cols), BF16)] * 3 + [pltpu.VMEM((1, cols), F32)],
        compiler_params=_cparams(3, 48 * 1024 * 1024), name="hyena_conv",
    )(z, z, gate, hr, hi, bias_arg, ch, sh)


def _outffn_kernel(x_ref, yf_ref, yb_ref, bon_ref, g_ref, ob_ref, yc_ref, mod_ref,
                   lnw_ref, lnb_ref, gpost_ref, gffn_ref, gffn_post_ref, w_ref, w1_ref, w2_ref, o_ref):
    bd = _block_diag(D_A, RW_HEAD)
    y = yf_ref[0] + yb_ref[0]
    mu = _segsum(y, bd) * (1.0 / RW_HEAD)
    yc = y - mu
    var = _segsum(yc * yc, bd) * (1.0 / RW_HEAD)
    yn = yc * lax.rsqrt(var + GN_EPS) * lnw_ref[...] + lnb_ref[...]
    ya = (yn + bon_ref[0]) * g_ref[0]
    mix = (jnp.dot(ya.astype(BF16), w_ref[0:D_A, :], preferred_element_type=F32)
           + jnp.dot(ob_ref[0].astype(BF16), w_ref[D_A:D_A + D_B, :], preferred_element_type=F32)
           + jnp.dot(yc_ref[0].astype(BF16), w_ref[D_A + D_B:D_MODEL, :], preferred_element_type=F32))
    mod = mod_ref[0]
    gt1 = mod[:, 2 * D_MODEL:3 * D_MODEL]
    sh2 = mod[:, 3 * D_MODEL:4 * D_MODEL]
    sc2 = mod[:, 4 * D_MODEL:5 * D_MODEL]
    gt2 = mod[:, 5 * D_MODEL:6 * D_MODEL]
    x1 = x_ref[0] + gt1 * _rms(mix, gpost_ref[...])
    h = (_rms(x1, gffn_ref[...]) * (1.0 + sc2) + sh2).astype(BF16)
    acc = None
    for c in range(D_FF // FFN_CHUNK):
        a = jnp.dot(h, w1_ref[:, c * FFN_CHUNK:(c + 1) * FFN_CHUNK], preferred_element_type=F32)
        a = jnp.square(jnp.maximum(a, 0.0)).astype(BF16)
        part = jnp.dot(a, w2_ref[c * FFN_CHUNK:(c + 1) * FFN_CHUNK, :], preferred_element_type=F32)
        acc = part if acc is None else acc + part
    o_ref[0] = x1 + gt2 * _rms(acc, gffn_post_ref[...])


def _outffn(x, yf, yb, bon, g, ob, yc, mod, p, w_out, w1, w2):
    B, L, _ = x.shape
    mb = mod.shape[0]
    row = lambda b, i: (b, i, 0)
    c2 = lambda b, i: (0, 0)
    t = lambda n: pl.BlockSpec((1, TL, n), row)
    vecs = [_pspec(p, 'rwkv_ln_w', (1, D_A)), _pspec(p, 'rwkv_ln_b', (1, D_A)), _pspec(p, 'g_mix_post', (1, D_MODEL)),
            _pspec(p, 'g_ffn_pre', (1, D_MODEL)), _pspec(p, 'g_ffn_post', (1, D_MODEL))]
    in_specs = [t(D_MODEL), t(D_A), t(D_A), t(D_A), t(D_A), t(D_B), t(D_C),
                pl.BlockSpec((1, 1, 6 * D_MODEL), (lambda b, i: (b, 0, 0)) if mb > 1 else (lambda b, i: (0, 0, 0)))]
    in_specs += [sp for sp, _ in vecs]
    in_specs += [_layer_weight_spec(w_out, 2), _layer_weight_spec(w1, 2), _layer_weight_spec(w2, 2)]
    return pl.pallas_call(
        _outffn_kernel, grid=(B, L // TL), in_specs=in_specs,
        out_specs=t(D_MODEL), out_shape=jax.ShapeDtypeStruct((B, L, D_MODEL), F32),
        compiler_params=_cparams(2, V7X_VMEM_LIMIT), name="outproj_ffn",
    )(x, yf, yb, bon, g, ob, yc, mod, *[a for _, a in vecs], w_out[0], w1[0], w2[0])


@functools.lru_cache(maxsize=None)
def _rope_tables(L):
    rows = L // GRID_W
    row = np.repeat(np.arange(rows), GRID_W).astype(np.float64)
    col = np.tile(np.arange(GRID_W), rows).astype(np.float64)
    half = DIFF_DH // 2
    inv = ROPE_BASE ** (-np.arange(0, half, 2, dtype=np.float64) / half)
    ang_r = row[:, None] * inv[None]
    ang_c = col[:, None] * inv[None]
    ang = np.concatenate([ang_r, ang_r, ang_c, ang_c], axis=-1)
    cos, sin = np.cos(ang), np.sin(ang)
    quarter = (np.arange(DIFF_DH) // 16) % 2
    sin_a = np.where(quarter == 0, -sin, 0.0)
    sin_b = np.where(quarter == 1, sin, 0.0)
    rep = D_B // DIFF_DH
    return tuple(np.tile(t, (1, rep)).astype(np.float32) for t in (cos, sin_a, sin_b))


@functools.lru_cache(maxsize=None)
def _hyena_tables(L):
    t = np.linspace(0.0, 1.0, L, dtype=np.float32).astype(np.float64)[:, None]
    ang = (2.0 * math.pi / L) * np.arange(L, dtype=np.float64)[:, None]
    bands = np.linspace(1e-4, HY_BANDS - 1, HY_BANDS, dtype=np.float32).astype(np.float64)[None, :]
    emb = np.concatenate([t, np.cos(bands * ang), -np.sin(bands * ang)], axis=-1)
    emb = np.pad(emb, ((0, 0), (0, 128 - HY_EMB))).astype(np.float32)
    n = np.arange(L, dtype=np.int64)
    theta = ((n[:, None] * n[None, :]) % (2 * L)).astype(np.float64) * (math.pi / L)
    return emb, np.cos(theta).astype(BF16), np.sin(theta).astype(BF16)


def _layer(x, mod, p, wb, l, lam_init, cache, rope_tabs, hy_tabs, filt):
    B, L, _ = x.shape
    (uq, uk, uv, r, v, nkk, wf, kaf, kdf, wbk, kab, kdb, g, bon, x1h, x2h, hv) = _inproj(
        x, mod, wb['w_in'], rope_tabs, p)
    if cache is None:
        s0 = jnp.zeros((B, 2, H_A, RW_HEAD, RW_HEAD), F32)
        kv_cache = None
    else:
        s0 = cache[0]
        kv_cache = (cache[1], cache[2])
    yf, yb, sfin = _scan(nkk, r, v, wf, kaf, kdf, wbk, kab, kdb, s0)
    ob = _attention(uq, uk, uv, kv_cache, l, p, lam_init)
    hr, hi = filt
    z1 = _hconv(hv, x1h, hr, hi, p, 0, hy_tabs)
    yc = _hconv(z1, x2h, hr, hi, p, 1, hy_tabs)
    x2 = _outffn(x, yf, yb, bon, g, ob, yc, mod, p, wb['w_out'], wb['w_ff1'], wb['w_ff2'])
    return x2, (sfin, uk, uv)


def _derive_params(d):
    w1 = d['hy_w1']
    d['hy_w1_pad'] = jnp.pad(w1, [(0, 0)] * (w1.ndim - 2) + [(0, 128 - HY_EMB), (0, 0)])
    dec = d['hy_decay']
    d['hy_decay4'] = jnp.tile(dec, (1,) * (dec.ndim - 1) + (4,))
    return d


_LAYER_KEYS = ('g_mix_pre', 'g_mix_post', 'g_ffn_pre', 'g_ffn_post', 'rwkv_conv', 'rwkv_w0', 'rwkv_w2', 'rwkv_a0',
               'rwkv_a2', 'rwkv_g2', 'rwkv_kk', 'rwkv_ka', 'rwkv_rk', 'rwkv_ln_w', 'rwkv_ln_b', 'diff_lq1', 'diff_lk1',
               'diff_lq2', 'diff_lk2', 'diff_subln', 'hy_conv_w', 'hy_conv_b', 'hy_w1', 'hy_b1', 'hy_freq', 'hy_w2',
               'hy_b2', 'hy_w3', 'hy_decay', 'hy_bias')


def kernel(x_prompt, x_sample, state_rwkv, cache_k, cache_v, c, c_ctx, w_mod, b_mod, g_mix_pre, g_mix_post, g_ffn_pre, g_ffn_post, w_in, rwkv_conv, rwkv_w0, rwkv_w2, rwkv_a0, rwkv_a2, rwkv_g2, rwkv_kk, rwkv_ka, rwkv_rk, rwkv_ln_w, rwkv_ln_b, diff_lq1, diff_lk1, diff_lq2, diff_lk2, diff_subln, hy_conv_w, hy_conv_b, hy_w1, hy_b1, hy_freq, hy_w2, hy_b2, hy_w3, hy_decay, hy_bias, w_out, w_ff1, w_ff2):
    stacked = dict(g_mix_pre=g_mix_pre, g_mix_post=g_mix_post, g_ffn_pre=g_ffn_pre, g_ffn_post=g_ffn_post,
                   rwkv_conv=rwkv_conv, rwkv_w0=rwkv_w0, rwkv_w2=rwkv_w2, rwkv_a0=rwkv_a0, rwkv_a2=rwkv_a2,
                   rwkv_g2=rwkv_g2, rwkv_kk=rwkv_kk, rwkv_ka=rwkv_ka, rwkv_rk=rwkv_rk, rwkv_ln_w=rwkv_ln_w,
                   rwkv_ln_b=rwkv_ln_b, diff_lq1=diff_lq1, diff_lk1=diff_lk1, diff_lq2=diff_lq2, diff_lk2=diff_lk2,
                   diff_subln=diff_subln, hy_conv_w=hy_conv_w, hy_conv_b=hy_conv_b, hy_w1=hy_w1, hy_b1=hy_b1,
                   hy_freq=hy_freq, hy_w2=hy_w2, hy_b2=hy_b2, hy_w3=hy_w3, hy_decay=hy_decay, hy_bias=hy_bias)
    _derive_params(stacked)
    Bc, Lc, _ = x_prompt.shape
    Bs, Ls, _ = x_sample.shape
    past = cache_k.shape[2]

    cond = jnp.zeros((8, D_MODEL), F32).at[0:Bs].set(c).at[Bs].set(c_ctx)
    mod_all = _modulation(cond, w_mod, b_mod)

    rope_tabs = _rope_tables(Ls)
    tabs_c = _hyena_tables(Lc)
    tabs_s = _hyena_tables(Ls)
    ck = cache_k.reshape(Bs, DEPTH, past, D_B)
    cv = cache_v.reshape(Bs, DEPTH, past, D_B)

    w_in_b, w_out_b, w_ff1_b, w_ff2_b = (w.astype(BF16) for w in (w_in, w_out, w_ff1, w_ff2))
    xp, xs = x_prompt, x_sample
    st_list, k_list, v_list = [], [], []
    for l in range(DEPTH):
        p = _LayerParams(stacked, l)
        wb = dict(w_in=(w_in_b, l), w_out=(w_out_b, l), w_ff1=(w_ff1_b, l), w_ff2=(w_ff2_b, l))
        lam_init = 0.8 - 0.6 * math.exp(-0.3 * l)
        mod_lat = mod_all[l, 0:Bs].reshape(Bs, 1, 6 * D_MODEL)
        mod_ctx = mod_all[l, Bs:Bs + 1].reshape(1, 1, 6 * D_MODEL)
        filt_c = _hyena_filters(Lc, p, tabs_c)
        filt_s = _hyena_filters(Ls, p, tabs_s)
        xp, (s_ctx, k_ctx, v_ctx) = _layer(xp, mod_ctx, p, wb, l, lam_init, None, None, tabs_c, filt_c)
        st_list.append(s_ctx)
        k_list.append(k_ctx)
        v_list.append(v_ctx)
        s0 = state_rwkv[:, l]
        xs, _ = _layer(xs, mod_lat, p, wb, l, lam_init, (s0, ck, cv), rope_tabs, tabs_s, filt_s)
    new_k = jnp.stack(k_list, axis=1).reshape(Bc, DEPTH, Lc, H_B, 2, DIFF_DH)
    new_v = jnp.stack(v_list, axis=1).reshape(Bc, DEPTH, Lc, H_B, 2 * DIFF_DH)
    return (xp, xs, jnp.stack(st_list, axis=1), new_k, new_v)
```

```python
import functools
import math

import jax
import jax.numpy as jnp
import numpy as np
from jax import lax
from jax.experimental import pallas as pl
from jax.experimental.pallas import tpu as pltpu

F32 = jnp.float32
BF16 = jnp.bfloat16

D_MODEL = 1024
DEPTH = 4
GRID_W = 64
D_A = 256
RW_HEAD = 64
H_A = 4
D_B = 512
DIFF_DH = 64
H_B = 4
D_C = 256
HY_BANDS = 16
HY_EMB = 33
HY_FFN = 64
D_FF = 4096
FFN_CHUNK = 1024
ROPE_BASE = 10000.0
RMS_EPS = 1e-6
GN_EPS = 64e-5
N_A_COLS = 1152
IN_COLS = 3456

TL = 256
MIB = 1024 * 1024
V7X_VMEM_BYTES = 64 * MIB
V7X_VMEM_CEILING = V7X_VMEM_BYTES * 7 // 8
VMEM_FLOOR = 16 * MIB


def _nbytes(shape, dtype):
    return math.prod(shape) * jnp.dtype(dtype).itemsize


def _cparams(n_axes, block_bytes=0, scratch_bytes=0, temp_bytes=0):
    need = 2 * block_bytes + scratch_bytes + temp_bytes
    return pltpu.CompilerParams(dimension_semantics=("arbitrary",) * n_axes,
                                vmem_limit_bytes=int(min(V7X_VMEM_CEILING, max(VMEM_FLOOR, need))))


def _split3(a):
    hi = a.astype(BF16)
    r1 = a - hi.astype(F32)
    mid = r1.astype(BF16)
    lo = (r1 - mid.astype(F32)).astype(BF16)
    return hi, mid, lo


def _split2(a):
    hi = a.astype(BF16)
    lo = (a - hi.astype(F32)).astype(BF16)
    return hi, lo


def _segsum(a, bd):
    hi, lo = _split2(a)
    return jnp.dot(hi, bd, preferred_element_type=F32) + jnp.dot(lo, bd, preferred_element_type=F32)


def _block_diag(n, seg):
    r = lax.broadcasted_iota(jnp.int32, (n, n), 0) // seg
    c = lax.broadcasted_iota(jnp.int32, (n, n), 1) // seg
    return (r == c).astype(BF16)


class _LayerParams:
    def __init__(self, stacked, layer):
        self.stacked = stacked
        self.layer = layer

    def __getitem__(self, name):
        return self.stacked[name][self.layer]


def _pspec(p, name, shape):
    if isinstance(p, _LayerParams):
        stack = p.stacked[name]
        stack = stack.reshape((stack.shape[0],) + shape)
        layer = p.layer
        return pl.BlockSpec((None,) + shape, lambda *g: (layer,) + (0,) * len(shape)), stack
    return pl.BlockSpec(shape, lambda *g: (0,) * len(shape)), p[name].reshape(shape)


def _layer_weight_spec(w, n_grid_axes):
    stack, layer = w
    _, k, n = stack.shape
    if n_grid_axes == 2:
        return pl.BlockSpec((None, k, n), lambda b, i: (layer, 0, 0))
    raise NotImplementedError(n_grid_axes)


def _rms(x, g):
    return x * lax.rsqrt(jnp.mean(x * x, axis=-1, keepdims=True) + RMS_EPS) * g


def _mod_kernel(cond_ref, w_ref, b_ref, o_ref):
    c = cond_ref[...]
    s = (c * jax.nn.sigmoid(c)).astype(BF16)
    o_ref[0] = jnp.dot(s, w_ref[0].astype(BF16), preferred_element_type=F32) + b_ref[0]


def _modulation(cond, w_mod, b_mod):
    tn = 1536
    return pl.pallas_call(
        _mod_kernel,
        grid=(DEPTH, 6 * D_MODEL // tn),
        in_specs=[pl.BlockSpec((8, D_MODEL), lambda l, n: (0, 0)),
                  pl.BlockSpec((1, D_MODEL, tn), lambda l, n: (l, 0, n)),
                  pl.BlockSpec((1, 1, tn), lambda l, n: (l, 0, n))],
        out_specs=pl.BlockSpec((1, 8, tn), lambda l, n: (l, 0, n)),
        out_shape=jax.ShapeDtypeStruct((DEPTH, 8, 6 * D_MODEL), F32),
        compiler_params=_cparams(2, block_bytes=_nbytes((D_MODEL + 16, tn), F32),
                                 temp_bytes=_nbytes((D_MODEL, tn), BF16)),
        name="modulation",
    )(cond, w_mod, b_mod.reshape(DEPTH, 1, 6 * D_MODEL))


def _rope(x, cos, sin_a, sin_b):
    n = x.shape[-1]
    return x * cos + pltpu.roll(x, n - 16, axis=1) * sin_a + pltpu.roll(x, 16, axis=1) * sin_b


def _shift_conv(u, prev_row, next_row, cw):
    n = u.shape[0]
    row = lax.broadcasted_iota(jnp.int32, (n, 1), 0)
    up = jnp.where(row == 0, prev_row, pltpu.roll(u, 1, axis=0))
    un = jnp.where(row == n - 1, next_row, pltpu.roll(u, n - 1, axis=0))
    return cw[0:1] * up + cw[1:2] * u + cw[2:3] * un


def _prep_body(ua, prev, nxt, uh, prevh, nxth,
               cw_ref, w0_ref, w2_ref, a0_ref, a2_ref, g2_ref, kkw_ref, kaw_ref, rk_ref, hcw_ref, hcb_ref,
               r_ref, v_ref, nkk_ref, wf_ref, kaf_ref, kdf_ref, wb_ref, kab_ref, kdb_ref, g_ref, bon_ref,
               x1_ref, x2_ref, hv_ref):
    bd = _block_diag(D_A, RW_HEAD)
    u_rkv = ua[:, 0:3 * D_A]
    rkv = _shift_conv(u_rkv, prev, nxt, cw_ref[...])
    r = rkv[:, 0:D_A]
    k = rkv[:, D_A:2 * D_A]
    v = rkv[:, 2 * D_A:3 * D_A]
    u_w = ua[:, 768:896]
    u_a = ua[:, 896:1024]
    u_g = ua[:, 1024:1152]

    g = jnp.dot(jax.nn.sigmoid(u_g).astype(BF16), g2_ref[...].astype(BF16), preferred_element_type=F32)
    kk = k * kkw_ref[...]
    kk = kk * lax.rsqrt(_segsum(kk * kk, bd) + 1e-12)
    r_ref[0] = r
    v_ref[0] = v
    nkk_ref[0] = -kk
    g_ref[0] = g

    kd_sum = None
    outs = ((wf_ref, kaf_ref, kdf_ref), (wb_ref, kab_ref, kdb_ref))
    for d in range(2):
        uw_d = u_w[:, d * 64:(d + 1) * 64]
        ua_d = u_a[:, d * 64:(d + 1) * 64]
        xw = w0_ref[d:d + 1, :] + jnp.dot(jnp.tanh(uw_d).astype(BF16), w2_ref[d].astype(BF16),
                                          preferred_element_type=F32)
        z = -xw
        softplus = jnp.maximum(z, 0.0) + jnp.log(1.0 + jnp.exp(-jnp.abs(z)))
        wl = -softplus - 0.5
        log_decay = -jnp.exp(wl)
        a = jax.nn.sigmoid(a0_ref[d:d + 1, :] + jnp.dot(ua_d.astype(BF16), a2_ref[d].astype(BF16),
                                                        preferred_element_type=F32))
        kd = k * (1.0 + (a - 1.0) * kaw_ref[...])
        w_o, ka_o, kd_o = outs[d]
        w_o[0] = log_decay
        ka_o[0] = kk * a
        kd_o[0] = kd
        kd_sum = kd if kd_sum is None else kd_sum + kd
    bon_ref[0] = _segsum(r * kd_sum * rk_ref[...], bd) * v

    hc = _shift_conv(uh, prevh, nxth, hcw_ref[...]) + hcb_ref[...]
    x1_ref[0] = hc[:, 0:D_C]
    x2_ref[0] = hc[:, D_C:2 * D_C]
    hv_ref[0] = hc[:, 2 * D_C:3 * D_C]


N_PREP_PARAMS = 11
N_PREP_OUTS = 14


def _inproj_kernel(*refs, rope):
    n_in = 6 + (3 if rope else 0)
    x_ref, xp_ref, xn_ref, mod_ref, g_ref, w_ref = refs[0:6]
    rope_refs = refs[6:n_in]
    prep_params = refs[n_in:n_in + N_PREP_PARAMS]
    uq_ref, uk_ref, uv_ref = refs[n_in + N_PREP_PARAMS:n_in + N_PREP_PARAMS + 3]
    prep_outs = refs[n_in + N_PREP_PARAMS + 3:]
    i = pl.program_id(1)
    nt = pl.num_programs(1)
    mod = mod_ref[0]
    sh1 = mod[:, 0:D_MODEL]
    sc1 = mod[:, D_MODEL:2 * D_MODEL]

    def normed(x):
        return (_rms(x, g_ref[...]) * (1.0 + sc1) + sh1).astype(BF16)

    n = x_ref.shape[1]
    h_all = jnp.concatenate([normed(x_ref[0]), normed(xp_ref[0]), normed(xn_ref[0])], axis=0)
    u_all = jnp.dot(h_all, w_ref[...], preferred_element_type=F32)
    u = u_all[0:n]
    q = u[:, N_A_COLS:N_A_COLS + D_B] * (DIFF_DH ** -0.5 * math.log2(math.e))
    k = u[:, N_A_COLS + D_B:N_A_COLS + 2 * D_B]
    if rope:
        cos_ref, sa_ref, sb_ref = rope_refs
        q = _rope(q, cos_ref[...], sa_ref[...], sb_ref[...])
        k = _rope(k, cos_ref[...], sa_ref[...], sb_ref[...])
    uq_ref[0] = q.astype(uq_ref.dtype)
    uk_ref[0] = k.astype(uk_ref.dtype)
    uv_ref[0] = u[:, N_A_COLS + 2 * D_B:N_A_COLS + 3 * D_B].astype(uv_ref.dtype)

    hy0 = N_A_COLS + 3 * D_B
    prev_row = u_all[n + 7:n + 8]
    next_row = u_all[n + 8:n + 9]
    has_prev = i > 0
    has_next = i < nt - 1
    _prep_body(u[:, 0:N_A_COLS], jnp.where(has_prev, prev_row[:, 0:3 * D_A], 0.0),
               jnp.where(has_next, next_row[:, 0:3 * D_A], 0.0),
               u[:, hy0:IN_COLS], jnp.where(has_prev, prev_row[:, hy0:IN_COLS], 0.0),
               jnp.where(has_next, next_row[:, hy0:IN_COLS], 0.0),
               *prep_params, *prep_outs)


def _inproj(x, mod, w_in, rope_tabs, p):
    B, L, _ = x.shape
    rope = rope_tabs is not None
    mb = mod.shape[0]
    nb8 = L // 8
    t8 = TL // 8
    row = lambda b, i: (b, i, 0)
    prev = lambda b, i: (b, jnp.maximum(i * t8 - 1, 0), 0)
    nxt = lambda b, i: (b, jnp.minimum((i + 1) * t8, nb8 - 1), 0)
    const2 = lambda b, i: (0, 0)

    params = [_pspec(p, 'rwkv_conv', (3, 3 * D_A)), _pspec(p, 'rwkv_w0', (2, D_A)),
              _pspec(p, 'rwkv_w2', (2, 64, D_A)), _pspec(p, 'rwkv_a0', (2, D_A)), _pspec(p, 'rwkv_a2', (2, 64, D_A)),
              _pspec(p, 'rwkv_g2', (128, D_A)), _pspec(p, 'rwkv_kk', (1, D_A)), _pspec(p, 'rwkv_ka', (1, D_A)),
              _pspec(p, 'rwkv_rk', (1, D_A)), _pspec(p, 'hy_conv_w', (3, 3 * D_C)),
              _pspec(p, 'hy_conv_b', (1, 3 * D_C))]
    assert len(params) == N_PREP_PARAMS
    g_spec, g_arg = _pspec(p, 'g_mix_pre', (1, D_MODEL))
    in_specs = [pl.BlockSpec((1, TL, D_MODEL), row), pl.BlockSpec((1, 8, D_MODEL), prev),
                pl.BlockSpec((1, 8, D_MODEL), nxt),
                pl.BlockSpec((1, 1, 6 * D_MODEL), (lambda b, i: (b, 0, 0)) if mb > 1 else (lambda b, i: (0, 0, 0))),
                g_spec,
                _layer_weight_spec(w_in, 2)]
    args = [x, x, x, mod, g_arg, w_in[0]]
    if rope:
        in_specs += [pl.BlockSpec((TL, D_B), lambda b, i: (i, 0))] * 3
        args += list(rope_tabs)
    in_specs += [s for s, _ in params]
    args += [a for _, a in params]
    qkv_dt = BF16 if rope else F32
    out_shape = ([jax.ShapeDtypeStruct((B, L, D_B), qkv_dt)] * 3
                 + [jax.ShapeDtypeStruct((B, L, D_A), F32)] * N_PREP_OUTS)
    out_specs = [pl.BlockSpec((1, TL, D_B), row)] * 3 + [pl.BlockSpec((1, TL, D_A), row)] * N_PREP_OUTS
    return pl.pallas_call(
        functools.partial(_inproj_kernel, rope=rope),
        grid=(B, L // TL), in_specs=in_specs, out_specs=out_specs, out_shape=out_shape,
        compiler_params=_cparams(
            2,
            block_bytes=(_nbytes((TL + 16, D_MODEL), F32) + _nbytes((D_MODEL, IN_COLS), BF16)
                         + 3 * _nbytes((TL, D_B), F32) * (2 if rope else 1) + N_PREP_OUTS * _nbytes((TL, D_A), F32)),
            temp_bytes=3 * _nbytes((TL + 16, IN_COLS), F32)),
        name="inproj",
    )(*args)


CHUNK = 64
CHUNKS_PER_ITER = 4
_NN = (((1,), (0,)), ((), ()))
_NT = (((1,), (1,)), ((), ()))
_TN = (((0,), (0,)), ((), ()))


def _mm3(a, b, dims):
    ah, al = _split2(a)
    bh, bl = _split2(b)
    f = lambda x, y: lax.dot_general(x, y, dims, preferred_element_type=F32)
    if dims is _TN:
        return f(ah, bh) + f(ah, bl) + f(al, bh)
    m = a.shape[0]
    both = f(jnp.concatenate([ah, al], axis=0), bh)
    return both[0:m] + both[m:2 * m] + f(ah, bl)


def _mm2(a, b, dims):
    ah = a.astype(BF16)
    bh, bl = _split2(b)
    n = b.shape[1]
    both = lax.dot_general(ah, jnp.concatenate([bh, bl], axis=1), dims, preferred_element_type=F32)
    return both[:, 0:n] + both[:, n:2 * n]


def _mm1(a, b, dims):
    return lax.dot_general(a.astype(BF16), b.astype(BF16), dims, preferred_element_type=F32)


def _unit_tri_inverses(ms, row, col):
    eye = (row == col).astype(F32)
    blk = lambda n: (row // n) == (col // n)
    m8 = [jnp.where(blk(8), m, 0.0) for m in ms]
    a2 = [_mm1(x, x, _NN) for x in m8]
    a4 = [_mm1(x, x, _NN) for x in a2]
    p = [eye + x + y + _mm1(x, y, _NN) for x, y in zip(m8, a2)]
    ts = [x + _mm1(x, y, _NN) for x, y in zip(p, a4)]
    n = 8
    while n < CHUNK:
        sel = jnp.logical_and(blk(2 * n), jnp.logical_not(blk(n)))
        ot = [_mm1(jnp.where(sel, m, 0.0), t, _NN) for m, t in zip(ms, ts)]
        ts = [t + _mm1(t, o, _NN) for t, o in zip(ts, ot)]
        n *= 2
    return ts


def _scan_kernel(nkkf_ref, rf_ref, vf_ref, lwf_ref, kaf_ref, kdf_ref,
                 nkkb_ref, rb_ref, vb_ref, lwb_ref, kab_ref, kdb_ref, s0_ref,
                 yf_ref, yb_ref, sfin_ref, s_ref):
    C = CHUNK
    j = pl.program_id(1)
    nj = pl.num_programs(1)
    row = lax.broadcasted_iota(jnp.int32, (C, C), 0)
    col = lax.broadcasted_iota(jnp.int32, (C, C), 1)
    dirs = ((nkkf_ref, rf_ref, vf_ref, lwf_ref, kaf_ref, kdf_ref, yf_ref),
            (nkkb_ref, rb_ref, vb_ref, lwb_ref, kab_ref, kdb_ref, yb_ref))
    n_chunks = nkkf_ref.shape[1] // C

    @pl.when(j == 0)
    def _():
        s_ref[...] = s0_ref[0]

    def build(c, d, probs):
        rows = pl.ds(pl.multiple_of(c * C, C), C)
        nkk_ref, r_ref, v_ref, lw_ref, b_ref, k_ref = dirs[d][0:6]
        alpha = nkk_ref[0, rows, :]
        r = r_ref[0, rows, :]
        v = v_ref[0, rows, :]
        before = (row > col) if d == 0 else (row < col)
        upto = (row >= col) if d == 0 else (row <= col)
        lw = lw_ref[0, rows, :]
        beta = b_ref[0, rows, :]
        kd = k_ref[0, rows, :]
        tri = upto.astype(BF16)
        l_hi, l_mid, l_lo = _split3(lw)
        lam = (jnp.dot(tri, l_hi, preferred_element_type=F32) + jnp.dot(tri, l_mid, preferred_element_type=F32)
               + jnp.dot(tri, l_lo, preferred_element_type=F32))
        lam_prev = lam - lw
        lam_mid = lam[C // 2:C // 2 + 1, :]
        lam_end = lam[C - 1:C, :] if d == 0 else lam[0:1, :]
        e_to_mid = jnp.exp(lam_mid - lam)
        a_m = alpha * jnp.exp(lam_prev - lam_mid)
        b_m = beta * e_to_mid
        k_m = kd * e_to_mid
        r_m = r * jnp.exp(lam - lam_mid)
        a_0 = alpha * jnp.exp(lam_prev)
        r_0 = r * jnp.exp(lam)
        e_to_end = jnp.exp(lam_end - lam)
        b_e = beta * e_to_end
        k_e = kd * e_to_end
        p_end = jnp.exp(lam_end)
        for h in range(H_A):
            sl = slice(h * RW_HEAD, (h + 1) * RW_HEAD)
            probs.append(dict(
                rows=rows, d=d, h=h, before=before, upto=upto, pc=p_end[:, sl],
                ar=jnp.concatenate([a_m[:, sl], r_m[:, sl]], axis=0),
                bk=jnp.concatenate([b_m[:, sl], k_m[:, sl]], axis=0),
                a0=a_0[:, sl], r0=r_0[:, sl], vh=v[:, sl],
                bke=jnp.concatenate([b_e[:, sl], k_e[:, sl]], axis=0)))

    def body(ci, carry):
        probs = []
        for cc in range(CHUNKS_PER_ITER):
            cf = ci * CHUNKS_PER_ITER + cc
            build(cf, 0, probs)
            build(n_chunks - 1 - cf, 1, probs)
        g12 = [_mm1(q['ar'], q['bk'], _NT) for q in probs]
        ms = [jnp.where(q['before'], g[0:C, 0:C], 0.0) for q, g in zip(probs, g12)]
        ns = [jnp.where(q['before'], g[0:C, C:2 * C], 0.0) for q, g in zip(probs, g12)]
        rbk = [jnp.concatenate([jnp.where(q['upto'], g[C:2 * C, 0:C], 0.0),
                                jnp.where(q['upto'], g[C:2 * C, C:2 * C], 0.0)], axis=1) for q, g in zip(probs, g12)]
        nvs = [_mm1(n, q['vh'], _NN) for q, n in zip(probs, ns)]
        ts = _unit_tri_inverses(ms, row, col)
        xs = [_mm2(t, jnp.concatenate([q['a0'], nv], axis=1), _NN) for q, t, nv in zip(probs, ts, nvs)]
        zeros = jnp.zeros((C, RW_HEAD), F32)
        ws = [jnp.concatenate([x, jnp.concatenate([zeros, q['vh']], axis=1)], axis=0) for q, x in zip(probs, xs)]
        rys = [_mm1(rb, w, _NN) for rb, w in zip(rbk, ws)]
        fgs = [_mm3(w, q['bke'], _TN) for q, w in zip(probs, ws)]
        ys = []
        for q, ry, fg in zip(probs, rys, fgs):
            s = s_ref[q['d'], q['h']]
            rp = q['r0'] + ry[:, 0:RW_HEAD]
            ys.append(ry[:, RW_HEAD:2 * RW_HEAD] + _mm1(rp, s, _NT))
            s_ref[q['d'], q['h']] = s * q['pc'] + _mm1(s, fg[0:RW_HEAD], _NN) + fg[RW_HEAD:2 * RW_HEAD]
        for k0 in range(0, len(probs), H_A):
            q0 = probs[k0]
            dirs[q0['d']][6][0, q0['rows'], :] = jnp.concatenate(ys[k0:k0 + H_A], axis=1)
        return carry

    lax.fori_loop(0, n_chunks // CHUNKS_PER_ITER, body, 0)

    @pl.when(j == nj - 1)
    def _():
        sfin_ref[0] = s_ref[...]


def _scan(nkk, r, v, lwf, kaf, kdf, lwb, kab, kdb, s0):
    B, L, _ = r.shape
    nj = L // TL
    tile = pl.BlockSpec((1, TL, D_A), lambda b, j: (b, j, 0))
    rtile = pl.BlockSpec((1, TL, D_A), lambda b, j: (b, nj - 1 - j, 0))
    st = pl.BlockSpec((1, 2, H_A, RW_HEAD, RW_HEAD), lambda b, j: (b, 0, 0, 0, 0))
    f32 = lambda *s: jax.ShapeDtypeStruct(s, F32)
    return pl.pallas_call(
        _scan_kernel, grid=(B, nj),
        in_specs=[tile] * 6 + [rtile] * 6 + [st],
        out_specs=[tile, rtile, st],
        out_shape=[f32(B, L, D_A), f32(B, L, D_A), f32(B, 2, H_A, RW_HEAD, RW_HEAD)],
        scratch_shapes=[pltpu.VMEM((2, H_A, RW_HEAD, RW_HEAD), F32)],
        compiler_params=_cparams(
            2, block_bytes=14 * _nbytes((TL, D_A), F32) + 2 * _nbytes((2, H_A, RW_HEAD, RW_HEAD), F32),
            scratch_bytes=_nbytes((2, H_A, RW_HEAD, RW_HEAD), F32),
            temp_bytes=2 * CHUNKS_PER_ITER * H_A * 12 * _nbytes((2 * CHUNK, 2 * CHUNK), F32)),
        name="rwkv_scan",
    )(nkk, r, v, lwf, kaf, kdf, nkk, r, v, lwb, kab, kdb, s0)


def _row_sum_bf16(e):
    n = e.shape[1] // 128
    acc = None
    for g0 in range(0, n, 4):
        part = e[:, g0 * 128:(g0 + 1) * 128]
        for j in range(g0 + 1, min(g0 + 4, n)):
            part = part + e[:, j * 128:(j + 1) * 128]
        pf = part.astype(F32)
        acc = pf if acc is None else acc + pf
    return jnp.sum(acc, axis=-1, keepdims=True)


def _attn_kernel(*refs, has_cache, lam_init):
    if has_cache:
        q_ref, k_ref, v_ref, ck_ref, cv_ref, lq1, lk1, lq2, lk2, sub_ref, o_ref = refs
    else:
        q_ref, k_ref, v_ref, lq1, lk1, lq2, lk2, sub_ref, o_ref = refs
    l1 = jnp.sum(lq1[...] * lk1[...], axis=-1, keepdims=True)
    l2 = jnp.sum(lq2[...] * lk2[...], axis=-1, keepdims=True)
    lam = jnp.exp(l1) - jnp.exp(l2) + lam_init
    q = q_ref[0].astype(BF16)
    kn = k_ref[0].astype(BF16)
    vn = v_ref[0].astype(BF16)
    if has_cache:
        kc = ck_ref[0, 0].astype(BF16)
        vc = cv_ref[0, 0].astype(BF16)
    dn = (((1,), (1,)), ((), ()))
    def scores(j):
        c0 = j * DIFF_DH
        qh = q[:, c0:c0 + DIFF_DH]
        s_n = lax.dot_general(qh, kn[:, c0:c0 + DIFF_DH], dn, preferred_element_type=F32)
        s_c = lax.dot_general(qh, kc[:, c0:c0 + DIFF_DH], dn, preferred_element_type=F32) if has_cache else None
        return s_n, s_c

    nxt = scores(0)
    for h in range(H_B):
        o = None
        for m in range(2):
            j = h * 2 + m
            s_n, s_c = nxt
            if j + 1 < 2 * H_B:
                nxt = scores(j + 1)
            mx = jnp.max(s_n, axis=-1, keepdims=True)
            if has_cache:
                mx = jnp.maximum(mx, jnp.max(s_c, axis=-1, keepdims=True))
            e_n = jnp.exp2((s_n - mx).astype(BF16))
            den = _row_sum_bf16(e_n)
            if has_cache:
                e_c = jnp.exp2((s_c - mx).astype(BF16))
                den = den + _row_sum_bf16(e_c)
            o_m = jnp.dot(e_n, vn[:, h * 128:(h + 1) * 128], preferred_element_type=F32)
            if has_cache:
                o_m = o_m + jnp.dot(e_c, vc[:, h * 128:(h + 1) * 128], preferred_element_type=F32)
            o = o_m / den if m == 0 else o - o_m * (lam / den)
        o_ref[0, :, h * 128:(h + 1) * 128] = _rms(o, sub_ref[...]) * (1.0 - lam_init)


def _attention(q, k, v, cache, l, p, lam_init):
    B, L, _ = q.shape
    has_cache = cache is not None
    tq = TL
    in_specs = [pl.BlockSpec((1, tq, D_B), lambda b, i: (b, i, 0)),
                pl.BlockSpec((1, L, D_B), lambda b, i: (b, 0, 0)),
                pl.BlockSpec((1, L, D_B), lambda b, i: (b, 0, 0))]
    args = [q, k, v]
    if has_cache:
        ck, cv = cache
        past = ck.shape[2]
        in_specs += [pl.BlockSpec((1, 1, past, D_B), lambda b, i: (b, l, 0, 0))] * 2
        args += [ck, cv]
    small = [_pspec(p, 'diff_lq1', (1, DIFF_DH)), _pspec(p, 'diff_lk1', (1, DIFF_DH)),
             _pspec(p, 'diff_lq2', (1, DIFF_DH)), _pspec(p, 'diff_lk2', (1, DIFF_DH)),
             _pspec(p, 'diff_subln', (1, 2 * DIFF_DH))]
    in_specs += [sp for sp, _ in small]
    args += [a for _, a in small]
    return pl.pallas_call(
        functools.partial(_attn_kernel, has_cache=has_cache, lam_init=lam_init),
        grid=(B, L // tq), in_specs=in_specs,
        out_specs=pl.BlockSpec((1, tq, D_B), lambda b, i: (b, i, 0)),
        out_shape=jax.ShapeDtypeStruct((B, L, D_B), F32),
        compiler_params=_cparams(
            2,
            block_bytes=(2 * _nbytes((tq, D_B), F32) + 2 * _nbytes((L, D_B), q.dtype)
                         + (2 * _nbytes((cache[0].shape[2], D_B), F32) if has_cache else 0)),
            temp_bytes=(2 * (_nbytes((tq, L + TL), F32) + _nbytes((tq, L + TL), BF16)) + 2 * _nbytes((L + TL, D_B), BF16)
                        + 4 * _nbytes((tq, D_B), F32))),
        name="diff_attention",
    )(*args)


def _hfilt_time_kernel(emb_ref, w1_ref, b1_ref, fr_ref, w2_ref, b2_ref, w3_ref, dec_ref, h_ref, ss_ref, acc_ref):
    i = pl.program_id(0)
    emb = emb_ref[...]
    fr = fr_ref[...]
    h = jnp.sin(fr * (jnp.dot(emb.astype(BF16), w1_ref[...].astype(BF16), preferred_element_type=F32) + b1_ref[...]))
    h = jnp.sin(fr * (jnp.dot(h.astype(BF16), w2_ref[...].astype(BF16), preferred_element_type=F32) + b2_ref[...]))
    h = jnp.dot(h.astype(BF16), w3_ref[...].astype(BF16), preferred_element_type=F32)
    h = h * jnp.exp(-emb[:, 0:1] * jnp.abs(dec_ref[...]))
    n = h.shape[0]
    row = lax.broadcasted_iota(jnp.int32, (n, 4 * D_C), 0) + i * n
    col = lax.broadcasted_iota(jnp.int32, (n, 4 * D_C), 1)
    is_bwd = ((col // D_C) % 2) == 1
    h = jnp.where(jnp.logical_and(is_bwd, row == 0), 0.0, h)
    h_ref[...] = h

    @pl.when(i == 0)
    def _():
        acc_ref[...] = jnp.zeros_like(acc_ref)

    acc_ref[...] += jnp.sum(h * h, axis=0, keepdims=True)
    ss_ref[...] = acc_ref[...]


def _hfilt_freq_kernel(h_ref, ss_ref, c_ref, s_ref, hr_ref, hi_ref, a_ref, b_ref, nyq_ref):
    i = pl.program_id(0)

    @pl.when(i == 0)
    def _():
        for o in range(2):
            hf = h_ref[:, o * 2 * D_C:o * 2 * D_C + D_C]
            hb = h_ref[:, o * 2 * D_C + D_C:(o + 1) * 2 * D_C]
            a = hf + hb
            n = a.shape[0]
            alt = 1.0 - 2.0 * (lax.broadcasted_iota(jnp.int32, (n, 1), 0) % 2).astype(F32)
            nyq_ref[:, o * D_C:(o + 1) * D_C] = jnp.sum(a * alt, axis=0, keepdims=True)
            a_ref[:, o * D_C:(o + 1) * D_C] = a.astype(BF16)
            b_ref[:, o * D_C:(o + 1) * D_C] = (hb - hf).astype(BF16)

    ss = ss_ref[...]
    tot = jnp.concatenate([ss[:, 0:D_C] + ss[:, D_C:2 * D_C], ss[:, 2 * D_C:3 * D_C] + ss[:, 3 * D_C:4 * D_C]], axis=1)
    scale = lax.rsqrt(tot + 1e-6)

    hr_ref[...] = jnp.dot(c_ref[...], a_ref[...], preferred_element_type=F32) * scale
    hi = jnp.dot(s_ref[...], b_ref[...], preferred_element_type=F32) * scale
    tf = hi.shape[0]
    row = lax.broadcasted_iota(jnp.int32, (tf, 1), 0) + i * tf
    hi_ref[...] = jnp.where(row == 0, nyq_ref[...] * scale, hi)


def _hyena_filters(L, p, tabs):
    emb, cmat, smat = tabs
    tr = 256
    params = [_pspec(p, 'hy_w1_pad', (128, HY_FFN)), _pspec(p, 'hy_b1', (1, HY_FFN)), _pspec(p, 'hy_freq', (1, HY_FFN)),
              _pspec(p, 'hy_w2', (HY_FFN, HY_FFN)), _pspec(p, 'hy_b2', (1, HY_FFN)),
              _pspec(p, 'hy_w3', (HY_FFN, 4 * D_C)), _pspec(p, 'hy_decay4', (1, 4 * D_C))]
    h_raw, ss = pl.pallas_call(
        _hfilt_time_kernel, grid=(L // tr,),
        in_specs=[pl.BlockSpec((tr, 128), lambda i: (i, 0))] + [sp for sp, _ in params],
        out_specs=[pl.BlockSpec((tr, 4 * D_C), lambda i: (i, 0)), pl.BlockSpec((1, 4 * D_C), lambda i: (0, 0))],
        out_shape=[jax.ShapeDtypeStruct((L, 4 * D_C), F32), jax.ShapeDtypeStruct((1, 4 * D_C), F32)],
        scratch_shapes=[pltpu.VMEM((1, 4 * D_C), F32)],
        compiler_params=_cparams(1, block_bytes=2 * _nbytes((tr, 4 * D_C), F32) + _nbytes((HY_FFN, 4 * D_C), F32),
                                 temp_bytes=3 * _nbytes((tr, 4 * D_C), F32)),
        name="hyena_filter_time",
    )(emb, *[a for _, a in params])
    tf = 256
    mat = pl.BlockSpec((tf, L), lambda i: (i, 0))
    return pl.pallas_call(
        _hfilt_freq_kernel, grid=(L // tf,),
        in_specs=[pl.BlockSpec((L, 4 * D_C), lambda i: (0, 0)), pl.BlockSpec((1, 4 * D_C), lambda i: (0, 0)),
                  mat, mat],
        out_specs=[pl.BlockSpec((tf, 2 * D_C), lambda i: (i, 0))] * 2,
        out_shape=[jax.ShapeDtypeStruct((L, 2 * D_C), F32)] * 2,
        scratch_shapes=[pltpu.VMEM((L, 2 * D_C), BF16)] * 2 + [pltpu.VMEM((1, 2 * D_C), F32)],
        compiler_params=_cparams(
            1, block_bytes=_nbytes((L, 4 * D_C), F32) + 2 * _nbytes((tf, L), BF16) + 2 * _nbytes((tf, 2 * D_C), F32),
            scratch_bytes=2 * _nbytes((L, 2 * D_C), BF16), temp_bytes=4 * _nbytes((L, D_C), F32)),
        name="hyena_filter_freq",
    )(h_raw, ss, cmat, smat)


def _hconv_kernel(z_ref, zt_ref, gate_ref, hr_ref, hi_ref, bias_ref, c_ref, s_ref, o_ref,
                  zb_ref, yr_ref, yi_ref, nyq_ref, *, BG, L):
    ph = pl.program_id(1)
    i = pl.program_id(2)
    inv_n = 1.0 / (2 * L)

    @pl.when(jnp.logical_and(ph == 0, i == 0))
    def _():
        alt = 1.0 - 2.0 * (lax.broadcasted_iota(jnp.int32, (L, 1), 0) % 2).astype(F32)
        for b in range(BG):
            z = z_ref[b]
            zb_ref[:, b * D_C:(b + 1) * D_C] = z.astype(BF16)
            nyq_ref[:, b * D_C:(b + 1) * D_C] = jnp.sum(z * alt, axis=0, keepdims=True)

    tf = c_ref.shape[0]

    @pl.when(ph == 0)
    def _():
        zr = jnp.dot(c_ref[...], zb_ref[...], preferred_element_type=F32)
        zs = jnp.dot(s_ref[...], zb_ref[...], preferred_element_type=F32)
        hr = hr_ref[...]
        hi = hi_ref[...]
        row = lax.broadcasted_iota(jnp.int32, (tf, 1), 0) + i * tf
        wgt = jnp.where(row == 0, inv_n, 2.0 * inv_n)
        rows = pl.ds(pl.multiple_of(i * tf, tf), tf)
        for b in range(BG):
            sl = slice(b * D_C, (b + 1) * D_C)
            yr_ref[rows, sl] = ((zr[:, sl] * hr + zs[:, sl] * hi) * wgt).astype(BF16)
            yi_ref[rows, sl] = ((zr[:, sl] * hi - zs[:, sl] * hr) * (2.0 * inv_n)).astype(BF16)

        @pl.when(i == 0)
        def _():
            for b in range(BG):
                sl = slice(b * D_C, (b + 1) * D_C)
                nyq_ref[:, sl] = nyq_ref[:, sl] * hi[0:1, :] * inv_n

    @pl.when(ph == 1)
    def _():
        y = (jnp.dot(c_ref[...], yr_ref[...], preferred_element_type=F32)
             - jnp.dot(s_ref[...], yi_ref[...], preferred_element_type=F32))
        row = lax.broadcasted_iota(jnp.int32, (tf, 1), 0) + i * tf
        alt = 1.0 - 2.0 * (row % 2).astype(F32)
        y = y + alt * nyq_ref[...]
        for b in range(BG):
            sl = slice(b * D_C, (b + 1) * D_C)
            o_ref[b] = gate_ref[b] * (y[:, sl] + bias_ref[...] * zt_ref[b])


def _hconv(z, gate, hr, hi, p, order, tabs):
    B, L, _ = z.shape
    if isinstance(p, _LayerParams):
        layer = p.layer
        bias_arg = p.stacked['hy_bias'].reshape(DEPTH, 2, 1, D_C)
        bias_spec = pl.BlockSpec((None, None, 1, D_C), lambda g, ph, i: (layer, order, 0, 0))
    else:
        bias_arg = p['hy_bias'][order].reshape(1, D_C)
        bias_spec = pl.BlockSpec((1, D_C), lambda g, ph, i: (0, 0))
    _, ch, sh = tabs
    BG =min(B, 8) if L <= 256 else min(B, 4)
    assert B % BG == 0
    tf = 256
    nt = L // tf
    mat = pl.BlockSpec((tf, L), lambda g, ph, i: (i, 0))
    tile = pl.BlockSpec((BG, tf, D_C), lambda g, ph, i: (g, i * ph, 0))
    hspec = pl.BlockSpec((tf, D_C), lambda g, ph, i: (i * (1 - ph), order))
    cols = BG * D_C
    return pl.pallas_call(
        functools.partial(_hconv_kernel, BG=BG, L=L), grid=(B // BG, 2, nt),
        in_specs=[pl.BlockSpec((BG, L, D_C), lambda g, ph, i: (g, 0, 0)), tile, tile, hspec, hspec,
                  bias_spec, mat, mat],
        out_specs=tile,
        out_shape=jax.ShapeDtypeStruct((B, L, D_C), F32),
        scratch_shapes=[pltpu.VMEM((L, cols), BF16)] * 3 + [pltpu.VMEM((1, cols), F32)],
        compiler_params=_cparams(
            3,
            block_bytes=(_nbytes((BG, L, D_C), F32) + 3 * _nbytes((BG, tf, D_C), F32) + 2 * _nbytes((tf, L), BF16)
                         + 2 * _nbytes((tf, D_C), F32)),
            scratch_bytes=3 * _nbytes((L, cols), BF16), temp_bytes=4 * _nbytes((tf, cols), F32)),
        name="hyena_conv",
    )(z, z, gate, hr, hi, bias_arg, ch, sh)


def _outffn_kernel(x_ref, yf_ref, yb_ref, bon_ref, g_ref, ob_ref, yc_ref, mod_ref,
                   lnw_ref, lnb_ref, gpost_ref, gffn_ref, gffn_post_ref, w_ref, w1_ref, w2_ref, o_ref):
    bd = _block_diag(D_A, RW_HEAD)
    y = yf_ref[0] + yb_ref[0]
    mu = _segsum(y, bd) * (1.0 / RW_HEAD)
    yc = y - mu
    var = _segsum(yc * yc, bd) * (1.0 / RW_HEAD)
    yn = yc * lax.rsqrt(var + GN_EPS) * lnw_ref[...] + lnb_ref[...]
    ya = (yn + bon_ref[0]) * g_ref[0]
    mix = (jnp.dot(ya.astype(BF16), w_ref[0:D_A, :], preferred_element_type=F32)
           + jnp.dot(ob_ref[0].astype(BF16), w_ref[D_A:D_A + D_B, :], preferred_element_type=F32)
           + jnp.dot(yc_ref[0].astype(BF16), w_ref[D_A + D_B:D_MODEL, :], preferred_element_type=F32))
    mod = mod_ref[0]
    gt1 = mod[:, 2 * D_MODEL:3 * D_MODEL]
    sh2 = mod[:, 3 * D_MODEL:4 * D_MODEL]
    sc2 = mod[:, 4 * D_MODEL:5 * D_MODEL]
    gt2 = mod[:, 5 * D_MODEL:6 * D_MODEL]
    x1 = x_ref[0] + gt1 * _rms(mix, gpost_ref[...])
    h = (_rms(x1, gffn_ref[...]) * (1.0 + sc2) + sh2).astype(BF16)
    acc = None
    for c in range(D_FF // FFN_CHUNK):
        a = jnp.dot(h, w1_ref[:, c * FFN_CHUNK:(c + 1) * FFN_CHUNK], preferred_element_type=F32)
        a = jnp.square(jnp.maximum(a, 0.0)).astype(BF16)
        part = jnp.dot(a, w2_ref[c * FFN_CHUNK:(c + 1) * FFN_CHUNK, :], preferred_element_type=F32)
        acc = part if acc is None else acc + part
    o_ref[0] = x1 + gt2 * _rms(acc, gffn_post_ref[...])


def _outffn(x, yf, yb, bon, g, ob, yc, mod, p, w_out, w1, w2):
    B, L, _ = x.shape
    mb = mod.shape[0]
    row = lambda b, i: (b, i, 0)
    c2 = lambda b, i: (0, 0)
    t = lambda n: pl.BlockSpec((1, TL, n), row)
    vecs = [_pspec(p, 'rwkv_ln_w', (1, D_A)), _pspec(p, 'rwkv_ln_b', (1, D_A)), _pspec(p, 'g_mix_post', (1, D_MODEL)),
            _pspec(p, 'g_ffn_pre', (1, D_MODEL)), _pspec(p, 'g_ffn_post', (1, D_MODEL))]
    in_specs = [t(D_MODEL), t(D_A), t(D_A), t(D_A), t(D_A), t(D_B), t(D_C),
                pl.BlockSpec((1, 1, 6 * D_MODEL), (lambda b, i: (b, 0, 0)) if mb > 1 else (lambda b, i: (0, 0, 0)))]
    in_specs += [sp for sp, _ in vecs]
    in_specs += [_layer_weight_spec(w_out, 2), _layer_weight_spec(w1, 2), _layer_weight_spec(w2, 2)]
    return pl.pallas_call(
        _outffn_kernel, grid=(B, L // TL), in_specs=in_specs,
        out_specs=t(D_MODEL), out_shape=jax.ShapeDtypeStruct((B, L, D_MODEL), F32),
        compiler_params=_cparams(
            2,
            block_bytes=(2 * _nbytes((TL, D_MODEL), F32) + 4 * _nbytes((TL, D_A), F32) + _nbytes((TL, D_B), F32)
                         + _nbytes((TL, D_C), F32) + _nbytes((D_MODEL, D_MODEL), BF16) + 2 * _nbytes((D_MODEL, D_FF), BF16)),
            temp_bytes=4 * _nbytes((TL, D_MODEL), F32) + 2 * _nbytes((TL, FFN_CHUNK), F32)),
        name="outproj_ffn",
    )(x, yf, yb, bon, g, ob, yc, mod, *[a for _, a in vecs], w_out[0], w1[0], w2[0])


@functools.lru_cache(maxsize=None)
def _rope_tables(L):
    rows = L // GRID_W
    row = np.repeat(np.arange(rows), GRID_W).astype(np.float64)
    col = np.tile(np.arange(GRID_W), rows).astype(np.float64)
    half = DIFF_DH // 2
    inv = ROPE_BASE ** (-np.arange(0, half, 2, dtype=np.float64) / half)
    ang_r = row[:, None] * inv[None]
    ang_c = col[:, None] * inv[None]
    ang = np.concatenate([ang_r, ang_r, ang_c, ang_c], axis=-1)
    cos, sin = np.cos(ang), np.sin(ang)
    quarter = (np.arange(DIFF_DH) // 16) % 2
    sin_a = np.where(quarter == 0, -sin, 0.0)
    sin_b = np.where(quarter == 1, sin, 0.0)
    rep = D_B // DIFF_DH
    return tuple(np.tile(t, (1, rep)).astype(np.float32) for t in (cos, sin_a, sin_b))


@functools.lru_cache(maxsize=None)
def _hyena_tables(L):
    t = np.linspace(0.0, 1.0, L, dtype=np.float32).astype(np.float64)[:, None]
    ang = (2.0 * math.pi / L) * np.arange(L, dtype=np.float64)[:, None]
    bands = np.linspace(1e-4, HY_BANDS - 1, HY_BANDS, dtype=np.float32).astype(np.float64)[None, :]
    emb = np.concatenate([t, np.cos(bands * ang), -np.sin(bands * ang)], axis=-1)
    emb = np.pad(emb, ((0, 0), (0, 128 - HY_EMB))).astype(np.float32)
    n = np.arange(L, dtype=np.int64)
    theta = ((n[:, None] * n[None, :]) % (2 * L)).astype(np.float64) * (math.pi / L)
    return emb, np.cos(theta).astype(BF16), np.sin(theta).astype(BF16)


def _layer(x, mod, p, wb, l, lam_init, cache, rope_tabs, hy_tabs, filt):
    B, L, _ = x.shape
    (uq, uk, uv, r, v, nkk, wf, kaf, kdf, wbk, kab, kdb, g, bon, x1h, x2h, hv) = _inproj(
        x, mod, wb['w_in'], rope_tabs, p)
    if cache is None:
        s0 = jnp.zeros((B, 2, H_A, RW_HEAD, RW_HEAD), F32)
        kv_cache = None
    else:
        s0 = cache[0]
        kv_cache = (cache[1], cache[2])
    yf, yb, sfin = _scan(nkk, r, v, wf, kaf, kdf, wbk, kab, kdb, s0)
    ob = _attention(uq, uk, uv, kv_cache, l, p, lam_init)
    hr, hi = filt
    z1 = _hconv(hv, x1h, hr, hi, p, 0, hy_tabs)
    yc = _hconv(z1, x2h, hr, hi, p, 1, hy_tabs)
    x2 = _outffn(x, yf, yb, bon, g, ob, yc, mod, p, wb['w_out'], wb['w_ff1'], wb['w_ff2'])
    return x2, (sfin, uk, uv)


def _derive_params(d):
    w1 = d['hy_w1']
    d['hy_w1_pad'] = jnp.pad(w1, [(0, 0)] * (w1.ndim - 2) + [(0, 128 - HY_EMB), (0, 0)])
    dec = d['hy_decay']
    d['hy_decay4'] = jnp.tile(dec, (1,) * (dec.ndim - 1) + (4,))
    return d


_LAYER_KEYS = ('g_mix_pre', 'g_mix_post', 'g_ffn_pre', 'g_ffn_post', 'rwkv_conv', 'rwkv_w0', 'rwkv_w2', 'rwkv_a0',
               'rwkv_a2', 'rwkv_g2', 'rwkv_kk', 'rwkv_ka', 'rwkv_rk', 'rwkv_ln_w', 'rwkv_ln_b', 'diff_lq1', 'diff_lk1',
               'diff_lq2', 'diff_lk2', 'diff_subln', 'hy_conv_w', 'hy_conv_b', 'hy_w1', 'hy_b1', 'hy_freq', 'hy_w2',
               'hy_b2', 'hy_w3', 'hy_decay', 'hy_bias')


def kernel(x_prompt, x_sample, state_rwkv, cache_k, cache_v, c, c_ctx, w_mod, b_mod, g_mix_pre, g_mix_post, g_ffn_pre, g_ffn_post, w_in, rwkv_conv, rwkv_w0, rwkv_w2, rwkv_a0, rwkv_a2, rwkv_g2, rwkv_kk, rwkv_ka, rwkv_rk, rwkv_ln_w, rwkv_ln_b, diff_lq1, diff_lk1, diff_lq2, diff_lk2, diff_subln, hy_conv_w, hy_conv_b, hy_w1, hy_b1, hy_freq, hy_w2, hy_b2, hy_w3, hy_decay, hy_bias, w_out, w_ff1, w_ff2):
    stacked = dict(g_mix_pre=g_mix_pre, g_mix_post=g_mix_post, g_ffn_pre=g_ffn_pre, g_ffn_post=g_ffn_post,
                   rwkv_conv=rwkv_conv, rwkv_w0=rwkv_w0, rwkv_w2=rwkv_w2, rwkv_a0=rwkv_a0, rwkv_a2=rwkv_a2,
                   rwkv_g2=rwkv_g2, rwkv_kk=rwkv_kk, rwkv_ka=rwkv_ka, rwkv_rk=rwkv_rk, rwkv_ln_w=rwkv_ln_w,
                   rwkv_ln_b=rwkv_ln_b, diff_lq1=diff_lq1, diff_lk1=diff_lk1, diff_lq2=diff_lq2, diff_lk2=diff_lk2,
                   diff_subln=diff_subln, hy_conv_w=hy_conv_w, hy_conv_b=hy_conv_b, hy_w1=hy_w1, hy_b1=hy_b1,
                   hy_freq=hy_freq, hy_w2=hy_w2, hy_b2=hy_b2, hy_w3=hy_w3, hy_decay=hy_decay, hy_bias=hy_bias)
    _derive_params(stacked)
    Bc, Lc, _ = x_prompt.shape
    Bs, Ls, _ = x_sample.shape
    past = cache_k.shape[2]

    cond = jnp.zeros((8, D_MODEL), F32).at[0:Bs].set(c).at[Bs].set(c_ctx)
    mod_all = _modulation(cond, w_mod, b_mod)

    rope_tabs = _rope_tables(Ls)
    tabs_c = _hyena_tables(Lc)
    tabs_s = _hyena_tables(Ls)
    ck = cache_k.reshape(Bs, DEPTH, past, D_B)
    cv = cache_v.reshape(Bs, DEPTH, past, D_B)

    w_in_b, w_out_b, w_ff1_b, w_ff2_b = (w.astype(BF16) for w in (w_in, w_out, w_ff1, w_ff2))
    xp, xs = x_prompt, x_sample
    st_list, k_list, v_list = [], [], []
    for l in range(DEPTH):
        p = _LayerParams(stacked, l)
        wb = dict(w_in=(w_in_b, l), w_out=(w_out_b, l), w_ff1=(w_ff1_b, l), w_ff2=(w_ff2_b, l))
        lam_init = 0.8 - 0.6 * math.exp(-0.3 * l)
        mod_lat = mod_all[l, 0:Bs].reshape(Bs, 1, 6 * D_MODEL)
        mod_ctx = mod_all[l, Bs:Bs + 1].reshape(1, 1, 6 * D_MODEL)
        filt_c = _hyena_filters(Lc, p, tabs_c)
        filt_s = _hyena_filters(Ls, p, tabs_s)
        xp, (s_ctx, k_ctx, v_ctx) = _layer(xp, mod_ctx, p, wb, l, lam_init, None, None, tabs_c, filt_c)
        st_list.append(s_ctx)
        k_list.append(k_ctx)
        v_list.append(v_ctx)
        s0 = state_rwkv[:, l]
        xs, _ = _layer(xs, mod_lat, p, wb, l, lam_init, (s0, ck, cv), rope_tabs, tabs_s, filt_s)
    new_k = jnp.stack(k_list, axis=1).reshape(Bc, DEPTH, Lc, H_B, 2, DIFF_DH)
    new_v = jnp.stack(v_list, axis=1).reshape(Bc, DEPTH, Lc, H_B, 2 * DIFF_DH)
    return (xp, xs, jnp.stack(st_list, axis=1), new_k, new_v)
```

```python
import functools
import math

import jax
import jax.numpy as jnp
import numpy as np
from jax import lax
from jax.experimental import pallas as pl
from jax.experimental.pallas import tpu as pltpu

F32 = jnp.float32
BF16 = jnp.bfloat16

D_MODEL = 1024
DEPTH = 4
GRID_W = 64
D_A = 256
RW_HEAD = 64
H_A = 4
D_B = 512
DIFF_DH = 64
H_B = 4
D_C = 256
HY_BANDS = 16
HY_EMB = 33
HY_FFN = 64
D_FF = 4096
FFN_CHUNK = 1024
ROPE_BASE = 10000.0
RMS_EPS = 1e-6
GN_EPS = 64e-5
N_A_COLS = 1152
IN_COLS = 3456

TL = 256
MIB = 1024 * 1024
V7X_VMEM_BYTES = 64 * MIB
V7X_VMEM_CEILING = V7X_VMEM_BYTES * 7 // 8
VMEM_FLOOR = 40 * MIB


def _nbytes(shape, dtype):
    return math.prod(shape) * jnp.dtype(dtype).itemsize


def _cparams(n_axes, block_bytes=0, scratch_bytes=0, temp_bytes=0):
    need = 2 * block_bytes + scratch_bytes + temp_bytes
    return pltpu.CompilerParams(dimension_semantics=("arbitrary",) * n_axes,
                                vmem_limit_bytes=int(min(V7X_VMEM_CEILING, max(VMEM_FLOOR, need))))


def _split3(a):
    hi = a.astype(BF16)
    r1 = a - hi.astype(F32)
    mid = r1.astype(BF16)
    lo = (r1 - mid.astype(F32)).astype(BF16)
    return hi, mid, lo


def _split2(a):
    hi = a.astype(BF16)
    lo = (a - hi.astype(F32)).astype(BF16)
    return hi, lo


def _segsum(a, bd):
    hi, lo = _split2(a)
    return jnp.dot(hi, bd, preferred_element_type=F32) + jnp.dot(lo, bd, preferred_element_type=F32)


def _block_diag(n, seg):
    r = lax.broadcasted_iota(jnp.int32, (n, n), 0) // seg
    c = lax.broadcasted_iota(jnp.int32, (n, n), 1) // seg
    return (r == c).astype(BF16)


class _LayerParams:
    def __init__(self, stacked, layer):
        self.stacked = stacked
        self.layer = layer

    def __getitem__(self, name):
        return self.stacked[name][self.layer]


def _pspec(p, name, shape):
    if isinstance(p, _LayerParams):
        stack = p.stacked[name]
        stack = stack.reshape((stack.shape[0],) + shape)
        layer = p.layer
        return pl.BlockSpec((None,) + shape, lambda *g: (layer,) + (0,) * len(shape)), stack
    return pl.BlockSpec(shape, lambda *g: (0,) * len(shape)), p[name].reshape(shape)


def _layer_weight_spec(w, n_grid_axes):
    stack, layer = w
    _, k, n = stack.shape
    if n_grid_axes == 2:
        return pl.BlockSpec((None, k, n), lambda b, i: (layer, 0, 0))
    raise NotImplementedError(n_grid_axes)


def _rms(x, g):
    return x * lax.rsqrt(jnp.mean(x * x, axis=-1, keepdims=True) + RMS_EPS) * g


def _mod_kernel(cond_ref, w_ref, b_ref, o_ref):
    c = cond_ref[...]
    s = (c * jax.nn.sigmoid(c)).astype(BF16)
    o_ref[0] = jnp.dot(s, w_ref[0].astype(BF16), preferred_element_type=F32) + b_ref[0]


def _modulation(cond, w_mod, b_mod):
    tn = 1536
    return pl.pallas_call(
        _mod_kernel,
        grid=(DEPTH, 6 * D_MODEL // tn),
        in_specs=[pl.BlockSpec((8, D_MODEL), lambda l, n: (0, 0)),
                  pl.BlockSpec((1, D_MODEL, tn), lambda l, n: (l, 0, n)),
                  pl.BlockSpec((1, 1, tn), lambda l, n: (l, 0, n))],
        out_specs=pl.BlockSpec((1, 8, tn), lambda l, n: (l, 0, n)),
        out_shape=jax.ShapeDtypeStruct((DEPTH, 8, 6 * D_MODEL), F32),
        compiler_params=_cparams(2, block_bytes=_nbytes((D_MODEL + 16, tn), F32),
                                 temp_bytes=_nbytes((D_MODEL, tn), BF16)),
        name="modulation",
    )(cond, w_mod, b_mod.reshape(DEPTH, 1, 6 * D_MODEL))


def _rope(x, cos, sin_a, sin_b):
    n = x.shape[-1]
    return x * cos + pltpu.roll(x, n - 16, axis=1) * sin_a + pltpu.roll(x, 16, axis=1) * sin_b


def _shift_conv(u, prev_row, next_row, cw):
    n = u.shape[0]
    row = lax.broadcasted_iota(jnp.int32, (n, 1), 0)
    up = jnp.where(row == 0, prev_row, pltpu.roll(u, 1, axis=0))
    un = jnp.where(row == n - 1, next_row, pltpu.roll(u, n - 1, axis=0))
    return cw[0:1] * up + cw[1:2] * u + cw[2:3] * un


def _prep_body(ua, prev, nxt, uh, prevh, nxth,
               cw_ref, w0_ref, w2_ref, a0_ref, a2_ref, g2_ref, kkw_ref, kaw_ref, rk_ref, hcw_ref, hcb_ref,
               r_ref, v_ref, nkk_ref, wf_ref, kaf_ref, kdf_ref, wb_ref, kab_ref, kdb_ref, g_ref, bon_ref,
               x1_ref, x2_ref, hv_ref):
    bd = _block_diag(D_A, RW_HEAD)
    u_rkv = ua[:, 0:3 * D_A]
    rkv = _shift_conv(u_rkv, prev, nxt, cw_ref[...])
    r = rkv[:, 0:D_A]
    k = rkv[:, D_A:2 * D_A]
    v = rkv[:, 2 * D_A:3 * D_A]
    u_w = ua[:, 768:896]
    u_a = ua[:, 896:1024]
    u_g = ua[:, 1024:1152]

    g = jnp.dot(jax.nn.sigmoid(u_g).astype(BF16), g2_ref[...].astype(BF16), preferred_element_type=F32)
    kk = k * kkw_ref[...]
    kk = kk * lax.rsqrt(_segsum(kk * kk, bd) + 1e-12)
    r_ref[0] = r
    v_ref[0] = v
    nkk_ref[0] = -kk
    g_ref[0] = g

    kd_sum = None
    outs = ((wf_ref, kaf_ref, kdf_ref), (wb_ref, kab_ref, kdb_ref))
    for d in range(2):
        uw_d = u_w[:, d * 64:(d + 1) * 64]
        ua_d = u_a[:, d * 64:(d + 1) * 64]
        xw = w0_ref[d:d + 1, :] + jnp.dot(jnp.tanh(uw_d).astype(BF16), w2_ref[d].astype(BF16),
                                          preferred_element_type=F32)
        z = -xw
        softplus = jnp.maximum(z, 0.0) + jnp.log(1.0 + jnp.exp(-jnp.abs(z)))
        wl = -softplus - 0.5
        log_decay = -jnp.exp(wl)
        a = jax.nn.sigmoid(a0_ref[d:d + 1, :] + jnp.dot(ua_d.astype(BF16), a2_ref[d].astype(BF16),
                                                        preferred_element_type=F32))
        kd = k * (1.0 + (a - 1.0) * kaw_ref[...])
        w_o, ka_o, kd_o = outs[d]
        w_o[0] = log_decay
        ka_o[0] = kk * a
        kd_o[0] = kd
        kd_sum = kd if kd_sum is None else kd_sum + kd
    bon_ref[0] = _segsum(r * kd_sum * rk_ref[...], bd) * v

    hc = _shift_conv(uh, prevh, nxth, hcw_ref[...]) + hcb_ref[...]
    x1_ref[0] = hc[:, 0:D_C]
    x2_ref[0] = hc[:, D_C:2 * D_C]
    hv_ref[0] = hc[:, 2 * D_C:3 * D_C]


N_PREP_PARAMS = 11
N_PREP_OUTS = 14


def _inproj_kernel(*refs, rope):
    n_in = 6 + (3 if rope else 0)
    x_ref, xp_ref, xn_ref, mod_ref, g_ref, w_ref = refs[0:6]
    rope_refs = refs[6:n_in]
    prep_params = refs[n_in:n_in + N_PREP_PARAMS]
    uq_ref, uk_ref, uv_ref = refs[n_in + N_PREP_PARAMS:n_in + N_PREP_PARAMS + 3]
    prep_outs = refs[n_in + N_PREP_PARAMS + 3:]
    i = pl.program_id(1)
    nt = pl.num_programs(1)
    mod = mod_ref[0]
    sh1 = mod[:, 0:D_MODEL]
    sc1 = mod[:, D_MODEL:2 * D_MODEL]

    def normed(x):
        return (_rms(x, g_ref[...]) * (1.0 + sc1) + sh1).astype(BF16)

    n = x_ref.shape[1]
    h_all = jnp.concatenate([normed(x_ref[0]), normed(xp_ref[0]), normed(xn_ref[0])], axis=0)
    u_all = jnp.dot(h_all, w_ref[...], preferred_element_type=F32)
    u = u_all[0:n]
    q = u[:, N_A_COLS:N_A_COLS + D_B] * (DIFF_DH ** -0.5 * math.log2(math.e))
    k = u[:, N_A_COLS + D_B:N_A_COLS + 2 * D_B]
    if rope:
        cos_ref, sa_ref, sb_ref = rope_refs
        q = _rope(q, cos_ref[...], sa_ref[...], sb_ref[...])
        k = _rope(k, cos_ref[...], sa_ref[...], sb_ref[...])
    uq_ref[0] = q.astype(uq_ref.dtype)
    uk_ref[0] = k.astype(uk_ref.dtype)
    uv_ref[0] = u[:, N_A_COLS + 2 * D_B:N_A_COLS + 3 * D_B].astype(uv_ref.dtype)

    hy0 = N_A_COLS + 3 * D_B
    prev_row = u_all[n + 7:n + 8]
    next_row = u_all[n + 8:n + 9]
    has_prev = i > 0
    has_next = i < nt - 1
    _prep_body(u[:, 0:N_A_COLS], jnp.where(has_prev, prev_row[:, 0:3 * D_A], 0.0),
               jnp.where(has_next, next_row[:, 0:3 * D_A], 0.0),
               u[:, hy0:IN_COLS], jnp.where(has_prev, prev_row[:, hy0:IN_COLS], 0.0),
               jnp.where(has_next, next_row[:, hy0:IN_COLS], 0.0),
               *prep_params, *prep_outs)


def _inproj(x, mod, w_in, rope_tabs, p):
    B, L, _ = x.shape
    rope = rope_tabs is not None
    mb = mod.shape[0]
    nb8 = L // 8
    t8 = TL // 8
    row = lambda b, i: (b, i, 0)
    prev = lambda b, i: (b, jnp.maximum(i * t8 - 1, 0), 0)
    nxt = lambda b, i: (b, jnp.minimum((i + 1) * t8, nb8 - 1), 0)
    const2 = lambda b, i: (0, 0)

    params = [_pspec(p, 'rwkv_conv', (3, 3 * D_A)), _pspec(p, 'rwkv_w0', (2, D_A)),
              _pspec(p, 'rwkv_w2', (2, 64, D_A)), _pspec(p, 'rwkv_a0', (2, D_A)), _pspec(p, 'rwkv_a2', (2, 64, D_A)),
              _pspec(p, 'rwkv_g2', (128, D_A)), _pspec(p, 'rwkv_kk', (1, D_A)), _pspec(p, 'rwkv_ka', (1, D_A)),
              _pspec(p, 'rwkv_rk', (1, D_A)), _pspec(p, 'hy_conv_w', (3, 3 * D_C)),
              _pspec(p, 'hy_conv_b', (1, 3 * D_C))]
    assert len(params) == N_PREP_PARAMS
    g_spec, g_arg = _pspec(p, 'g_mix_pre', (1, D_MODEL))
    in_specs = [pl.BlockSpec((1, TL, D_MODEL), row), pl.BlockSpec((1, 8, D_MODEL), prev),
                pl.BlockSpec((1, 8, D_MODEL), nxt),
                pl.BlockSpec((1, 1, 6 * D_MODEL), (lambda b, i: (b, 0, 0)) if mb > 1 else (lambda b, i: (0, 0, 0))),
                g_spec,
                _layer_weight_spec(w_in, 2)]
    args = [x, x, x, mod, g_arg, w_in[0]]
    if rope:
        in_specs += [pl.BlockSpec((TL, D_B), lambda b, i: (i, 0))] * 3
        args += list(rope_tabs)
    in_specs += [s for s, _ in params]
    args += [a for _, a in params]
    qkv_dt = BF16 if rope else F32
    out_shape = ([jax.ShapeDtypeStruct((B, L, D_B), qkv_dt)] * 3
                 + [jax.ShapeDtypeStruct((B, L, D_A), F32)] * N_PREP_OUTS)
    out_specs = [pl.BlockSpec((1, TL, D_B), row)] * 3 + [pl.BlockSpec((1, TL, D_A), row)] * N_PREP_OUTS
    return pl.pallas_call(
        functools.partial(_inproj_kernel, rope=rope),
        grid=(B, L // TL), in_specs=in_specs, out_specs=out_specs, out_shape=out_shape,
        compiler_params=_cparams(
            2,
            block_bytes=(_nbytes((TL + 16, D_MODEL), F32) + _nbytes((D_MODEL, IN_COLS), BF16)
                         + 3 * _nbytes((TL, D_B), F32) * (2 if rope else 1) + N_PREP_OUTS * _nbytes((TL, D_A), F32)),
            temp_bytes=3 * _nbytes((TL + 16, IN_COLS), F32)),
        name="inproj",
    )(*args)


CHUNK = 64
CHUNKS_PER_ITER = 4
_NN = (((1,), (0,)), ((), ()))
_NT = (((1,), (1,)), ((), ()))
_TN = (((0,), (0,)), ((), ()))


def _mm3(a, b, dims):
    ah, al = _split2(a)
    bh, bl = _split2(b)
    f = lambda x, y: lax.dot_general(x, y, dims, preferred_element_type=F32)
    if dims is _TN:
        return f(ah, bh) + f(ah, bl) + f(al, bh)
    m = a.shape[0]
    both = f(jnp.concatenate([ah, al], axis=0), bh)
    return both[0:m] + both[m:2 * m] + f(ah, bl)


def _mm2(a, b, dims):
    ah = a.astype(BF16)
    bh, bl = _split2(b)
    n = b.shape[1]
    both = lax.dot_general(ah, jnp.concatenate([bh, bl], axis=1), dims, preferred_element_type=F32)
    return both[:, 0:n] + both[:, n:2 * n]


def _mm1(a, b, dims):
    return lax.dot_general(a.astype(BF16), b.astype(BF16), dims, preferred_element_type=F32)


def _unit_tri_inverses(ms, row, col):
    eye = (row == col).astype(F32)
    blk = lambda n: (row // n) == (col // n)
    m8 = [jnp.where(blk(8), m, 0.0) for m in ms]
    a2 = [_mm1(x, x, _NN) for x in m8]
    a4 = [_mm1(x, x, _NN) for x in a2]
    p = [eye + x + y + _mm1(x, y, _NN) for x, y in zip(m8, a2)]
    ts = [x + _mm1(x, y, _NN) for x, y in zip(p, a4)]
    n = 8
    while n < CHUNK:
        sel = jnp.logical_and(blk(2 * n), jnp.logical_not(blk(n)))
        ot = [_mm1(jnp.where(sel, m, 0.0), t, _NN) for m, t in zip(ms, ts)]
        ts = [t + _mm1(t, o, _NN) for t, o in zip(ts, ot)]
        n *= 2
    return ts


def _scan_kernel(nkkf_ref, rf_ref, vf_ref, lwf_ref, kaf_ref, kdf_ref,
                 nkkb_ref, rb_ref, vb_ref, lwb_ref, kab_ref, kdb_ref, s0_ref,
                 yf_ref, yb_ref, sfin_ref, s_ref):
    C = CHUNK
    j = pl.program_id(1)
    nj = pl.num_programs(1)
    row = lax.broadcasted_iota(jnp.int32, (C, C), 0)
    col = lax.broadcasted_iota(jnp.int32, (C, C), 1)
    dirs = ((nkkf_ref, rf_ref, vf_ref, lwf_ref, kaf_ref, kdf_ref, yf_ref),
            (nkkb_ref, rb_ref, vb_ref, lwb_ref, kab_ref, kdb_ref, yb_ref))
    n_chunks = nkkf_ref.shape[1] // C

    @pl.when(j == 0)
    def _():
        s_ref[...] = s0_ref[0]

    def build(c, d, probs):
        rows = pl.ds(pl.multiple_of(c * C, C), C)
        nkk_ref, r_ref, v_ref, lw_ref, b_ref, k_ref = dirs[d][0:6]
        alpha = nkk_ref[0, rows, :]
        r = r_ref[0, rows, :]
        v = v_ref[0, rows, :]
        before = (row > col) if d == 0 else (row < col)
        upto = (row >= col) if d == 0 else (row <= col)
        lw = lw_ref[0, rows, :]
        beta = b_ref[0, rows, :]
        kd = k_ref[0, rows, :]
        tri = upto.astype(BF16)
        l_hi, l_mid, l_lo = _split3(lw)
        lam = (jnp.dot(tri, l_hi, preferred_element_type=F32) + jnp.dot(tri, l_mid, preferred_element_type=F32)
               + jnp.dot(tri, l_lo, preferred_element_type=F32))
        lam_prev = lam - lw
        lam_mid = lam[C // 2:C // 2 + 1, :]
        lam_end = lam[C - 1:C, :] if d == 0 else lam[0:1, :]
        e_to_mid = jnp.exp(lam_mid - lam)
        a_m = alpha * jnp.exp(lam_prev - lam_mid)
        b_m = beta * e_to_mid
        k_m = kd * e_to_mid
        r_m = r * jnp.exp(lam - lam_mid)
        a_0 = alpha * jnp.exp(lam_prev)
        r_0 = r * jnp.exp(lam)
        e_to_end = jnp.exp(lam_end - lam)
        b_e = beta * e_to_end
        k_e = kd * e_to_end
        p_end = jnp.exp(lam_end)
        for h in range(H_A):
            sl = slice(h * RW_HEAD, (h + 1) * RW_HEAD)
            probs.append(dict(
                rows=rows, d=d, h=h, before=before, upto=upto, pc=p_end[:, sl],
                ar=jnp.concatenate([a_m[:, sl], r_m[:, sl]], axis=0),
                bk=jnp.concatenate([b_m[:, sl], k_m[:, sl]], axis=0),
                a0=a_0[:, sl], r0=r_0[:, sl], vh=v[:, sl],
                bke=jnp.concatenate([b_e[:, sl], k_e[:, sl]], axis=0)))

    def body(ci, carry):
        probs = []
        for cc in range(CHUNKS_PER_ITER):
            cf = ci * CHUNKS_PER_ITER + cc
            build(cf, 0, probs)
            build(n_chunks - 1 - cf, 1, probs)
        g12 = [_mm1(q['ar'], q['bk'], _NT) for q in probs]
        ms = [jnp.where(q['before'], g[0:C, 0:C], 0.0) for q, g in zip(probs, g12)]
        ns = [jnp.where(q['before'], g[0:C, C:2 * C], 0.0) for q, g in zip(probs, g12)]
        rbk = [jnp.concatenate([jnp.where(q['upto'], g[C:2 * C, 0:C], 0.0),
                                jnp.where(q['upto'], g[C:2 * C, C:2 * C], 0.0)], axis=1) for q, g in zip(probs, g12)]
        nvs = [_mm1(n, q['vh'], _NN) for q, n in zip(probs, ns)]
        ts = _unit_tri_inverses(ms, row, col)
        xs = [_mm2(t, jnp.concatenate([q['a0'], nv], axis=1), _NN) for q, t, nv in zip(probs, ts, nvs)]
        zeros = jnp.zeros((C, RW_HEAD), F32)
        ws = [jnp.concatenate([x, jnp.concatenate([zeros, q['vh']], axis=1)], axis=0) for q, x in zip(probs, xs)]
        rys = [_mm1(rb, w, _NN) for rb, w in zip(rbk, ws)]
        fgs = [_mm3(w, q['bke'], _TN) for q, w in zip(probs, ws)]
        ys = []
        for q, ry, fg in zip(probs, rys, fgs):
            s = s_ref[q['d'], q['h']]
            rp = q['r0'] + ry[:, 0:RW_HEAD]
            ys.append(ry[:, RW_HEAD:2 * RW_HEAD] + _mm1(rp, s, _NT))
            s_ref[q['d'], q['h']] = s * q['pc'] + _mm1(s, fg[0:RW_HEAD], _NN) + fg[RW_HEAD:2 * RW_HEAD]
        for k0 in range(0, len(probs), H_A):
            q0 = probs[k0]
            dirs[q0['d']][6][0, q0['rows'], :] = jnp.concatenate(ys[k0:k0 + H_A], axis=1)
        return carry

    lax.fori_loop(0, n_chunks // CHUNKS_PER_ITER, body, 0)

    @pl.when(j == nj - 1)
    def _():
        sfin_ref[0] = s_ref[...]


def _scan(nkk, r, v, lwf, kaf, kdf, lwb, kab, kdb, s0):
    B, L, _ = r.shape
    nj = L // TL
    tile = pl.BlockSpec((1, TL, D_A), lambda b, j: (b, j, 0))
    rtile = pl.BlockSpec((1, TL, D_A), lambda b, j: (b, nj - 1 - j, 0))
    st = pl.BlockSpec((1, 2, H_A, RW_HEAD, RW_HEAD), lambda b, j: (b, 0, 0, 0, 0))
    f32 = lambda *s: jax.ShapeDtypeStruct(s, F32)
    return pl.pallas_call(
        _scan_kernel, grid=(B, nj),
        in_specs=[tile] * 6 + [rtile] * 6 + [st],
        out_specs=[tile, rtile, st],
        out_shape=[f32(B, L, D_A), f32(B, L, D_A), f32(B, 2, H_A, RW_HEAD, RW_HEAD)],
        scratch_shapes=[pltpu.VMEM((2, H_A, RW_HEAD, RW_HEAD), F32)],
        compiler_params=_cparams(
            2, block_bytes=14 * _nbytes((TL, D_A), F32) + 2 * _nbytes((2, H_A, RW_HEAD, RW_HEAD), F32),
            scratch_bytes=_nbytes((2, H_A, RW_HEAD, RW_HEAD), F32),
            temp_bytes=2 * CHUNKS_PER_ITER * H_A * 12 * _nbytes((2 * CHUNK, 2 * CHUNK), F32)),
        name="rwkv_scan",
    )(nkk, r, v, lwf, kaf, kdf, nkk, r, v, lwb, kab, kdb, s0)


def _row_sum_bf16(e):
    n = e.shape[1] // 128
    acc = None
    for g0 in range(0, n, 4):
        part = e[:, g0 * 128:(g0 + 1) * 128]
        for j in range(g0 + 1, min(g0 + 4, n)):
            part = part + e[:, j * 128:(j + 1) * 128]
        pf = part.astype(F32)
        acc = pf if acc is None else acc + pf
    return jnp.sum(acc, axis=-1, keepdims=True)


def _attn_kernel(*refs, has_cache, lam_init):
    if has_cache:
        q_ref, k_ref, v_ref, ck_ref, cv_ref, lq1, lk1, lq2, lk2, sub_ref, o_ref = refs
    else:
        q_ref, k_ref, v_ref, lq1, lk1, lq2, lk2, sub_ref, o_ref = refs
    l1 = jnp.sum(lq1[...] * lk1[...], axis=-1, keepdims=True)
    l2 = jnp.sum(lq2[...] * lk2[...], axis=-1, keepdims=True)
    lam = jnp.exp(l1) - jnp.exp(l2) + lam_init
    q = q_ref[0].astype(BF16)
    kn = k_ref[0].astype(BF16)
    vn = v_ref[0].astype(BF16)
    if has_cache:
        kc = ck_ref[0, 0].astype(BF16)
        vc = cv_ref[0, 0].astype(BF16)
    dn = (((1,), (1,)), ((), ()))
    def scores(j):
        c0 = j * DIFF_DH
        qh = q[:, c0:c0 + DIFF_DH]
        s_n = lax.dot_general(qh, kn[:, c0:c0 + DIFF_DH], dn, preferred_element_type=F32)
        s_c = lax.dot_general(qh, kc[:, c0:c0 + DIFF_DH], dn, preferred_element_type=F32) if has_cache else None
        return s_n, s_c

    nxt = scores(0)
    for h in range(H_B):
        o = None
        for m in range(2):
            j = h * 2 + m
            s_n, s_c = nxt
            if j + 1 < 2 * H_B:
                nxt = scores(j + 1)
            mx = jnp.max(s_n, axis=-1, keepdims=True)
            if has_cache:
                mx = jnp.maximum(mx, jnp.max(s_c, axis=-1, keepdims=True))
            e_n = jnp.exp2((s_n - mx).astype(BF16))
            den = _row_sum_bf16(e_n)
            if has_cache:
                e_c = jnp.exp2((s_c - mx).astype(BF16))
                den = den + _row_sum_bf16(e_c)
            o_m = jnp.dot(e_n, vn[:, h * 128:(h + 1) * 128], preferred_element_type=F32)
            if has_cache:
                o_m = o_m + jnp.dot(e_c, vc[:, h * 128:(h + 1) * 128], preferred_element_type=F32)
            o = o_m / den if m == 0 else o - o_m * (lam / den)
        o_ref[0, :, h * 128:(h + 1) * 128] = _rms(o, sub_ref[...]) * (1.0 - lam_init)


def _attention(q, k, v, cache, l, p, lam_init):
    B, L, _ = q.shape
    has_cache = cache is not None
    tq = TL
    in_specs = [pl.BlockSpec((1, tq, D_B), lambda b, i: (b, i, 0)),
                pl.BlockSpec((1, L, D_B), lambda b, i: (b, 0, 0)),
                pl.BlockSpec((1, L, D_B), lambda b, i: (b, 0, 0))]
    args = [q, k, v]
    if has_cache:
        ck, cv = cache
        past = ck.shape[2]
        in_specs += [pl.BlockSpec((1, 1, past, D_B), lambda b, i: (b, l, 0, 0))] * 2
        args += [ck, cv]
    small = [_pspec(p, 'diff_lq1', (1, DIFF_DH)), _pspec(p, 'diff_lk1', (1, DIFF_DH)),
             _pspec(p, 'diff_lq2', (1, DIFF_DH)), _pspec(p, 'diff_lk2', (1, DIFF_DH)),
             _pspec(p, 'diff_subln', (1, 2 * DIFF_DH))]
    in_specs += [sp for sp, _ in small]
    args += [a for _, a in small]
    return pl.pallas_call(
        functools.partial(_attn_kernel, has_cache=has_cache, lam_init=lam_init),
        grid=(B, L // tq), in_specs=in_specs,
        out_specs=pl.BlockSpec((1, tq, D_B), lambda b, i: (b, i, 0)),
        out_shape=jax.ShapeDtypeStruct((B, L, D_B), F32),
        compiler_params=_cparams(
            2,
            block_bytes=(2 * _nbytes((tq, D_B), F32) + 2 * _nbytes((L, D_B), q.dtype)
                         + (2 * _nbytes((cache[0].shape[2], D_B), F32) if has_cache else 0)),
            temp_bytes=(2 * (_nbytes((tq, L + TL), F32) + _nbytes((tq, L + TL), BF16)) + 2 * _nbytes((L + TL, D_B), BF16)
                        + 4 * _nbytes((tq, D_B), F32))),
        name="diff_attention",
    )(*args)


def _hfilt_time_kernel(emb_ref, w1_ref, b1_ref, fr_ref, w2_ref, b2_ref, w3_ref, dec_ref, h_ref, ss_ref, acc_ref):
    i = pl.program_id(0)
    emb = emb_ref[...]
    fr = fr_ref[...]
    h = jnp.sin(fr * (jnp.dot(emb.astype(BF16), w1_ref[...].astype(BF16), preferred_element_type=F32) + b1_ref[...]))
    h = jnp.sin(fr * (jnp.dot(h.astype(BF16), w2_ref[...].astype(BF16), preferred_element_type=F32) + b2_ref[...]))
    h = jnp.dot(h.astype(BF16), w3_ref[...].astype(BF16), preferred_element_type=F32)
    h = h * jnp.exp(-emb[:, 0:1] * jnp.abs(dec_ref[...]))
    n = h.shape[0]
    row = lax.broadcasted_iota(jnp.int32, (n, 4 * D_C), 0) + i * n
    col = lax.broadcasted_iota(jnp.int32, (n, 4 * D_C), 1)
    is_bwd = ((col // D_C) % 2) == 1
    h = jnp.where(jnp.logical_and(is_bwd, row == 0), 0.0, h)
    h_ref[...] = h

    @pl.when(i == 0)
    def _():
        acc_ref[...] = jnp.zeros_like(acc_ref)

    acc_ref[...] += jnp.sum(h * h, axis=0, keepdims=True)
    ss_ref[...] = acc_ref[...]


def _hfilt_freq_kernel(h_ref, ss_ref, c_ref, s_ref, hr_ref, hi_ref, a_ref, b_ref, nyq_ref):
    i = pl.program_id(0)

    @pl.when(i == 0)
    def _():
        for o in range(2):
            hf = h_ref[:, o * 2 * D_C:o * 2 * D_C + D_C]
            hb = h_ref[:, o * 2 * D_C + D_C:(o + 1) * 2 * D_C]
            a = hf + hb
            n = a.shape[0]
            alt = 1.0 - 2.0 * (lax.broadcasted_iota(jnp.int32, (n, 1), 0) % 2).astype(F32)
            nyq_ref[:, o * D_C:(o + 1) * D_C] = jnp.sum(a * alt, axis=0, keepdims=True)
            a_ref[:, o * D_C:(o + 1) * D_C] = a.astype(BF16)
            b_ref[:, o * D_C:(o + 1) * D_C] = (hb - hf).astype(BF16)

    ss = ss_ref[...]
    tot = jnp.concatenate([ss[:, 0:D_C] + ss[:, D_C:2 * D_C], ss[:, 2 * D_C:3 * D_C] + ss[:, 3 * D_C:4 * D_C]], axis=1)
    scale = lax.rsqrt(tot + 1e-6)

    hr_ref[...] = jnp.dot(c_ref[...], a_ref[...], preferred_element_type=F32) * scale
    hi = jnp.dot(s_ref[...], b_ref[...], preferred_element_type=F32) * scale
    tf = hi.shape[0]
    row = lax.broadcasted_iota(jnp.int32, (tf, 1), 0) + i * tf
    hi_ref[...] = jnp.where(row == 0, nyq_ref[...] * scale, hi)


def _hyena_filters(L, p, tabs):
    emb, cmat, smat = tabs
    tr = 256
    params = [_pspec(p, 'hy_w1_pad', (128, HY_FFN)), _pspec(p, 'hy_b1', (1, HY_FFN)), _pspec(p, 'hy_freq', (1, HY_FFN)),
              _pspec(p, 'hy_w2', (HY_FFN, HY_FFN)), _pspec(p, 'hy_b2', (1, HY_FFN)),
              _pspec(p, 'hy_w3', (HY_FFN, 4 * D_C)), _pspec(p, 'hy_decay4', (1, 4 * D_C))]
    h_raw, ss = pl.pallas_call(
        _hfilt_time_kernel, grid=(L // tr,),
        in_specs=[pl.BlockSpec((tr, 128), lambda i: (i, 0))] + [sp for sp, _ in params],
        out_specs=[pl.BlockSpec((tr, 4 * D_C), lambda i: (i, 0)), pl.BlockSpec((1, 4 * D_C), lambda i: (0, 0))],
        out_shape=[jax.ShapeDtypeStruct((L, 4 * D_C), F32), jax.ShapeDtypeStruct((1, 4 * D_C), F32)],
        scratch_shapes=[pltpu.VMEM((1, 4 * D_C), F32)],
        compiler_params=_cparams(1, block_bytes=2 * _nbytes((tr, 4 * D_C), F32) + _nbytes((HY_FFN, 4 * D_C), F32),
                                 temp_bytes=3 * _nbytes((tr, 4 * D_C), F32)),
        name="hyena_filter_time",
    )(emb, *[a for _, a in params])
    tf = 256
    mat = pl.BlockSpec((tf, L), lambda i: (i, 0))
    return pl.pallas_call(
        _hfilt_freq_kernel, grid=(L // tf,),
        in_specs=[pl.BlockSpec((L, 4 * D_C), lambda i: (0, 0)), pl.BlockSpec((1, 4 * D_C), lambda i: (0, 0)),
                  mat, mat],
        out_specs=[pl.BlockSpec((tf, 2 * D_C), lambda i: (i, 0))] * 2,
        out_shape=[jax.ShapeDtypeStruct((L, 2 * D_C), F32)] * 2,
        scratch_shapes=[pltpu.VMEM((L, 2 * D_C), BF16)] * 2 + [pltpu.VMEM((1, 2 * D_C), F32)],
        compiler_params=_cparams(
            1, block_bytes=_nbytes((L, 4 * D_C), F32) + 2 * _nbytes((tf, L), BF16) + 2 * _nbytes((tf, 2 * D_C), F32),
            scratch_bytes=2 * _nbytes((L, 2 * D_C), BF16), temp_bytes=4 * _nbytes((L, D_C), F32)),
        name="hyena_filter_freq",
    )(h_raw, ss, cmat, smat)


def _hconv_kernel(z_ref, zt_ref, gate_ref, hr_ref, hi_ref, bias_ref, c_ref, s_ref, o_ref,
                  zb_ref, yr_ref, yi_ref, nyq_ref, *, BG, L):
    ph = pl.program_id(1)
    i = pl.program_id(2)
    inv_n = 1.0 / (2 * L)

    @pl.when(jnp.logical_and(ph == 0, i == 0))
    def _():
        alt = 1.0 - 2.0 * (lax.broadcasted_iota(jnp.int32, (L, 1), 0) % 2).astype(F32)
        for b in range(BG):
            z = z_ref[b]
            zb_ref[:, b * D_C:(b + 1) * D_C] = z.astype(BF16)
            nyq_ref[:, b * D_C:(b + 1) * D_C] = jnp.sum(z * alt, axis=0, keepdims=True)

    tf = c_ref.shape[0]

    @pl.when(ph == 0)
    def _():
        zr = jnp.dot(c_ref[...], zb_ref[...], preferred_element_type=F32)
        zs = jnp.dot(s_ref[...], zb_ref[...], preferred_element_type=F32)
        hr = hr_ref[...]
        hi = hi_ref[...]
        row = lax.broadcasted_iota(jnp.int32, (tf, 1), 0) + i * tf
        wgt = jnp.where(row == 0, inv_n, 2.0 * inv_n)
        rows = pl.ds(pl.multiple_of(i * tf, tf), tf)
        for b in range(BG):
            sl = slice(b * D_C, (b + 1) * D_C)
            yr_ref[rows, sl] = ((zr[:, sl] * hr + zs[:, sl] * hi) * wgt).astype(BF16)
            yi_ref[rows, sl] = ((zr[:, sl] * hi - zs[:, sl] * hr) * (2.0 * inv_n)).astype(BF16)

        @pl.when(i == 0)
        def _():
            for b in range(BG):
                sl = slice(b * D_C, (b + 1) * D_C)
                nyq_ref[:, sl] = nyq_ref[:, sl] * hi[0:1, :] * inv_n

    @pl.when(ph == 1)
    def _():
        y = (jnp.dot(c_ref[...], yr_ref[...], preferred_element_type=F32)
             - jnp.dot(s_ref[...], yi_ref[...], preferred_element_type=F32))
        row = lax.broadcasted_iota(jnp.int32, (tf, 1), 0) + i * tf
        alt = 1.0 - 2.0 * (row % 2).astype(F32)
        y = y + alt * nyq_ref[...]
        for b in range(BG):
            sl = slice(b * D_C, (b + 1) * D_C)
            o_ref[b] = gate_ref[b] * (y[:, sl] + bias_ref[...] * zt_ref[b])


def _hconv(z, gate, hr, hi, p, order, tabs):
    B, L, _ = z.shape
    if isinstance(p, _LayerParams):
        layer = p.layer
        bias_arg = p.stacked['hy_bias'].reshape(DEPTH, 2, 1, D_C)
        bias_spec = pl.BlockSpec((None, None, 1, D_C), lambda g, ph, i: (layer, order, 0, 0))
    else:
        bias_arg = p['hy_bias'][order].reshape(1, D_C)
        bias_spec = pl.BlockSpec((1, D_C), lambda g, ph, i: (0, 0))
    _, ch, sh = tabs
    BG =min(B, 8) if L <= 256 else min(B, 4)
    assert B % BG == 0
    tf = 256
    nt = L // tf
    mat = pl.BlockSpec((tf, L), lambda g, ph, i: (i, 0))
    tile = pl.BlockSpec((BG, tf, D_C), lambda g, ph, i: (g, i * ph, 0))
    hspec = pl.BlockSpec((tf, D_C), lambda g, ph, i: (i * (1 - ph), order))
    cols = BG * D_C
    return pl.pallas_call(
        functools.partial(_hconv_kernel, BG=BG, L=L), grid=(B // BG, 2, nt),
        in_specs=[pl.BlockSpec((BG, L, D_C), lambda g, ph, i: (g, 0, 0)), tile, tile, hspec, hspec,
                  bias_spec, mat, mat],
        out_specs=tile,
        out_shape=jax.ShapeDtypeStruct((B, L, D_C), F32),
        scratch_shapes=[pltpu.VMEM((L, cols), BF16)] * 3 + [pltpu.VMEM((1, cols), F32)],
        compiler_params=_cparams(
            3,
            block_bytes=(_nbytes((BG, L, D_C), F32) + 3 * _nbytes((BG, tf, D_C), F32) + 2 * _nbytes((tf, L), BF16)
                         + 2 * _nbytes((tf, D_C), F32)),
            scratch_bytes=3 * _nbytes((L, cols), BF16), temp_bytes=4 * _nbytes((tf, cols), F32)),
        name="hyena_conv",
    )(z, z, gate, hr, hi, bias_arg, ch, sh)


def _outffn_kernel(x_ref, yf_ref, yb_ref, bon_ref, g_ref, ob_ref, yc_ref, mod_ref,
                   lnw_ref, lnb_ref, gpost_ref, gffn_ref, gffn_post_ref, w_ref, w1_ref, w2_ref, o_ref):
    bd = _block_diag(D_A, RW_HEAD)
    y = yf_ref[0] + yb_ref[0]
    mu = _segsum(y, bd) * (1.0 / RW_HEAD)
    yc = y - mu
    var = _segsum(yc * yc, bd) * (1.0 / RW_HEAD)
    yn = yc * lax.rsqrt(var + GN_EPS) * lnw_ref[...] + lnb_ref[...]
    ya = (yn + bon_ref[0]) * g_ref[0]
    mix = (jnp.dot(ya.astype(BF16), w_ref[0:D_A, :], preferred_element_type=F32)
           + jnp.dot(ob_ref[0].astype(BF16), w_ref[D_A:D_A + D_B, :], preferred_element_type=F32)
           + jnp.dot(yc_ref[0].astype(BF16), w_ref[D_A + D_B:D_MODEL, :], preferred_element_type=F32))
    mod = mod_ref[0]
    gt1 = mod[:, 2 * D_MODEL:3 * D_MODEL]
    sh2 = mod[:, 3 * D_MODEL:4 * D_MODEL]
    sc2 = mod[:, 4 * D_MODEL:5 * D_MODEL]
    gt2 = mod[:, 5 * D_MODEL:6 * D_MODEL]
    x1 = x_ref[0] + gt1 * _rms(mix, gpost_ref[...])
    h = (_rms(x1, gffn_ref[...]) * (1.0 + sc2) + sh2).astype(BF16)
    acc = None
    for c in range(D_FF // FFN_CHUNK):
        a = jnp.dot(h, w1_ref[:, c * FFN_CHUNK:(c + 1) * FFN_CHUNK], preferred_element_type=F32)
        a = jnp.square(jnp.maximum(a, 0.0)).astype(BF16)
        part = jnp.dot(a, w2_ref[c * FFN_CHUNK:(c + 1) * FFN_CHUNK, :], preferred_element_type=F32)
        acc = part if acc is None else acc + part
    o_ref[0] = x1 + gt2 * _rms(acc, gffn_post_ref[...])


def _outffn(x, yf, yb, bon, g, ob, yc, mod, p, w_out, w1, w2):
    B, L, _ = x.shape
    mb = mod.shape[0]
    row = lambda b, i: (b, i, 0)
    c2 = lambda b, i: (0, 0)
    t = lambda n: pl.BlockSpec((1, TL, n), row)
    vecs = [_pspec(p, 'rwkv_ln_w', (1, D_A)), _pspec(p, 'rwkv_ln_b', (1, D_A)), _pspec(p, 'g_mix_post', (1, D_MODEL)),
            _pspec(p, 'g_ffn_pre', (1, D_MODEL)), _pspec(p, 'g_ffn_post', (1, D_MODEL))]
    in_specs = [t(D_MODEL), t(D_A), t(D_A), t(D_A), t(D_A), t(D_B), t(D_C),
                pl.BlockSpec((1, 1, 6 * D_MODEL), (lambda b, i: (b, 0, 0)) if mb > 1 else (lambda b, i: (0, 0, 0)))]
    in_specs += [sp for sp, _ in vecs]
    in_specs += [_layer_weight_spec(w_out, 2), _layer_weight_spec(w1, 2), _layer_weight_spec(w2, 2)]
    return pl.pallas_call(
        _outffn_kernel, grid=(B, L // TL), in_specs=in_specs,
        out_specs=t(D_MODEL), out_shape=jax.ShapeDtypeStruct((B, L, D_MODEL), F32),
        compiler_params=_cparams(
            2,
            block_bytes=(2 * _nbytes((TL, D_MODEL), F32) + 4 * _nbytes((TL, D_A), F32) + _nbytes((TL, D_B), F32)
                         + _nbytes((TL, D_C), F32) + _nbytes((D_MODEL, D_MODEL), BF16) + 2 * _nbytes((D_MODEL, D_FF), BF16)),
            temp_bytes=4 * _nbytes((TL, D_MODEL), F32) + 2 * _nbytes((TL, FFN_CHUNK), F32)),
        name="outproj_ffn",
    )(x, yf, yb, bon, g, ob, yc, mod, *[a for _, a in vecs], w_out[0], w1[0], w2[0])


@functools.lru_cache(maxsize=None)
def _rope_tables(L):
    rows = L // GRID_W
    row = np.repeat(np.arange(rows), GRID_W).astype(np.float64)
    col = np.tile(np.arange(GRID_W), rows).astype(np.float64)
    half = DIFF_DH // 2
    inv = ROPE_BASE ** (-np.arange(0, half, 2, dtype=np.float64) / half)
    ang_r = row[:, None] * inv[None]
    ang_c = col[:, None] * inv[None]
    ang = np.concatenate([ang_r, ang_r, ang_c, ang_c], axis=-1)
    cos, sin = np.cos(ang), np.sin(ang)
    quarter = (np.arange(DIFF_DH) // 16) % 2
    sin_a = np.where(quarter == 0, -sin, 0.0)
    sin_b = np.where(quarter == 1, sin, 0.0)
    rep = D_B // DIFF_DH
    return tuple(np.tile(t, (1, rep)).astype(np.float32) for t in (cos, sin_a, sin_b))


@functools.lru_cache(maxsize=None)
def _hyena_tables(L):
    t = np.linspace(0.0, 1.0, L, dtype=np.float32).astype(np.float64)[:, None]
    ang = (2.0 * math.pi / L) * np.arange(L, dtype=np.float64)[:, None]
    bands = np.linspace(1e-4, HY_BANDS - 1, HY_BANDS, dtype=np.float32).astype(np.float64)[None, :]
    emb = np.concatenate([t, np.cos(bands * ang), -np.sin(bands * ang)], axis=-1)
    emb = np.pad(emb, ((0, 0), (0, 128 - HY_EMB))).astype(np.float32)
    n = np.arange(L, dtype=np.int64)
    theta = ((n[:, None] * n[None, :]) % (2 * L)).astype(np.float64) * (math.pi / L)
    return emb, np.cos(theta).astype(BF16), np.sin(theta).astype(BF16)


def _layer(x, mod, p, wb, l, lam_init, cache, rope_tabs, hy_tabs, filt):
    B, L, _ = x.shape
    (uq, uk, uv, r, v, nkk, wf, kaf, kdf, wbk, kab, kdb, g, bon, x1h, x2h, hv) = _inproj(
        x, mod, wb['w_in'], rope_tabs, p)
    if cache is None:
        s0 = jnp.zeros((B, 2, H_A, RW_HEAD, RW_HEAD), F32)
        kv_cache = None
    else:
        s0 = cache[0]
        kv_cache = (cache[1], cache[2])
    yf, yb, sfin = _scan(nkk, r, v, wf, kaf, kdf, wbk, kab, kdb, s0)
    ob = _attention(uq, uk, uv, kv_cache, l, p, lam_init)
    hr, hi = filt
    z1 = _hconv(hv, x1h, hr, hi, p, 0, hy_tabs)
    yc = _hconv(z1, x2h, hr, hi, p, 1, hy_tabs)
    x2 = _outffn(x, yf, yb, bon, g, ob, yc, mod, p, wb['w_out'], wb['w_ff1'], wb['w_ff2'])
    return x2, (sfin, uk, uv)


def _derive_params(d):
    w1 = d['hy_w1']
    d['hy_w1_pad'] = jnp.pad(w1, [(0, 0)] * (w1.ndim - 2) + [(0, 128 - HY_EMB), (0, 0)])
    dec = d['hy_decay']
    d['hy_decay4'] = jnp.tile(dec, (1,) * (dec.ndim - 1) + (4,))
    return d


_LAYER_KEYS = ('g_mix_pre', 'g_mix_post', 'g_ffn_pre', 'g_ffn_post', 'rwkv_conv', 'rwkv_w0', 'rwkv_w2', 'rwkv_a0',
               'rwkv_a2', 'rwkv_g2', 'rwkv_kk', 'rwkv_ka', 'rwkv_rk', 'rwkv_ln_w', 'rwkv_ln_b', 'diff_lq1', 'diff_lk1',
               'diff_lq2', 'diff_lk2', 'diff_subln', 'hy_conv_w', 'hy_conv_b', 'hy_w1', 'hy_b1', 'hy_freq', 'hy_w2',
               'hy_b2', 'hy_w3', 'hy_decay', 'hy_bias')


def kernel(x_prompt, x_sample, state_rwkv, cache_k, cache_v, c, c_ctx, w_mod, b_mod, g_mix_pre, g_mix_post, g_ffn_pre, g_ffn_post, w_in, rwkv_conv, rwkv_w0, rwkv_w2, rwkv_a0, rwkv_a2, rwkv_g2, rwkv_kk, rwkv_ka, rwkv_rk, rwkv_ln_w, rwkv_ln_b, diff_lq1, diff_lk1, diff_lq2, diff_lk2, diff_subln, hy_conv_w, hy_conv_b, hy_w1, hy_b1, hy_freq, hy_w2, hy_b2, hy_w3, hy_decay, hy_bias, w_out, w_ff1, w_ff2):
    stacked = dict(g_mix_pre=g_mix_pre, g_mix_post=g_mix_post, g_ffn_pre=g_ffn_pre, g_ffn_post=g_ffn_post,
                   rwkv_conv=rwkv_conv, rwkv_w0=rwkv_w0, rwkv_w2=rwkv_w2, rwkv_a0=rwkv_a0, rwkv_a2=rwkv_a2,
                   rwkv_g2=rwkv_g2, rwkv_kk=rwkv_kk, rwkv_ka=rwkv_ka, rwkv_rk=rwkv_rk, rwkv_ln_w=rwkv_ln_w,
                   rwkv_ln_b=rwkv_ln_b, diff_lq1=diff_lq1, diff_lk1=diff_lk1, diff_lq2=diff_lq2, diff_lk2=diff_lk2,
                   diff_subln=diff_subln, hy_conv_w=hy_conv_w, hy_conv_b=hy_conv_b, hy_w1=hy_w1, hy_b1=hy_b1,
                   hy_freq=hy_freq, hy_w2=hy_w2, hy_b2=hy_b2, hy_w3=hy_w3, hy_decay=hy_decay, hy_bias=hy_bias)
    _derive_params(stacked)
    Bc, Lc, _ = x_prompt.shape
    Bs, Ls, _ = x_sample.shape
    past = cache_k.shape[2]

    cond = jnp.zeros((8, D_MODEL), F32).at[0:Bs].set(c).at[Bs].set(c_ctx)
    mod_all = _modulation(cond, w_mod, b_mod)

    rope_tabs = _rope_tables(Ls)
    tabs_c = _hyena_tables(Lc)
    tabs_s = _hyena_tables(Ls)
    ck = cache_k.reshape(Bs, DEPTH, past, D_B)
    cv = cache_v.reshape(Bs, DEPTH, past, D_B)

    w_in_b, w_out_b, w_ff1_b, w_ff2_b = (w.astype(BF16) for w in (w_in, w_out, w_ff1, w_ff2))
    xp, xs = x_prompt, x_sample
    st_list, k_list, v_list = [], [], []
    for l in range(DEPTH):
        p = _LayerParams(stacked, l)
        wb = dict(w_in=(w_in_b, l), w_out=(w_out_b, l), w_ff1=(w_ff1_b, l), w_ff2=(w_ff2_b, l))
        lam_init = 0.8 - 0.6 * math.exp(-0.3 * l)
        mod_lat = mod_all[l, 0:Bs].reshape(Bs, 1, 6 * D_MODEL)
        mod_ctx = mod_all[l, Bs:Bs + 1].reshape(1, 1, 6 * D_MODEL)
        filt_c = _hyena_filters(Lc, p, tabs_c)
        filt_s = _hyena_filters(Ls, p, tabs_s)
        xp, (s_ctx, k_ctx, v_ctx) = _layer(xp, mod_ctx, p, wb, l, lam_init, None, None, tabs_c, filt_c)
        st_list.append(s_ctx)
        k_list.append(k_ctx)
        v_list.append(v_ctx)
        s0 = state_rwkv[:, l]
        xs, _ = _layer(xs, mod_lat, p, wb, l, lam_init, (s0, ck, cv), rope_tabs, tabs_s, filt_s)
    new_k = jnp.stack(k_list, axis=1).reshape(Bc, DEPTH, Lc, H_B, 2, DIFF_DH)
    new_v = jnp.stack(v_list, axis=1).reshape(Bc, DEPTH, Lc, H_B, 2 * DIFF_DH)
    return (xp, xs, jnp.stack(st_list, axis=1), new_k, new_v)
```

```python
import functools
import math

import jax
import jax.numpy as jnp
import numpy as np
from jax import lax
from jax.experimental import pallas as pl
from jax.experimental.pallas import tpu as pltpu

F32 = jnp.float32
BF16 = jnp.bfloat16

D_MODEL = 1024
DEPTH = 4
GRID_W = 64
D_A = 256
RW_HEAD = 64
H_A = 4
D_B = 512
DIFF_DH = 64
H_B = 4
D_C = 256
HY_BANDS = 16
HY_EMB = 33
HY_FFN = 64
D_FF = 4096
FFN_CHUNK = 1024
ROPE_BASE = 10000.0
RMS_EPS = 1e-6
GN_EPS = 64e-5
N_A_COLS = 1152
IN_COLS = 3456

TL = 256
MIB = 1024 * 1024
V7X_VMEM_BYTES = 64 * MIB
V7X_VMEM_CEILING = V7X_VMEM_BYTES * 7 // 8
VMEM_FLOOR = 48 * MIB


def _nbytes(shape, dtype):
    return math.prod(shape) * jnp.dtype(dtype).itemsize


def _cparams(n_axes, block_bytes=0, scratch_bytes=0, temp_bytes=0):
    need = 2 * block_bytes + scratch_bytes + temp_bytes
    return pltpu.CompilerParams(dimension_semantics=("arbitrary",) * n_axes,
                                vmem_limit_bytes=int(min(V7X_VMEM_CEILING, max(VMEM_FLOOR, need))))


def _split3(a):
    hi = a.astype(BF16)
    r1 = a - hi.astype(F32)
    mid = r1.astype(BF16)
    lo = (r1 - mid.astype(F32)).astype(BF16)
    return hi, mid, lo


def _split2(a):
    hi = a.astype(BF16)
    lo = (a - hi.astype(F32)).astype(BF16)
    return hi, lo


def _segsum(a, bd):
    hi, lo = _split2(a)
    return jnp.dot(hi, bd, preferred_element_type=F32) + jnp.dot(lo, bd, preferred_element_type=F32)


def _block_diag(n, seg):
    r = lax.broadcasted_iota(jnp.int32, (n, n), 0) // seg
    c = lax.broadcasted_iota(jnp.int32, (n, n), 1) // seg
    return (r == c).astype(BF16)


class _LayerParams:
    def __init__(self, stacked, layer):
        self.stacked = stacked
        self.layer = layer

    def __getitem__(self, name):
        return self.stacked[name][self.layer]


def _pspec(p, name, shape):
    if isinstance(p, _LayerParams):
        stack = p.stacked[name]
        stack = stack.reshape((stack.shape[0],) + shape)
        layer = p.layer
        return pl.BlockSpec((None,) + shape, lambda *g: (layer,) + (0,) * len(shape)), stack
    return pl.BlockSpec(shape, lambda *g: (0,) * len(shape)), p[name].reshape(shape)


def _layer_weight_spec(w, n_grid_axes):
    stack, layer = w
    _, k, n = stack.shape
    if n_grid_axes == 2:
        return pl.BlockSpec((None, k, n), lambda b, i: (layer, 0, 0))
    raise NotImplementedError(n_grid_axes)


def _rms(x, g):
    return x * lax.rsqrt(jnp.mean(x * x, axis=-1, keepdims=True) + RMS_EPS) * g


def _mod_kernel(cond_ref, w_ref, b_ref, o_ref):
    c = cond_ref[...]
    s = (c * jax.nn.sigmoid(c)).astype(BF16)
    o_ref[0] = jnp.dot(s, w_ref[0].astype(BF16), preferred_element_type=F32) + b_ref[0]


def _modulation(cond, w_mod, b_mod):
    tn = 1536
    return pl.pallas_call(
        _mod_kernel,
        grid=(DEPTH, 6 * D_MODEL // tn),
        in_specs=[pl.BlockSpec((8, D_MODEL), lambda l, n: (0, 0)),
                  pl.BlockSpec((1, D_MODEL, tn), lambda l, n: (l, 0, n)),
                  pl.BlockSpec((1, 1, tn), lambda l, n: (l, 0, n))],
        out_specs=pl.BlockSpec((1, 8, tn), lambda l, n: (l, 0, n)),
        out_shape=jax.ShapeDtypeStruct((DEPTH, 8, 6 * D_MODEL), F32),
        compiler_params=_cparams(2, block_bytes=_nbytes((D_MODEL + 16, tn), F32),
                                 temp_bytes=_nbytes((D_MODEL, tn), BF16)),
        name="modulation",
    )(cond, w_mod, b_mod.reshape(DEPTH, 1, 6 * D_MODEL))


def _rope(x, cos, sin_a, sin_b):
    n = x.shape[-1]
    return x * cos + pltpu.roll(x, n - 16, axis=1) * sin_a + pltpu.roll(x, 16, axis=1) * sin_b


def _shift_conv(u, prev_row, next_row, cw):
    n = u.shape[0]
    row = lax.broadcasted_iota(jnp.int32, (n, 1), 0)
    up = jnp.where(row == 0, prev_row, pltpu.roll(u, 1, axis=0))
    un = jnp.where(row == n - 1, next_row, pltpu.roll(u, n - 1, axis=0))
    return cw[0:1] * up + cw[1:2] * u + cw[2:3] * un


def _prep_body(ua, prev, nxt, uh, prevh, nxth,
               cw_ref, w0_ref, w2_ref, a0_ref, a2_ref, g2_ref, kkw_ref, kaw_ref, rk_ref, hcw_ref, hcb_ref,
               r_ref, v_ref, nkk_ref, wf_ref, kaf_ref, kdf_ref, wb_ref, kab_ref, kdb_ref, g_ref, bon_ref,
               x1_ref, x2_ref, hv_ref):
    bd = _block_diag(D_A, RW_HEAD)
    u_rkv = ua[:, 0:3 * D_A]
    rkv = _shift_conv(u_rkv, prev, nxt, cw_ref[...])
    r = rkv[:, 0:D_A]
    k = rkv[:, D_A:2 * D_A]
    v = rkv[:, 2 * D_A:3 * D_A]
    u_w = ua[:, 768:896]
    u_a = ua[:, 896:1024]
    u_g = ua[:, 1024:1152]

    g = jnp.dot(jax.nn.sigmoid(u_g).astype(BF16), g2_ref[...].astype(BF16), preferred_element_type=F32)
    kk = k * kkw_ref[...]
    kk = kk * lax.rsqrt(_segsum(kk * kk, bd) + 1e-12)
    r_ref[0] = r
    v_ref[0] = v
    nkk_ref[0] = -kk
    g_ref[0] = g

    kd_sum = None
    outs = ((wf_ref, kaf_ref, kdf_ref), (wb_ref, kab_ref, kdb_ref))
    for d in range(2):
        uw_d = u_w[:, d * 64:(d + 1) * 64]
        ua_d = u_a[:, d * 64:(d + 1) * 64]
        xw = w0_ref[d:d + 1, :] + jnp.dot(jnp.tanh(uw_d).astype(BF16), w2_ref[d].astype(BF16),
                                          preferred_element_type=F32)
        z = -xw
        softplus = jnp.maximum(z, 0.0) + jnp.log(1.0 + jnp.exp(-jnp.abs(z)))
        wl = -softplus - 0.5
        log_decay = -jnp.exp(wl)
        a = jax.nn.sigmoid(a0_ref[d:d + 1, :] + jnp.dot(ua_d.astype(BF16), a2_ref[d].astype(BF16),
                                                        preferred_element_type=F32))
        kd = k * (1.0 + (a - 1.0) * kaw_ref[...])
        w_o, ka_o, kd_o = outs[d]
        w_o[0] = log_decay
        ka_o[0] = kk * a
        kd_o[0] = kd
        kd_sum = kd if kd_sum is None else kd_sum + kd
    bon_ref[0] = _segsum(r * kd_sum * rk_ref[...], bd) * v

    hc = _shift_conv(uh, prevh, nxth, hcw_ref[...]) + hcb_ref[...]
    x1_ref[0] = hc[:, 0:D_C]
    x2_ref[0] = hc[:, D_C:2 * D_C]
    hv_ref[0] = hc[:, 2 * D_C:3 * D_C]


N_PREP_PARAMS = 11
N_PREP_OUTS = 14


def _inproj_kernel(*refs, rope):
    n_in = 6 + (3 if rope else 0)
    x_ref, xp_ref, xn_ref, mod_ref, g_ref, w_ref = refs[0:6]
    rope_refs = refs[6:n_in]
    prep_params = refs[n_in:n_in + N_PREP_PARAMS]
    uq_ref, uk_ref, uv_ref = refs[n_in + N_PREP_PARAMS:n_in + N_PREP_PARAMS + 3]
    prep_outs = refs[n_in + N_PREP_PARAMS + 3:]
    i = pl.program_id(1)
    nt = pl.num_programs(1)
    mod = mod_ref[0]
    sh1 = mod[:, 0:D_MODEL]
    sc1 = mod[:, D_MODEL:2 * D_MODEL]

    def normed(x):
        return (_rms(x, g_ref[...]) * (1.0 + sc1) + sh1).astype(BF16)

    n = x_ref.shape[1]
    h_all = jnp.concatenate([normed(x_ref[0]), normed(xp_ref[0]), normed(xn_ref[0])], axis=0)
    u_all = jnp.dot(h_all, w_ref[...], preferred_element_type=F32)
    u = u_all[0:n]
    q = u[:, N_A_COLS:N_A_COLS + D_B] * (DIFF_DH ** -0.5 * math.log2(math.e))
    k = u[:, N_A_COLS + D_B:N_A_COLS + 2 * D_B]
    if rope:
        cos_ref, sa_ref, sb_ref = rope_refs
        q = _rope(q, cos_ref[...], sa_ref[...], sb_ref[...])
        k = _rope(k, cos_ref[...], sa_ref[...], sb_ref[...])
    uq_ref[0] = q.astype(uq_ref.dtype)
    uk_ref[0] = k.astype(uk_ref.dtype)
    uv_ref[0] = u[:, N_A_COLS + 2 * D_B:N_A_COLS + 3 * D_B].astype(uv_ref.dtype)

    hy0 = N_A_COLS + 3 * D_B
    prev_row = u_all[n + 7:n + 8]
    next_row = u_all[n + 8:n + 9]
    has_prev = i > 0
    has_next = i < nt - 1
    _prep_body(u[:, 0:N_A_COLS], jnp.where(has_prev, prev_row[:, 0:3 * D_A], 0.0),
               jnp.where(has_next, next_row[:, 0:3 * D_A], 0.0),
               u[:, hy0:IN_COLS], jnp.where(has_prev, prev_row[:, hy0:IN_COLS], 0.0),
               jnp.where(has_next, next_row[:, hy0:IN_COLS], 0.0),
               *prep_params, *prep_outs)


def _inproj(x, mod, w_in, rope_tabs, p):
    B, L, _ = x.shape
    rope = rope_tabs is not None
    mb = mod.shape[0]
    nb8 = L // 8
    t8 = TL // 8
    row = lambda b, i: (b, i, 0)
    prev = lambda b, i: (b, jnp.maximum(i * t8 - 1, 0), 0)
    nxt = lambda b, i: (b, jnp.minimum((i + 1) * t8, nb8 - 1), 0)
    const2 = lambda b, i: (0, 0)

    params = [_pspec(p, 'rwkv_conv', (3, 3 * D_A)), _pspec(p, 'rwkv_w0', (2, D_A)),
              _pspec(p, 'rwkv_w2', (2, 64, D_A)), _pspec(p, 'rwkv_a0', (2, D_A)), _pspec(p, 'rwkv_a2', (2, 64, D_A)),
              _pspec(p, 'rwkv_g2', (128, D_A)), _pspec(p, 'rwkv_kk', (1, D_A)), _pspec(p, 'rwkv_ka', (1, D_A)),
              _pspec(p, 'rwkv_rk', (1, D_A)), _pspec(p, 'hy_conv_w', (3, 3 * D_C)),
              _pspec(p, 'hy_conv_b', (1, 3 * D_C))]
    assert len(params) == N_PREP_PARAMS
    g_spec, g_arg = _pspec(p, 'g_mix_pre', (1, D_MODEL))
    in_specs = [pl.BlockSpec((1, TL, D_MODEL), row), pl.BlockSpec((1, 8, D_MODEL), prev),
                pl.BlockSpec((1, 8, D_MODEL), nxt),
                pl.BlockSpec((1, 1, 6 * D_MODEL), (lambda b, i: (b, 0, 0)) if mb > 1 else (lambda b, i: (0, 0, 0))),
                g_spec,
                _layer_weight_spec(w_in, 2)]
    args = [x, x, x, mod, g_arg, w_in[0]]
    if rope:
        in_specs += [pl.BlockSpec((TL, D_B), lambda b, i: (i, 0))] * 3
        args += list(rope_tabs)
    in_specs += [s for s, _ in params]
    args += [a for _, a in params]
    qkv_dt = BF16 if rope else F32
    out_shape = ([jax.ShapeDtypeStruct((B, L, D_B), qkv_dt)] * 3
                 + [jax.ShapeDtypeStruct((B, L, D_A), F32)] * N_PREP_OUTS)
    out_specs = [pl.BlockSpec((1, TL, D_B), row)] * 3 + [pl.BlockSpec((1, TL, D_A), row)] * N_PREP_OUTS
    return pl.pallas_call(
        functools.partial(_inproj_kernel, rope=rope),
        grid=(B, L // TL), in_specs=in_specs, out_specs=out_specs, out_shape=out_shape,
        compiler_params=_cparams(
            2,
            block_bytes=(_nbytes((TL + 16, D_MODEL), F32) + _nbytes((D_MODEL, IN_COLS), BF16)
                         + 3 * _nbytes((TL, D_B), F32) * (2 if rope else 1) + N_PREP_OUTS * _nbytes((TL, D_A), F32)),
            temp_bytes=3 * _nbytes((TL + 16, IN_COLS), F32)),
        name="inproj",
    )(*args)


CHUNK = 64
CHUNKS_PER_ITER = 4
_NN = (((1,), (0,)), ((), ()))
_NT = (((1,), (1,)), ((), ()))
_TN = (((0,), (0,)), ((), ()))


def _mm3(a, b, dims):
    ah, al = _split2(a)
    bh, bl = _split2(b)
    f = lambda x, y: lax.dot_general(x, y, dims, preferred_element_type=F32)
    if dims is _TN:
        return f(ah, bh) + f(ah, bl) + f(al, bh)
    m = a.shape[0]
    both = f(jnp.concatenate([ah, al], axis=0), bh)
    return both[0:m] + both[m:2 * m] + f(ah, bl)


def _mm2(a, b, dims):
    ah = a.astype(BF16)
    bh, bl = _split2(b)
    n = b.shape[1]
    both = lax.dot_general(ah, jnp.concatenate([bh, bl], axis=1), dims, preferred_element_type=F32)
    return both[:, 0:n] + both[:, n:2 * n]


def _mm1(a, b, dims):
    return lax.dot_general(a.astype(BF16), b.astype(BF16), dims, preferred_element_type=F32)


def _unit_tri_inverses(ms, row, col):
    eye = (row == col).astype(F32)
    blk = lambda n: (row // n) == (col // n)
    m8 = [jnp.where(blk(8), m, 0.0) for m in ms]
    a2 = [_mm1(x, x, _NN) for x in m8]
    a4 = [_mm1(x, x, _NN) for x in a2]
    p = [eye + x + y + _mm1(x, y, _NN) for x, y in zip(m8, a2)]
    ts = [x + _mm1(x, y, _NN) for x, y in zip(p, a4)]
    n = 8
    while n < CHUNK:
        sel = jnp.logical_and(blk(2 * n), jnp.logical_not(blk(n)))
        ot = [_mm1(jnp.where(sel, m, 0.0), t, _NN) for m, t in zip(ms, ts)]
        ts = [t + _mm1(t, o, _NN) for t, o in zip(ts, ot)]
        n *= 2
    return ts


def _scan_kernel(nkkf_ref, rf_ref, vf_ref, lwf_ref, kaf_ref, kdf_ref,
                 nkkb_ref, rb_ref, vb_ref, lwb_ref, kab_ref, kdb_ref, s0_ref,
                 yf_ref, yb_ref, sfin_ref, s_ref):
    C = CHUNK
    j = pl.program_id(1)
    nj = pl.num_programs(1)
    row = lax.broadcasted_iota(jnp.int32, (C, C), 0)
    col = lax.broadcasted_iota(jnp.int32, (C, C), 1)
    dirs = ((nkkf_ref, rf_ref, vf_ref, lwf_ref, kaf_ref, kdf_ref, yf_ref),
            (nkkb_ref, rb_ref, vb_ref, lwb_ref, kab_ref, kdb_ref, yb_ref))
    n_chunks = nkkf_ref.shape[1] // C

    @pl.when(j == 0)
    def _():
        s_ref[...] = s0_ref[0]

    def build(c, d, probs):
        rows = pl.ds(pl.multiple_of(c * C, C), C)
        nkk_ref, r_ref, v_ref, lw_ref, b_ref, k_ref = dirs[d][0:6]
        alpha = nkk_ref[0, rows, :]
        r = r_ref[0, rows, :]
        v = v_ref[0, rows, :]
        before = (row > col) if d == 0 else (row < col)
        upto = (row >= col) if d == 0 else (row <= col)
        lw = lw_ref[0, rows, :]
        beta = b_ref[0, rows, :]
        kd = k_ref[0, rows, :]
        tri = upto.astype(BF16)
        l_hi, l_mid, l_lo = _split3(lw)
        lam = (jnp.dot(tri, l_hi, preferred_element_type=F32) + jnp.dot(tri, l_mid, preferred_element_type=F32)
               + jnp.dot(tri, l_lo, preferred_element_type=F32))
        lam_prev = lam - lw
        lam_mid = lam[C // 2:C // 2 + 1, :]
        lam_end = lam[C - 1:C, :] if d == 0 else lam[0:1, :]
        e_to_mid = jnp.exp(lam_mid - lam)
        a_m = alpha * jnp.exp(lam_prev - lam_mid)
        b_m = beta * e_to_mid
        k_m = kd * e_to_mid
        r_m = r * jnp.exp(lam - lam_mid)
        a_0 = alpha * jnp.exp(lam_prev)
        r_0 = r * jnp.exp(lam)
        e_to_end = jnp.exp(lam_end - lam)
        b_e = beta * e_to_end
        k_e = kd * e_to_end
        p_end = jnp.exp(lam_end)
        for h in range(H_A):
            sl = slice(h * RW_HEAD, (h + 1) * RW_HEAD)
            probs.append(dict(
                rows=rows, d=d, h=h, before=before, upto=upto, pc=p_end[:, sl],
                ar=jnp.concatenate([a_m[:, sl], r_m[:, sl]], axis=0),
                bk=jnp.concatenate([b_m[:, sl], k_m[:, sl]], axis=0),
                a0=a_0[:, sl], r0=r_0[:, sl], vh=v[:, sl],
                bke=jnp.concatenate([b_e[:, sl], k_e[:, sl]], axis=0)))

    def body(ci, carry):
        probs = []
        for cc in range(CHUNKS_PER_ITER):
            cf = ci * CHUNKS_PER_ITER + cc
            build(cf, 0, probs)
            build(n_chunks - 1 - cf, 1, probs)
        g12 = [_mm1(q['ar'], q['bk'], _NT) for q in probs]
        ms = [jnp.where(q['before'], g[0:C, 0:C], 0.0) for q, g in zip(probs, g12)]
        ns = [jnp.where(q['before'], g[0:C, C:2 * C], 0.0) for q, g in zip(probs, g12)]
        rbk = [jnp.concatenate([jnp.where(q['upto'], g[C:2 * C, 0:C], 0.0),
                                jnp.where(q['upto'], g[C:2 * C, C:2 * C], 0.0)], axis=1) for q, g in zip(probs, g12)]
        nvs = [_mm1(n, q['vh'], _NN) for q, n in zip(probs, ns)]
        ts = _unit_tri_inverses(ms, row, col)
        xs = [_mm2(t, jnp.concatenate([q['a0'], nv], axis=1), _NN) for q, t, nv in zip(probs, ts, nvs)]
        zeros = jnp.zeros((C, RW_HEAD), F32)
        ws = [jnp.concatenate([x, jnp.concatenate([zeros, q['vh']], axis=1)], axis=0) for q, x in zip(probs, xs)]
        rys = [_mm1(rb, w, _NN) for rb, w in zip(rbk, ws)]
        fgs = [_mm3(w, q['bke'], _TN) for q, w in zip(probs, ws)]
        ys = []
        for q, ry, fg in zip(probs, rys, fgs):
            s = s_ref[q['d'], q['h']]
            rp = q['r0'] + ry[:, 0:RW_HEAD]
            ys.append(ry[:, RW_HEAD:2 * RW_HEAD] + _mm1(rp, s, _NT))
            s_ref[q['d'], q['h']] = s * q['pc'] + _mm1(s, fg[0:RW_HEAD], _NN) + fg[RW_HEAD:2 * RW_HEAD]
        for k0 in range(0, len(probs), H_A):
            q0 = probs[k0]
            dirs[q0['d']][6][0, q0['rows'], :] = jnp.concatenate(ys[k0:k0 + H_A], axis=1)
        return carry

    lax.fori_loop(0, n_chunks // CHUNKS_PER_ITER, body, 0)

    @pl.when(j == nj - 1)
    def _():
        sfin_ref[0] = s_ref[...]


def _scan(nkk, r, v, lwf, kaf, kdf, lwb, kab, kdb, s0):
    B, L, _ = r.shape
    nj = L // TL
    tile = pl.BlockSpec((1, TL, D_A), lambda b, j: (b, j, 0))
    rtile = pl.BlockSpec((1, TL, D_A), lambda b, j: (b, nj - 1 - j, 0))
    st = pl.BlockSpec((1, 2, H_A, RW_HEAD, RW_HEAD), lambda b, j: (b, 0, 0, 0, 0))
    f32 = lambda *s: jax.ShapeDtypeStruct(s, F32)
    return pl.pallas_call(
        _scan_kernel, grid=(B, nj),
        in_specs=[tile] * 6 + [rtile] * 6 + [st],
        out_specs=[tile, rtile, st],
        out_shape=[f32(B, L, D_A), f32(B, L, D_A), f32(B, 2, H_A, RW_HEAD, RW_HEAD)],
        scratch_shapes=[pltpu.VMEM((2, H_A, RW_HEAD, RW_HEAD), F32)],
        compiler_params=_cparams(
            2, block_bytes=14 * _nbytes((TL, D_A), F32) + 2 * _nbytes((2, H_A, RW_HEAD, RW_HEAD), F32),
            scratch_bytes=_nbytes((2, H_A, RW_HEAD, RW_HEAD), F32),
            temp_bytes=2 * CHUNKS_PER_ITER * H_A * 12 * _nbytes((2 * CHUNK, 2 * CHUNK), F32)),
        name="rwkv_scan",
    )(nkk, r, v, lwf, kaf, kdf, nkk, r, v, lwb, kab, kdb, s0)


def _row_sum_bf16(e):
    n = e.shape[1] // 128
    acc = None
    for g0 in range(0, n, 4):
        part = e[:, g0 * 128:(g0 + 1) * 128]
        for j in range(g0 + 1, min(g0 + 4, n)):
            part = part + e[:, j * 128:(j + 1) * 128]
        pf = part.astype(F32)
        acc = pf if acc is None else acc + pf
    return jnp.sum(acc, axis=-1, keepdims=True)


def _attn_kernel(*refs, has_cache, lam_init):
    if has_cache:
        q_ref, k_ref, v_ref, ck_ref, cv_ref, lq1, lk1, lq2, lk2, sub_ref, o_ref = refs
    else:
        q_ref, k_ref, v_ref, lq1, lk1, lq2, lk2, sub_ref, o_ref = refs
    l1 = jnp.sum(lq1[...] * lk1[...], axis=-1, keepdims=True)
    l2 = jnp.sum(lq2[...] * lk2[...], axis=-1, keepdims=True)
    lam = jnp.exp(l1) - jnp.exp(l2) + lam_init
    q = q_ref[0].astype(BF16)
    kn = k_ref[0].astype(BF16)
    vn = v_ref[0].astype(BF16)
    if has_cache:
        kc = ck_ref[0, 0].astype(BF16)
        vc = cv_ref[0, 0].astype(BF16)
    dn = (((1,), (1,)), ((), ()))
    def scores(j):
        c0 = j * DIFF_DH
        qh = q[:, c0:c0 + DIFF_DH]
        s_n = lax.dot_general(qh, kn[:, c0:c0 + DIFF_DH], dn, preferred_element_type=F32)
        s_c = lax.dot_general(qh, kc[:, c0:c0 + DIFF_DH], dn, preferred_element_type=F32) if has_cache else None
        return s_n, s_c

    nxt = scores(0)
    for h in range(H_B):
        o = None
        for m in range(2):
            j = h * 2 + m
            s_n, s_c = nxt
            if j + 1 < 2 * H_B:
                nxt = scores(j + 1)
            mx = jnp.max(s_n, axis=-1, keepdims=True)
            if has_cache:
                mx = jnp.maximum(mx, jnp.max(s_c, axis=-1, keepdims=True))
            e_n = jnp.exp2((s_n - mx).astype(BF16))
            den = _row_sum_bf16(e_n)
            if has_cache:
                e_c = jnp.exp2((s_c - mx).astype(BF16))
                den = den + _row_sum_bf16(e_c)
            o_m = jnp.dot(e_n, vn[:, h * 128:(h + 1) * 128], preferred_element_type=F32)
            if has_cache:
                o_m = o_m + jnp.dot(e_c, vc[:, h * 128:(h + 1) * 128], preferred_element_type=F32)
            o = o_m / den if m == 0 else o - o_m * (lam / den)
        o_ref[0, :, h * 128:(h + 1) * 128] = _rms(o, sub_ref[...]) * (1.0 - lam_init)


def _attention(q, k, v, cache, l, p, lam_init):
    B, L, _ = q.shape
    has_cache = cache is not None
    tq = TL
    in_specs = [pl.BlockSpec((1, tq, D_B), lambda b, i: (b, i, 0)),
                pl.BlockSpec((1, L, D_B), lambda b, i: (b, 0, 0)),
                pl.BlockSpec((1, L, D_B), lambda b, i: (b, 0, 0))]
    args = [q, k, v]
    if has_cache:
        ck, cv = cache
        past = ck.shape[2]
        in_specs += [pl.BlockSpec((1, 1, past, D_B), lambda b, i: (b, l, 0, 0))] * 2
        args += [ck, cv]
    small = [_pspec(p, 'diff_lq1', (1, DIFF_DH)), _pspec(p, 'diff_lk1', (1, DIFF_DH)),
             _pspec(p, 'diff_lq2', (1, DIFF_DH)), _pspec(p, 'diff_lk2', (1, DIFF_DH)),
             _pspec(p, 'diff_subln', (1, 2 * DIFF_DH))]
    in_specs += [sp for sp, _ in small]
    args += [a for _, a in small]
    return pl.pallas_call(
        functools.partial(_attn_kernel, has_cache=has_cache, lam_init=lam_init),
        grid=(B, L // tq), in_specs=in_specs,
        out_specs=pl.BlockSpec((1, tq, D_B), lambda b, i: (b, i, 0)),
        out_shape=jax.ShapeDtypeStruct((B, L, D_B), F32),
        compiler_params=_cparams(
            2,
            block_bytes=(2 * _nbytes((tq, D_B), F32) + 2 * _nbytes((L, D_B), q.dtype)
                         + (2 * _nbytes((cache[0].shape[2], D_B), F32) if has_cache else 0)),
            temp_bytes=(2 * (_nbytes((tq, L + TL), F32) + _nbytes((tq, L + TL), BF16)) + 2 * _nbytes((L + TL, D_B), BF16)
                        + 4 * _nbytes((tq, D_B), F32))),
        name="diff_attention",
    )(*args)


def _hfilt_time_kernel(emb_ref, w1_ref, b1_ref, fr_ref, w2_ref, b2_ref, w3_ref, dec_ref, h_ref, ss_ref, acc_ref):
    i = pl.program_id(0)
    emb = emb_ref[...]
    fr = fr_ref[...]
    h = jnp.sin(fr * (jnp.dot(emb.astype(BF16), w1_ref[...].astype(BF16), preferred_element_type=F32) + b1_ref[...]))
    h = jnp.sin(fr * (jnp.dot(h.astype(BF16), w2_ref[...].astype(BF16), preferred_element_type=F32) + b2_ref[...]))
    h = jnp.dot(h.astype(BF16), w3_ref[...].astype(BF16), preferred_element_type=F32)
    h = h * jnp.exp(-emb[:, 0:1] * jnp.abs(dec_ref[...]))
    n = h.shape[0]
    row = lax.broadcasted_iota(jnp.int32, (n, 4 * D_C), 0) + i * n
    col = lax.broadcasted_iota(jnp.int32, (n, 4 * D_C), 1)
    is_bwd = ((col // D_C) % 2) == 1
    h = jnp.where(jnp.logical_and(is_bwd, row == 0), 0.0, h)
    h_ref[...] = h

    @pl.when(i == 0)
    def _():
        acc_ref[...] = jnp.zeros_like(acc_ref)

    acc_ref[...] += jnp.sum(h * h, axis=0, keepdims=True)
    ss_ref[...] = acc_ref[...]


def _hfilt_freq_kernel(h_ref, ss_ref, c_ref, s_ref, hr_ref, hi_ref, a_ref, b_ref, nyq_ref):
    i = pl.program_id(0)

    @pl.when(i == 0)
    def _():
        for o in range(2):
            hf = h_ref[:, o * 2 * D_C:o * 2 * D_C + D_C]
            hb = h_ref[:, o * 2 * D_C + D_C:(o + 1) * 2 * D_C]
            a = hf + hb
            n = a.shape[0]
            alt = 1.0 - 2.0 * (lax.broadcasted_iota(jnp.int32, (n, 1), 0) % 2).astype(F32)
            nyq_ref[:, o * D_C:(o + 1) * D_C] = jnp.sum(a * alt, axis=0, keepdims=True)
            a_ref[:, o * D_C:(o + 1) * D_C] = a.astype(BF16)
            b_ref[:, o * D_C:(o + 1) * D_C] = (hb - hf).astype(BF16)

    ss = ss_ref[...]
    tot = jnp.concatenate([ss[:, 0:D_C] + ss[:, D_C:2 * D_C], ss[:, 2 * D_C:3 * D_C] + ss[:, 3 * D_C:4 * D_C]], axis=1)
    scale = lax.rsqrt(tot + 1e-6)

    hr_ref[...] = jnp.dot(c_ref[...], a_ref[...], preferred_element_type=F32) * scale
    hi = jnp.dot(s_ref[...], b_ref[...], preferred_element_type=F32) * scale
    tf = hi.shape[0]
    row = lax.broadcasted_iota(jnp.int32, (tf, 1), 0) + i * tf
    hi_ref[...] = jnp.where(row == 0, nyq_ref[...] * scale, hi)


def _hyena_filters(L, p, tabs):
    emb, cmat, smat = tabs
    tr = 256
    params = [_pspec(p, 'hy_w1_pad', (128, HY_FFN)), _pspec(p, 'hy_b1', (1, HY_FFN)), _pspec(p, 'hy_freq', (1, HY_FFN)),
              _pspec(p, 'hy_w2', (HY_FFN, HY_FFN)), _pspec(p, 'hy_b2', (1, HY_FFN)),
              _pspec(p, 'hy_w3', (HY_FFN, 4 * D_C)), _pspec(p, 'hy_decay4', (1, 4 * D_C))]
    h_raw, ss = pl.pallas_call(
        _hfilt_time_kernel, grid=(L // tr,),
        in_specs=[pl.BlockSpec((tr, 128), lambda i: (i, 0))] + [sp for sp, _ in params],
        out_specs=[pl.BlockSpec((tr, 4 * D_C), lambda i: (i, 0)), pl.BlockSpec((1, 4 * D_C), lambda i: (0, 0))],
        out_shape=[jax.ShapeDtypeStruct((L, 4 * D_C), F32), jax.ShapeDtypeStruct((1, 4 * D_C), F32)],
        scratch_shapes=[pltpu.VMEM((1, 4 * D_C), F32)],
        compiler_params=_cparams(1, block_bytes=2 * _nbytes((tr, 4 * D_C), F32) + _nbytes((HY_FFN, 4 * D_C), F32),
                                 temp_bytes=3 * _nbytes((tr, 4 * D_C), F32)),
        name="hyena_filter_time",
    )(emb, *[a for _, a in params])
    tf = 256
    mat = pl.BlockSpec((tf, L), lambda i: (i, 0))
    return pl.pallas_call(
        _hfilt_freq_kernel, grid=(L // tf,),
        in_specs=[pl.BlockSpec((L, 4 * D_C), lambda i: (0, 0)), pl.BlockSpec((1, 4 * D_C), lambda i: (0, 0)),
                  mat, mat],
        out_specs=[pl.BlockSpec((tf, 2 * D_C), lambda i: (i, 0))] * 2,
        out_shape=[jax.ShapeDtypeStruct((L, 2 * D_C), F32)] * 2,
        scratch_shapes=[pltpu.VMEM((L, 2 * D_C), BF16)] * 2 + [pltpu.VMEM((1, 2 * D_C), F32)],
        compiler_params=_cparams(
            1, block_bytes=_nbytes((L, 4 * D_C), F32) + 2 * _nbytes((tf, L), BF16) + 2 * _nbytes((tf, 2 * D_C), F32),
            scratch_bytes=2 * _nbytes((L, 2 * D_C), BF16), temp_bytes=4 * _nbytes((L, D_C), F32)),
        name="hyena_filter_freq",
    )(h_raw, ss, cmat, smat)


def _hconv_kernel(z_ref, zt_ref, gate_ref, hr_ref, hi_ref, bias_ref, c_ref, s_ref, o_ref,
                  zb_ref, yr_ref, yi_ref, nyq_ref, *, BG, L):
    ph = pl.program_id(1)
    i = pl.program_id(2)
    inv_n = 1.0 / (2 * L)

    @pl.when(jnp.logical_and(ph == 0, i == 0))
    def _():
        alt = 1.0 - 2.0 * (lax.broadcasted_iota(jnp.int32, (L, 1), 0) % 2).astype(F32)
        for b in range(BG):
            z = z_ref[b]
            zb_ref[:, b * D_C:(b + 1) * D_C] = z.astype(BF16)
            nyq_ref[:, b * D_C:(b + 1) * D_C] = jnp.sum(z * alt, axis=0, keepdims=True)

    tf = c_ref.shape[0]

    @pl.when(ph == 0)
    def _():
        zr = jnp.dot(c_ref[...], zb_ref[...], preferred_element_type=F32)
        zs = jnp.dot(s_ref[...], zb_ref[...], preferred_element_type=F32)
        hr = hr_ref[...]
        hi = hi_ref[...]
        row = lax.broadcasted_iota(jnp.int32, (tf, 1), 0) + i * tf
        wgt = jnp.where(row == 0, inv_n, 2.0 * inv_n)
        rows = pl.ds(pl.multiple_of(i * tf, tf), tf)
        for b in range(BG):
            sl = slice(b * D_C, (b + 1) * D_C)
            yr_ref[rows, sl] = ((zr[:, sl] * hr + zs[:, sl] * hi) * wgt).astype(BF16)
            yi_ref[rows, sl] = ((zr[:, sl] * hi - zs[:, sl] * hr) * (2.0 * inv_n)).astype(BF16)

        @pl.when(i == 0)
        def _():
            for b in range(BG):
                sl = slice(b * D_C, (b + 1) * D_C)
                nyq_ref[:, sl] = nyq_ref[:, sl] * hi[0:1, :] * inv_n

    @pl.when(ph == 1)
    def _():
        y = (jnp.dot(c_ref[...], yr_ref[...], preferred_element_type=F32)
             - jnp.dot(s_ref[...], yi_ref[...], preferred_element_type=F32))
        row = lax.broadcasted_iota(jnp.int32, (tf, 1), 0) + i * tf
        alt = 1.0 - 2.0 * (row % 2).astype(F32)
        y = y + alt * nyq_ref[...]
        for b in range(BG):
            sl = slice(b * D_C, (b + 1) * D_C)
            o_ref[b] = gate_ref[b] * (y[:, sl] + bias_ref[...] * zt_ref[b])


def _hconv(z, gate, hr, hi, p, order, tabs):
    B, L, _ = z.shape
    if isinstance(p, _LayerParams):
        layer = p.layer
        bias_arg = p.stacked['hy_bias'].reshape(DEPTH, 2, 1, D_C)
        bias_spec = pl.BlockSpec((None, None, 1, D_C), lambda g, ph, i: (layer, order, 0, 0))
    else:
        bias_arg = p['hy_bias'][order].reshape(1, D_C)
        bias_spec = pl.BlockSpec((1, D_C), lambda g, ph, i: (0, 0))
    _, ch, sh = tabs
    BG =min(B, 8) if L <= 256 else min(B, 4)
    assert B % BG == 0
    tf = 256
    nt = L // tf
    mat = pl.BlockSpec((tf, L), lambda g, ph, i: (i, 0))
    tile = pl.BlockSpec((BG, tf, D_C), lambda g, ph, i: (g, i * ph, 0))
    hspec = pl.BlockSpec((tf, D_C), lambda g, ph, i: (i * (1 - ph), order))
    cols = BG * D_C
    return pl.pallas_call(
        functools.partial(_hconv_kernel, BG=BG, L=L), grid=(B // BG, 2, nt),
        in_specs=[pl.BlockSpec((BG, L, D_C), lambda g, ph, i: (g, 0, 0)), tile, tile, hspec, hspec,
                  bias_spec, mat, mat],
        out_specs=tile,
        out_shape=jax.ShapeDtypeStruct((B, L, D_C), F32),
        scratch_shapes=[pltpu.VMEM((L, cols), BF16)] * 3 + [pltpu.VMEM((1, cols), F32)],
        compiler_params=_cparams(
            3,
            block_bytes=(_nbytes((BG, L, D_C), F32) + 3 * _nbytes((BG, tf, D_C), F32) + 2 * _nbytes((tf, L), BF16)
                         + 2 * _nbytes((tf, D_C), F32)),
            scratch_bytes=3 * _nbytes((L, cols), BF16), temp_bytes=4 * _nbytes((tf, cols), F32)),
        name="hyena_conv",
    )(z, z, gate, hr, hi, bias_arg, ch, sh)


def _outffn_kernel(x_ref, yf_ref, yb_ref, bon_ref, g_ref, ob_ref, yc_ref, mod_ref,
                   lnw_ref, lnb_ref, gpost_ref, gffn_ref, gffn_post_ref, w_ref, w1_ref, w2_ref, o_ref):
    bd = _block_diag(D_A, RW_HEAD)
    y = yf_ref[0] + yb_ref[0]
    mu = _segsum(y, bd) * (1.0 / RW_HEAD)
    yc = y - mu
    var = _segsum(yc * yc, bd) * (1.0 / RW_HEAD)
    yn = yc * lax.rsqrt(var + GN_EPS) * lnw_ref[...] + lnb_ref[...]
    ya = (yn + bon_ref[0]) * g_ref[0]
    mix = (jnp.dot(ya.astype(BF16), w_ref[0:D_A, :], preferred_element_type=F32)
           + jnp.dot(ob_ref[0].astype(BF16), w_ref[D_A:D_A + D_B, :], preferred_element_type=F32)
           + jnp.dot(yc_ref[0].astype(BF16), w_ref[D_A + D_B:D_MODEL, :], preferred_element_type=F32))
    mod = mod_ref[0]
    gt1 = mod[:, 2 * D_MODEL:3 * D_MODEL]
    sh2 = mod[:, 3 * D_MODEL:4 * D_MODEL]
    sc2 = mod[:, 4 * D_MODEL:5 * D_MODEL]
    gt2 = mod[:, 5 * D_MODEL:6 * D_MODEL]
    x1 = x_ref[0] + gt1 * _rms(mix, gpost_ref[...])
    h = (_rms(x1, gffn_ref[...]) * (1.0 + sc2) + sh2).astype(BF16)
    acc = None
    for c in range(D_FF // FFN_CHUNK):
        a = jnp.dot(h, w1_ref[:, c * FFN_CHUNK:(c + 1) * FFN_CHUNK], preferred_element_type=F32)
        a = jnp.square(jnp.maximum(a, 0.0)).astype(BF16)
        part = jnp.dot(a, w2_ref[c * FFN_CHUNK:(c + 1) * FFN_CHUNK, :], preferred_element_type=F32)
        acc = part if acc is None else acc + part
    o_ref[0] = x1 + gt2 * _rms(acc, gffn_post_ref[...])


def _outffn(x, yf, yb, bon, g, ob, yc, mod, p, w_out, w1, w2):
    B, L, _ = x.shape
    mb = mod.shape[0]
    row = lambda b, i: (b, i, 0)
    c2 = lambda b, i: (0, 0)
    t = lambda n: pl.BlockSpec((1, TL, n), row)
    vecs = [_pspec(p, 'rwkv_ln_w', (1, D_A)), _pspec(p, 'rwkv_ln_b', (1, D_A)), _pspec(p, 'g_mix_post', (1, D_MODEL)),
            _pspec(p, 'g_ffn_pre', (1, D_MODEL)), _pspec(p, 'g_ffn_post', (1, D_MODEL))]
    in_specs = [t(D_MODEL), t(D_A), t(D_A), t(D_A), t(D_A), t(D_B), t(D_C),
                pl.BlockSpec((1, 1, 6 * D_MODEL), (lambda b, i: (b, 0, 0)) if mb > 1 else (lambda b, i: (0, 0, 0)))]
    in_specs += [sp for sp, _ in vecs]
    in_specs += [_layer_weight_spec(w_out, 2), _layer_weight_spec(w1, 2), _layer_weight_spec(w2, 2)]
    return pl.pallas_call(
        _outffn_kernel, grid=(B, L // TL), in_specs=in_specs,
        out_specs=t(D_MODEL), out_shape=jax.ShapeDtypeStruct((B, L, D_MODEL), F32),
        compiler_params=_cparams(
            2,
            block_bytes=(2 * _nbytes((TL, D_MODEL), F32) + 4 * _nbytes((TL, D_A), F32) + _nbytes((TL, D_B), F32)
                         + _nbytes((TL, D_C), F32) + _nbytes((D_MODEL, D_MODEL), BF16) + 2 * _nbytes((D_MODEL, D_FF), BF16)),
            temp_bytes=4 * _nbytes((TL, D_MODEL), F32) + 2 * _nbytes((TL, FFN_CHUNK), F32)),
        name="outproj_ffn",
    )(x, yf, yb, bon, g, ob, yc, mod, *[a for _, a in vecs], w_out[0], w1[0], w2[0])


@functools.lru_cache(maxsize=None)
def _rope_tables(L):
    rows = L // GRID_W
    row = np.repeat(np.arange(rows), GRID_W).astype(np.float64)
    col = np.tile(np.arange(GRID_W), rows).astype(np.float64)
    half = DIFF_DH // 2
    inv = ROPE_BASE ** (-np.arange(0, half, 2, dtype=np.float64) / half)
    ang_r = row[:, None] * inv[None]
    ang_c = col[:, None] * inv[None]
    ang = np.concatenate([ang_r, ang_r, ang_c, ang_c], axis=-1)
    cos, sin = np.cos(ang), np.sin(ang)
    quarter = (np.arange(DIFF_DH) // 16) % 2
    sin_a = np.where(quarter == 0, -sin, 0.0)
    sin_b = np.where(quarter == 1, sin, 0.0)
    rep = D_B // DIFF_DH
    return tuple(np.tile(t, (1, rep)).astype(np.float32) for t in (cos, sin_a, sin_b))


@functools.lru_cache(maxsize=None)
def _hyena_tables(L):
    t = np.linspace(0.0, 1.0, L, dtype=np.float32).astype(np.float64)[:, None]
    ang = (2.0 * math.pi / L) * np.arange(L, dtype=np.float64)[:, None]
    bands = np.linspace(1e-4, HY_BANDS - 1, HY_BANDS, dtype=np.float32).astype(np.float64)[None, :]
    emb = np.concatenate([t, np.cos(bands * ang), -np.sin(bands * ang)], axis=-1)
    emb = np.pad(emb, ((0, 0), (0, 128 - HY_EMB))).astype(np.float32)
    n = np.arange(L, dtype=np.int64)
    theta = ((n[:, None] * n[None, :]) % (2 * L)).astype(np.float64) * (math.pi / L)
    return emb, np.cos(theta).astype(BF16), np.sin(theta).astype(BF16)


def _layer(x, mod, p, wb, l, lam_init, cache, rope_tabs, hy_tabs, filt):
    B, L, _ = x.shape
    (uq, uk, uv, r, v, nkk, wf, kaf, kdf, wbk, kab, kdb, g, bon, x1h, x2h, hv) = _inproj(
        x, mod, wb['w_in'], rope_tabs, p)
    if cache is None:
        s0 = jnp.zeros((B, 2, H_A, RW_HEAD, RW_HEAD), F32)
        kv_cache = None
    else:
        s0 = cache[0]
        kv_cache = (cache[1], cache[2])
    yf, yb, sfin = _scan(nkk, r, v, wf, kaf, kdf, wbk, kab, kdb, s0)
    ob = _attention(uq, uk, uv, kv_cache, l, p, lam_init)
    hr, hi = filt
    z1 = _hconv(hv, x1h, hr, hi, p, 0, hy_tabs)
    yc = _hconv(z1, x2h, hr, hi, p, 1, hy_tabs)
    x2 = _outffn(x, yf, yb, bon, g, ob, yc, mod, p, wb['w_out'], wb['w_ff1'], wb['w_ff2'])
    return x2, (sfin, uk, uv)


def _derive_params(d):
    w1 = d['hy_w1']
    d['hy_w1_pad'] = jnp.pad(w1, [(0, 0)] * (w1.ndim - 2) + [(0, 128 - HY_EMB), (0, 0)])
    dec = d['hy_decay']
    d['hy_decay4'] = jnp.tile(dec, (1,) * (dec.ndim - 1) + (4,))
    return d


_LAYER_KEYS = ('g_mix_pre', 'g_mix_post', 'g_ffn_pre', 'g_ffn_post', 'rwkv_conv', 'rwkv_w0', 'rwkv_w2', 'rwkv_a0',
               'rwkv_a2', 'rwkv_g2', 'rwkv_kk', 'rwkv_ka', 'rwkv_rk', 'rwkv_ln_w', 'rwkv_ln_b', 'diff_lq1', 'diff_lk1',
               'diff_lq2', 'diff_lk2', 'diff_subln', 'hy_conv_w', 'hy_conv_b', 'hy_w1', 'hy_b1', 'hy_freq', 'hy_w2',
               'hy_b2', 'hy_w3', 'hy_decay', 'hy_bias')


def kernel(x_prompt, x_sample, state_rwkv, cache_k, cache_v, c, c_ctx, w_mod, b_mod, g_mix_pre, g_mix_post, g_ffn_pre, g_ffn_post, w_in, rwkv_conv, rwkv_w0, rwkv_w2, rwkv_a0, rwkv_a2, rwkv_g2, rwkv_kk, rwkv_ka, rwkv_rk, rwkv_ln_w, rwkv_ln_b, diff_lq1, diff_lk1, diff_lq2, diff_lk2, diff_subln, hy_conv_w, hy_conv_b, hy_w1, hy_b1, hy_freq, hy_w2, hy_b2, hy_w3, hy_decay, hy_bias, w_out, w_ff1, w_ff2):
    stacked = dict(g_mix_pre=g_mix_pre, g_mix_post=g_mix_post, g_ffn_pre=g_ffn_pre, g_ffn_post=g_ffn_post,
                   rwkv_conv=rwkv_conv, rwkv_w0=rwkv_w0, rwkv_w2=rwkv_w2, rwkv_a0=rwkv_a0, rwkv_a2=rwkv_a2,
                   rwkv_g2=rwkv_g2, rwkv_kk=rwkv_kk, rwkv_ka=rwkv_ka, rwkv_rk=rwkv_rk, rwkv_ln_w=rwkv_ln_w,
                   rwkv_ln_b=rwkv_ln_b, diff_lq1=diff_lq1, diff_lk1=diff_lk1, diff_lq2=diff_lq2, diff_lk2=diff_lk2,
                   diff_subln=diff_subln, hy_conv_w=hy_conv_w, hy_conv_b=hy_conv_b, hy_w1=hy_w1, hy_b1=hy_b1,
                   hy_freq=hy_freq, hy_w2=hy_w2, hy_b2=hy_b2, hy_w3=hy_w3, hy_decay=hy_decay, hy_bias=hy_bias)
    _derive_params(stacked)
    Bc, Lc, _ = x_prompt.shape
    Bs, Ls, _ = x_sample.shape
    past = cache_k.shape[2]

    cond = jnp.zeros((8, D_MODEL), F32).at[0:Bs].set(c).at[Bs].set(c_ctx)
    mod_all = _modulation(cond, w_mod, b_mod)

    rope_tabs = _rope_tables(Ls)
    tabs_c = _hyena_tables(Lc)
    tabs_s = _hyena_tables(Ls)
    ck = cache_k.reshape(Bs, DEPTH, past, D_B)
    cv = cache_v.reshape(Bs, DEPTH, past, D_B)

    w_in_b, w_out_b, w_ff1_b, w_ff2_b = (w.astype(BF16) for w in (w_in, w_out, w_ff1, w_ff2))
    xp, xs = x_prompt, x_sample
    st_list, k_list, v_list = [], [], []
    for l in range(DEPTH):
        p = _LayerParams(stacked, l)
        wb = dict(w_in=(w_in_b, l), w_out=(w_out_b, l), w_ff1=(w_ff1_b, l), w_ff2=(w_ff2_b, l))
        lam_init = 0.8 - 0.6 * math.exp(-0.3 * l)
        mod_lat = mod_all[l, 0:Bs].reshape(Bs, 1, 6 * D_MODEL)
        mod_ctx = mod_all[l, Bs:Bs + 1].reshape(1, 1, 6 * D_MODEL)
        filt_c = _hyena_filters(Lc, p, tabs_c)
        filt_s = _hyena_filters(Ls, p, tabs_s)
        xp, (s_ctx, k_ctx, v_ctx) = _layer(xp, mod_ctx, p, wb, l, lam_init, None, None, tabs_c, filt_c)
        st_list.append(s_ctx)
        k_list.append(k_ctx)
        v_list.append(v_ctx)
        s0 = state_rwkv[:, l]
        xs, _ = _layer(xs, mod_lat, p, wb, l, lam_init, (s0, ck, cv), rope_tabs, tabs_s, filt_s)
    new_k = jnp.stack(k_list, axis=1).reshape(Bc, DEPTH, Lc, H_B, 2, DIFF_DH)
    new_v = jnp.stack(v_list, axis=1).reshape(Bc, DEPTH, Lc, H_B, 2 * DIFF_DH)
    return (xp, xs, jnp.stack(st_list, axis=1), new_k, new_v)
```

```python
import functools
import math

import jax
import jax.numpy as jnp
import numpy as np
from jax import lax
from jax.experimental import pallas as pl
from jax.experimental.pallas import tpu as pltpu

F32 = jnp.float32
BF16 = jnp.bfloat16

D_MODEL = 1024
DEPTH = 4
GRID_W = 64
D_A = 256
RW_HEAD = 64
H_A = 4
D_B = 512
DIFF_DH = 64
H_B = 4
D_C = 256
HY_BANDS = 16
HY_EMB = 33
HY_FFN = 64
D_FF = 4096
FFN_CHUNK = 1024
ROPE_BASE = 10000.0
RMS_EPS = 1e-6
GN_EPS = 64e-5
N_A_COLS = 1152
IN_COLS = 3456

TL = 256
MIB = 1024 * 1024
V7X_VMEM_BYTES = 64 * MIB
V7X_VMEM_CEILING = V7X_VMEM_BYTES * 7 // 8
VMEM_FLOOR = 48 * MIB


def _nbytes(shape, dtype):
    return math.prod(shape) * jnp.dtype(dtype).itemsize


def _cparams(n_axes, block_bytes=0, scratch_bytes=0, temp_bytes=0):
    need = 2 * block_bytes + scratch_bytes + temp_bytes
    return pltpu.CompilerParams(dimension_semantics=("arbitrary",) * n_axes,
                                vmem_limit_bytes=int(min(V7X_VMEM_CEILING, max(VMEM_FLOOR, need))))


def _split3(a):
    hi = a.astype(BF16)
    r1 = a - hi.astype(F32)
    mid = r1.astype(BF16)
    lo = (r1 - mid.astype(F32)).astype(BF16)
    return hi, mid, lo


def _split2(a):
    hi = a.astype(BF16)
    lo = (a - hi.astype(F32)).astype(BF16)
    return hi, lo


def _segsum(a, bd):
    hi, lo = _split2(a)
    return jnp.dot(hi, bd, preferred_element_type=F32) + jnp.dot(lo, bd, preferred_element_type=F32)


def _block_diag(n, seg):
    r = lax.broadcasted_iota(jnp.int32, (n, n), 0) // seg
    c = lax.broadcasted_iota(jnp.int32, (n, n), 1) // seg
    return (r == c).astype(BF16)


class _LayerParams:
    def __init__(self, stacked, layer):
        self.stacked = stacked
        self.layer = layer

    def __getitem__(self, name):
        return self.stacked[name][self.layer]


def _pspec(p, name, shape):
    if isinstance(p, _LayerParams):
        stack = p.stacked[name]
        stack = stack.reshape((stack.shape[0],) + shape)
        layer = p.layer
        return pl.BlockSpec((None,) + shape, lambda *g: (layer,) + (0,) * len(shape)), stack
    return pl.BlockSpec(shape, lambda *g: (0,) * len(shape)), p[name].reshape(shape)


def _layer_weight_spec(w, n_grid_axes):
    stack, layer = w
    _, k, n = stack.shape
    if n_grid_axes == 2:
        return pl.BlockSpec((None, k, n), lambda b, i: (layer, 0, 0))
    raise NotImplementedError(n_grid_axes)


def _rms(x, g):
    return x * lax.rsqrt(jnp.mean(x * x, axis=-1, keepdims=True) + RMS_EPS) * g


def _mod_kernel(cond_ref, w_ref, b_ref, o_ref):
    c = cond_ref[...]
    s = (c * jax.nn.sigmoid(c)).astype(BF16)
    o_ref[0] = jnp.dot(s, w_ref[0].astype(BF16), preferred_element_type=F32) + b_ref[0]


def _modulation(cond, w_mod, b_mod):
    tn = 1536
    return pl.pallas_call(
        _mod_kernel,
        grid=(DEPTH, 6 * D_MODEL // tn),
        in_specs=[pl.BlockSpec((8, D_MODEL), lambda l, n: (0, 0)),
                  pl.BlockSpec((1, D_MODEL, tn), lambda l, n: (l, 0, n)),
                  pl.BlockSpec((1, 1, tn), lambda l, n: (l, 0, n))],
        out_specs=pl.BlockSpec((1, 8, tn), lambda l, n: (l, 0, n)),
        out_shape=jax.ShapeDtypeStruct((DEPTH, 8, 6 * D_MODEL), F32),
        compiler_params=_cparams(2, block_bytes=_nbytes((D_MODEL + 16, tn), F32),
                                 temp_bytes=_nbytes((D_MODEL, tn), BF16)),
        name="modulation",
    )(cond, w_mod, b_mod.reshape(DEPTH, 1, 6 * D_MODEL))


def _rope(x, cos, sin_a, sin_b):
    n = x.shape[-1]
    return x * cos + pltpu.roll(x, n - 16, axis=1) * sin_a + pltpu.roll(x, 16, axis=1) * sin_b


def _shift_conv(u, prev_row, next_row, cw):
    n = u.shape[0]
    row = lax.broadcasted_iota(jnp.int32, (n, 1), 0)
    up = jnp.where(row == 0, prev_row, pltpu.roll(u, 1, axis=0))
    un = jnp.where(row == n - 1, next_row, pltpu.roll(u, n - 1, axis=0))
    return cw[0:1] * up + cw[1:2] * u + cw[2:3] * un


def _prep_body(ua, prev, nxt, uh, prevh, nxth,
               cw_ref, w0_ref, w2_ref, a0_ref, a2_ref, g2_ref, kkw_ref, kaw_ref, rk_ref, hcw_ref, hcb_ref,
               r_ref, v_ref, nkk_ref, wf_ref, kaf_ref, kdf_ref, wb_ref, kab_ref, kdb_ref, g_ref, bon_ref,
               x1_ref, x2_ref, hv_ref):
    bd = _block_diag(D_A, RW_HEAD)
    u_rkv = ua[:, 0:3 * D_A]
    rkv = _shift_conv(u_rkv, prev, nxt, cw_ref[...])
    r = rkv[:, 0:D_A]
    k = rkv[:, D_A:2 * D_A]
    v = rkv[:, 2 * D_A:3 * D_A]
    u_w = ua[:, 768:896]
    u_a = ua[:, 896:1024]
    u_g = ua[:, 1024:1152]

    g = jnp.dot(jax.nn.sigmoid(u_g).astype(BF16), g2_ref[...].astype(BF16), preferred_element_type=F32)
    kk = k * kkw_ref[...]
    kk = kk * lax.rsqrt(_segsum(kk * kk, bd) + 1e-12)
    r_ref[0] = r
    v_ref[0] = v
    nkk_ref[0] = -kk
    g_ref[0] = g

    kd_sum = None
    outs = ((wf_ref, kaf_ref, kdf_ref), (wb_ref, kab_ref, kdb_ref))
    for d in range(2):
        uw_d = u_w[:, d * 64:(d + 1) * 64]
        ua_d = u_a[:, d * 64:(d + 1) * 64]
        xw = w0_ref[d:d + 1, :] + jnp.dot(jnp.tanh(uw_d).astype(BF16), w2_ref[d].astype(BF16),
                                          preferred_element_type=F32)
        z = -xw
        softplus = jnp.maximum(z, 0.0) + jnp.log(1.0 + jnp.exp(-jnp.abs(z)))
        wl = -softplus - 0.5
        log_decay = -jnp.exp(wl)
        a = jax.nn.sigmoid(a0_ref[d:d + 1, :] + jnp.dot(ua_d.astype(BF16), a2_ref[d].astype(BF16),
                                                        preferred_element_type=F32))
        kd = k * (1.0 + (a - 1.0) * kaw_ref[...])
        w_o, ka_o, kd_o = outs[d]
        w_o[0] = log_decay
        ka_o[0] = kk * a
        kd_o[0] = kd
        kd_sum = kd if kd_sum is None else kd_sum + kd
    bon_ref[0] = _segsum(r * kd_sum * rk_ref[...], bd) * v

    hc = _shift_conv(uh, prevh, nxth, hcw_ref[...]) + hcb_ref[...]
    x1_ref[0] = hc[:, 0:D_C]
    x2_ref[0] = hc[:, D_C:2 * D_C]
    hv_ref[0] = hc[:, 2 * D_C:3 * D_C]


N_PREP_PARAMS = 11
N_PREP_OUTS = 14


def _inproj_kernel(*refs, rope):
    n_in = 6 + (3 if rope else 0)
    x_ref, xp_ref, xn_ref, mod_ref, g_ref, w_ref = refs[0:6]
    rope_refs = refs[6:n_in]
    prep_params = refs[n_in:n_in + N_PREP_PARAMS]
    uq_ref, uk_ref, uv_ref = refs[n_in + N_PREP_PARAMS:n_in + N_PREP_PARAMS + 3]
    prep_outs = refs[n_in + N_PREP_PARAMS + 3:]
    i = pl.program_id(1)
    nt = pl.num_programs(1)
    mod = mod_ref[0]
    sh1 = mod[:, 0:D_MODEL]
    sc1 = mod[:, D_MODEL:2 * D_MODEL]

    def normed(x):
        return (_rms(x, g_ref[...]) * (1.0 + sc1) + sh1).astype(BF16)

    n = x_ref.shape[1]
    h_all = jnp.concatenate([normed(x_ref[0]), normed(xp_ref[0]), normed(xn_ref[0])], axis=0)
    u_all = jnp.dot(h_all, w_ref[...], preferred_element_type=F32)
    u = u_all[0:n]
    q = u[:, N_A_COLS:N_A_COLS + D_B] * (DIFF_DH ** -0.5 * math.log2(math.e))
    k = u[:, N_A_COLS + D_B:N_A_COLS + 2 * D_B]
    if rope:
        cos_ref, sa_ref, sb_ref = rope_refs
        q = _rope(q, cos_ref[...], sa_ref[...], sb_ref[...])
        k = _rope(k, cos_ref[...], sa_ref[...], sb_ref[...])
    uq_ref[0] = q.astype(uq_ref.dtype)
    uk_ref[0] = k.astype(uk_ref.dtype)
    uv_ref[0] = u[:, N_A_COLS + 2 * D_B:N_A_COLS + 3 * D_B].astype(uv_ref.dtype)

    hy0 = N_A_COLS + 3 * D_B
    prev_row = u_all[n + 7:n + 8]
    next_row = u_all[n + 8:n + 9]
    has_prev = i > 0
    has_next = i < nt - 1
    _prep_body(u[:, 0:N_A_COLS], jnp.where(has_prev, prev_row[:, 0:3 * D_A], 0.0),
               jnp.where(has_next, next_row[:, 0:3 * D_A], 0.0),
               u[:, hy0:IN_COLS], jnp.where(has_prev, prev_row[:, hy0:IN_COLS], 0.0),
               jnp.where(has_next, next_row[:, hy0:IN_COLS], 0.0),
               *prep_params, *prep_outs)


def _inproj(x, mod, w_in, rope_tabs, p):
    B, L, _ = x.shape
    rope = rope_tabs is not None
    mb = mod.shape[0]
    nb8 = L // 8
    t8 = TL // 8
    row = lambda b, i: (b, i, 0)
    prev = lambda b, i: (b, jnp.maximum(i * t8 - 1, 0), 0)
    nxt = lambda b, i: (b, jnp.minimum((i + 1) * t8, nb8 - 1), 0)
    const2 = lambda b, i: (0, 0)

    params = [_pspec(p, 'rwkv_conv', (3, 3 * D_A)), _pspec(p, 'rwkv_w0', (2, D_A)),
              _pspec(p, 'rwkv_w2', (2, 64, D_A)), _pspec(p, 'rwkv_a0', (2, D_A)), _pspec(p, 'rwkv_a2', (2, 64, D_A)),
              _pspec(p, 'rwkv_g2', (128, D_A)), _pspec(p, 'rwkv_kk', (1, D_A)), _pspec(p, 'rwkv_ka', (1, D_A)),
              _pspec(p, 'rwkv_rk', (1, D_A)), _pspec(p, 'hy_conv_w', (3, 3 * D_C)),
              _pspec(p, 'hy_conv_b', (1, 3 * D_C))]
    assert len(params) == N_PREP_PARAMS
    g_spec, g_arg = _pspec(p, 'g_mix_pre', (1, D_MODEL))
    in_specs = [pl.BlockSpec((1, TL, D_MODEL), row), pl.BlockSpec((1, 8, D_MODEL), prev),
                pl.BlockSpec((1, 8, D_MODEL), nxt),
                pl.BlockSpec((1, 1, 6 * D_MODEL), (lambda b, i: (b, 0, 0)) if mb > 1 else (lambda b, i: (0, 0, 0))),
                g_spec,
                _layer_weight_spec(w_in, 2)]
    args = [x, x, x, mod, g_arg, w_in[0]]
    if rope:
        in_specs += [pl.BlockSpec((TL, D_B), lambda b, i: (i, 0))] * 3
        args += list(rope_tabs)
    in_specs += [s for s, _ in params]
    args += [a for _, a in params]
    qkv_dt = BF16 if rope else F32
    out_shape = ([jax.ShapeDtypeStruct((B, L, D_B), qkv_dt)] * 3
                 + [jax.ShapeDtypeStruct((B, L, D_A), F32)] * N_PREP_OUTS)
    out_specs = [pl.BlockSpec((1, TL, D_B), row)] * 3 + [pl.BlockSpec((1, TL, D_A), row)] * N_PREP_OUTS
    return pl.pallas_call(
        functools.partial(_inproj_kernel, rope=rope),
        grid=(B, L // TL), in_specs=in_specs, out_specs=out_specs, out_shape=out_shape,
        compiler_params=_cparams(
            2,
            block_bytes=(_nbytes((TL + 16, D_MODEL), F32) + _nbytes((D_MODEL, IN_COLS), BF16)
                         + 3 * _nbytes((TL, D_B), F32) * (2 if rope else 1) + N_PREP_OUTS * _nbytes((TL, D_A), F32)),
            temp_bytes=3 * _nbytes((TL + 16, IN_COLS), F32)),
        name="inproj",
    )(*args)


CHUNK = 64
CHUNKS_PER_ITER = 4
_NN = (((1,), (0,)), ((), ()))
_NT = (((1,), (1,)), ((), ()))
_TN = (((0,), (0,)), ((), ()))


def _mm3(a, b, dims):
    ah, al = _split2(a)
    bh, bl = _split2(b)
    f = lambda x, y: lax.dot_general(x, y, dims, preferred_element_type=F32)
    if dims is _TN:
        return f(ah, bh) + f(ah, bl) + f(al, bh)
    m = a.shape[0]
    both = f(jnp.concatenate([ah, al], axis=0), bh)
    return both[0:m] + both[m:2 * m] + f(ah, bl)


def _mm2(a, b, dims):
    ah = a.astype(BF16)
    bh, bl = _split2(b)
    n = b.shape[1]
    both = lax.dot_general(ah, jnp.concatenate([bh, bl], axis=1), dims, preferred_element_type=F32)
    return both[:, 0:n] + both[:, n:2 * n]


def _mm1(a, b, dims):
    return lax.dot_general(a.astype(BF16), b.astype(BF16), dims, preferred_element_type=F32)


def _unit_tri_inverses(ms, row, col):
    eye = (row == col).astype(F32)
    blk = lambda n: (row // n) == (col // n)
    m8 = [jnp.where(blk(8), m, 0.0) for m in ms]
    a2 = [_mm1(x, x, _NN) for x in m8]
    a4 = [_mm1(x, x, _NN) for x in a2]
    p = [eye + x + y + _mm1(x, y, _NN) for x, y in zip(m8, a2)]
    ts = [x + _mm1(x, y, _NN) for x, y in zip(p, a4)]
    n = 8
    while n < CHUNK:
        sel = jnp.logical_and(blk(2 * n), jnp.logical_not(blk(n)))
        ot = [_mm1(jnp.where(sel, m, 0.0), t, _NN) for m, t in zip(ms, ts)]
        ts = [t + _mm1(t, o, _NN) for t, o in zip(ts, ot)]
        n *= 2
    return ts


def _scan_kernel(nkkf_ref, rf_ref, vf_ref, lwf_ref, kaf_ref, kdf_ref,
                 nkkb_ref, rb_ref, vb_ref, lwb_ref, kab_ref, kdb_ref, s0_ref,
                 yf_ref, yb_ref, sfin_ref, s_ref):
    C = CHUNK
    j = pl.program_id(1)
    nj = pl.num_programs(1)
    row = lax.broadcasted_iota(jnp.int32, (C, C), 0)
    col = lax.broadcasted_iota(jnp.int32, (C, C), 1)
    dirs = ((nkkf_ref, rf_ref, vf_ref, lwf_ref, kaf_ref, kdf_ref, yf_ref),
            (nkkb_ref, rb_ref, vb_ref, lwb_ref, kab_ref, kdb_ref, yb_ref))
    n_chunks = nkkf_ref.shape[1] // C

    @pl.when(j == 0)
    def _():
        s_ref[...] = s0_ref[0]

    def build(c, d, probs):
        rows = pl.ds(pl.multiple_of(c * C, C), C)
        nkk_ref, r_ref, v_ref, lw_ref, b_ref, k_ref = dirs[d][0:6]
        alpha = nkk_ref[0, rows, :]
        r = r_ref[0, rows, :]
        v = v_ref[0, rows, :]
        before = (row > col) if d == 0 else (row < col)
        upto = (row >= col) if d == 0 else (row <= col)
        lw = lw_ref[0, rows, :]
        beta = b_ref[0, rows, :]
        kd = k_ref[0, rows, :]
        tri = upto.astype(BF16)
        l_hi, l_mid, l_lo = _split3(lw)
        lam = (jnp.dot(tri, l_hi, preferred_element_type=F32) + jnp.dot(tri, l_mid, preferred_element_type=F32)
               + jnp.dot(tri, l_lo, preferred_element_type=F32))
        lam_prev = lam - lw
        lam_mid = lam[C // 2:C // 2 + 1, :]
        lam_end = lam[C - 1:C, :] if d == 0 else lam[0:1, :]
        e_to_mid = jnp.exp(lam_mid - lam)
        a_m = alpha * jnp.exp(lam_prev - lam_mid)
        b_m = beta * e_to_mid
        k_m = kd * e_to_mid
        r_m = r * jnp.exp(lam - lam_mid)
        a_0 = alpha * jnp.exp(lam_prev)
        r_0 = r * jnp.exp(lam)
        e_to_end = jnp.exp(lam_end - lam)
        b_e = beta * e_to_end
        k_e = kd * e_to_end
        p_end = jnp.exp(lam_end)
        for h in range(H_A):
            sl = slice(h * RW_HEAD, (h + 1) * RW_HEAD)
            probs.append(dict(
                rows=rows, d=d, h=h, before=before, upto=upto, pc=p_end[:, sl],
                ar=jnp.concatenate([a_m[:, sl], r_m[:, sl]], axis=0),
                bk=jnp.concatenate([b_m[:, sl], k_m[:, sl]], axis=0),
                a0=a_0[:, sl], r0=r_0[:, sl], vh=v[:, sl],
                bke=jnp.concatenate([b_e[:, sl], k_e[:, sl]], axis=0)))

    def body(ci, carry):
        probs = []
        for cc in range(CHUNKS_PER_ITER):
            cf = ci * CHUNKS_PER_ITER + cc
            build(cf, 0, probs)
            build(n_chunks - 1 - cf, 1, probs)
        g12 = [_mm1(q['ar'], q['bk'], _NT) for q in probs]
        ms = [jnp.where(q['before'], g[0:C, 0:C], 0.0) for q, g in zip(probs, g12)]
        ns = [jnp.where(q['before'], g[0:C, C:2 * C], 0.0) for q, g in zip(probs, g12)]
        rbk = [jnp.concatenate([jnp.where(q['upto'], g[C:2 * C, 0:C], 0.0),
                                jnp.where(q['upto'], g[C:2 * C, C:2 * C], 0.0)], axis=1) for q, g in zip(probs, g12)]
        nvs = [_mm1(n, q['vh'], _NN) for q, n in zip(probs, ns)]
        ts = _unit_tri_inverses(ms, row, col)
        xs = [_mm2(t, jnp.concatenate([q['a0'], nv], axis=1), _NN) for q, t, nv in zip(probs, ts, nvs)]
        zeros = jnp.zeros((C, RW_HEAD), F32)
        ws = [jnp.concatenate([x, jnp.concatenate([zeros, q['vh']], axis=1)], axis=0) for q, x in zip(probs, xs)]
        rys = [_mm1(rb, w, _NN) for rb, w in zip(rbk, ws)]
        fgs = [_mm3(w, q['bke'], _TN) for q, w in zip(probs, ws)]
        ys = []
        for q, ry, fg in zip(probs, rys, fgs):
            s = s_ref[q['d'], q['h']]
            rp = q['r0'] + ry[:, 0:RW_HEAD]
            ys.append(ry[:, RW_HEAD:2 * RW_HEAD] + _mm1(rp, s, _NT))
            s_ref[q['d'], q['h']] = s * q['pc'] + _mm1(s, fg[0:RW_HEAD], _NN) + fg[RW_HEAD:2 * RW_HEAD]
        for k0 in range(0, len(probs), H_A):
            q0 = probs[k0]
            dirs[q0['d']][6][0, q0['rows'], :] = jnp.concatenate(ys[k0:k0 + H_A], axis=1)
        return carry

    lax.fori_loop(0, n_chunks // CHUNKS_PER_ITER, body, 0)

    @pl.when(j == nj - 1)
    def _():
        sfin_ref[0] = s_ref[...]


def _scan(nkk, r, v, lwf, kaf, kdf, lwb, kab, kdb, s0):
    B, L, _ = r.shape
    nj = L // TL
    tile = pl.BlockSpec((1, TL, D_A), lambda b, j: (b, j, 0))
    rtile = pl.BlockSpec((1, TL, D_A), lambda b, j: (b, nj - 1 - j, 0))
    st = pl.BlockSpec((1, 2, H_A, RW_HEAD, RW_HEAD), lambda b, j: (b, 0, 0, 0, 0))
    f32 = lambda *s: jax.ShapeDtypeStruct(s, F32)
    return pl.pallas_call(
        _scan_kernel, grid=(B, nj),
        in_specs=[tile] * 6 + [rtile] * 6 + [st],
        out_specs=[tile, rtile, st],
        out_shape=[f32(B, L, D_A), f32(B, L, D_A), f32(B, 2, H_A, RW_HEAD, RW_HEAD)],
        scratch_shapes=[pltpu.VMEM((2, H_A, RW_HEAD, RW_HEAD), F32)],
        compiler_params=_cparams(
            2, block_bytes=14 * _nbytes((TL, D_A), F32) + 2 * _nbytes((2, H_A, RW_HEAD, RW_HEAD), F32),
            scratch_bytes=_nbytes((2, H_A, RW_HEAD, RW_HEAD), F32),
            temp_bytes=2 * CHUNKS_PER_ITER * H_A * 12 * _nbytes((2 * CHUNK, 2 * CHUNK), F32)),
        name="rwkv_scan",
    )(nkk, r, v, lwf, kaf, kdf, nkk, r, v, lwb, kab, kdb, s0)


def _row_sum_bf16(e):
    n = e.shape[1] // 128
    acc = None
    for g0 in range(0, n, 4):
        part = e[:, g0 * 128:(g0 + 1) * 128]
        for j in range(g0 + 1, min(g0 + 4, n)):
            part = part + e[:, j * 128:(j + 1) * 128]
        pf = part.astype(F32)
        acc = pf if acc is None else acc + pf
    return jnp.sum(acc, axis=-1, keepdims=True)


def _attn_kernel(*refs, has_cache, lam_init):
    if has_cache:
        q_ref, k_ref, v_ref, ck_ref, cv_ref, lq1, lk1, lq2, lk2, sub_ref, o_ref = refs
    else:
        q_ref, k_ref, v_ref, lq1, lk1, lq2, lk2, sub_ref, o_ref = refs
    l1 = jnp.sum(lq1[...] * lk1[...], axis=-1, keepdims=True)
    l2 = jnp.sum(lq2[...] * lk2[...], axis=-1, keepdims=True)
    lam = jnp.exp(l1) - jnp.exp(l2) + lam_init
    q = q_ref[0].astype(BF16)
    kn = k_ref[0].astype(BF16)
    vn = v_ref[0].astype(BF16)
    if has_cache:
        kc = ck_ref[0, 0].astype(BF16)
        vc = cv_ref[0, 0].astype(BF16)
    dn = (((1,), (1,)), ((), ()))
    def scores(j):
        c0 = j * DIFF_DH
        qh = q[:, c0:c0 + DIFF_DH]
        s_n = lax.dot_general(qh, kn[:, c0:c0 + DIFF_DH], dn, preferred_element_type=F32)
        s_c = lax.dot_general(qh, kc[:, c0:c0 + DIFF_DH], dn, preferred_element_type=F32) if has_cache else None
        return s_n, s_c

    nxt = scores(0)
    for h in range(H_B):
        o = None
        for m in range(2):
            j = h * 2 + m
            s_n, s_c = nxt
            if j + 1 < 2 * H_B:
                nxt = scores(j + 1)
            mx = jnp.max(s_n, axis=-1, keepdims=True)
            if has_cache:
                mx = jnp.maximum(mx, jnp.max(s_c, axis=-1, keepdims=True))
            e_n = jnp.exp2((s_n - mx).astype(BF16))
            den = _row_sum_bf16(e_n)
            if has_cache:
                e_c = jnp.exp2((s_c - mx).astype(BF16))
                den = den + _row_sum_bf16(e_c)
            o_m = jnp.dot(e_n, vn[:, h * 128:(h + 1) * 128], preferred_element_type=F32)
            if has_cache:
                o_m = o_m + jnp.dot(e_c, vc[:, h * 128:(h + 1) * 128], preferred_element_type=F32)
            o = o_m / den if m == 0 else o - o_m * (lam / den)
        o_ref[0, :, h * 128:(h + 1) * 128] = _rms(o, sub_ref[...]) * (1.0 - lam_init)


def _attention(q, k, v, cache, l, p, lam_init):
    B, L, _ = q.shape
    has_cache = cache is not None
    tq = TL
    in_specs = [pl.BlockSpec((1, tq, D_B), lambda b, i: (b, i, 0)),
                pl.BlockSpec((1, L, D_B), lambda b, i: (b, 0, 0)),
                pl.BlockSpec((1, L, D_B), lambda b, i: (b, 0, 0))]
    args = [q, k, v]
    if has_cache:
        ck, cv = cache
        past = ck.shape[2]
        in_specs += [pl.BlockSpec((1, 1, past, D_B), lambda b, i: (b, l, 0, 0))] * 2
        args += [ck, cv]
    small = [_pspec(p, 'diff_lq1', (1, DIFF_DH)), _pspec(p, 'diff_lk1', (1, DIFF_DH)),
             _pspec(p, 'diff_lq2', (1, DIFF_DH)), _pspec(p, 'diff_lk2', (1, DIFF_DH)),
             _pspec(p, 'diff_subln', (1, 2 * DIFF_DH))]
    in_specs += [sp for sp, _ in small]
    args += [a for _, a in small]
    return pl.pallas_call(
        functools.partial(_attn_kernel, has_cache=has_cache, lam_init=lam_init),
        grid=(B, L // tq), in_specs=in_specs,
        out_specs=pl.BlockSpec((1, tq, D_B), lambda b, i: (b, i, 0)),
        out_shape=jax.ShapeDtypeStruct((B, L, D_B), F32),
        compiler_params=_cparams(
            2,
            block_bytes=(2 * _nbytes((tq, D_B), F32) + 2 * _nbytes((L, D_B), q.dtype)
                         + (2 * _nbytes((cache[0].shape[2], D_B), F32) if has_cache else 0)),
            temp_bytes=(2 * (_nbytes((tq, L + TL), F32) + _nbytes((tq, L + TL), BF16)) + 2 * _nbytes((L + TL, D_B), BF16)
                        + 4 * _nbytes((tq, D_B), F32))),
        name="diff_attention",
    )(*args)


def _hfilt_time_kernel(emb_ref, w1_ref, b1_ref, fr_ref, w2_ref, b2_ref, w3_ref, dec_ref, h_ref, ss_ref, acc_ref):
    i = pl.program_id(0)
    emb = emb_ref[...]
    fr = fr_ref[...]
    h = jnp.sin(fr * (jnp.dot(emb.astype(BF16), w1_ref[...].astype(BF16), preferred_element_type=F32) + b1_ref[...]))
    h = jnp.sin(fr * (jnp.dot(h.astype(BF16), w2_ref[...].astype(BF16), preferred_element_type=F32) + b2_ref[...]))
    h = jnp.dot(h.astype(BF16), w3_ref[...].astype(BF16), preferred_element_type=F32)
    h = h * jnp.exp(-emb[:, 0:1] * jnp.abs(dec_ref[...]))
    n = h.shape[0]
    row = lax.broadcasted_iota(jnp.int32, (n, 4 * D_C), 0) + i * n
    col = lax.broadcasted_iota(jnp.int32, (n, 4 * D_C), 1)
    is_bwd = ((col // D_C) % 2) == 1
    h = jnp.where(jnp.logical_and(is_bwd, row == 0), 0.0, h)
    h_ref[...] = h

    @pl.when(i == 0)
    def _():
        acc_ref[...] = jnp.zeros_like(acc_ref)

    acc_ref[...] += jnp.sum(h * h, axis=0, keepdims=True)
    ss_ref[...] = acc_ref[...]


def _hfilt_freq_kernel(h_ref, ss_ref, c_ref, s_ref, hr_ref, hi_ref, a_ref, b_ref, nyq_ref):
    i = pl.program_id(0)

    @pl.when(i == 0)
    def _():
        for o in range(2):
            hf = h_ref[:, o * 2 * D_C:o * 2 * D_C + D_C]
            hb = h_ref[:, o * 2 * D_C + D_C:(o + 1) * 2 * D_C]
            a = hf + hb
            n = a.shape[0]
            alt = 1.0 - 2.0 * (lax.broadcasted_iota(jnp.int32, (n, 1), 0) % 2).astype(F32)
            nyq_ref[:, o * D_C:(o + 1) * D_C] = jnp.sum(a * alt, axis=0, keepdims=True)
            a_ref[:, o * D_C:(o + 1) * D_C] = a.astype(BF16)
            b_ref[:, o * D_C:(o + 1) * D_C] = (hb - hf).astype(BF16)

    ss = ss_ref[...]
    tot = jnp.concatenate([ss[:, 0:D_C] + ss[:, D_C:2 * D_C], ss[:, 2 * D_C:3 * D_C] + ss[:, 3 * D_C:4 * D_C]], axis=1)
    scale = lax.rsqrt(tot + 1e-6)

    hr_ref[...] = jnp.dot(c_ref[...], a_ref[...], preferred_element_type=F32) * scale
    hi = jnp.dot(s_ref[...], b_ref[...], preferred_element_type=F32) * scale
    tf = hi.shape[0]
    row = lax.broadcasted_iota(jnp.int32, (tf, 1), 0) + i * tf
    hi_ref[...] = jnp.where(row == 0, nyq_ref[...] * scale, hi)


def _hyena_filters(L, p, tabs):
    emb, cmat, smat = tabs
    tr = 256
    params = [_pspec(p, 'hy_w1_pad', (128, HY_FFN)), _pspec(p, 'hy_b1', (1, HY_FFN)), _pspec(p, 'hy_freq', (1, HY_FFN)),
              _pspec(p, 'hy_w2', (HY_FFN, HY_FFN)), _pspec(p, 'hy_b2', (1, HY_FFN)),
              _pspec(p, 'hy_w3', (HY_FFN, 4 * D_C)), _pspec(p, 'hy_decay4', (1, 4 * D_C))]
    h_raw, ss = pl.pallas_call(
        _hfilt_time_kernel, grid=(L // tr,),
        in_specs=[pl.BlockSpec((tr, 128), lambda i: (i, 0))] + [sp for sp, _ in params],
        out_specs=[pl.BlockSpec((tr, 4 * D_C), lambda i: (i, 0)), pl.BlockSpec((1, 4 * D_C), lambda i: (0, 0))],
        out_shape=[jax.ShapeDtypeStruct((L, 4 * D_C), F32), jax.ShapeDtypeStruct((1, 4 * D_C), F32)],
        scratch_shapes=[pltpu.VMEM((1, 4 * D_C), F32)],
        compiler_params=_cparams(1, block_bytes=2 * _nbytes((tr, 4 * D_C), F32) + _nbytes((HY_FFN, 4 * D_C), F32),
                                 temp_bytes=3 * _nbytes((tr, 4 * D_C), F32)),
        name="hyena_filter_time",
    )(emb, *[a for _, a in params])
    tf = 256
    mat = pl.BlockSpec((tf, L), lambda i: (i, 0))
    return pl.pallas_call(
        _hfilt_freq_kernel, grid=(L // tf,),
        in_specs=[pl.BlockSpec((L, 4 * D_C), lambda i: (0, 0)), pl.BlockSpec((1, 4 * D_C), lambda i: (0, 0)),
                  mat, mat],
        out_specs=[pl.BlockSpec((tf, 2 * D_C), lambda i: (i, 0))] * 2,
        out_shape=[jax.ShapeDtypeStruct((L, 2 * D_C), F32)] * 2,
        scratch_shapes=[pltpu.VMEM((L, 2 * D_C), BF16)] * 2 + [pltpu.VMEM((1, 2 * D_C), F32)],
        compiler_params=_cparams(
            1, block_bytes=_nbytes((L, 4 * D_C), F32) + 2 * _nbytes((tf, L), BF16) + 2 * _nbytes((tf, 2 * D_C), F32),
            scratch_bytes=2 * _nbytes((L, 2 * D_C), BF16), temp_bytes=4 * _nbytes((L, D_C), F32)),
        name="hyena_filter_freq",
    )(h_raw, ss, cmat, smat)


def _hconv_kernel(z_ref, zt_ref, gate_ref, hr_ref, hi_ref, bias_ref, c_ref, s_ref, o_ref,
                  zb_ref, yr_ref, yi_ref, nyq_ref, *, BG, L):
    ph = pl.program_id(1)
    i = pl.program_id(2)
    inv_n = 1.0 / (2 * L)

    @pl.when(jnp.logical_and(ph == 0, i == 0))
    def _():
        alt = 1.0 - 2.0 * (lax.broadcasted_iota(jnp.int32, (L, 1), 0) % 2).astype(F32)
        for b in range(BG):
            z = z_ref[b]
            zb_ref[:, b * D_C:(b + 1) * D_C] = z.astype(BF16)
            nyq_ref[:, b * D_C:(b + 1) * D_C] = jnp.sum(z * alt, axis=0, keepdims=True)

    tf = c_ref.shape[0]

    @pl.when(ph == 0)
    def _():
        zr = jnp.dot(c_ref[...], zb_ref[...], preferred_element_type=F32)
        zs = jnp.dot(s_ref[...], zb_ref[...], preferred_element_type=F32)
        hr = hr_ref[...]
        hi = hi_ref[...]
        row = lax.broadcasted_iota(jnp.int32, (tf, 1), 0) + i * tf
        wgt = jnp.where(row == 0, inv_n, 2.0 * inv_n)
        rows = pl.ds(pl.multiple_of(i * tf, tf), tf)
        for b in range(BG):
            sl = slice(b * D_C, (b + 1) * D_C)
            yr_ref[rows, sl] = ((zr[:, sl] * hr + zs[:, sl] * hi) * wgt).astype(BF16)
            yi_ref[rows, sl] = ((zr[:, sl] * hi - zs[:, sl] * hr) * (2.0 * inv_n)).astype(BF16)

        @pl.when(i == 0)
        def _():
            for b in range(BG):
                sl = slice(b * D_C, (b + 1) * D_C)
                nyq_ref[:, sl] = nyq_ref[:, sl] * hi[0:1, :] * inv_n

    @pl.when(ph == 1)
    def _():
        y = (jnp.dot(c_ref[...], yr_ref[...], preferred_element_type=F32)
             - jnp.dot(s_ref[...], yi_ref[...], preferred_element_type=F32))
        row = lax.broadcasted_iota(jnp.int32, (tf, 1), 0) + i * tf
        alt = 1.0 - 2.0 * (row % 2).astype(F32)
        y = y + alt * nyq_ref[...]
        for b in range(BG):
            sl = slice(b * D_C, (b + 1) * D_C)
            o_ref[b] = gate_ref[b] * (y[:, sl] + bias_ref[...] * zt_ref[b])


def _hconv(z, gate, hr, hi, p, order, tabs):
    B, L, _ = z.shape
    if isinstance(p, _LayerParams):
        layer = p.layer
        bias_arg = p.stacked['hy_bias'].reshape(DEPTH, 2, 1, D_C)
        bias_spec = pl.BlockSpec((None, None, 1, D_C), lambda g, ph, i: (layer, order, 0, 0))
    else:
        bias_arg = p['hy_bias'][order].reshape(1, D_C)
        bias_spec = pl.BlockSpec((1, D_C), lambda g, ph, i: (0, 0))
    _, ch, sh = tabs
    BG = min(B, 16) if L <= 256 else min(B, 4)
    assert B % BG == 0
    tf = 256
    nt = L // tf
    mat = pl.BlockSpec((tf, L), lambda g, ph, i: (i, 0))
    tile = pl.BlockSpec((BG, tf, D_C), lambda g, ph, i: (g, i * ph, 0))
    hspec = pl.BlockSpec((tf, D_C), lambda g, ph, i: (i * (1 - ph), order))
    cols = BG * D_C
    return pl.pallas_call(
        functools.partial(_hconv_kernel, BG=BG, L=L), grid=(B // BG, 2, nt),
        in_specs=[pl.BlockSpec((BG, L, D_C), lambda g, ph, i: (g, 0, 0)), tile, tile, hspec, hspec,
                  bias_spec, mat, mat],
        out_specs=tile,
        out_shape=jax.ShapeDtypeStruct((B, L, D_C), F32),
        scratch_shapes=[pltpu.VMEM((L, cols), BF16)] * 3 + [pltpu.VMEM((1, cols), F32)],
        compiler_params=_cparams(
            3,
            block_bytes=(_nbytes((BG, L, D_C), F32) + 3 * _nbytes((BG, tf, D_C), F32) + 2 * _nbytes((tf, L), BF16)
                         + 2 * _nbytes((tf, D_C), F32)),
            scratch_bytes=3 * _nbytes((L, cols), BF16), temp_bytes=4 * _nbytes((tf, cols), F32)),
        name="hyena_conv",
    )(z, z, gate, hr, hi, bias_arg, ch, sh)


def _outffn_kernel(x_ref, yf_ref, yb_ref, bon_ref, g_ref, ob_ref, yc_ref, mod_ref,
                   lnw_ref, lnb_ref, gpost_ref, gffn_ref, gffn_post_ref, w_ref, w1_ref, w2_ref, o_ref):
    bd = _block_diag(D_A, RW_HEAD)
    y = yf_ref[0] + yb_ref[0]
    mu = _segsum(y, bd) * (1.0 / RW_HEAD)
    yc = y - mu
    var = _segsum(yc * yc, bd) * (1.0 / RW_HEAD)
    yn = yc * lax.rsqrt(var + GN_EPS) * lnw_ref[...] + lnb_ref[...]
    ya = (yn + bon_ref[0]) * g_ref[0]
    mix = (jnp.dot(ya.astype(BF16), w_ref[0:D_A, :], preferred_element_type=F32)
           + jnp.dot(ob_ref[0].astype(BF16), w_ref[D_A:D_A + D_B, :], preferred_element_type=F32)
           + jnp.dot(yc_ref[0].astype(BF16), w_ref[D_A + D_B:D_MODEL, :], preferred_element_type=F32))
    mod = mod_ref[0]
    gt1 = mod[:, 2 * D_MODEL:3 * D_MODEL]
    sh2 = mod[:, 3 * D_MODEL:4 * D_MODEL]
    sc2 = mod[:, 4 * D_MODEL:5 * D_MODEL]
    gt2 = mod[:, 5 * D_MODEL:6 * D_MODEL]
    x1 = x_ref[0] + gt1 * _rms(mix, gpost_ref[...])
    h = (_rms(x1, gffn_ref[...]) * (1.0 + sc2) + sh2).astype(BF16)
    acc = None
    for c in range(D_FF // FFN_CHUNK):
        a = jnp.dot(h, w1_ref[:, c * FFN_CHUNK:(c + 1) * FFN_CHUNK], preferred_element_type=F32)
        a = jnp.square(jnp.maximum(a, 0.0)).astype(BF16)
        part = jnp.dot(a, w2_ref[c * FFN_CHUNK:(c + 1) * FFN_CHUNK, :], preferred_element_type=F32)
        acc = part if acc is None else acc + part
    o_ref[0] = x1 + gt2 * _rms(acc, gffn_post_ref[...])


def _outffn(x, yf, yb, bon, g, ob, yc, mod, p, w_out, w1, w2):
    B, L, _ = x.shape
    mb = mod.shape[0]
    row = lambda b, i: (b, i, 0)
    c2 = lambda b, i: (0, 0)
    t = lambda n: pl.BlockSpec((1, TL, n), row)
    vecs = [_pspec(p, 'rwkv_ln_w', (1, D_A)), _pspec(p, 'rwkv_ln_b', (1, D_A)), _pspec(p, 'g_mix_post', (1, D_MODEL)),
            _pspec(p, 'g_ffn_pre', (1, D_MODEL)), _pspec(p, 'g_ffn_post', (1, D_MODEL))]
    in_specs = [t(D_MODEL), t(D_A), t(D_A), t(D_A), t(D_A), t(D_B), t(D_C),
                pl.BlockSpec((1, 1, 6 * D_MODEL), (lambda b, i: (b, 0, 0)) if mb > 1 else (lambda b, i: (0, 0, 0)))]
    in_specs += [sp for sp, _ in vecs]
    in_specs += [_layer_weight_spec(w_out, 2), _layer_weight_spec(w1, 2), _layer_weight_spec(w2, 2)]
    return pl.pallas_call(
        _outffn_kernel, grid=(B, L // TL), in_specs=in_specs,
        out_specs=t(D_MODEL), out_shape=jax.ShapeDtypeStruct((B, L, D_MODEL), F32),
        compiler_params=_cparams(
            2,
            block_bytes=(2 * _nbytes((TL, D_MODEL), F32) + 4 * _nbytes((TL, D_A), F32) + _nbytes((TL, D_B), F32)
                         + _nbytes((TL, D_C), F32) + _nbytes((D_MODEL, D_MODEL), BF16) + 2 * _nbytes((D_MODEL, D_FF), BF16)),
            temp_bytes=4 * _nbytes((TL, D_MODEL), F32) + 2 * _nbytes((TL, FFN_CHUNK), F32)),
        name="outproj_ffn",
    )(x, yf, yb, bon, g, ob, yc, mod, *[a for _, a in vecs], w_out[0], w1[0], w2[0])


@functools.lru_cache(maxsize=None)
def _rope_tables(L):
    rows = L // GRID_W
    row = np.repeat(np.arange(rows), GRID_W).astype(np.float64)
    col = np.tile(np.arange(GRID_W), rows).astype(np.float64)
    half = DIFF_DH // 2
    inv = ROPE_BASE ** (-np.arange(0, half, 2, dtype=np.float64) / half)
    ang_r = row[:, None] * inv[None]
    ang_c = col[:, None] * inv[None]
    ang = np.concatenate([ang_r, ang_r, ang_c, ang_c], axis=-1)
    cos, sin = np.cos(ang), np.sin(ang)
    quarter = (np.arange(DIFF_DH) // 16) % 2
    sin_a = np.where(quarter == 0, -sin, 0.0)
    sin_b = np.where(quarter == 1, sin, 0.0)
    rep = D_B // DIFF_DH
    return tuple(np.tile(t, (1, rep)).astype(np.float32) for t in (cos, sin_a, sin_b))


@functools.lru_cache(maxsize=None)
def _hyena_tables(L):
    t = np.linspace(0.0, 1.0, L, dtype=np.float32).astype(np.float64)[:, None]
    ang = (2.0 * math.pi / L) * np.arange(L, dtype=np.float64)[:, None]
    bands = np.linspace(1e-4, HY_BANDS - 1, HY_BANDS, dtype=np.float32).astype(np.float64)[None, :]
    emb = np.concatenate([t, np.cos(bands * ang), -np.sin(bands * ang)], axis=-1)
    emb = np.pad(emb, ((0, 0), (0, 128 - HY_EMB))).astype(np.float32)
    n = np.arange(L, dtype=np.int64)
    theta = ((n[:, None] * n[None, :]) % (2 * L)).astype(np.float64) * (math.pi / L)
    return emb, np.cos(theta).astype(BF16), np.sin(theta).astype(BF16)


def _layer(x, mod, p, wb, l, lam_init, cache, rope_tabs, hy_tabs, filt):
    B, L, _ = x.shape
    (uq, uk, uv, r, v, nkk, wf, kaf, kdf, wbk, kab, kdb, g, bon, x1h, x2h, hv) = _inproj(
        x, mod, wb['w_in'], rope_tabs, p)
    if cache is None:
        s0 = jnp.zeros((B, 2, H_A, RW_HEAD, RW_HEAD), F32)
        kv_cache = None
    else:
        s0 = cache[0]
        kv_cache = (cache[1], cache[2])
    yf, yb, sfin = _scan(nkk, r, v, wf, kaf, kdf, wbk, kab, kdb, s0)
    ob = _attention(uq, uk, uv, kv_cache, l, p, lam_init)
    hr, hi = filt
    z1 = _hconv(hv, x1h, hr, hi, p, 0, hy_tabs)
    yc = _hconv(z1, x2h, hr, hi, p, 1, hy_tabs)
    x2 = _outffn(x, yf, yb, bon, g, ob, yc, mod, p, wb['w_out'], wb['w_ff1'], wb['w_ff2'])
    return x2, (sfin, uk, uv)


def _derive_params(d):
    w1 = d['hy_w1']
    d['hy_w1_pad'] = jnp.pad(w1, [(0, 0)] * (w1.ndim - 2) + [(0, 128 - HY_EMB), (0, 0)])
    dec = d['hy_decay']
    d['hy_decay4'] = jnp.tile(dec, (1,) * (dec.ndim - 1) + (4,))
    return d


_LAYER_KEYS = ('g_mix_pre', 'g_mix_post', 'g_ffn_pre', 'g_ffn_post', 'rwkv_conv', 'rwkv_w0', 'rwkv_w2', 'rwkv_a0',
               'rwkv_a2', 'rwkv_g2', 'rwkv_kk', 'rwkv_ka', 'rwkv_rk', 'rwkv_ln_w', 'rwkv_ln_b', 'diff_lq1', 'diff_lk1',
               'diff_lq2', 'diff_lk2', 'diff_subln', 'hy_conv_w', 'hy_conv_b', 'hy_w1', 'hy_b1', 'hy_freq', 'hy_w2',
               'hy_b2', 'hy_w3', 'hy_decay', 'hy_bias')


def kernel(x_prompt, x_sample, state_rwkv, cache_k, cache_v, c, c_ctx, w_mod, b_mod, g_mix_pre, g_mix_post, g_ffn_pre, g_ffn_post, w_in, rwkv_conv, rwkv_w0, rwkv_w2, rwkv_a0, rwkv_a2, rwkv_g2, rwkv_kk, rwkv_ka, rwkv_rk, rwkv_ln_w, rwkv_ln_b, diff_lq1, diff_lk1, diff_lq2, diff_lk2, diff_subln, hy_conv_w, hy_conv_b, hy_w1, hy_b1, hy_freq, hy_w2, hy_b2, hy_w3, hy_decay, hy_bias, w_out, w_ff1, w_ff2):
    stacked = dict(g_mix_pre=g_mix_pre, g_mix_post=g_mix_post, g_ffn_pre=g_ffn_pre, g_ffn_post=g_ffn_post,
                   rwkv_conv=rwkv_conv, rwkv_w0=rwkv_w0, rwkv_w2=rwkv_w2, rwkv_a0=rwkv_a0, rwkv_a2=rwkv_a2,
                   rwkv_g2=rwkv_g2, rwkv_kk=rwkv_kk, rwkv_ka=rwkv_ka, rwkv_rk=rwkv_rk, rwkv_ln_w=rwkv_ln_w,
                   rwkv_ln_b=rwkv_ln_b, diff_lq1=diff_lq1, diff_lk1=diff_lk1, diff_lq2=diff_lq2, diff_lk2=diff_lk2,
                   diff_subln=diff_subln, hy_conv_w=hy_conv_w, hy_conv_b=hy_conv_b, hy_w1=hy_w1, hy_b1=hy_b1,
                   hy_freq=hy_freq, hy_w2=hy_w2, hy_b2=hy_b2, hy_w3=hy_w3, hy_decay=hy_decay, hy_bias=hy_bias)
    _derive_params(stacked)
    Bc, Lc, _ = x_prompt.shape
    Bs, Ls, _ = x_sample.shape
    past = cache_k.shape[2]

    cond = jnp.zeros((8, D_MODEL), F32).at[0:Bs].set(c).at[Bs].set(c_ctx)
    mod_all = _modulation(cond, w_mod, b_mod)

    rope_tabs = _rope_tables(Ls)
    tabs_c = _hyena_tables(Lc)
    tabs_s = _hyena_tables(Ls)
    ck = cache_k.reshape(Bs, DEPTH, past, D_B)
    cv = cache_v.reshape(Bs, DEPTH, past, D_B)

    w_in_b, w_out_b, w_ff1_b, w_ff2_b = (w.astype(BF16) for w in (w_in, w_out, w_ff1, w_ff2))
    xp, xs = x_prompt, x_sample
    st_list, k_list, v_list = [], [], []
    for l in range(DEPTH):
        p = _LayerParams(stacked, l)
        wb = dict(w_in=(w_in_b, l), w_out=(w_out_b, l), w_ff1=(w_ff1_b, l), w_ff2=(w_ff2_b, l))
        lam_init = 0.8 - 0.6 * math.exp(-0.3 * l)
        mod_lat = mod_all[l, 0:Bs].reshape(Bs, 1, 6 * D_MODEL)
        mod_ctx = mod_all[l, Bs:Bs + 1].reshape(1, 1, 6 * D_MODEL)
        filt_c = _hyena_filters(Lc, p, tabs_c)
        filt_s = _hyena_filters(Ls, p, tabs_s)
        xp, (s_ctx, k_ctx, v_ctx) = _layer(xp, mod_ctx, p, wb, l, lam_init, None, None, tabs_c, filt_c)
        st_list.append(s_ctx)
        k_list.append(k_ctx)
        v_list.append(v_ctx)
        s0 = state_rwkv[:, l]
        xs, _ = _layer(xs, mod_lat, p, wb, l, lam_init, (s0, ck, cv), rope_tabs, tabs_s, filt_s)
    new_k = jnp.stack(k_list, axis=1).reshape(Bc, DEPTH, Lc, H_B, 2, DIFF_DH)
    new_v = jnp.stack(v_list, axis=1).reshape(Bc, DEPTH, Lc, H_B, 2 * DIFF_DH)
    return (xp, xs, jnp.stack(st_list, axis=1), new_k, new_v)
```

```python
import functools
import math

import jax
import jax.numpy as jnp
import numpy as np
from jax import lax
from jax.experimental import pallas as pl
from jax.experimental.pallas import tpu as pltpu

F32 = jnp.float32
BF16 = jnp.bfloat16

D_MODEL = 1024
DEPTH = 4
GRID_W = 64
D_A = 256
RW_HEAD = 64
H_A = 4
D_B = 512
DIFF_DH = 64
H_B = 4
D_C = 256
HY_BANDS = 16
HY_EMB = 33
HY_FFN = 64
D_FF = 4096
FFN_CHUNK = 1024
ROPE_BASE = 10000.0
RMS_EPS = 1e-6
GN_EPS = 64e-5
N_A_COLS = 1152
IN_COLS = 3456

TL = 256
MIB = 1024 * 1024
V7X_VMEM_BYTES = 64 * MIB
V7X_VMEM_CEILING = V7X_VMEM_BYTES * 7 // 8
VMEM_FLOOR = 48 * MIB


def _nbytes(shape, dtype):
    return math.prod(shape) * jnp.dtype(dtype).itemsize


def _cparams(n_axes, block_bytes=0, scratch_bytes=0, temp_bytes=0):
    need = 2 * block_bytes + scratch_bytes + temp_bytes
    return pltpu.CompilerParams(dimension_semantics=("arbitrary",) * n_axes,
                                vmem_limit_bytes=int(min(V7X_VMEM_CEILING, max(VMEM_FLOOR, need))))


def _split3(a):
    hi = a.astype(BF16)
    r1 = a - hi.astype(F32)
    mid = r1.astype(BF16)
    lo = (r1 - mid.astype(F32)).astype(BF16)
    return hi, mid, lo


def _split2(a):
    hi = a.astype(BF16)
    lo = (a - hi.astype(F32)).astype(BF16)
    return hi, lo


def _segsum(a, bd):
    hi, lo = _split2(a)
    return jnp.dot(hi, bd, preferred_element_type=F32) + jnp.dot(lo, bd, preferred_element_type=F32)


def _block_diag(n, seg):
    r = lax.broadcasted_iota(jnp.int32, (n, n), 0) // seg
    c = lax.broadcasted_iota(jnp.int32, (n, n), 1) // seg
    return (r == c).astype(BF16)


class _LayerParams:
    def __init__(self, stacked, layer):
        self.stacked = stacked
        self.layer = layer

    def __getitem__(self, name):
        return self.stacked[name][self.layer]


def _pspec(p, name, shape):
    if isinstance(p, _LayerParams):
        stack = p.stacked[name]
        stack = stack.reshape((stack.shape[0],) + shape)
        layer = p.layer
        return pl.BlockSpec((None,) + shape, lambda *g: (layer,) + (0,) * len(shape)), stack
    return pl.BlockSpec(shape, lambda *g: (0,) * len(shape)), p[name].reshape(shape)


def _layer_weight_spec(w, n_grid_axes):
    stack, layer = w
    _, k, n = stack.shape
    if n_grid_axes == 2:
        return pl.BlockSpec((None, k, n), lambda b, i: (layer, 0, 0))
    raise NotImplementedError(n_grid_axes)


def _rms(x, g):
    return x * lax.rsqrt(jnp.mean(x * x, axis=-1, keepdims=True) + RMS_EPS) * g


def _mod_kernel(cond_ref, w_ref, b_ref, o_ref):
    c = cond_ref[...]
    s = (c * jax.nn.sigmoid(c)).astype(BF16)
    o_ref[0] = jnp.dot(s, w_ref[0].astype(BF16), preferred_element_type=F32) + b_ref[0]


def _modulation(cond, w_mod, b_mod):
    tn = 1536
    return pl.pallas_call(
        _mod_kernel,
        grid=(DEPTH, 6 * D_MODEL // tn),
        in_specs=[pl.BlockSpec((8, D_MODEL), lambda l, n: (0, 0)),
                  pl.BlockSpec((1, D_MODEL, tn), lambda l, n: (l, 0, n)),
                  pl.BlockSpec((1, 1, tn), lambda l, n: (l, 0, n))],
        out_specs=pl.BlockSpec((1, 8, tn), lambda l, n: (l, 0, n)),
        out_shape=jax.ShapeDtypeStruct((DEPTH, 8, 6 * D_MODEL), F32),
        compiler_params=_cparams(2, block_bytes=_nbytes((D_MODEL + 16, tn), F32),
                                 temp_bytes=_nbytes((D_MODEL, tn), BF16)),
        name="modulation",
    )(cond, w_mod, b_mod.reshape(DEPTH, 1, 6 * D_MODEL))


def _rope(x, cos, sin_a, sin_b):
    n = x.shape[-1]
    return x * cos + pltpu.roll(x, n - 16, axis=1) * sin_a + pltpu.roll(x, 16, axis=1) * sin_b


def _shift_conv(u, prev_row, next_row, cw):
    n = u.shape[0]
    row = lax.broadcasted_iota(jnp.int32, (n, 1), 0)
    up = jnp.where(row == 0, prev_row, pltpu.roll(u, 1, axis=0))
    un = jnp.where(row == n - 1, next_row, pltpu.roll(u, n - 1, axis=0))
    return cw[0:1] * up + cw[1:2] * u + cw[2:3] * un


def _prep_body(ua, prev, nxt, uh, prevh, nxth,
               cw_ref, w0_ref, w2_ref, a0_ref, a2_ref, g2_ref, kkw_ref, kaw_ref, rk_ref, hcw_ref, hcb_ref,
               r_ref, v_ref, nkk_ref, wf_ref, kaf_ref, kdf_ref, wb_ref, kab_ref, kdb_ref, g_ref, bon_ref,
               x1_ref, x2_ref, hv_ref):
    bd = _block_diag(D_A, RW_HEAD)
    u_rkv = ua[:, 0:3 * D_A]
    rkv = _shift_conv(u_rkv, prev, nxt, cw_ref[...])
    r = rkv[:, 0:D_A]
    k = rkv[:, D_A:2 * D_A]
    v = rkv[:, 2 * D_A:3 * D_A]
    u_w = ua[:, 768:896]
    u_a = ua[:, 896:1024]
    u_g = ua[:, 1024:1152]

    g = jnp.dot(jax.nn.sigmoid(u_g).astype(BF16), g2_ref[...].astype(BF16), preferred_element_type=F32)
    kk = k * kkw_ref[...]
    kk = kk * lax.rsqrt(_segsum(kk * kk, bd) + 1e-12)
    r_ref[0] = r
    v_ref[0] = v
    nkk_ref[0] = -kk
    g_ref[0] = g

    kd_sum = None
    outs = ((wf_ref, kaf_ref, kdf_ref), (wb_ref, kab_ref, kdb_ref))
    for d in range(2):
        uw_d = u_w[:, d * 64:(d + 1) * 64]
        ua_d = u_a[:, d * 64:(d + 1) * 64]
        xw = w0_ref[d:d + 1, :] + jnp.dot(jnp.tanh(uw_d).astype(BF16), w2_ref[d].astype(BF16),
                                          preferred_element_type=F32)
        z = -xw
        softplus = jnp.maximum(z, 0.0) + jnp.log(1.0 + jnp.exp(-jnp.abs(z)))
        wl = -softplus - 0.5
        log_decay = -jnp.exp(wl)
        a = jax.nn.sigmoid(a0_ref[d:d + 1, :] + jnp.dot(ua_d.astype(BF16), a2_ref[d].astype(BF16),
                                                        preferred_element_type=F32))
        kd = k * (1.0 + (a - 1.0) * kaw_ref[...])
        w_o, ka_o, kd_o = outs[d]
        w_o[0] = log_decay
        ka_o[0] = kk * a
        kd_o[0] = kd
        kd_sum = kd if kd_sum is None else kd_sum + kd
    bon_ref[0] = _segsum(r * kd_sum * rk_ref[...], bd) * v

    hc = _shift_conv(uh, prevh, nxth, hcw_ref[...]) + hcb_ref[...]
    x1_ref[0] = hc[:, 0:D_C]
    x2_ref[0] = hc[:, D_C:2 * D_C]
    hv_ref[0] = hc[:, 2 * D_C:3 * D_C]


N_PREP_PARAMS = 11
N_PREP_OUTS = 14


def _inproj_kernel(*refs, rope):
    n_in = 6 + (3 if rope else 0)
    x_ref, xp_ref, xn_ref, mod_ref, g_ref, w_ref = refs[0:6]
    rope_refs = refs[6:n_in]
    prep_params = refs[n_in:n_in + N_PREP_PARAMS]
    uq_ref, uk_ref, uv_ref = refs[n_in + N_PREP_PARAMS:n_in + N_PREP_PARAMS + 3]
    prep_outs = refs[n_in + N_PREP_PARAMS + 3:]
    i = pl.program_id(1)
    nt = pl.num_programs(1)
    mod = mod_ref[0]
    sh1 = mod[:, 0:D_MODEL]
    sc1 = mod[:, D_MODEL:2 * D_MODEL]

    def normed(x):
        return (_rms(x, g_ref[...]) * (1.0 + sc1) + sh1).astype(BF16)

    n = x_ref.shape[1]
    h_all = jnp.concatenate([normed(x_ref[0]), normed(xp_ref[0]), normed(xn_ref[0])], axis=0)
    u_all = jnp.dot(h_all, w_ref[...], preferred_element_type=F32)
    u = u_all[0:n]
    q = u[:, N_A_COLS:N_A_COLS + D_B] * (DIFF_DH ** -0.5 * math.log2(math.e))
    k = u[:, N_A_COLS + D_B:N_A_COLS + 2 * D_B]
    if rope:
        cos_ref, sa_ref, sb_ref = rope_refs
        q = _rope(q, cos_ref[...], sa_ref[...], sb_ref[...])
        k = _rope(k, cos_ref[...], sa_ref[...], sb_ref[...])
    uq_ref[0] = q.astype(uq_ref.dtype)
    uk_ref[0] = k.astype(uk_ref.dtype)
    uv_ref[0] = u[:, N_A_COLS + 2 * D_B:N_A_COLS + 3 * D_B].astype(uv_ref.dtype)

    hy0 = N_A_COLS + 3 * D_B
    prev_row = u_all[n + 7:n + 8]
    next_row = u_all[n + 8:n + 9]
    has_prev = i > 0
    has_next = i < nt - 1
    _prep_body(u[:, 0:N_A_COLS], jnp.where(has_prev, prev_row[:, 0:3 * D_A], 0.0),
               jnp.where(has_next, next_row[:, 0:3 * D_A], 0.0),
               u[:, hy0:IN_COLS], jnp.where(has_prev, prev_row[:, hy0:IN_COLS], 0.0),
               jnp.where(has_next, next_row[:, hy0:IN_COLS], 0.0),
               *prep_params, *prep_outs)


def _inproj(x, mod, w_in, rope_tabs, p):
    B, L, _ = x.shape
    rope = rope_tabs is not None
    mb = mod.shape[0]
    nb8 = L // 8
    t8 = TL // 8
    row = lambda b, i: (b, i, 0)
    prev = lambda b, i: (b, jnp.maximum(i * t8 - 1, 0), 0)
    nxt = lambda b, i: (b, jnp.minimum((i + 1) * t8, nb8 - 1), 0)
    const2 = lambda b, i: (0, 0)

    params = [_pspec(p, 'rwkv_conv', (3, 3 * D_A)), _pspec(p, 'rwkv_w0', (2, D_A)),
              _pspec(p, 'rwkv_w2', (2, 64, D_A)), _pspec(p, 'rwkv_a0', (2, D_A)), _pspec(p, 'rwkv_a2', (2, 64, D_A)),
              _pspec(p, 'rwkv_g2', (128, D_A)), _pspec(p, 'rwkv_kk', (1, D_A)), _pspec(p, 'rwkv_ka', (1, D_A)),
              _pspec(p, 'rwkv_rk', (1, D_A)), _pspec(p, 'hy_conv_w', (3, 3 * D_C)),
              _pspec(p, 'hy_conv_b', (1, 3 * D_C))]
    assert len(params) == N_PREP_PARAMS
    g_spec, g_arg = _pspec(p, 'g_mix_pre', (1, D_MODEL))
    in_specs = [pl.BlockSpec((1, TL, D_MODEL), row), pl.BlockSpec((1, 8, D_MODEL), prev),
                pl.BlockSpec((1, 8, D_MODEL), nxt),
                pl.BlockSpec((1, 1, 6 * D_MODEL), (lambda b, i: (b, 0, 0)) if mb > 1 else (lambda b, i: (0, 0, 0))),
                g_spec,
                _layer_weight_spec(w_in, 2)]
    args = [x, x, x, mod, g_arg, w_in[0]]
    if rope:
        in_specs += [pl.BlockSpec((TL, D_B), lambda b, i: (i, 0))] * 3
        args += list(rope_tabs)
    in_specs += [s for s, _ in params]
    args += [a for _, a in params]
    qkv_dt = BF16 if rope else F32
    out_shape = ([jax.ShapeDtypeStruct((B, L, D_B), qkv_dt)] * 3
                 + [jax.ShapeDtypeStruct((B, L, D_A), F32)] * N_PREP_OUTS)
    out_specs = [pl.BlockSpec((1, TL, D_B), row)] * 3 + [pl.BlockSpec((1, TL, D_A), row)] * N_PREP_OUTS
    return pl.pallas_call(
        functools.partial(_inproj_kernel, rope=rope),
        grid=(B, L // TL), in_specs=in_specs, out_specs=out_specs, out_shape=out_shape,
        compiler_params=_cparams(
            2,
            block_bytes=(_nbytes((TL + 16, D_MODEL), F32) + _nbytes((D_MODEL, IN_COLS), BF16)
                         + 3 * _nbytes((TL, D_B), F32) * (2 if rope else 1) + N_PREP_OUTS * _nbytes((TL, D_A), F32)),
            temp_bytes=3 * _nbytes((TL + 16, IN_COLS), F32)),
        name="inproj",
    )(*args)


CHUNK = 64
CHUNKS_PER_ITER = 4
_NN = (((1,), (0,)), ((), ()))
_NT = (((1,), (1,)), ((), ()))
_TN = (((0,), (0,)), ((), ()))


def _mm3(a, b, dims):
    ah, al = _split2(a)
    bh, bl = _split2(b)
    f = lambda x, y: lax.dot_general(x, y, dims, preferred_element_type=F32)
    if dims is _TN:
        return f(ah, bh) + f(ah, bl) + f(al, bh)
    m = a.shape[0]
    both = f(jnp.concatenate([ah, al], axis=0), bh)
    return both[0:m] + both[m:2 * m] + f(ah, bl)


def _mm2(a, b, dims):
    ah = a.astype(BF16)
    bh, bl = _split2(b)
    n = b.shape[1]
    both = lax.dot_general(ah, jnp.concatenate([bh, bl], axis=1), dims, preferred_element_type=F32)
    return both[:, 0:n] + both[:, n:2 * n]


def _mm1(a, b, dims):
    return lax.dot_general(a.astype(BF16), b.astype(BF16), dims, preferred_element_type=F32)


def _unit_tri_inverses(ms, row, col):
    eye = (row == col).astype(F32)
    blk = lambda n: (row // n) == (col // n)
    m8 = [jnp.where(blk(8), m, 0.0) for m in ms]
    a2 = [_mm1(x, x, _NN) for x in m8]
    a4 = [_mm1(x, x, _NN) for x in a2]
    p = [eye + x + y + _mm1(x, y, _NN) for x, y in zip(m8, a2)]
    ts = [x + _mm1(x, y, _NN) for x, y in zip(p, a4)]
    n = 8
    while n < CHUNK:
        sel = jnp.logical_and(blk(2 * n), jnp.logical_not(blk(n)))
        ot = [_mm1(jnp.where(sel, m, 0.0), t, _NN) for m, t in zip(ms, ts)]
        ts = [t + _mm1(t, o, _NN) for t, o in zip(ts, ot)]
        n *= 2
    return ts


def _scan_kernel(nkkf_ref, rf_ref, vf_ref, lwf_ref, kaf_ref, kdf_ref,
                 nkkb_ref, rb_ref, vb_ref, lwb_ref, kab_ref, kdb_ref, s0_ref,
                 yf_ref, yb_ref, sfin_ref, s_ref):
    C = CHUNK
    j = pl.program_id(1)
    nj = pl.num_programs(1)
    row = lax.broadcasted_iota(jnp.int32, (C, C), 0)
    col = lax.broadcasted_iota(jnp.int32, (C, C), 1)
    dirs = ((nkkf_ref, rf_ref, vf_ref, lwf_ref, kaf_ref, kdf_ref, yf_ref),
            (nkkb_ref, rb_ref, vb_ref, lwb_ref, kab_ref, kdb_ref, yb_ref))
    n_chunks = nkkf_ref.shape[1] // C

    @pl.when(j == 0)
    def _():
        s_ref[...] = s0_ref[0]

    def build(c, d, probs):
        rows = pl.ds(pl.multiple_of(c * C, C), C)
        nkk_ref, r_ref, v_ref, lw_ref, b_ref, k_ref = dirs[d][0:6]
        alpha = nkk_ref[0, rows, :]
        r = r_ref[0, rows, :]
        v = v_ref[0, rows, :]
        before = (row > col) if d == 0 else (row < col)
        upto = (row >= col) if d == 0 else (row <= col)
        lw = lw_ref[0, rows, :]
        beta = b_ref[0, rows, :]
        kd = k_ref[0, rows, :]
        tri = upto.astype(BF16)
        l_hi, l_mid, l_lo = _split3(lw)
        lam = (jnp.dot(tri, l_hi, preferred_element_type=F32) + jnp.dot(tri, l_mid, preferred_element_type=F32)
               + jnp.dot(tri, l_lo, preferred_element_type=F32))
        lam_prev = lam - lw
        lam_mid = lam[C // 2:C // 2 + 1, :]
        lam_end = lam[C - 1:C, :] if d == 0 else lam[0:1, :]
        e_to_mid = jnp.exp(lam_mid - lam)
        a_m = alpha * jnp.exp(lam_prev - lam_mid)
        b_m = beta * e_to_mid
        k_m = kd * e_to_mid
        r_m = r * jnp.exp(lam - lam_mid)
        a_0 = alpha * jnp.exp(lam_prev)
        r_0 = r * jnp.exp(lam)
        e_to_end = jnp.exp(lam_end - lam)
        b_e = beta * e_to_end
        k_e = kd * e_to_end
        p_end = jnp.exp(lam_end)
        for h in range(H_A):
            sl = slice(h * RW_HEAD, (h + 1) * RW_HEAD)
            probs.append(dict(
                rows=rows, d=d, h=h, before=before, upto=upto, pc=p_end[:, sl],
                ar=jnp.concatenate([a_m[:, sl], r_m[:, sl]], axis=0),
                bk=jnp.concatenate([b_m[:, sl], k_m[:, sl]], axis=0),
                a0=a_0[:, sl], r0=r_0[:, sl], vh=v[:, sl],
                bke=jnp.concatenate([b_e[:, sl], k_e[:, sl]], axis=0)))

    def body(ci, carry):
        probs = []
        for cc in range(CHUNKS_PER_ITER):
            cf = ci * CHUNKS_PER_ITER + cc
            build(cf, 0, probs)
            build(n_chunks - 1 - cf, 1, probs)
        g12 = [_mm1(q['ar'], q['bk'], _NT) for q in probs]
        ms = [jnp.where(q['before'], g[0:C, 0:C], 0.0) for q, g in zip(probs, g12)]
        ns = [jnp.where(q['before'], g[0:C, C:2 * C], 0.0) for q, g in zip(probs, g12)]
        rbk = [jnp.concatenate([jnp.where(q['upto'], g[C:2 * C, 0:C], 0.0),
                                jnp.where(q['upto'], g[C:2 * C, C:2 * C], 0.0)], axis=1) for q, g in zip(probs, g12)]
        nvs = [_mm1(n, q['vh'], _NN) for q, n in zip(probs, ns)]
        ts = _unit_tri_inverses(ms, row, col)
        xs = [_mm2(t, jnp.concatenate([q['a0'], nv], axis=1), _NN) for q, t, nv in zip(probs, ts, nvs)]
        zeros = jnp.zeros((C, RW_HEAD), F32)
        ws = [jnp.concatenate([x, jnp.concatenate([zeros, q['vh']], axis=1)], axis=0) for q, x in zip(probs, xs)]
        rys = [_mm1(rb, w, _NN) for rb, w in zip(rbk, ws)]
        fgs = [_mm2(w, q['bke'], _TN) for q, w in zip(probs, ws)]
        ys = []
        for q, ry, fg in zip(probs, rys, fgs):
            s = s_ref[q['d'], q['h']]
            rp = q['r0'] + ry[:, 0:RW_HEAD]
            ys.append(ry[:, RW_HEAD:2 * RW_HEAD] + _mm1(rp, s, _NT))
            s_ref[q['d'], q['h']] = s * q['pc'] + _mm1(s, fg[0:RW_HEAD], _NN) + fg[RW_HEAD:2 * RW_HEAD]
        for k0 in range(0, len(probs), H_A):
            q0 = probs[k0]
            dirs[q0['d']][6][0, q0['rows'], :] = jnp.concatenate(ys[k0:k0 + H_A], axis=1)
        return carry

    lax.fori_loop(0, n_chunks // CHUNKS_PER_ITER, body, 0)

    @pl.when(j == nj - 1)
    def _():
        sfin_ref[0] = s_ref[...]


def _scan(nkk, r, v, lwf, kaf, kdf, lwb, kab, kdb, s0):
    B, L, _ = r.shape
    nj = L // TL
    tile = pl.BlockSpec((1, TL, D_A), lambda b, j: (b, j, 0))
    rtile = pl.BlockSpec((1, TL, D_A), lambda b, j: (b, nj - 1 - j, 0))
    st = pl.BlockSpec((1, 2, H_A, RW_HEAD, RW_HEAD), lambda b, j: (b, 0, 0, 0, 0))
    f32 = lambda *s: jax.ShapeDtypeStruct(s, F32)
    return pl.pallas_call(
        _scan_kernel, grid=(B, nj),
        in_specs=[tile] * 6 + [rtile] * 6 + [st],
        out_specs=[tile, rtile, st],
        out_shape=[f32(B, L, D_A), f32(B, L, D_A), f32(B, 2, H_A, RW_HEAD, RW_HEAD)],
        scratch_shapes=[pltpu.VMEM((2, H_A, RW_HEAD, RW_HEAD), F32)],
        compiler_params=_cparams(
            2, block_bytes=14 * _nbytes((TL, D_A), F32) + 2 * _nbytes((2, H_A, RW_HEAD, RW_HEAD), F32),
            scratch_bytes=_nbytes((2, H_A, RW_HEAD, RW_HEAD), F32),
            temp_bytes=2 * CHUNKS_PER_ITER * H_A * 12 * _nbytes((2 * CHUNK, 2 * CHUNK), F32)),
        name="rwkv_scan",
    )(nkk, r, v, lwf, kaf, kdf, nkk, r, v, lwb, kab, kdb, s0)


def _row_sum_bf16(e):
    n = e.shape[1] // 128
    acc = None
    for g0 in range(0, n, 4):
        part = e[:, g0 * 128:(g0 + 1) * 128]
        for j in range(g0 + 1, min(g0 + 4, n)):
            part = part + e[:, j * 128:(j + 1) * 128]
        pf = part.astype(F32)
        acc = pf if acc is None else acc + pf
    return jnp.sum(acc, axis=-1, keepdims=True)


def _attn_kernel(*refs, has_cache, lam_init):
    if has_cache:
        q_ref, k_ref, v_ref, ck_ref, cv_ref, lq1, lk1, lq2, lk2, sub_ref, o_ref = refs
    else:
        q_ref, k_ref, v_ref, lq1, lk1, lq2, lk2, sub_ref, o_ref = refs
    l1 = jnp.sum(lq1[...] * lk1[...], axis=-1, keepdims=True)
    l2 = jnp.sum(lq2[...] * lk2[...], axis=-1, keepdims=True)
    lam = jnp.exp(l1) - jnp.exp(l2) + lam_init
    q = q_ref[0].astype(BF16)
    kn = k_ref[0].astype(BF16)
    vn = v_ref[0].astype(BF16)
    if has_cache:
        kc = ck_ref[0, 0].astype(BF16)
        vc = cv_ref[0, 0].astype(BF16)
    dn = (((1,), (1,)), ((), ()))
    def scores(j):
        c0 = j * DIFF_DH
        qh = q[:, c0:c0 + DIFF_DH]
        s_n = lax.dot_general(qh, kn[:, c0:c0 + DIFF_DH], dn, preferred_element_type=F32)
        s_c = lax.dot_general(qh, kc[:, c0:c0 + DIFF_DH], dn, preferred_element_type=F32) if has_cache else None
        return s_n, s_c

    nxt = scores(0)
    for h in range(H_B):
        o = None
        for m in range(2):
            j = h * 2 + m
            s_n, s_c = nxt
            if j + 1 < 2 * H_B:
                nxt = scores(j + 1)
            mx = jnp.max(s_n, axis=-1, keepdims=True)
            if has_cache:
                mx = jnp.maximum(mx, jnp.max(s_c, axis=-1, keepdims=True))
            e_n = jnp.exp2((s_n - mx).astype(BF16))
            den = _row_sum_bf16(e_n)
            if has_cache:
                e_c = jnp.exp2((s_c - mx).astype(BF16))
                den = den + _row_sum_bf16(e_c)
            o_m = jnp.dot(e_n, vn[:, h * 128:(h + 1) * 128], preferred_element_type=F32)
            if has_cache:
                o_m = o_m + jnp.dot(e_c, vc[:, h * 128:(h + 1) * 128], preferred_element_type=F32)
            o = o_m / den if m == 0 else o - o_m * (lam / den)
        o_ref[0, :, h * 128:(h + 1) * 128] = _rms(o, sub_ref[...]) * (1.0 - lam_init)


def _attention(q, k, v, cache, l, p, lam_init):
    B, L, _ = q.shape
    has_cache = cache is not None
    tq = TL
    in_specs = [pl.BlockSpec((1, tq, D_B), lambda b, i: (b, i, 0)),
                pl.BlockSpec((1, L, D_B), lambda b, i: (b, 0, 0)),
                pl.BlockSpec((1, L, D_B), lambda b, i: (b, 0, 0))]
    args = [q, k, v]
    if has_cache:
        ck, cv = cache
        past = ck.shape[2]
        in_specs += [pl.BlockSpec((1, 1, past, D_B), lambda b, i: (b, l, 0, 0))] * 2
        args += [ck, cv]
    small = [_pspec(p, 'diff_lq1', (1, DIFF_DH)), _pspec(p, 'diff_lk1', (1, DIFF_DH)),
             _pspec(p, 'diff_lq2', (1, DIFF_DH)), _pspec(p, 'diff_lk2', (1, DIFF_DH)),
             _pspec(p, 'diff_subln', (1, 2 * DIFF_DH))]
    in_specs += [sp for sp, _ in small]
    args += [a for _, a in small]
    return pl.pallas_call(
        functools.partial(_attn_kernel, has_cache=has_cache, lam_init=lam_init),
        grid=(B, L // tq), in_specs=in_specs,
        out_specs=pl.BlockSpec((1, tq, D_B), lambda b, i: (b, i, 0)),
        out_shape=jax.ShapeDtypeStruct((B, L, D_B), F32),
        compiler_params=_cparams(
            2,
            block_bytes=(2 * _nbytes((tq, D_B), F32) + 2 * _nbytes((L, D_B), q.dtype)
                         + (2 * _nbytes((cache[0].shape[2], D_B), F32) if has_cache else 0)),
            temp_bytes=(2 * (_nbytes((tq, L + TL), F32) + _nbytes((tq, L + TL), BF16)) + 2 * _nbytes((L + TL, D_B), BF16)
                        + 4 * _nbytes((tq, D_B), F32))),
        name="diff_attention",
    )(*args)


def _hfilt_time_kernel(emb_ref, w1_ref, b1_ref, fr_ref, w2_ref, b2_ref, w3_ref, dec_ref, h_ref, ss_ref, acc_ref):
    i = pl.program_id(0)
    emb = emb_ref[...]
    fr = fr_ref[...]
    h = jnp.sin(fr * (jnp.dot(emb.astype(BF16), w1_ref[...].astype(BF16), preferred_element_type=F32) + b1_ref[...]))
    h = jnp.sin(fr * (jnp.dot(h.astype(BF16), w2_ref[...].astype(BF16), preferred_element_type=F32) + b2_ref[...]))
    h = jnp.dot(h.astype(BF16), w3_ref[...].astype(BF16), preferred_element_type=F32)
    h = h * jnp.exp(-emb[:, 0:1] * jnp.abs(dec_ref[...]))
    n = h.shape[0]
    row = lax.broadcasted_iota(jnp.int32, (n, 4 * D_C), 0) + i * n
    col = lax.broadcasted_iota(jnp.int32, (n, 4 * D_C), 1)
    is_bwd = ((col // D_C) % 2) == 1
    h = jnp.where(jnp.logical_and(is_bwd, row == 0), 0.0, h)
    h_ref[...] = h

    @pl.when(i == 0)
    def _():
        acc_ref[...] = jnp.zeros_like(acc_ref)

    acc_ref[...] += jnp.sum(h * h, axis=0, keepdims=True)
    ss_ref[...] = acc_ref[...]


def _hfilt_freq_kernel(h_ref, ss_ref, c_ref, s_ref, hr_ref, hi_ref, a_ref, b_ref, nyq_ref):
    i = pl.program_id(0)

    @pl.when(i == 0)
    def _():
        for o in range(2):
            hf = h_ref[:, o * 2 * D_C:o * 2 * D_C + D_C]
            hb = h_ref[:, o * 2 * D_C + D_C:(o + 1) * 2 * D_C]
            a = hf + hb
            n = a.shape[0]
            alt = 1.0 - 2.0 * (lax.broadcasted_iota(jnp.int32, (n, 1), 0) % 2).astype(F32)
            nyq_ref[:, o * D_C:(o + 1) * D_C] = jnp.sum(a * alt, axis=0, keepdims=True)
            a_ref[:, o * D_C:(o + 1) * D_C] = a.astype(BF16)
            b_ref[:, o * D_C:(o + 1) * D_C] = (hb - hf).astype(BF16)

    ss = ss_ref[...]
    tot = jnp.concatenate([ss[:, 0:D_C] + ss[:, D_C:2 * D_C], ss[:, 2 * D_C:3 * D_C] + ss[:, 3 * D_C:4 * D_C]], axis=1)
    scale = lax.rsqrt(tot + 1e-6)

    hr_ref[...] = jnp.dot(c_ref[...], a_ref[...], preferred_element_type=F32) * scale
    hi = jnp.dot(s_ref[...], b_ref[...], preferred_element_type=F32) * scale
    tf = hi.shape[0]
    row = lax.broadcasted_iota(jnp.int32, (tf, 1), 0) + i * tf
    hi_ref[...] = jnp.where(row == 0, nyq_ref[...] * scale, hi)


def _hyena_filters(L, p, tabs):
    emb, cmat, smat = tabs
    tr = 256
    params = [_pspec(p, 'hy_w1_pad', (128, HY_FFN)), _pspec(p, 'hy_b1', (1, HY_FFN)), _pspec(p, 'hy_freq', (1, HY_FFN)),
              _pspec(p, 'hy_w2', (HY_FFN, HY_FFN)), _pspec(p, 'hy_b2', (1, HY_FFN)),
              _pspec(p, 'hy_w3', (HY_FFN, 4 * D_C)), _pspec(p, 'hy_decay4', (1, 4 * D_C))]
    h_raw, ss = pl.pallas_call(
        _hfilt_time_kernel, grid=(L // tr,),
        in_specs=[pl.BlockSpec((tr, 128), lambda i: (i, 0))] + [sp for sp, _ in params],
        out_specs=[pl.BlockSpec((tr, 4 * D_C), lambda i: (i, 0)), pl.BlockSpec((1, 4 * D_C), lambda i: (0, 0))],
        out_shape=[jax.ShapeDtypeStruct((L, 4 * D_C), F32), jax.ShapeDtypeStruct((1, 4 * D_C), F32)],
        scratch_shapes=[pltpu.VMEM((1, 4 * D_C), F32)],
        compiler_params=_cparams(1, block_bytes=2 * _nbytes((tr, 4 * D_C), F32) + _nbytes((HY_FFN, 4 * D_C), F32),
                                 temp_bytes=3 * _nbytes((tr, 4 * D_C), F32)),
        name="hyena_filter_time",
    )(emb, *[a for _, a in params])
    tf = 256
    mat = pl.BlockSpec((tf, L), lambda i: (i, 0))
    return pl.pallas_call(
        _hfilt_freq_kernel, grid=(L // tf,),
        in_specs=[pl.BlockSpec((L, 4 * D_C), lambda i: (0, 0)), pl.BlockSpec((1, 4 * D_C), lambda i: (0, 0)),
                  mat, mat],
        out_specs=[pl.BlockSpec((tf, 2 * D_C), lambda i: (i, 0))] * 2,
        out_shape=[jax.ShapeDtypeStruct((L, 2 * D_C), F32)] * 2,
        scratch_shapes=[pltpu.VMEM((L, 2 * D_C), BF16)] * 2 + [pltpu.VMEM((1, 2 * D_C), F32)],
        compiler_params=_cparams(
            1, block_bytes=_nbytes((L, 4 * D_C), F32) + 2 * _nbytes((tf, L), BF16) + 2 * _nbytes((tf, 2 * D_C), F32),
            scratch_bytes=2 * _nbytes((L, 2 * D_C), BF16), temp_bytes=4 * _nbytes((L, D_C), F32)),
        name="hyena_filter_freq",
    )(h_raw, ss, cmat, smat)


def _hconv_kernel(z_ref, zt_ref, gate_ref, hr_ref, hi_ref, bias_ref, c_ref, s_ref, o_ref,
                  zb_ref, yr_ref, yi_ref, nyq_ref, *, BG, L):
    ph = pl.program_id(1)
    i = pl.program_id(2)
    inv_n = 1.0 / (2 * L)

    @pl.when(jnp.logical_and(ph == 0, i == 0))
    def _():
        alt = 1.0 - 2.0 * (lax.broadcasted_iota(jnp.int32, (L, 1), 0) % 2).astype(F32)
        for b in range(BG):
            z = z_ref[b]
            zb_ref[:, b * D_C:(b + 1) * D_C] = z.astype(BF16)
            nyq_ref[:, b * D_C:(b + 1) * D_C] = jnp.sum(z * alt, axis=0, keepdims=True)

    tf = c_ref.shape[0]

    @pl.when(ph == 0)
    def _():
        zr = jnp.dot(c_ref[...], zb_ref[...], preferred_element_type=F32)
        zs = jnp.dot(s_ref[...], zb_ref[...], preferred_element_type=F32)
        hr = hr_ref[...]
        hi = hi_ref[...]
        row = lax.broadcasted_iota(jnp.int32, (tf, 1), 0) + i * tf
        wgt = jnp.where(row == 0, inv_n, 2.0 * inv_n)
        rows = pl.ds(pl.multiple_of(i * tf, tf), tf)
        for b in range(BG):
            sl = slice(b * D_C, (b + 1) * D_C)
            yr_ref[rows, sl] = ((zr[:, sl] * hr + zs[:, sl] * hi) * wgt).astype(BF16)
            yi_ref[rows, sl] = ((zr[:, sl] * hi - zs[:, sl] * hr) * (2.0 * inv_n)).astype(BF16)

        @pl.when(i == 0)
        def _():
            for b in range(BG):
                sl = slice(b * D_C, (b + 1) * D_C)
                nyq_ref[:, sl] = nyq_ref[:, sl] * hi[0:1, :] * inv_n

    @pl.when(ph == 1)
    def _():
        y = (jnp.dot(c_ref[...], yr_ref[...], preferred_element_type=F32)
             - jnp.dot(s_ref[...], yi_ref[...], preferred_element_type=F32))
        row = lax.broadcasted_iota(jnp.int32, (tf, 1), 0) + i * tf
        alt = 1.0 - 2.0 * (row % 2).astype(F32)
        y = y + alt * nyq_ref[...]
        for b in range(BG):
            sl = slice(b * D_C, (b + 1) * D_C)
            o_ref[b] = gate_ref[b] * (y[:, sl] + bias_ref[...] * zt_ref[b])


def _hconv(z, gate, hr, hi, p, order, tabs):
    B, L, _ = z.shape
    if isinstance(p, _LayerParams):
        layer = p.layer
        bias_arg = p.stacked['hy_bias'].reshape(DEPTH, 2, 1, D_C)
        bias_spec = pl.BlockSpec((None, None, 1, D_C), lambda g, ph, i: (layer, order, 0, 0))
    else:
        bias_arg = p['hy_bias'][order].reshape(1, D_C)
        bias_spec = pl.BlockSpec((1, D_C), lambda g, ph, i: (0, 0))
    _, ch, sh = tabs
    BG = min(B, 16) if L <= 256 else min(B, 4)
    assert B % BG == 0
    tf = 256
    nt = L // tf
    mat = pl.BlockSpec((tf, L), lambda g, ph, i: (i, 0))
    tile = pl.BlockSpec((BG, tf, D_C), lambda g, ph, i: (g, i * ph, 0))
    hspec = pl.BlockSpec((tf, D_C), lambda g, ph, i: (i * (1 - ph), order))
    cols = BG * D_C
    return pl.pallas_call(
        functools.partial(_hconv_kernel, BG=BG, L=L), grid=(B // BG, 2, nt),
        in_specs=[pl.BlockSpec((BG, L, D_C), lambda g, ph, i: (g, 0, 0)), tile, tile, hspec, hspec,
                  bias_spec, mat, mat],
        out_specs=tile,
        out_shape=jax.ShapeDtypeStruct((B, L, D_C), F32),
        scratch_shapes=[pltpu.VMEM((L, cols), BF16)] * 3 + [pltpu.VMEM((1, cols), F32)],
        compiler_params=_cparams(
            3,
            block_bytes=(_nbytes((BG, L, D_C), F32) + 3 * _nbytes((BG, tf, D_C), F32) + 2 * _nbytes((tf, L), BF16)
                         + 2 * _nbytes((tf, D_C), F32)),
            scratch_bytes=3 * _nbytes((L, cols), BF16), temp_bytes=4 * _nbytes((tf, cols), F32)),
        name="hyena_conv",
    )(z, z, gate, hr, hi, bias_arg, ch, sh)


def _outffn_kernel(x_ref, yf_ref, yb_ref, bon_ref, g_ref, ob_ref, yc_ref, mod_ref,
                   lnw_ref, lnb_ref, gpost_ref, gffn_ref, gffn_post_ref, w_ref, w1_ref, w2_ref, o_ref):
    bd = _block_diag(D_A, RW_HEAD)
    y = yf_ref[0] + yb_ref[0]
    mu = _segsum(y, bd) * (1.0 / RW_HEAD)
    yc = y - mu
    var = _segsum(yc * yc, bd) * (1.0 / RW_HEAD)
    yn = yc * lax.rsqrt(var + GN_EPS) * lnw_ref[...] + lnb_ref[...]
    ya = (yn + bon_ref[0]) * g_ref[0]
    mix = (jnp.dot(ya.astype(BF16), w_ref[0:D_A, :], preferred_element_type=F32)
           + jnp.dot(ob_ref[0].astype(BF16), w_ref[D_A:D_A + D_B, :], preferred_element_type=F32)
           + jnp.dot(yc_ref[0].astype(BF16), w_ref[D_A + D_B:D_MODEL, :], preferred_element_type=F32))
    mod = mod_ref[0]
    gt1 = mod[:, 2 * D_MODEL:3 * D_MODEL]
    sh2 = mod[:, 3 * D_MODEL:4 * D_MODEL]
    sc2 = mod[:, 4 * D_MODEL:5 * D_MODEL]
    gt2 = mod[:, 5 * D_MODEL:6 * D_MODEL]
    x1 = x_ref[0] + gt1 * _rms(mix, gpost_ref[...])
    h = (_rms(x1, gffn_ref[...]) * (1.0 + sc2) + sh2).astype(BF16)
    acc = None
    for c in range(D_FF // FFN_CHUNK):
        a = jnp.dot(h, w1_ref[:, c * FFN_CHUNK:(c + 1) * FFN_CHUNK], preferred_element_type=F32)
        a = jnp.square(jnp.maximum(a, 0.0)).astype(BF16)
        part = jnp.dot(a, w2_ref[c * FFN_CHUNK:(c + 1) * FFN_CHUNK, :], preferred_element_type=F32)
        acc = part if acc is None else acc + part
    o_ref[0] = x1 + gt2 * _rms(acc, gffn_post_ref[...])


def _outffn(x, yf, yb, bon, g, ob, yc, mod, p, w_out, w1, w2):
    B, L, _ = x.shape
    mb = mod.shape[0]
    row = lambda b, i: (b, i, 0)
    c2 = lambda b, i: (0, 0)
    t = lambda n: pl.BlockSpec((1, TL, n), row)
    vecs = [_pspec(p, 'rwkv_ln_w', (1, D_A)), _pspec(p, 'rwkv_ln_b', (1, D_A)), _pspec(p, 'g_mix_post', (1, D_MODEL)),
            _pspec(p, 'g_ffn_pre', (1, D_MODEL)), _pspec(p, 'g_ffn_post', (1, D_MODEL))]
    in_specs = [t(D_MODEL), t(D_A), t(D_A), t(D_A), t(D_A), t(D_B), t(D_C),
                pl.BlockSpec((1, 1, 6 * D_MODEL), (lambda b, i: (b, 0, 0)) if mb > 1 else (lambda b, i: (0, 0, 0)))]
    in_specs += [sp for sp, _ in vecs]
    in_specs += [_layer_weight_spec(w_out, 2), _layer_weight_spec(w1, 2), _layer_weight_spec(w2, 2)]
    return pl.pallas_call(
        _outffn_kernel, grid=(B, L // TL), in_specs=in_specs,
        out_specs=t(D_MODEL), out_shape=jax.ShapeDtypeStruct((B, L, D_MODEL), F32),
        compiler_params=_cparams(
            2,
            block_bytes=(2 * _nbytes((TL, D_MODEL), F32) + 4 * _nbytes((TL, D_A), F32) + _nbytes((TL, D_B), F32)
                         + _nbytes((TL, D_C), F32) + _nbytes((D_MODEL, D_MODEL), BF16) + 2 * _nbytes((D_MODEL, D_FF), BF16)),
            temp_bytes=4 * _nbytes((TL, D_MODEL), F32) + 2 * _nbytes((TL, FFN_CHUNK), F32)),
        name="outproj_ffn",
    )(x, yf, yb, bon, g, ob, yc, mod, *[a for _, a in vecs], w_out[0], w1[0], w2[0])


@functools.lru_cache(maxsize=None)
def _rope_tables(L):
    rows = L // GRID_W
    row = np.repeat(np.arange(rows), GRID_W).astype(np.float64)
    col = np.tile(np.arange(GRID_W), rows).astype(np.float64)
    half = DIFF_DH // 2
    inv = ROPE_BASE ** (-np.arange(0, half, 2, dtype=np.float64) / half)
    ang_r = row[:, None] * inv[None]
    ang_c = col[:, None] * inv[None]
    ang = np.concatenate([ang_r, ang_r, ang_c, ang_c], axis=-1)
    cos, sin = np.cos(ang), np.sin(ang)
    quarter = (np.arange(DIFF_DH) // 16) % 2
    sin_a = np.where(quarter == 0, -sin, 0.0)
    sin_b = np.where(quarter == 1, sin, 0.0)
    rep = D_B // DIFF_DH
    return tuple(np.tile(t, (1, rep)).astype(np.float32) for t in (cos, sin_a, sin_b))


@functools.lru_cache(maxsize=None)
def _hyena_tables(L):
    t = np.linspace(0.0, 1.0, L, dtype=np.float32).astype(np.float64)[:, None]
    ang = (2.0 * math.pi / L) * np.arange(L, dtype=np.float64)[:, None]
    bands = np.linspace(1e-4, HY_BANDS - 1, HY_BANDS, dtype=np.float32).astype(np.float64)[None, :]
    emb = np.concatenate([t, np.cos(bands * ang), -np.sin(bands * ang)], axis=-1)
    emb = np.pad(emb, ((0, 0), (0, 128 - HY_EMB))).astype(np.float32)
    n = np.arange(L, dtype=np.int64)
    theta = ((n[:, None] * n[None, :]) % (2 * L)).astype(np.float64) * (math.pi / L)
    return emb, np.cos(theta).astype(BF16), np.sin(theta).astype(BF16)


def _layer(x, mod, p, wb, l, lam_init, cache, rope_tabs, hy_tabs, filt):
    B, L, _ = x.shape
    (uq, uk, uv, r, v, nkk, wf, kaf, kdf, wbk, kab, kdb, g, bon, x1h, x2h, hv) = _inproj(
        x, mod, wb['w_in'], rope_tabs, p)
    if cache is None:
        s0 = jnp.zeros((B, 2, H_A, RW_HEAD, RW_HEAD), F32)
        kv_cache = None
    else:
        s0 = cache[0]
        kv_cache = (cache[1], cache[2])
    yf, yb, sfin = _scan(nkk, r, v, wf, kaf, kdf, wbk, kab, kdb, s0)
    ob = _attention(uq, uk, uv, kv_cache, l, p, lam_init)
    hr, hi = filt
    z1 = _hconv(hv, x1h, hr, hi, p, 0, hy_tabs)
    yc = _hconv(z1, x2h, hr, hi, p, 1, hy_tabs)
    x2 = _outffn(x, yf, yb, bon, g, ob, yc, mod, p, wb['w_out'], wb['w_ff1'], wb['w_ff2'])
    return x2, (sfin, uk, uv)


def _derive_params(d):
    w1 = d['hy_w1']
    d['hy_w1_pad'] = jnp.pad(w1, [(0, 0)] * (w1.ndim - 2) + [(0, 128 - HY_EMB), (0, 0)])
    dec = d['hy_decay']
    d['hy_decay4'] = jnp.tile(dec, (1,) * (dec.ndim - 1) + (4,))
    return d


_LAYER_KEYS = ('g_mix_pre', 'g_mix_post', 'g_ffn_pre', 'g_ffn_post', 'rwkv_conv', 'rwkv_w0', 'rwkv_w2', 'rwkv_a0',
               'rwkv_a2', 'rwkv_g2', 'rwkv_kk', 'rwkv_ka', 'rwkv_rk', 'rwkv_ln_w', 'rwkv_ln_b', 'diff_lq1', 'diff_lk1',
               'diff_lq2', 'diff_lk2', 'diff_subln', 'hy_conv_w', 'hy_conv_b', 'hy_w1', 'hy_b1', 'hy_freq', 'hy_w2',
               'hy_b2', 'hy_w3', 'hy_decay', 'hy_bias')


def kernel(x_prompt, x_sample, state_rwkv, cache_k, cache_v, c, c_ctx, w_mod, b_mod, g_mix_pre, g_mix_post, g_ffn_pre, g_ffn_post, w_in, rwkv_conv, rwkv_w0, rwkv_w2, rwkv_a0, rwkv_a2, rwkv_g2, rwkv_kk, rwkv_ka, rwkv_rk, rwkv_ln_w, rwkv_ln_b, diff_lq1, diff_lk1, diff_lq2, diff_lk2, diff_subln, hy_conv_w, hy_conv_b, hy_w1, hy_b1, hy_freq, hy_w2, hy_b2, hy_w3, hy_decay, hy_bias, w_out, w_ff1, w_ff2):
    stacked = dict(g_mix_pre=g_mix_pre, g_mix_post=g_mix_post, g_ffn_pre=g_ffn_pre, g_ffn_post=g_ffn_post,
                   rwkv_conv=rwkv_conv, rwkv_w0=rwkv_w0, rwkv_w2=rwkv_w2, rwkv_a0=rwkv_a0, rwkv_a2=rwkv_a2,
                   rwkv_g2=rwkv_g2, rwkv_kk=rwkv_kk, rwkv_ka=rwkv_ka, rwkv_rk=rwkv_rk, rwkv_ln_w=rwkv_ln_w,
                   rwkv_ln_b=rwkv_ln_b, diff_lq1=diff_lq1, diff_lk1=diff_lk1, diff_lq2=diff_lq2, diff_lk2=diff_lk2,
                   diff_subln=diff_subln, hy_conv_w=hy_conv_w, hy_conv_b=hy_conv_b, hy_w1=hy_w1, hy_b1=hy_b1,
                   hy_freq=hy_freq, hy_w2=hy_w2, hy_b2=hy_b2, hy_w3=hy_w3, hy_decay=hy_decay, hy_bias=hy_bias)
    _derive_params(stacked)
    Bc, Lc, _ = x_prompt.shape
    Bs, Ls, _ = x_sample.shape
    past = cache_k.shape[2]

    cond = jnp.zeros((8, D_MODEL), F32).at[0:Bs].set(c).at[Bs].set(c_ctx)
    mod_all = _modulation(cond, w_mod, b_mod)

    rope_tabs = _rope_tables(Ls)
    tabs_c = _hyena_tables(Lc)
    tabs_s = _hyena_tables(Ls)
    ck = cache_k.reshape(Bs, DEPTH, past, D_B)
    cv = cache_v.reshape(Bs, DEPTH, past, D_B)

    w_in_b, w_out_b, w_ff1_b, w_ff2_b = (w.astype(BF16) for w in (w_in, w_out, w_ff1, w_ff2))
    xp, xs = x_prompt, x_sample
    st_list, k_list, v_list = [], [], []
    for l in range(DEPTH):
        p = _LayerParams(stacked, l)
        wb = dict(w_in=(w_in_b, l), w_out=(w_out_b, l), w_ff1=(w_ff1_b, l), w_ff2=(w_ff2_b, l))
        lam_init = 0.8 - 0.6 * math.exp(-0.3 * l)
        mod_lat = mod_all[l, 0:Bs].reshape(Bs, 1, 6 * D_MODEL)
        mod_ctx = mod_all[l, Bs:Bs + 1].reshape(1, 1, 6 * D_MODEL)
        filt_c = _hyena_filters(Lc, p, tabs_c)
        filt_s = _hyena_filters(Ls, p, tabs_s)
        xp, (s_ctx, k_ctx, v_ctx) = _layer(xp, mod_ctx, p, wb, l, lam_init, None, None, tabs_c, filt_c)
        st_list.append(s_ctx)
        k_list.append(k_ctx)
        v_list.append(v_ctx)
        s0 = state_rwkv[:, l]
        xs, _ = _layer(xs, mod_lat, p, wb, l, lam_init, (s0, ck, cv), rope_tabs, tabs_s, filt_s)
    new_k = jnp.stack(k_list, axis=1).reshape(Bc, DEPTH, Lc, H_B, 2, DIFF_DH)
    new_v = jnp.stack(v_list, axis=1).reshape(Bc, DEPTH, Lc, H_B, 2 * DIFF_DH)
    return (xp, xs, jnp.stack(st_list, axis=1), new_k, new_v)
```
